```python
import math
import jax
import jax.numpy as jnp
from jax import lax
import numpy as np

D_MODEL = 1024
BATCH = 32
SEQ = 256
DEPTH = 2
DEC_BATCH = 4
DEC_SEQ = 1024
PAST_LEN = 256

GRID_W = 64
N_MIXERS = 2
N_ATTN_LAYERS = (DEPTH + 1) // 2
N_REC_LAYERS = DEPTH // 2
N_HEADS = 16
N_KV_HEADS = 4
HEAD_DIM = D_MODEL // N_HEADS
AXIS_DIM = HEAD_DIM // 2
ATTN_WIDTH = N_HEADS * HEAD_DIM
KV_WIDTH = N_KV_HEADS * HEAD_DIM
Q_BLOCK = 128
ROPE_THETA = 10000.0
REC_EXPAND = 128
N_REC_HEADS = D_MODEL // REC_EXPAND
REC_DK = REC_EXPAND
REC_DV = D_MODEL // N_REC_HEADS
REC_KEY_WIDTH = N_REC_HEADS * REC_DK
REC_VAL_WIDTH = N_REC_HEADS * REC_DV
CHUNK = 16
NORM_EPS = 1e-6
LN_EPS = 1e-5
DEEPNORM_ALPHA = (2.0 * DEPTH) ** 0.25
DEEPNORM_BETA = (8.0 * DEPTH) ** -0.25

kernel_name = "hybrid_gqa_hgrn2_diffusion_step"


def _rms_norm(x, g):
    xf = x.astype(jnp.float32)
    y = xf * lax.rsqrt(jnp.mean(xf * xf, axis=-1, keepdims=True) + NORM_EPS)
    return (y * g.astype(jnp.float32)).astype(x.dtype)


def _layer_norm(x, g, b):
    xf = x.astype(jnp.float32)
    mu = jnp.mean(xf, axis=-1, keepdims=True)
    var = jnp.mean(jnp.square(xf - mu), axis=-1, keepdims=True)
    y = (xf - mu) * lax.rsqrt(var + LN_EPS) * g.astype(jnp.float32) + b.astype(jnp.float32)
    return y.astype(x.dtype)


def _adaln(cond, w, b):
    mod = jax.nn.silu(cond.astype(jnp.float32)) @ w.astype(jnp.float32) + b.astype(jnp.float32)
    mod = mod.astype(cond.dtype)
    return jnp.split(mod, 3, axis=-1)


def _axial_rope(n_tokens):
    n_rows = n_tokens // GRID_W
    rows = jnp.repeat(jnp.arange(n_rows, dtype=jnp.float32), GRID_W)
    cols = jnp.tile(jnp.arange(GRID_W, dtype=jnp.float32), n_rows)
    inv_freq = 1.0 / (ROPE_THETA ** (jnp.arange(0, AXIS_DIM, 2, dtype=jnp.float32) / AXIS_DIM))
    ang_r = rows[:, None] * inv_freq[None, :]
    ang_c = cols[:, None] * inv_freq[None, :]
    ang = jnp.concatenate([ang_r, ang_r, ang_c, ang_c], axis=-1)
    return jnp.cos(ang)[:, None, :], jnp.sin(ang)[:, None, :]


def _apply_rope(x, cos, sin):
    xf = x.astype(jnp.float32)

    def rot_half(u):
        u1, u2 = jnp.split(u, 2, axis=-1)
        return jnp.concatenate([-u2, u1], axis=-1)

    xr, xc = jnp.split(xf, 2, axis=-1)
    rotated = jnp.concatenate([rot_half(xr), rot_half(xc)], axis=-1)
    return (xf * cos + rotated * sin).astype(x.dtype)


def _block_attention(q, k, v):
    B, Lq, H, d = q.shape
    G = H // N_KV_HEADS
    nb = Lq // Q_BLOCK
    qb = q.reshape(B, nb, Q_BLOCK, N_KV_HEADS, G, d).transpose(1, 0, 2, 3, 4, 5)
    kf = k.astype(jnp.float32)
    vf = v.astype(jnp.float32)
    scale = 1.0 / math.sqrt(d)

    def one_block(q_blk):
        s = jnp.einsum('bqkgd,bskd->bkgqs', q_blk.astype(jnp.float32), kf) * scale
        p = jax.nn.softmax(s, axis=-1)
        return jnp.einsum('bkgqs,bskd->bqkgd', p, vf).astype(q.dtype)

    out = lax.map(one_block, qb)
    return out.transpose(1, 0, 2, 3, 4, 5).reshape(B, Lq, H, d)


def _attn_project(h, w_in, q_gain, k_gain):
    B, L, _ = h.shape
    proj = h @ w_in
    q, k, v, g = jnp.split(proj, [ATTN_WIDTH, ATTN_WIDTH + KV_WIDTH, ATTN_WIDTH + 2 * KV_WIDTH], axis=-1)
    q = _rms_norm(q.reshape(B, L, N_HEADS, HEAD_DIM), q_gain)
    k = _rms_norm(k.reshape(B, L, N_KV_HEADS, HEAD_DIM), k_gain)
    v = v.reshape(B, L, N_KV_HEADS, HEAD_DIM)
    return q, k, v, g


def _attn_out(o, g, w_out):
    B, L = o.shape[0], o.shape[1]
    gated = o.reshape(B, L, ATTN_WIDTH).astype(jnp.float32) * jax.nn.silu(g.astype(jnp.float32))
    return gated.astype(o.dtype) @ w_out


def _attn_context(h, w_in, q_gain, k_gain, w_out):
    q, k, v, g = _attn_project(h, w_in, q_gain, k_gain)
    o = _block_attention(q, k, v)
    return _attn_out(o, g, w_out), k, v


def _attn_latent(h, ctx_k, ctx_v, w_in, q_gain, k_gain, w_out):
    q, k, v, g = _attn_project(h, w_in, q_gain, k_gain)
    cos, sin = _axial_rope(h.shape[1])
    q = _apply_rope(q, cos, sin)
    k = _apply_rope(k, cos, sin)
    keys = jnp.concatenate([ctx_k.astype(k.dtype), k], axis=1)
    vals = jnp.concatenate([ctx_v.astype(v.dtype), v], axis=1)
    o = _block_attention(q, keys, vals)
    return _attn_out(o, g, w_out)


def _gla_chunked(q, k, v, log_f, s0):
    B, H, L, dk = q.shape
    dv = v.shape[-1]
    n = L // CHUNK

    def chunks(u):
        return jnp.moveaxis(u.reshape(B, H, n, CHUNK, u.shape[-1]), 2, 0)

    b = jnp.cumsum(chunks(log_f), axis=3)
    mask = jnp.tril(jnp.ones((CHUNK, CHUNK), dtype=bool))[:, :, None]

    def step(S, xs):
        q_i, k_i, v_i, b_i = xs
        diff = b_i[:, :, :, None, :] - b_i[:, :, None, :, :]
        decay = jnp.where(mask, jnp.exp(jnp.minimum(diff, 0.0)), 0.0)
        scores = jnp.einsum('bhtk,bhsk,bhtsk->bhts', q_i, k_i, decay)
        b_last = b_i[:, :, -1, :]
        o_i = (jnp.einsum('bhts,bhsv->bhtv', scores, v_i)
               + jnp.einsum('bhtk,bhkv->bhtv', q_i * jnp.exp(b_i), S))
        S = (jnp.exp(b_last)[..., None] * S
             + jnp.einsum('bhsk,bhsv->bhkv', k_i * jnp.exp(b_last[:, :, None, :] - b_i), v_i))
        return S, o_i

    s_fin, o = lax.scan(step, s0.astype(jnp.float32), (chunks(q), chunks(k), chunks(v), b))
    o = jnp.moveaxis(o, 0, 2).reshape(B, H, L, dv)
    return o, s_fin


def _rec_mix(h, s0_fw, s0_bw, w_in, lb, norm_gain, w_out):
    B, L, _ = h.shape
    proj = h @ w_in
    q, f_fw, f_bw, i_in, g = jnp.split(
        proj, [REC_KEY_WIDTH, 2 * REC_KEY_WIDTH, 3 * REC_KEY_WIDTH, 3 * REC_KEY_WIDTH + REC_VAL_WIDTH], axis=-1)

    def heads(u, dh):
        return u.astype(jnp.float32).reshape(B, L, N_REC_HEADS, dh).transpose(0, 2, 1, 3)

    qh = jax.nn.silu(heads(q, REC_DK))
    vh = heads(i_in, REC_DV)
    lbf = lb.astype(jnp.float32)

    def gates(f_raw, lb_d):
        z = heads(f_raw, REC_DK)
        lb_h = lb_d.reshape(N_REC_HEADS, 1, REC_DK)
        forget = lb_h + (1.0 - lb_h) * jax.nn.sigmoid(z)
        inp = (1.0 - lb_h) * jax.nn.sigmoid(-z)
        return jnp.log(forget), inp

    lf_fw, k_fw = gates(f_fw, lbf[0])
    lf_bw, k_bw = gates(f_bw, lbf[1])
    o_fw, s_fw = _gla_chunked(qh, k_fw, vh, lf_fw, s0_fw)

    def flip(u):
        return jnp.flip(u, axis=2)

    o_bw, s_bw = _gla_chunked(flip(qh), flip(k_bw), flip(vh), flip(lf_bw), s0_bw)
    o = (o_fw + flip(o_bw)).transpose(0, 2, 1, 3)
    o = _rms_norm(o, norm_gain).reshape(B, L, REC_VAL_WIDTH)
    gated = (o * jax.nn.silu(g.astype(jnp.float32))).astype(h.dtype)
    return gated @ w_out, s_fw, s_bw


def setup_inputs(seed: int = 0) -> dict:
    key = jax.random.key(seed)
    ks = jax.random.split(key, 20)
    f32 = jnp.float32
    attn_in_w = 2 * ATTN_WIDTH + 2 * KV_WIDTH
    rec_in_w = 3 * REC_KEY_WIDTH + 2 * REC_VAL_WIDTH
    return {
        'x_prompt': jax.random.normal(ks[0], (BATCH, SEQ, D_MODEL), f32),
        'x_sample': jax.random.normal(ks[1], (DEC_BATCH, DEC_SEQ, D_MODEL), f32),
        'cache_k': jax.random.normal(ks[2], (DEC_BATCH, N_ATTN_LAYERS, PAST_LEN, N_KV_HEADS, HEAD_DIM), f32),
        'cache_v': jax.random.normal(ks[3], (DEC_BATCH, N_ATTN_LAYERS, PAST_LEN, N_KV_HEADS, HEAD_DIM), f32),
        'state_rec': 0.5 * jax.random.normal(ks[4], (DEC_BATCH, N_REC_LAYERS, 2, N_REC_HEADS, REC_DK, REC_DV), f32),
        'c': jax.random.normal(ks[5], (DEC_BATCH, D_MODEL), f32),
        'c_ctx': jax.random.normal(ks[6], (D_MODEL,), f32),
        'ada_w': 0.3 * D_MODEL ** -0.5 * jax.random.normal(ks[7], (DEPTH, D_MODEL, 3 * D_MODEL), f32),
        'ada_b': 0.02 * jax.random.normal(ks[8], (DEPTH, 3 * D_MODEL), f32),
        'attn_w_in': D_MODEL ** -0.5 * jax.random.normal(ks[9], (N_ATTN_LAYERS, D_MODEL, attn_in_w), f32),
        'attn_q_gain': 1.0 + 0.05 * jax.random.normal(ks[10], (N_ATTN_LAYERS, HEAD_DIM), f32),
        'attn_k_gain': 1.0 + 0.05 * jax.random.normal(ks[11], (N_ATTN_LAYERS, HEAD_DIM), f32),
        'attn_w_out': DEEPNORM_BETA * ATTN_WIDTH ** -0.5 * jax.random.normal(ks[12], (N_ATTN_LAYERS, ATTN_WIDTH, D_MODEL), f32),
        'rec_w_in': D_MODEL ** -0.5 * jax.random.normal(ks[13], (N_REC_LAYERS, D_MODEL, rec_in_w), f32),
        'rec_lower_bounds': 0.1 * jax.random.normal(ks[14], (DEPTH, 2, REC_KEY_WIDTH), f32),
        'rec_norm_gain': 1.0 + 0.05 * jax.random.normal(ks[15], (N_REC_LAYERS, REC_DV), f32),
        'rec_w_out': DEEPNORM_BETA * REC_VAL_WIDTH ** -0.5 * jax.random.normal(ks[16], (N_REC_LAYERS, REC_VAL_WIDTH, D_MODEL), f32),
        'ln_gain': 1.0 + 0.05 * jax.random.normal(ks[17], (DEPTH, D_MODEL), f32),
        'ln_bias': 0.02 * jax.random.normal(ks[18], (DEPTH, D_MODEL), f32),
    }


def reference(x_prompt, x_sample, cache_k, cache_v, state_rec, c, c_ctx,
              ada_w, ada_b, attn_w_in, attn_q_gain, attn_k_gain, attn_w_out,
              rec_w_in, rec_lower_bounds, rec_norm_gain, rec_w_out, ln_gain, ln_bias):
    lb_soft = jax.nn.softmax(rec_lower_bounds.astype(jnp.float32), axis=0)
    lb_all = jnp.cumsum(lb_soft, axis=0) - lb_soft[0]

    xp = x_prompt
    xs = x_sample
    new_k, new_v, new_s = [], [], []
    for i in range(DEPTH):
        j = i // N_MIXERS
        sh_p, sc_p, g_p = _adaln(c_ctx, ada_w[i], ada_b[i])
        sh_s, sc_s, g_s = _adaln(c, ada_w[i], ada_b[i])
        sh_s, sc_s, g_s = sh_s[:, None, :], sc_s[:, None, :], g_s[:, None, :]
        hp = xp * (1.0 + sc_p) + sh_p
        hs = xs * (1.0 + sc_s) + sh_s
        if i % N_MIXERS == 0:
            out_p, kp, vp = _attn_context(hp, attn_w_in[j], attn_q_gain[j], attn_k_gain[j], attn_w_out[j])
            new_k.append(kp.astype(x_prompt.dtype))
            new_v.append(vp.astype(x_prompt.dtype))
            out_s = _attn_latent(hs, cache_k[:, j], cache_v[:, j], attn_w_in[j], attn_q_gain[j],
                                 attn_k_gain[j], attn_w_out[j])
        else:
            zeros = jnp.zeros((xp.shape[0], N_REC_HEADS, REC_DK, REC_DV), jnp.float32)
            out_p, sp_fw, sp_bw = _rec_mix(hp, zeros, zeros, rec_w_in[j], lb_all[i],
                                           rec_norm_gain[j], rec_w_out[j])
            new_s.append(jnp.stack([sp_fw, sp_bw], axis=1).astype(x_prompt.dtype))
            out_s, _, _ = _rec_mix(hs, state_rec[:, j, 0], state_rec[:, j, 1], rec_w_in[j], lb_all[i],
                                   rec_norm_gain[j], rec_w_out[j])
        xp = _layer_norm(DEEPNORM_ALPHA * xp + g_p * out_p, ln_gain[i], ln_bias[i])
        xs = _layer_norm(DEEPNORM_ALPHA * xs + g_s * out_s, ln_gain[i], ln_bias[i])

    new_cache_k = jnp.stack(new_k, axis=1)
    new_cache_v = jnp.stack(new_v, axis=1)
    new_state_rec = jnp.stack(new_s, axis=1)
    return (xp, xs, new_cache_k, new_cache_v, new_state_rec)
```

```python
import functools
import math

import jax
import jax.numpy as jnp
from jax import lax
from jax.experimental import pallas as pl
from jax.experimental.pallas import tpu as pltpu

F32 = jnp.float32
BF16 = jnp.bfloat16

D_MODEL = 1024
DEPTH = 2
GRID_W = 64
N_HEADS = 16
N_KV_HEADS = 4
HEAD_DIM = 64
AXIS_DIM = HEAD_DIM // 2
ATTN_WIDTH = N_HEADS * HEAD_DIM
KV_WIDTH = N_KV_HEADS * HEAD_DIM
ROPE_THETA = 10000.0
N_REC_HEADS = 8
REC_DK = 128
REC_DV = 128
REC_WIDTH = N_REC_HEADS * REC_DK
NORM_EPS = 1e-6
LN_EPS = 1e-5
DEEPNORM_ALPHA = (2.0 * DEPTH) ** 0.25

COND_ROWS = 8
ROW_TILE = 256
GROUP_LANES = 256
GLA_TILE = 128
GLA_LEVELS = (1, 2, 4, 8, 16, 32, 64)
VMEM_LIMIT = 56 * 1024 * 1024


def _sigmoid(x):
    return 1.0 / (1.0 + jnp.exp(-x))


def _dot(a, b):
    return jnp.dot(a, b, preferred_element_type=F32)


def _dot_nt(a, b):
    return lax.dot_general(a, b, (((1,), (1,)), ((), ())), preferred_element_type=F32)


def _dot_tn(a, b):
    return lax.dot_general(a, b, (((0,), (0,)), ((), ())), preferred_element_type=F32)


def _layer_norm(y, g, b):
    mu = jnp.mean(y, axis=-1, keepdims=True)
    yc = y - mu
    var = jnp.mean(yc * yc, axis=-1, keepdims=True)
    return yc * lax.rsqrt(var + LN_EPS) * g + b


def _mods_kernel(cond_ref, w_ref, b_ref, out_ref):
    c = cond_ref[...]
    s = (c * _sigmoid(c)).astype(BF16)
    out_ref[...] = _dot(s, w_ref[...].astype(BF16)) + b_ref[...]


def _mods(cond, ada_w, ada_b):
    d = D_MODEL
    return pl.pallas_call(
        _mods_kernel,
        grid=(DEPTH, 3),
        in_specs=[
            pl.BlockSpec((COND_ROWS, d), lambda l, j: (0, 0)),
            pl.BlockSpec((None, d, d), lambda l, j: (l, 0, j)),
            pl.BlockSpec((None, 1, d), lambda l, j: (l, 0, j)),
        ],
        out_specs=pl.BlockSpec((None, COND_ROWS, d), lambda l, j: (l, 0, j)),
        out_shape=jax.ShapeDtypeStruct((DEPTH, COND_ROWS, 3 * d), F32),
        compiler_params=pltpu.CompilerParams(vmem_limit_bytes=VMEM_LIMIT),
        name="adaln_mods",
    )(cond, ada_w, ada_b.reshape(DEPTH, 1, 3 * d))


def _lower_bounds_kernel(r_ref, out_ref):
    r = [r_ref[i] for i in range(DEPTH)]
    m = functools.reduce(jnp.maximum, r)
    e = [jnp.exp(x - m) for x in r]
    tot = functools.reduce(lambda a, b: a + b, e)
    soft = [x / tot for x in e]
    acc = soft[0]
    for i in range(DEPTH):
        if i > 0:
            acc = acc + soft[i]
        out_ref[i] = acc - soft[0]


def _lower_bounds(rec_lower_bounds):
    return pl.pallas_call(
        _lower_bounds_kernel,
        out_shape=jax.ShapeDtypeStruct(rec_lower_bounds.shape, F32),
        name="rec_lower_bounds",
    )(rec_lower_bounds)


def _attn_proj_kernel(*refs, rope):
    if rope:
        (x_ref, mod_ref, w_ref, pn_ref, qg_ref, kg_ref, cos_ref, sa_ref, sb_ref,
         q_out, k_out, v_out, g_out) = refs
    else:
        (x_ref, mod_ref, w_ref, pn_ref, qg_ref, kg_ref,
         q_out, k_out, v_out, g_out) = refs
    d = D_MODEL
    mod = mod_ref[...]
    shift, scale = mod[:, :d], mod[:, d:2 * d]
    h = (x_ref[...] * (1.0 + scale) + shift).astype(BF16)
    pn = pn_ref[...]

    def norm_rope(u, gain):
        ms = _dot((u * u).astype(BF16), pn)
        y = u * lax.rsqrt(ms + NORM_EPS) * gain
        if rope:
            y = (y * cos_ref[...]
                 + pltpu.roll(y, GROUP_LANES - AXIS_DIM // 2, 1) * sa_ref[...]
                 + pltpu.roll(y, AXIS_DIM // 2, 1) * sb_ref[...])
        return y

    for j in range(N_KV_HEADS):
        lo = j * GROUP_LANES
        u = _dot(h, w_ref[:, lo:lo + GROUP_LANES])
        q_out[:, lo:lo + GROUP_LANES] = norm_rope(u, qg_ref[...]).astype(q_out.dtype)
    u = _dot(h, w_ref[:, ATTN_WIDTH:ATTN_WIDTH + KV_WIDTH])
    k_out[...] = norm_rope(u, kg_ref[...]).astype(k_out.dtype)
    u = _dot(h, w_ref[:, ATTN_WIDTH + KV_WIDTH:ATTN_WIDTH + 2 * KV_WIDTH])
    v_out[...] = u.astype(v_out.dtype)
    for j in range(N_KV_HEADS):
        lo = ATTN_WIDTH + 2 * KV_WIDTH + j * GROUP_LANES
        u = _dot(h, w_ref[:, lo:lo + GROUP_LANES])
        g_out[:, j * GROUP_LANES:(j + 1) * GROUP_LANES] = (u * _sigmoid(u)).astype(g_out.dtype)


def _attn_proj(x2, mods4, w_bf, pn, qg, kg, rope_tabs, *, seq_len, mod_row0, per_seq_mod, kv_dtype):
    n_tok = x2.shape[0]
    d = D_MODEL
    tiles_per_seq = seq_len // ROW_TILE
    rope = rope_tabs is not None

    def mod_map(i):
        row = mod_row0 + (i // tiles_per_seq if per_seq_mod else 0)
        return (0, row, 0, 0)

    full = lambda i: (0, 0)
    in_specs = [
        pl.BlockSpec((ROW_TILE, d), lambda i: (i, 0)),
        pl.BlockSpec((None, None, 1, 3 * d), mod_map),
        pl.BlockSpec(w_bf.shape, full),
        pl.BlockSpec(pn.shape, full),
        pl.BlockSpec(qg.shape, full),
        pl.BlockSpec(kg.shape, full),
    ]
    args = [x2, mods4, w_bf, pn, qg, kg]
    if rope:
        for t in rope_tabs:
            in_specs.append(pl.BlockSpec((ROW_TILE, GROUP_LANES), lambda i: (i % tiles_per_seq, 0)))
            args.append(t)
    out_shape = (
        jax.ShapeDtypeStruct((n_tok, ATTN_WIDTH), BF16),
        jax.ShapeDtypeStruct((n_tok, KV_WIDTH), kv_dtype),
        jax.ShapeDtypeStruct((n_tok, KV_WIDTH), kv_dtype),
        jax.ShapeDtypeStruct((n_tok, ATTN_WIDTH), BF16),
    )
    out_specs = (
        pl.BlockSpec((ROW_TILE, ATTN_WIDTH), lambda i: (i, 0)),
        pl.BlockSpec((ROW_TILE, KV_WIDTH), lambda i: (i, 0)),
        pl.BlockSpec((ROW_TILE, KV_WIDTH), lambda i: (i, 0)),
        pl.BlockSpec((ROW_TILE, ATTN_WIDTH), lambda i: (i, 0)),
    )
    return pl.pallas_call(
        functools.partial(_attn_proj_kernel, rope=rope),
        grid=(n_tok // ROW_TILE,),
        in_specs=in_specs,
        out_specs=out_specs,
        out_shape=out_shape,
        compiler_params=pltpu.CompilerParams(vmem_limit_bytes=VMEM_LIMIT),
        name="attn_proj_rope" if rope else "attn_proj",
    )(*args)


def _attn_core_kernel(*refs, has_cache):
    if has_cache:
        (q_ref, k_ref, v_ref, kc_ref, vc_ref, sg_ref, x_ref, mod_ref, w_ref, lng_ref, lnb_ref,
         out_ref) = refs
    else:
        (q_ref, k_ref, v_ref, sg_ref, x_ref, mod_ref, w_ref, lng_ref, lnb_ref, out_ref) = refs
    d = D_MODEL
    q = q_ref[...]
    k = k_ref[...].astype(BF16)
    v = v_ref[...].astype(BF16)
    if has_cache:
        kc = kc_ref[...].astype(BF16)
        vc = vc_ref[...].astype(BF16)
    heads_per_kv = N_HEADS // N_KV_HEADS
    outs = []
    for j in range(N_KV_HEADS):
        sl = slice(j * HEAD_DIM, (j + 1) * HEAD_DIM)
        kj, vj = k[:, sl], v[:, sl]
        if has_cache:
            kcj, vcj = kc[:, sl], vc[:, sl]
        for hh in range(heads_per_kv):
            hd = j * heads_per_kv + hh
            qh = q[:, hd * HEAD_DIM:(hd + 1) * HEAD_DIM]
            s = _dot_nt(qh, kj)
            m = jnp.max(s, axis=-1, keepdims=True)
            if has_cache:
                s_c = _dot_nt(qh, kcj)
                m = jnp.maximum(m, jnp.max(s_c, axis=-1, keepdims=True))
            p = jnp.exp(s - m)
            l = jnp.sum(p, axis=-1, keepdims=True)
            o = _dot(p.astype(BF16), vj)
            if has_cache:
                p_c = jnp.exp(s_c - m)
                l = l + jnp.sum(p_c, axis=-1, keepdims=True)
                o = o + _dot(p_c.astype(BF16), vcj)
            outs.append(o * (1.0 / l))
    o = jnp.concatenate(outs, axis=1)
    gated = (o * sg_ref[...].astype(F32)).astype(BF16)
    branch = _dot(gated, w_ref[...])
    gate = mod_ref[...][:, 2 * d:]
    y = DEEPNORM_ALPHA * x_ref[...] + gate * branch
    out_ref[...] = _layer_norm(y, lng_ref[...], lnb_ref[...])


def _attn_core(q, k, v, cache, sg, x2, mods4, w_out_bf, ln_g, ln_b, *, n_seq, seq_len, q_tile,
               mod_row0, per_seq_mod):
    d = D_MODEL
    tiles = seq_len // q_tile
    has_cache = cache is not None
    q3 = q.reshape(n_seq, seq_len, ATTN_WIDTH)
    k3 = k.reshape(n_seq, seq_len, KV_WIDTH)
    v3 = v.reshape(n_seq, seq_len, KV_WIDTH)
    sg3 = sg.reshape(n_seq, seq_len, ATTN_WIDTH)
    x3 = x2.reshape(n_seq, seq_len, d)

    def mod_map(b, i):
        return (0, mod_row0 + (b if per_seq_mod else 0), 0, 0)

    tile_spec = lambda width: pl.BlockSpec((None, q_tile, width), lambda b, i: (b, i, 0))
    seq_spec = lambda length, width: pl.BlockSpec((None, length, width), lambda b, i: (b, 0, 0))
    full2 = lambda b, i: (0, 0)
    in_specs = [tile_spec(ATTN_WIDTH), seq_spec(seq_len, KV_WIDTH), seq_spec(seq_len, KV_WIDTH)]
    args = [q3, k3, v3]
    if has_cache:
        ck, cv = cache
        in_specs += [seq_spec(ck.shape[1], KV_WIDTH), seq_spec(cv.shape[1], KV_WIDTH)]
        args += [ck, cv]
    in_specs += [
        tile_spec(ATTN_WIDTH),
        tile_spec(d),
        pl.BlockSpec((None, None, 1, 3 * d), mod_map),
        pl.BlockSpec(w_out_bf.shape, full2),
        pl.BlockSpec(ln_g.shape, full2),
        pl.BlockSpec(ln_b.shape, full2),
    ]
    args += [sg3, x3, mods4, w_out_bf, ln_g, ln_b]
    out = pl.pallas_call(
        functools.partial(_attn_core_kernel, has_cache=has_cache),
        grid=(n_seq, tiles),
        in_specs=in_specs,
        out_specs=tile_spec(d),
        out_shape=jax.ShapeDtypeStruct((n_seq, seq_len, d), F32),
        compiler_params=pltpu.CompilerParams(vmem_limit_bytes=VMEM_LIMIT),
        name="attn_core_cache" if has_cache else "attn_core",
    )(*args)
    return out.reshape(n_seq * seq_len, d)


def _rec_proj_kernel(x_ref, mod_ref, w_ref, lb_ref, q_out, v_out, g_out, lf_fw, lf_bw, k_fw, k_bw):
    d = D_MODEL
    mod = mod_ref[...]
    shift, scale = mod[:, :d], mod[:, d:2 * d]
    h = (x_ref[...] * (1.0 + scale) + shift).astype(BF16)
    u = _dot(h, w_ref[:, 0:REC_WIDTH])
    q_out[...] = (u * _sigmoid(u)).astype(q_out.dtype)
    for direction, (lf_out, k_out) in enumerate(((lf_fw, k_fw), (lf_bw, k_bw))):
        z = _dot(h, w_ref[:, (1 + direction) * REC_WIDTH:(2 + direction) * REC_WIDTH])
        lb = lb_ref[direction:direction + 1, :]
        sig = _sigmoid(z)
        forget = lb + (1.0 - lb) * sig
        lf_out[...] = jnp.log(forget)
        k_out[...] = ((1.0 - lb) * (1.0 - sig)).astype(k_out.dtype)
    u = _dot(h, w_ref[:, 3 * REC_WIDTH:4 * REC_WIDTH])
    v_out[...] = u.astype(v_out.dtype)
    u = _dot(h, w_ref[:, 4 * REC_WIDTH:5 * REC_WIDTH])
    g_out[...] = (u * _sigmoid(u)).astype(g_out.dtype)


def _rec_proj(x2, mods4, w_bf, lb, *, seq_len, mod_row0, per_seq_mod):
    n_tok = x2.shape[0]
    d = D_MODEL
    tiles_per_seq = seq_len // ROW_TILE

    def mod_map(i):
        row = mod_row0 + (i // tiles_per_seq if per_seq_mod else 0)
        return (1, row, 0, 0)

    row_spec = pl.BlockSpec((ROW_TILE, REC_WIDTH), lambda i: (i, 0))
    full = lambda i: (0, 0)
    out_dtypes = (BF16, BF16, BF16, F32, F32, BF16, BF16)
    return pl.pallas_call(
        _rec_proj_kernel,
        grid=(n_tok // ROW_TILE,),
        in_specs=[
            pl.BlockSpec((ROW_TILE, d), lambda i: (i, 0)),
            pl.BlockSpec((None, None, 1, 3 * d), mod_map),
            pl.BlockSpec(w_bf.shape, full),
            pl.BlockSpec(lb.shape, full),
        ],
        out_specs=tuple(row_spec for _ in out_dtypes),
        out_shape=tuple(jax.ShapeDtypeStruct((n_tok, REC_WIDTH), t) for t in out_dtypes),
        compiler_params=pltpu.CompilerParams(vmem_limit_bytes=VMEM_LIMIT),
        name="rec_proj",
    )(x2, mods4, w_bf, lb)


def _block_centre(b, half, backward):
    t_rows, lanes = b.shape
    size = 2 * half
    c = half if backward else half - 1
    if size >= 8:
        b3 = b.reshape(t_rows // size, size, lanes)
        return jnp.broadcast_to(b3[:, c:c + 1, :], b3.shape).reshape(t_rows, lanes)
    pos = lax.broadcasted_iota(jnp.int32, b.shape, 0) % size
    out = b
    for r in range(size):
        if r == c:
            continue
        shifted = pltpu.roll(b, (r - c) % t_rows, 0)
        out = jnp.where(pos == r, shifted, out)
    return out


def _gla_tile(q, k, v, lf, st, tri, masks_ref, backward):
    t_rows = q.shape[0]
    hi = lf.astype(BF16)
    lo = (lf - hi.astype(F32)).astype(BF16)
    b = _dot(tri, hi) + _dot(tri, lo)
    scores = masks_ref[0] * _dot_nt(q.astype(BF16), k.astype(BF16))
    for li, half in enumerate(GLA_LEVELS):
        x = jnp.exp(-jnp.abs(b - _block_centre(b, half, backward)))
        z = _dot_nt((q * x).astype(BF16), (k * x).astype(BF16))
        scores = scores + masks_ref[1 + li] * z
    edge = b[0:1, :] if backward else b[t_rows - 1:t_rows, :]
    o = _dot(scores.astype(BF16), v)
    o = o + _dot_nt((q * jnp.exp(b)).astype(BF16), st.astype(BF16))
    k_edge = (k * jnp.exp(edge - b)).astype(BF16)
    st_new = st * jnp.exp(edge) + _dot_tn(v, k_edge)
    return o, st_new


def _gla_kernel(*refs, seq_len, has_state, want_state):
    refs = list(refs)
    q_ref, v_ref, lff_ref, lfb_ref, kf_ref, kb_ref = refs[:6]
    pos = 6
    if has_state:
        s0_ref = refs[pos]
        pos += 1
    tril_ref, triu_ref, mf_ref, mb_ref = refs[pos:pos + 4]
    pos += 4
    o_ref = refs[pos]
    pos += 1
    if want_state:
        s_out_ref = refs[pos]
        pos += 1
    st_ref = refs[pos]

    t = GLA_TILE
    n_tiles = seq_len // t
    o_ref[...] = jnp.zeros(o_ref.shape, o_ref.dtype)
    for direction in range(2):
        if has_state:
            st_ref[direction] = s0_ref[direction].T
        else:
            st_ref[direction] = jnp.zeros((REC_DV, REC_DK), F32)

    def body(i, carry):
        for direction, (lf_ref, k_ref, tri_ref, m_ref) in enumerate(
                ((lff_ref, kf_ref, tril_ref, mf_ref), (lfb_ref, kb_ref, triu_ref, mb_ref))):
            tile = i if direction == 0 else n_tiles - 1 - i
            rows = pl.ds(pl.multiple_of(tile * t, t), t)
            o, st_new = _gla_tile(
                q_ref[rows, :].astype(F32), k_ref[rows, :].astype(F32), v_ref[rows, :],
                lf_ref[rows, :], st_ref[direction], tri_ref[...], m_ref, direction == 1)
            st_ref[direction] = st_new
            o_ref[rows, :] += o
        return carry

    lax.fori_loop(0, n_tiles, body, 0)
    if want_state:
        for direction in range(2):
            s_out_ref[direction] = st_ref[direction].T


def _gla(q, v, lf_fw, lf_bw, k_fw, k_bw, s0, consts, *, n_seq, seq_len, want_state):
    has_state = s0 is not None
    width = REC_WIDTH
    seq3 = lambda a: a.reshape(n_seq, seq_len, width)
    head_spec = pl.BlockSpec((None, seq_len, REC_DK), lambda b, h: (b, 0, h))
    state_spec = pl.BlockSpec((None, 2, None, REC_DK, REC_DV), lambda b, h: (b, 0, h, 0, 0))
    in_specs = [head_spec] * 6
    args = [seq3(q), seq3(v), seq3(lf_fw), seq3(lf_bw), seq3(k_fw), seq3(k_bw)]
    if has_state:
        in_specs.append(state_spec)
        args.append(s0)
    for c in consts:
        in_specs.append(pl.BlockSpec(c.shape, lambda b, h, nd=c.ndim: (0,) * nd))
        args.append(c)
    out_shape = [jax.ShapeDtypeStruct((n_seq, seq_len, width), F32)]
    out_specs = [head_spec]
    if want_state:
        out_shape.append(jax.ShapeDtypeStruct((n_seq, 2, N_REC_HEADS, REC_DK, REC_DV), F32))
        out_specs.append(state_spec)
    res = pl.pallas_call(
        functools.partial(_gla_kernel, seq_len=seq_len, has_state=has_state, want_state=want_state),
        grid=(n_seq, N_REC_HEADS),
        in_specs=in_specs,
        out_specs=tuple(out_specs),
        out_shape=tuple(out_shape),
        scratch_shapes=[pltpu.VMEM((2, REC_DV, REC_DK), F32)],
        compiler_params=pltpu.CompilerParams(vmem_limit_bytes=VMEM_LIMIT),
        name="gla_state_in" if has_state else "gla_state_out",
    )(*args)
    o = res[0].reshape(n_seq * seq_len, width)
    return (o, res[1]) if want_state else (o, None)


def _rec_out_kernel(o_ref, sg_ref, x_ref, mod_ref, ng_ref, w_ref, lng_ref, lnb_ref, out_ref):
    d = D_MODEL
    parts = []
    for hd in range(N_REC_HEADS):
        oh = o_ref[:, hd * REC_DV:(hd + 1) * REC_DV]
        ms = jnp.mean(oh * oh, axis=-1, keepdims=True)
        parts.append(oh * lax.rsqrt(ms + NORM_EPS) * ng_ref[...])
    o = jnp.concatenate(parts, axis=1)
    gated = (o * sg_ref[...].astype(F32)).astype(BF16)
    branch = _dot(gated, w_ref[...])
    gate = mod_ref[...][:, 2 * d:]
    y = DEEPNORM_ALPHA * x_ref[...] + gate * branch
    out_ref[...] = _layer_norm(y, lng_ref[...], lnb_ref[...])


def _rec_out(o, sg, x2, mods4, norm_gain, w_out_bf, ln_g, ln_b, *, seq_len, mod_row0, per_seq_mod):
    n_tok = x2.shape[0]
    d = D_MODEL
    tiles_per_seq = seq_len // ROW_TILE

    def mod_map(i):
        row = mod_row0 + (i // tiles_per_seq if per_seq_mod else 0)
        return (1, row, 0, 0)

    row_spec = pl.BlockSpec((ROW_TILE, d), lambda i: (i, 0))
    full = lambda i: (0, 0)
    return pl.pallas_call(
        _rec_out_kernel,
        grid=(n_tok // ROW_TILE,),
        in_specs=[
            row_spec, row_spec, row_spec,
            pl.BlockSpec((None, None, 1, 3 * d), mod_map),
            pl.BlockSpec(norm_gain.shape, full),
            pl.BlockSpec(w_out_bf.shape, full),
            pl.BlockSpec(ln_g.shape, full),
            pl.BlockSpec(ln_b.shape, full),
        ],
        out_specs=row_spec,
        out_shape=jax.ShapeDtypeStruct((n_tok, d), F32),
        compiler_params=pltpu.CompilerParams(vmem_limit_bytes=VMEM_LIMIT),
        name="rec_out",
    )(o, sg, x2, mods4, norm_gain, w_out_bf, ln_g, ln_b)


def _rope_tables(n_tokens):
    n_rows = n_tokens // GRID_W
    rows = jnp.repeat(jnp.arange(n_rows, dtype=F32), GRID_W)
    cols = jnp.tile(jnp.arange(GRID_W, dtype=F32), n_rows)
    inv_freq = 1.0 / (ROPE_THETA ** (jnp.arange(0, AXIS_DIM, 2, dtype=F32) / AXIS_DIM))
    ang_r = rows[:, None] * inv_freq[None, :]
    ang_c = cols[:, None] * inv_freq[None, :]
    ang = jnp.concatenate([ang_r, ang_r, ang_c, ang_c], axis=-1)
    cos, sin = jnp.cos(ang), jnp.sin(ang)
    first = (jnp.arange(HEAD_DIM) % AXIS_DIM) < AXIS_DIM // 2
    sin_a = jnp.where(first[None, :], -sin, 0.0)
    sin_b = jnp.where(first[None, :], 0.0, sin)
    reps = GROUP_LANES // HEAD_DIM
    return tuple(jnp.tile(t, (1, reps)) for t in (cos, sin_a, sin_b))


def _head_mean_matrix():
    idx = jnp.arange(GROUP_LANES) // HEAD_DIM
    return jnp.where(idx[:, None] == idx[None, :], 1.0 / HEAD_DIM, 0.0).astype(BF16)


def _gla_consts():
    t = GLA_TILE
    r = jnp.arange(t)[:, None]
    c = jnp.arange(t)[None, :]
    tril = (c <= r).astype(BF16)
    triu = (c >= r).astype(BF16)
    masks = [(r == c)]
    for half in GLA_LEVELS:
        size = 2 * half
        masks.append((r // size == c // size) & (r % size >= half) & (c % size < half))
    m_fw = jnp.stack(masks).astype(F32)
    m_bw = jnp.swapaxes(m_fw, 1, 2)
    return tril, triu, m_fw, m_bw


def kernel(x_prompt, x_sample, cache_k, cache_v, state_rec, c, c_ctx, ada_w, ada_b, attn_w_in,
           attn_q_gain, attn_k_gain, attn_w_out, rec_w_in, rec_lower_bounds, rec_norm_gain,
           rec_w_out, ln_gain, ln_bias):
    d = D_MODEL
    n_p, len_p, _ = x_prompt.shape
    n_s, len_s, _ = x_sample.shape
    past = cache_k.shape[2]

    cond = jnp.zeros((COND_ROWS, d), F32).at[0].set(c_ctx).at[1:1 + n_s].set(c)
    mods4 = _mods(cond, ada_w, ada_b).reshape(DEPTH, COND_ROWS, 1, 3 * d)
    lb_all = _lower_bounds(rec_lower_bounds)

    xp = x_prompt.reshape(n_p * len_p, d)
    xs = x_sample.reshape(n_s * len_s, d)

    w_in = attn_w_in[0].astype(BF16)
    w_out = attn_w_out[0].astype(BF16)
    reps = GROUP_LANES // HEAD_DIM
    qg = (jnp.tile(attn_q_gain[0], reps) * (1.0 / math.sqrt(HEAD_DIM))).reshape(1, GROUP_LANES)
    kg = jnp.tile(attn_k_gain[0], reps).reshape(1, GROUP_LANES)
    pn = _head_mean_matrix()
    ln_g = ln_gain[0].reshape(1, d)
    ln_b = ln_bias[0].reshape(1, d)

    q_p, k_p, v_p, g_p = _attn_proj(xp, mods4, w_in, pn, qg, kg, None, seq_len=len_p,
                                    mod_row0=0, per_seq_mod=False, kv_dtype=F32)
    q_s, k_s, v_s, g_s = _attn_proj(xs, mods4, w_in, pn, qg, kg, _rope_tables(len_s), seq_len=len_s,
                                    mod_row0=1, per_seq_mod=True, kv_dtype=BF16)
    xp1 = _attn_core(q_p, k_p, v_p, None, g_p, xp, mods4, w_out, ln_g, ln_b, n_seq=n_p,
                     seq_len=len_p, q_tile=len_p, mod_row0=0, per_seq_mod=False)
    cache = (cache_k[:, 0].reshape(n_s, past, KV_WIDTH), cache_v[:, 0].reshape(n_s, past, KV_WIDTH))
    xs1 = _attn_core(q_s, k_s, v_s, cache, g_s, xs, mods4, w_out, ln_g, ln_b, n_seq=n_s,
                     seq_len=len_s, q_tile=ROW_TILE, mod_row0=1, per_seq_mod=True)
    new_cache_k = k_p.reshape(n_p, 1, len_p, N_KV_HEADS, HEAD_DIM)
    new_cache_v = v_p.reshape(n_p, 1, len_p, N_KV_HEADS, HEAD_DIM)

    rw_in = rec_w_in[0].astype(BF16)
    rw_out = rec_w_out[0].astype(BF16)
    lb = lb_all[1]
    ng = rec_norm_gain[0].reshape(1, REC_DV)
    ln_g = ln_gain[1].reshape(1, d)
    ln_b = ln_bias[1].reshape(1, d)
    consts = _gla_consts()

    outs = []
    states = None
    for x1, n_seq, seq_len, row0, per_seq, s0 in (
            (xp1, n_p, len_p, 0, False, None),
            (xs1, n_s, len_s, 1, True, state_rec[:, 0])):
        q, v, g, lf_fw, lf_bw, k_fw, k_bw = _rec_proj(x1, mods4, rw_in, lb, seq_len=seq_len,
                                                       mod_row0=row0, per_seq_mod=per_seq)
        o, st = _gla(q, v, lf_fw, lf_bw, k_fw, k_bw, s0, consts, n_seq=n_seq, seq_len=seq_len,
                     want_state=s0 is None)
        if st is not None:
            states = st
        outs.append(_rec_out(o, g, x1, mods4, ng, rw_out, ln_g, ln_b, seq_len=seq_len,
                             mod_row0=row0, per_seq_mod=per_seq))

    y_prompt = outs[0].reshape(n_p, len_p, d)
    y_sample = outs[1].reshape(n_s, len_s, d)
    new_state_rec = states.reshape(n_p, 1, 2, N_REC_HEADS, REC_DK, REC_DV)
    return (y_prompt, y_sample, new_cache_k, new_cache_v, new_state_rec)
```

```python
import functools
import math

import jax
import jax.numpy as jnp
from jax import lax
from jax.experimental import pallas as pl
from jax.experimental.pallas import tpu as pltpu

F32 = jnp.float32
BF16 = jnp.bfloat16

D_MODEL = 1024
DEPTH = 2
GRID_W = 64
N_HEADS = 16
N_KV_HEADS = 4
HEAD_DIM = 64
AXIS_DIM = HEAD_DIM // 2
ATTN_WIDTH = N_HEADS * HEAD_DIM
KV_WIDTH = N_KV_HEADS * HEAD_DIM
ROPE_THETA = 10000.0
N_REC_HEADS = 8
REC_DK = 128
REC_DV = 128
REC_WIDTH = N_REC_HEADS * REC_DK
NORM_EPS = 1e-6
LN_EPS = 1e-5
DEEPNORM_ALPHA = (2.0 * DEPTH) ** 0.25

COND_ROWS = 8
ROW_TILE = 256
GROUP_LANES = 256
GLA_TILE = 128
GLA_LEVELS = (1, 2, 4, 8, 16, 32, 64)
GLA_HEADS_PER_STEP = 4
LOG2_E = 1.4426950408889634
VMEM_LIMIT = 56 * 1024 * 1024


def _sigmoid(x):
    return 1.0 / (1.0 + jnp.exp(-x))


def _dot(a, b):
    return jnp.dot(a, b, preferred_element_type=F32)


def _dot_nt(a, b):
    return lax.dot_general(a, b, (((1,), (1,)), ((), ())), preferred_element_type=F32)


def _dot_tn(a, b):
    return lax.dot_general(a, b, (((0,), (0,)), ((), ())), preferred_element_type=F32)


def _layer_norm(y, g, b):
    mu = jnp.mean(y, axis=-1, keepdims=True)
    yc = y - mu
    var = jnp.mean(yc * yc, axis=-1, keepdims=True)
    return yc * lax.rsqrt(var + LN_EPS) * g + b


def _mods_kernel(cond_ref, w_ref, b_ref, out_ref):
    c = cond_ref[...]
    s = (c * _sigmoid(c)).astype(BF16)
    out_ref[...] = _dot(s, w_ref[...].astype(BF16)) + b_ref[...]


def _mods(cond, ada_w, ada_b):
    d = D_MODEL
    return pl.pallas_call(
        _mods_kernel,
        grid=(DEPTH, 3),
        in_specs=[
            pl.BlockSpec((COND_ROWS, d), lambda l, j: (0, 0)),
            pl.BlockSpec((None, d, d), lambda l, j: (l, 0, j)),
            pl.BlockSpec((None, 1, d), lambda l, j: (l, 0, j)),
        ],
        out_specs=pl.BlockSpec((None, COND_ROWS, d), lambda l, j: (l, 0, j)),
        out_shape=jax.ShapeDtypeStruct((DEPTH, COND_ROWS, 3 * d), F32),
        compiler_params=pltpu.CompilerParams(vmem_limit_bytes=VMEM_LIMIT),
        name="adaln_mods",
    )(cond, ada_w, ada_b.reshape(DEPTH, 1, 3 * d))


def _lower_bounds_kernel(r_ref, out_ref):
    r = [r_ref[i] for i in range(DEPTH)]
    m = functools.reduce(jnp.maximum, r)
    e = [jnp.exp(x - m) for x in r]
    tot = functools.reduce(lambda a, b: a + b, e)
    soft = [x / tot for x in e]
    acc = soft[0]
    for i in range(DEPTH):
        if i > 0:
            acc = acc + soft[i]
        out_ref[i] = acc - soft[0]


def _lower_bounds(rec_lower_bounds):
    return pl.pallas_call(
        _lower_bounds_kernel,
        out_shape=jax.ShapeDtypeStruct(rec_lower_bounds.shape, F32),
        name="rec_lower_bounds",
    )(rec_lower_bounds)


def _attn_proj_kernel(*refs, rope):
    if rope:
        (x_ref, mod_ref, w_ref, pn_ref, qg_ref, kg_ref, cos_ref, sa_ref, sb_ref,
         q_out, k_out, v_out, g_out) = refs
    else:
        (x_ref, mod_ref, w_ref, pn_ref, qg_ref, kg_ref,
         q_out, k_out, v_out, g_out) = refs
    d = D_MODEL
    mod = mod_ref[...]
    shift, scale = mod[:, :d], mod[:, d:2 * d]
    h = (x_ref[...] * (1.0 + scale) + shift).astype(BF16)
    pn = pn_ref[...]

    def norm_rope(u, gain):
        ms = _dot((u * u).astype(BF16), pn)
        y = u * lax.rsqrt(ms + NORM_EPS) * gain
        if rope:
            y = (y * cos_ref[...]
                 + pltpu.roll(y, GROUP_LANES - AXIS_DIM // 2, 1) * sa_ref[...]
                 + pltpu.roll(y, AXIS_DIM // 2, 1) * sb_ref[...])
        return y

    for j in range(N_KV_HEADS):
        lo = j * GROUP_LANES
        u = _dot(h, w_ref[:, lo:lo + GROUP_LANES])
        q_out[:, lo:lo + GROUP_LANES] = norm_rope(u, qg_ref[...]).astype(q_out.dtype)
    u = _dot(h, w_ref[:, ATTN_WIDTH:ATTN_WIDTH + KV_WIDTH])
    k_out[...] = norm_rope(u, kg_ref[...]).astype(k_out.dtype)
    u = _dot(h, w_ref[:, ATTN_WIDTH + KV_WIDTH:ATTN_WIDTH + 2 * KV_WIDTH])
    v_out[...] = u.astype(v_out.dtype)
    for j in range(N_KV_HEADS):
        lo = ATTN_WIDTH + 2 * KV_WIDTH + j * GROUP_LANES
        u = _dot(h, w_ref[:, lo:lo + GROUP_LANES])
        g_out[:, j * GROUP_LANES:(j + 1) * GROUP_LANES] = (u * _sigmoid(u)).astype(g_out.dtype)


def _attn_proj(x2, mods4, w_bf, pn, qg, kg, rope_tabs, *, seq_len, mod_row0, per_seq_mod, kv_dtype):
    n_tok = x2.shape[0]
    d = D_MODEL
    tiles_per_seq = seq_len // ROW_TILE
    rope = rope_tabs is not None

    def mod_map(i):
        row = mod_row0 + (i // tiles_per_seq if per_seq_mod else 0)
        return (0, row, 0, 0)

    full = lambda i: (0, 0)
    in_specs = [
        pl.BlockSpec((ROW_TILE, d), lambda i: (i, 0)),
        pl.BlockSpec((None, None, 1, 3 * d), mod_map),
        pl.BlockSpec(w_bf.shape, full),
        pl.BlockSpec(pn.shape, full),
        pl.BlockSpec(qg.shape, full),
        pl.BlockSpec(kg.shape, full),
    ]
    args = [x2, mods4, w_bf, pn, qg, kg]
    if rope:
        for t in rope_tabs:
            in_specs.append(pl.BlockSpec((ROW_TILE, GROUP_LANES), lambda i: (i % tiles_per_seq, 0)))
            args.append(t)
    out_shape = (
        jax.ShapeDtypeStruct((n_tok, ATTN_WIDTH), BF16),
        jax.ShapeDtypeStruct((n_tok, KV_WIDTH), kv_dtype),
        jax.ShapeDtypeStruct((n_tok, KV_WIDTH), kv_dtype),
        jax.ShapeDtypeStruct((n_tok, ATTN_WIDTH), BF16),
    )
    out_specs = (
        pl.BlockSpec((ROW_TILE, ATTN_WIDTH), lambda i: (i, 0)),
        pl.BlockSpec((ROW_TILE, KV_WIDTH), lambda i: (i, 0)),
        pl.BlockSpec((ROW_TILE, KV_WIDTH), lambda i: (i, 0)),
        pl.BlockSpec((ROW_TILE, ATTN_WIDTH), lambda i: (i, 0)),
    )
    return pl.pallas_call(
        functools.partial(_attn_proj_kernel, rope=rope),
        grid=(n_tok // ROW_TILE,),
        in_specs=in_specs,
        out_specs=out_specs,
        out_shape=out_shape,
        compiler_params=pltpu.CompilerParams(vmem_limit_bytes=VMEM_LIMIT),
        name="attn_proj_rope" if rope else "attn_proj",
    )(*args)


def _attn_core_kernel(*refs, has_cache):
    if has_cache:
        (q_ref, k_ref, v_ref, kc_ref, vc_ref, sg_ref, x_ref, mod_ref, w_ref, lng_ref, lnb_ref,
         out_ref) = refs
    else:
        (q_ref, k_ref, v_ref, sg_ref, x_ref, mod_ref, w_ref, lng_ref, lnb_ref, out_ref) = refs
    d = D_MODEL
    q = q_ref[...]
    k = k_ref[...].astype(BF16)
    v = v_ref[...].astype(BF16)
    if has_cache:
        kc = kc_ref[...].astype(BF16)
        vc = vc_ref[...].astype(BF16)
    heads_per_kv = N_HEADS // N_KV_HEADS
    outs = []
    for j in range(N_KV_HEADS):
        sl = slice(j * HEAD_DIM, (j + 1) * HEAD_DIM)
        kj, vj = k[:, sl], v[:, sl]
        if has_cache:
            kcj, vcj = kc[:, sl], vc[:, sl]
        for hh in range(heads_per_kv):
            hd = j * heads_per_kv + hh
            qh = q[:, hd * HEAD_DIM:(hd + 1) * HEAD_DIM]
            s = _dot_nt(qh, kj)
            m = jnp.max(s, axis=-1, keepdims=True)
            if has_cache:
                s_c = _dot_nt(qh, kcj)
                m = jnp.maximum(m, jnp.max(s_c, axis=-1, keepdims=True))
            p = jnp.exp(s - m)
            l = jnp.sum(p, axis=-1, keepdims=True)
            o = _dot(p.astype(BF16), vj)
            if has_cache:
                p_c = jnp.exp(s_c - m)
                l = l + jnp.sum(p_c, axis=-1, keepdims=True)
                o = o + _dot(p_c.astype(BF16), vcj)
            outs.append(o * (1.0 / l))
    o = jnp.concatenate(outs, axis=1)
    gated = (o * sg_ref[...].astype(F32)).astype(BF16)
    branch = _dot(gated, w_ref[...])
    gate = mod_ref[...][:, 2 * d:]
    y = DEEPNORM_ALPHA * x_ref[...] + gate * branch
    out_ref[...] = _layer_norm(y, lng_ref[...], lnb_ref[...])


def _attn_core(q, k, v, cache, sg, x2, mods4, w_out_bf, ln_g, ln_b, *, n_seq, seq_len, q_tile,
               mod_row0, per_seq_mod):
    d = D_MODEL
    tiles = seq_len // q_tile
    has_cache = cache is not None
    q3 = q.reshape(n_seq, seq_len, ATTN_WIDTH)
    k3 = k.reshape(n_seq, seq_len, KV_WIDTH)
    v3 = v.reshape(n_seq, seq_len, KV_WIDTH)
    sg3 = sg.reshape(n_seq, seq_len, ATTN_WIDTH)
    x3 = x2.reshape(n_seq, seq_len, d)

    def mod_map(b, i):
        return (0, mod_row0 + (b if per_seq_mod else 0), 0, 0)

    tile_spec = lambda width: pl.BlockSpec((None, q_tile, width), lambda b, i: (b, i, 0))
    seq_spec = lambda length, width: pl.BlockSpec((None, length, width), lambda b, i: (b, 0, 0))
    full2 = lambda b, i: (0, 0)
    in_specs = [tile_spec(ATTN_WIDTH), seq_spec(seq_len, KV_WIDTH), seq_spec(seq_len, KV_WIDTH)]
    args = [q3, k3, v3]
    if has_cache:
        ck, cv = cache
        in_specs += [seq_spec(ck.shape[1], KV_WIDTH), seq_spec(cv.shape[1], KV_WIDTH)]
        args += [ck, cv]
    in_specs += [
        tile_spec(ATTN_WIDTH),
        tile_spec(d),
        pl.BlockSpec((None, None, 1, 3 * d), mod_map),
        pl.BlockSpec(w_out_bf.shape, full2),
        pl.BlockSpec(ln_g.shape, full2),
        pl.BlockSpec(ln_b.shape, full2),
    ]
    args += [sg3, x3, mods4, w_out_bf, ln_g, ln_b]
    out = pl.pallas_call(
        functools.partial(_attn_core_kernel, has_cache=has_cache),
        grid=(n_seq, tiles),
        in_specs=in_specs,
        out_specs=tile_spec(d),
        out_shape=jax.ShapeDtypeStruct((n_seq, seq_len, d), F32),
        compiler_params=pltpu.CompilerParams(vmem_limit_bytes=VMEM_LIMIT),
        name="attn_core_cache" if has_cache else "attn_core",
    )(*args)
    return out.reshape(n_seq * seq_len, d)


def _rec_proj_kernel(x_ref, mod_ref, w_ref, lb_ref, q_out, v_out, g_out, lf_fw, lf_bw, k_fw, k_bw):
    d = D_MODEL
    mod = mod_ref[...]
    shift, scale = mod[:, :d], mod[:, d:2 * d]
    h = (x_ref[...] * (1.0 + scale) + shift).astype(BF16)
    u = _dot(h, w_ref[:, 0:REC_WIDTH])
    q_out[...] = (u * _sigmoid(u)).astype(q_out.dtype)
    for direction, (lf_out, k_out) in enumerate(((lf_fw, k_fw), (lf_bw, k_bw))):
        z = _dot(h, w_ref[:, (1 + direction) * REC_WIDTH:(2 + direction) * REC_WIDTH])
        lb = lb_ref[direction:direction + 1, :]
        sig = _sigmoid(z)
        forget = lb + (1.0 - lb) * sig
        lf_out[...] = jnp.log(forget)
        k_out[...] = ((1.0 - lb) * (1.0 - sig)).astype(k_out.dtype)
    u = _dot(h, w_ref[:, 3 * REC_WIDTH:4 * REC_WIDTH])
    v_out[...] = u.astype(v_out.dtype)
    u = _dot(h, w_ref[:, 4 * REC_WIDTH:5 * REC_WIDTH])
    g_out[...] = (u * _sigmoid(u)).astype(g_out.dtype)


def _rec_proj(x2, mods4, w_bf, lb, *, seq_len, mod_row0, per_seq_mod):
    n_tok = x2.shape[0]
    d = D_MODEL
    tiles_per_seq = seq_len // ROW_TILE

    def mod_map(i):
        row = mod_row0 + (i // tiles_per_seq if per_seq_mod else 0)
        return (1, row, 0, 0)

    row_spec = pl.BlockSpec((ROW_TILE, REC_WIDTH), lambda i: (i, 0))
    full = lambda i: (0, 0)
    out_dtypes = (BF16, BF16, BF16, F32, F32, BF16, BF16)
    return pl.pallas_call(
        _rec_proj_kernel,
        grid=(n_tok // ROW_TILE,),
        in_specs=[
            pl.BlockSpec((ROW_TILE, d), lambda i: (i, 0)),
            pl.BlockSpec((None, None, 1, 3 * d), mod_map),
            pl.BlockSpec(w_bf.shape, full),
            pl.BlockSpec(lb.shape, full),
        ],
        out_specs=tuple(row_spec for _ in out_dtypes),
        out_shape=tuple(jax.ShapeDtypeStruct((n_tok, REC_WIDTH), t) for t in out_dtypes),
        compiler_params=pltpu.CompilerParams(vmem_limit_bytes=VMEM_LIMIT),
        name="rec_proj",
    )(x2, mods4, w_bf, lb)


def _block_centre(b, half, backward):
    t_rows, lanes = b.shape
    size = 2 * half
    c = half if backward else half - 1
    if size >= 8:
        b3 = b.reshape(t_rows // size, size, lanes)
        return jnp.broadcast_to(b3[:, c:c + 1, :], b3.shape).reshape(t_rows, lanes)
    pos = lax.broadcasted_iota(jnp.int32, b.shape, 0) % size
    out = b
    for r in range(size):
        if r == c:
            continue
        shifted = pltpu.roll(b, (r - c) % t_rows, 0)
        out = jnp.where(pos == r, shifted, out)
    return out


def _neg_abs(x):
    bits = lax.bitcast_convert_type(x, jnp.int32) | jnp.int32(-2 ** 31)
    return lax.bitcast_convert_type(bits, F32)


def _block_diag(a, b):
    za = jnp.zeros(a.shape, a.dtype)
    return jnp.concatenate(
        [jnp.concatenate([a, za], axis=1), jnp.concatenate([za, b], axis=1)], axis=0)


def _gla_pair_tile(q, k, v, lf, st_ref, tri2, masks_ref, backward):
    t_rows = q.shape[0]
    dk = REC_DK
    hi = lf.astype(BF16)
    lo = (lf - hi.astype(F32)).astype(BF16)
    b = _dot(tri2, jnp.concatenate([hi, lo], axis=0)) * LOG2_E

    def pair_scores(a, c):
        return _dot_nt(a, _block_diag(c[:, :dk], c[:, dk:]))

    scores = masks_ref[0] * pair_scores(q, k)
    for li, half in enumerate(GLA_LEVELS):
        x = jnp.exp2(_neg_abs(b - _block_centre(b, half, backward))).astype(BF16)
        scores = scores + masks_ref[1 + li] * pair_scores(q * x, k * x)
    edge = b[0:1, :] if backward else b[t_rows - 1:t_rows, :]
    o = _dot(scores.astype(BF16), _block_diag(v[:, :dk], v[:, dk:]))
    st_a, st_b = st_ref[0], st_ref[1]
    q_in = q * jnp.exp2(b).astype(BF16)
    o = o + _dot_nt(q_in, _block_diag(st_a.astype(BF16), st_b.astype(BF16)))
    k_edge = k * jnp.exp2(edge - b).astype(BF16)
    carry = jnp.exp2(edge)
    st_ref[0] = st_a * carry[:, :dk] + _dot_tn(v[:, :dk], k_edge[:, :dk])
    st_ref[1] = st_b * carry[:, dk:] + _dot_tn(v[:, dk:], k_edge[:, dk:])
    return o


def _gla_kernel(*refs, seq_len, has_state, want_state):
    refs = list(refs)
    q_ref, v_ref, lff_ref, lfb_ref, kf_ref, kb_ref = refs[:6]
    pos = 6
    if has_state:
        s0_ref = refs[pos]
        pos += 1
    tril_ref, triu_ref, mf_ref, mb_ref = refs[pos:pos + 4]
    pos += 4
    o_ref = refs[pos]
    pos += 1
    if want_state:
        s_out_ref = refs[pos]
        pos += 1
    st_ref = refs[pos]

    t = GLA_TILE
    n_tiles = seq_len // t
    pair_lanes = 2 * REC_DK
    o_ref[...] = jnp.zeros(o_ref.shape, o_ref.dtype)
    for direction in range(2):
        for hd in range(GLA_HEADS_PER_STEP):
            if has_state:
                st_ref[direction, hd] = s0_ref[direction, hd].T
            else:
                st_ref[direction, hd] = jnp.zeros((REC_DV, REC_DK), F32)

    def body(i, carry):
        for direction, (lf_ref, k_ref, tri_ref, m_ref) in enumerate(
                ((lff_ref, kf_ref, tril_ref, mf_ref), (lfb_ref, kb_ref, triu_ref, mb_ref))):
            tile = i if direction == 0 else n_tiles - 1 - i
            rows = pl.ds(pl.multiple_of(tile * t, t), t)
            for pair in range(GLA_HEADS_PER_STEP // 2):
                lanes = slice(pair * pair_lanes, (pair + 1) * pair_lanes)
                o = _gla_pair_tile(
                    q_ref[rows, lanes], k_ref[rows, lanes], v_ref[rows, lanes], lf_ref[rows, lanes],
                    st_ref.at[direction, pl.ds(2 * pair, 2)], tri_ref[...], m_ref, direction == 1)
                o_ref[rows, lanes] += o
        return carry

    lax.fori_loop(0, n_tiles, body, 0)
    if want_state:
        for direction in range(2):
            for hd in range(GLA_HEADS_PER_STEP):
                s_out_ref[direction, hd] = st_ref[direction, hd].T


def _gla(q, v, lf_fw, lf_bw, k_fw, k_bw, s0, consts, *, n_seq, seq_len, want_state):
    has_state = s0 is not None
    width = REC_WIDTH
    hps = GLA_HEADS_PER_STEP
    seq3 = lambda a: a.reshape(n_seq, seq_len, width)
    head_spec = pl.BlockSpec((None, seq_len, hps * REC_DK), lambda b, h: (b, 0, h))
    state_spec = pl.BlockSpec((None, 2, hps, REC_DK, REC_DV), lambda b, h: (b, 0, h, 0, 0))
    in_specs = [head_spec] * 6
    args = [seq3(q), seq3(v), seq3(lf_fw), seq3(lf_bw), seq3(k_fw), seq3(k_bw)]
    if has_state:
        in_specs.append(state_spec)
        args.append(s0)
    for c in consts:
        in_specs.append(pl.BlockSpec(c.shape, lambda b, h, nd=c.ndim: (0,) * nd))
        args.append(c)
    out_shape = [jax.ShapeDtypeStruct((n_seq, seq_len, width), F32)]
    out_specs = [head_spec]
    if want_state:
        out_shape.append(jax.ShapeDtypeStruct((n_seq, 2, N_REC_HEADS, REC_DK, REC_DV), F32))
        out_specs.append(state_spec)
    res = pl.pallas_call(
        functools.partial(_gla_kernel, seq_len=seq_len, has_state=has_state, want_state=want_state),
        grid=(n_seq, N_REC_HEADS // hps),
        in_specs=in_specs,
        out_specs=tuple(out_specs),
        out_shape=tuple(out_shape),
        scratch_shapes=[pltpu.VMEM((2, hps, REC_DV, REC_DK), F32)],
        compiler_params=pltpu.CompilerParams(vmem_limit_bytes=VMEM_LIMIT),
        name="gla_state_in" if has_state else "gla_state_out",
    )(*args)
    o = res[0].reshape(n_seq * seq_len, width)
    return (o, res[1]) if want_state else (o, None)


def _rec_out_kernel(o_ref, sg_ref, x_ref, mod_ref, ng_ref, w_ref, lng_ref, lnb_ref, out_ref):
    d = D_MODEL
    parts = []
    for hd in range(N_REC_HEADS):
        oh = o_ref[:, hd * REC_DV:(hd + 1) * REC_DV]
        ms = jnp.mean(oh * oh, axis=-1, keepdims=True)
        parts.append(oh * lax.rsqrt(ms + NORM_EPS) * ng_ref[...])
    o = jnp.concatenate(parts, axis=1)
    gated = (o * sg_ref[...].astype(F32)).astype(BF16)
    branch = _dot(gated, w_ref[...])
    gate = mod_ref[...][:, 2 * d:]
    y = DEEPNORM_ALPHA * x_ref[...] + gate * branch
    out_ref[...] = _layer_norm(y, lng_ref[...], lnb_ref[...])


def _rec_out(o, sg, x2, mods4, norm_gain, w_out_bf, ln_g, ln_b, *, seq_len, mod_row0, per_seq_mod):
    n_tok = x2.shape[0]
    d = D_MODEL
    tiles_per_seq = seq_len // ROW_TILE

    def mod_map(i):
        row = mod_row0 + (i // tiles_per_seq if per_seq_mod else 0)
        return (1, row, 0, 0)

    row_spec = pl.BlockSpec((ROW_TILE, d), lambda i: (i, 0))
    full = lambda i: (0, 0)
    return pl.pallas_call(
        _rec_out_kernel,
        grid=(n_tok // ROW_TILE,),
        in_specs=[
            row_spec, row_spec, row_spec,
            pl.BlockSpec((None, None, 1, 3 * d), mod_map),
            pl.BlockSpec(norm_gain.shape, full),
            pl.BlockSpec(w_out_bf.shape, full),
            pl.BlockSpec(ln_g.shape, full),
            pl.BlockSpec(ln_b.shape, full),
        ],
        out_specs=row_spec,
        out_shape=jax.ShapeDtypeStruct((n_tok, d), F32),
        compiler_params=pltpu.CompilerParams(vmem_limit_bytes=VMEM_LIMIT),
        name="rec_out",
    )(o, sg, x2, mods4, norm_gain, w_out_bf, ln_g, ln_b)


def _rope_tables(n_tokens):
    n_rows = n_tokens // GRID_W
    rows = jnp.repeat(jnp.arange(n_rows, dtype=F32), GRID_W)
    cols = jnp.tile(jnp.arange(GRID_W, dtype=F32), n_rows)
    inv_freq = 1.0 / (ROPE_THETA ** (jnp.arange(0, AXIS_DIM, 2, dtype=F32) / AXIS_DIM))
    ang_r = rows[:, None] * inv_freq[None, :]
    ang_c = cols[:, None] * inv_freq[None, :]
    ang = jnp.concatenate([ang_r, ang_r, ang_c, ang_c], axis=-1)
    cos, sin = jnp.cos(ang), jnp.sin(ang)
    first = (jnp.arange(HEAD_DIM) % AXIS_DIM) < AXIS_DIM // 2
    sin_a = jnp.where(first[None, :], -sin, 0.0)
    sin_b = jnp.where(first[None, :], 0.0, sin)
    reps = GROUP_LANES // HEAD_DIM
    return tuple(jnp.tile(t, (1, reps)) for t in (cos, sin_a, sin_b))


def _head_mean_matrix():
    idx = jnp.arange(GROUP_LANES) // HEAD_DIM
    return jnp.where(idx[:, None] == idx[None, :], 1.0 / HEAD_DIM, 0.0).astype(BF16)


def _gla_consts():
    t = GLA_TILE
    r = jnp.arange(t)[:, None]
    c = jnp.arange(t)[None, :]
    tril = (c <= r).astype(BF16)
    triu = (c >= r).astype(BF16)
    masks = [(r == c)]
    for half in GLA_LEVELS:
        size = 2 * half
        masks.append((r // size == c // size) & (r % size >= half) & (c % size < half))
    m_fw = jnp.stack(masks).astype(F32)
    m_bw = jnp.swapaxes(m_fw, 1, 2)
    twice = lambda a: jnp.concatenate([a, a], axis=-1)
    return twice(tril), twice(triu), twice(m_fw), twice(m_bw)


def kernel(x_prompt, x_sample, cache_k, cache_v, state_rec, c, c_ctx, ada_w, ada_b, attn_w_in,
           attn_q_gain, attn_k_gain, attn_w_out, rec_w_in, rec_lower_bounds, rec_norm_gain,
           rec_w_out, ln_gain, ln_bias):
    d = D_MODEL
    n_p, len_p, _ = x_prompt.shape
    n_s, len_s, _ = x_sample.shape
    past = cache_k.shape[2]

    cond = jnp.zeros((COND_ROWS, d), F32).at[0].set(c_ctx).at[1:1 + n_s].set(c)
    mods4 = _mods(cond, ada_w, ada_b).reshape(DEPTH, COND_ROWS, 1, 3 * d)
    lb_all = _lower_bounds(rec_lower_bounds)

    xp = x_prompt.reshape(n_p * len_p, d)
    xs = x_sample.reshape(n_s * len_s, d)

    w_in = attn_w_in[0].astype(BF16)
    w_out = attn_w_out[0].astype(BF16)
    reps = GROUP_LANES // HEAD_DIM
    qg = (jnp.tile(attn_q_gain[0], reps) * (1.0 / math.sqrt(HEAD_DIM))).reshape(1, GROUP_LANES)
    kg = jnp.tile(attn_k_gain[0], reps).reshape(1, GROUP_LANES)
    pn = _head_mean_matrix()
    ln_g = ln_gain[0].reshape(1, d)
    ln_b = ln_bias[0].reshape(1, d)

    q_p, k_p, v_p, g_p = _attn_proj(xp, mods4, w_in, pn, qg, kg, None, seq_len=len_p,
                                    mod_row0=0, per_seq_mod=False, kv_dtype=F32)
    q_s, k_s, v_s, g_s = _attn_proj(xs, mods4, w_in, pn, qg, kg, _rope_tables(len_s), seq_len=len_s,
                                    mod_row0=1, per_seq_mod=True, kv_dtype=BF16)
    xp1 = _attn_core(q_p, k_p, v_p, None, g_p, xp, mods4, w_out, ln_g, ln_b, n_seq=n_p,
                     seq_len=len_p, q_tile=len_p, mod_row0=0, per_seq_mod=False)
    cache = (cache_k[:, 0].reshape(n_s, past, KV_WIDTH), cache_v[:, 0].reshape(n_s, past, KV_WIDTH))
    xs1 = _attn_core(q_s, k_s, v_s, cache, g_s, xs, mods4, w_out, ln_g, ln_b, n_seq=n_s,
                     seq_len=len_s, q_tile=ROW_TILE, mod_row0=1, per_seq_mod=True)
    new_cache_k = k_p.reshape(n_p, 1, len_p, N_KV_HEADS, HEAD_DIM)
    new_cache_v = v_p.reshape(n_p, 1, len_p, N_KV_HEADS, HEAD_DIM)

    rw_in = rec_w_in[0].astype(BF16)
    rw_out = rec_w_out[0].astype(BF16)
    lb = lb_all[1]
    ng = rec_norm_gain[0].reshape(1, REC_DV)
    ln_g = ln_gain[1].reshape(1, d)
    ln_b = ln_bias[1].reshape(1, d)
    consts = _gla_consts()

    outs = []
    states = None
    for x1, n_seq, seq_len, row0, per_seq, s0 in (
            (xp1, n_p, len_p, 0, False, None),
            (xs1, n_s, len_s, 1, True, state_rec[:, 0])):
        q, v, g, lf_fw, lf_bw, k_fw, k_bw = _rec_proj(x1, mods4, rw_in, lb, seq_len=seq_len,
                                                       mod_row0=row0, per_seq_mod=per_seq)
        o, st = _gla(q, v, lf_fw, lf_bw, k_fw, k_bw, s0, consts, n_seq=n_seq, seq_len=seq_len,
                     want_state=s0 is None)
        if st is not None:
            states = st
        outs.append(_rec_out(o, g, x1, mods4, ng, rw_out, ln_g, ln_b, seq_len=seq_len,
                             mod_row0=row0, per_seq_mod=per_seq))

    y_prompt = outs[0].reshape(n_p, len_p, d)
    y_sample = outs[1].reshape(n_s, len_s, d)
    new_state_rec = states.reshape(n_p, 1, 2, N_REC_HEADS, REC_DK, REC_DV)
    return (y_prompt, y_sample, new_cache_k, new_cache_v, new_state_rec)
```

```python
import functools
import math

import jax
import jax.numpy as jnp
from jax import lax
from jax.experimental import pallas as pl
from jax.experimental.pallas import tpu as pltpu

F32 = jnp.float32
BF16 = jnp.bfloat16

D_MODEL = 1024
DEPTH = 2
GRID_W = 64
N_HEADS = 16
N_KV_HEADS = 4
HEAD_DIM = 64
AXIS_DIM = HEAD_DIM // 2
ATTN_WIDTH = N_HEADS * HEAD_DIM
KV_WIDTH = N_KV_HEADS * HEAD_DIM
ROPE_THETA = 10000.0
N_REC_HEADS = 8
REC_DK = 128
REC_DV = 128
REC_WIDTH = N_REC_HEADS * REC_DK
NORM_EPS = 1e-6
LN_EPS = 1e-5
DEEPNORM_ALPHA = (2.0 * DEPTH) ** 0.25

SUBLANES = 8
COND_ROWS = SUBLANES
ROW_TILE = 256
GROUP_LANES = 256
GLA_TILE = 128
GLA_LEVELS = (1, 2, 4, 8, 16, 32, 64)
GLA_HEADS_PER_STEP = 4
LOG2_E = 1.4426950408889634
VMEM_LIMIT = 56 * 1024 * 1024


def _sigmoid(x):
    return 1.0 / (1.0 + jnp.exp(-x))


def _dot(a, b):
    return jnp.dot(a, b, preferred_element_type=F32)


def _dot_nt(a, b):
    return lax.dot_general(a, b, (((1,), (1,)), ((), ())), preferred_element_type=F32)


def _dot_tn(a, b):
    return lax.dot_general(a, b, (((0,), (0,)), ((), ())), preferred_element_type=F32)


def _layer_norm(y, g, b):
    mu = jnp.mean(y, axis=-1, keepdims=True)
    yc = y - mu
    var = jnp.mean(yc * yc, axis=-1, keepdims=True)
    return yc * lax.rsqrt(var + LN_EPS) * g + b


def _mods_kernel(cond_ref, w_ref, b_ref, out_ref):
    c = cond_ref[...]
    s = (c * _sigmoid(c)).astype(BF16)
    out_ref[...] = _dot(s, w_ref[...].astype(BF16)) + b_ref[...]


def _mods(cond, ada_w, ada_b):
    d = D_MODEL
    return pl.pallas_call(
        _mods_kernel,
        grid=(DEPTH, 3),
        in_specs=[
            pl.BlockSpec((COND_ROWS, d), lambda l, j: (0, 0)),
            pl.BlockSpec((None, d, d), lambda l, j: (l, 0, j)),
            pl.BlockSpec((None, 1, d), lambda l, j: (l, 0, j)),
        ],
        out_specs=pl.BlockSpec((None, COND_ROWS, d), lambda l, j: (l, 0, j)),
        out_shape=jax.ShapeDtypeStruct((DEPTH, COND_ROWS, 3 * d), F32),
        compiler_params=pltpu.CompilerParams(vmem_limit_bytes=VMEM_LIMIT),
        name="adaln_mods",
    )(cond, ada_w, ada_b.reshape(DEPTH, 1, 3 * d))


def _lower_bounds_kernel(r_ref, out_ref):
    r = [r_ref[i] for i in range(DEPTH)]
    m = functools.reduce(jnp.maximum, r)
    e = [jnp.exp(x - m) for x in r]
    tot = functools.reduce(lambda a, b: a + b, e)
    soft = [x / tot for x in e]
    acc = soft[0]
    for i in range(DEPTH):
        if i > 0:
            acc = acc + soft[i]
        out_ref[i] = acc - soft[0]


def _lower_bounds(rec_lower_bounds):
    return pl.pallas_call(
        _lower_bounds_kernel,
        out_shape=jax.ShapeDtypeStruct(rec_lower_bounds.shape, F32),
        name="rec_lower_bounds",
    )(rec_lower_bounds)


def _attn_proj_kernel(*refs, rope):
    if rope:
        (x_ref, mod_ref, w_ref, pn_ref, qg_ref, kg_ref, cos_ref, sa_ref, sb_ref,
         q_out, k_out, v_out, g_out) = refs
    else:
        (x_ref, mod_ref, w_ref, pn_ref, qg_ref, kg_ref,
         q_out, k_out, v_out, g_out) = refs
    d = D_MODEL
    mod = mod_ref[...]
    shift, scale = mod[:, :d], mod[:, d:2 * d]
    h = (x_ref[...] * (1.0 + scale) + shift).astype(BF16)
    pn = pn_ref[...]

    def norm_rope(u, gain):
        ms = _dot((u * u).astype(BF16), pn)
        y = u * lax.rsqrt(ms + NORM_EPS) * gain
        if rope:
            y = (y * cos_ref[...]
                 + pltpu.roll(y, GROUP_LANES - AXIS_DIM // 2, 1) * sa_ref[...]
                 + pltpu.roll(y, AXIS_DIM // 2, 1) * sb_ref[...])
        return y

    for j in range(N_KV_HEADS):
        lo = j * GROUP_LANES
        u = _dot(h, w_ref[:, lo:lo + GROUP_LANES])
        q_out[:, lo:lo + GROUP_LANES] = norm_rope(u, qg_ref[...]).astype(q_out.dtype)
    u = _dot(h, w_ref[:, ATTN_WIDTH:ATTN_WIDTH + KV_WIDTH])
    k_out[...] = norm_rope(u, kg_ref[...]).astype(k_out.dtype)
    u = _dot(h, w_ref[:, ATTN_WIDTH + KV_WIDTH:ATTN_WIDTH + 2 * KV_WIDTH])
    v_out[...] = u.astype(v_out.dtype)
    for j in range(N_KV_HEADS):
        lo = ATTN_WIDTH + 2 * KV_WIDTH + j * GROUP_LANES
        u = _dot(h, w_ref[:, lo:lo + GROUP_LANES])
        g_out[:, j * GROUP_LANES:(j + 1) * GROUP_LANES] = (u * _sigmoid(u)).astype(g_out.dtype)


def _attn_proj(x2, mods4, w_bf, pn, qg, kg, rope_tabs, *, seq_len, mod_row0, per_seq_mod, kv_dtype):
    n_tok = x2.shape[0]
    d = D_MODEL
    tiles_per_seq = seq_len // ROW_TILE
    rope = rope_tabs is not None

    def mod_map(i):
        row = mod_row0 + (i // tiles_per_seq if per_seq_mod else 0)
        return (0, row, 0, 0)

    full = lambda i: (0, 0)
    in_specs = [
        pl.BlockSpec((ROW_TILE, d), lambda i: (i, 0)),
        pl.BlockSpec((None, None, 1, 3 * d), mod_map),
        pl.BlockSpec(w_bf.shape, full),
        pl.BlockSpec(pn.shape, full),
        pl.BlockSpec(qg.shape, full),
        pl.BlockSpec(kg.shape, full),
    ]
    args = [x2, mods4, w_bf, pn, qg, kg]
    if rope:
        for t in rope_tabs:
            in_specs.append(pl.BlockSpec((ROW_TILE, GROUP_LANES), lambda i: (i % tiles_per_seq, 0)))
            args.append(t)
    out_shape = (
        jax.ShapeDtypeStruct((n_tok, ATTN_WIDTH), BF16),
        jax.ShapeDtypeStruct((n_tok, KV_WIDTH), kv_dtype),
        jax.ShapeDtypeStruct((n_tok, KV_WIDTH), kv_dtype),
        jax.ShapeDtypeStruct((n_tok, ATTN_WIDTH), BF16),
    )
    out_specs = (
        pl.BlockSpec((ROW_TILE, ATTN_WIDTH), lambda i: (i, 0)),
        pl.BlockSpec((ROW_TILE, KV_WIDTH), lambda i: (i, 0)),
        pl.BlockSpec((ROW_TILE, KV_WIDTH), lambda i: (i, 0)),
        pl.BlockSpec((ROW_TILE, ATTN_WIDTH), lambda i: (i, 0)),
    )
    return pl.pallas_call(
        functools.partial(_attn_proj_kernel, rope=rope),
        grid=(n_tok // ROW_TILE,),
        in_specs=in_specs,
        out_specs=out_specs,
        out_shape=out_shape,
        compiler_params=pltpu.CompilerParams(vmem_limit_bytes=VMEM_LIMIT),
        name="attn_proj_rope" if rope else "attn_proj",
    )(*args)


def _attn_core_kernel(*refs, has_cache):
    if has_cache:
        (q_ref, k_ref, v_ref, kc_ref, vc_ref, sg_ref, x_ref, mod_ref, w_ref, lng_ref, lnb_ref,
         out_ref) = refs
    else:
        (q_ref, k_ref, v_ref, sg_ref, x_ref, mod_ref, w_ref, lng_ref, lnb_ref, out_ref) = refs
    d = D_MODEL
    q = q_ref[...]
    k = k_ref[...].astype(BF16)
    v = v_ref[...].astype(BF16)
    if has_cache:
        kc = kc_ref[...].astype(BF16)
        vc = vc_ref[...].astype(BF16)
    heads_per_kv = N_HEADS // N_KV_HEADS
    outs = []
    for j in range(N_KV_HEADS):
        sl = slice(j * HEAD_DIM, (j + 1) * HEAD_DIM)
        kj, vj = k[:, sl], v[:, sl]
        if has_cache:
            kcj, vcj = kc[:, sl], vc[:, sl]
        for hh in range(heads_per_kv):
            hd = j * heads_per_kv + hh
            qh = q[:, hd * HEAD_DIM:(hd + 1) * HEAD_DIM]
            s = _dot_nt(qh, kj)
            m = jnp.max(s, axis=-1, keepdims=True)
            if has_cache:
                s_c = _dot_nt(qh, kcj)
                m = jnp.maximum(m, jnp.max(s_c, axis=-1, keepdims=True))
            p = jnp.exp(s - m)
            l = jnp.sum(p, axis=-1, keepdims=True)
            o = _dot(p.astype(BF16), vj)
            if has_cache:
                p_c = jnp.exp(s_c - m)
                l = l + jnp.sum(p_c, axis=-1, keepdims=True)
                o = o + _dot(p_c.astype(BF16), vcj)
            outs.append(o * (1.0 / l))
    o = jnp.concatenate(outs, axis=1)
    gated = (o * sg_ref[...].astype(F32)).astype(BF16)
    branch = _dot(gated, w_ref[...])
    gate = mod_ref[...][:, 2 * d:]
    y = DEEPNORM_ALPHA * x_ref[...] + gate * branch
    out_ref[...] = _layer_norm(y, lng_ref[...], lnb_ref[...])


def _attn_core(q, k, v, cache, sg, x2, mods4, w_out_bf, ln_g, ln_b, *, n_seq, seq_len, q_tile,
               mod_row0, per_seq_mod):
    d = D_MODEL
    tiles = seq_len // q_tile
    has_cache = cache is not None
    q3 = q.reshape(n_seq, seq_len, ATTN_WIDTH)
    k3 = k.reshape(n_seq, seq_len, KV_WIDTH)
    v3 = v.reshape(n_seq, seq_len, KV_WIDTH)
    sg3 = sg.reshape(n_seq, seq_len, ATTN_WIDTH)
    x3 = x2.reshape(n_seq, seq_len, d)

    def mod_map(b, i):
        return (0, mod_row0 + (b if per_seq_mod else 0), 0, 0)

    tile_spec = lambda width: pl.BlockSpec((None, q_tile, width), lambda b, i: (b, i, 0))
    seq_spec = lambda length, width: pl.BlockSpec((None, length, width), lambda b, i: (b, 0, 0))
    full2 = lambda b, i: (0, 0)
    in_specs = [tile_spec(ATTN_WIDTH), seq_spec(seq_len, KV_WIDTH), seq_spec(seq_len, KV_WIDTH)]
    args = [q3, k3, v3]
    if has_cache:
        ck, cv = cache
        in_specs += [seq_spec(ck.shape[1], KV_WIDTH), seq_spec(cv.shape[1], KV_WIDTH)]
        args += [ck, cv]
    in_specs += [
        tile_spec(ATTN_WIDTH),
        tile_spec(d),
        pl.BlockSpec((None, None, 1, 3 * d), mod_map),
        pl.BlockSpec(w_out_bf.shape, full2),
        pl.BlockSpec(ln_g.shape, full2),
        pl.BlockSpec(ln_b.shape, full2),
    ]
    args += [sg3, x3, mods4, w_out_bf, ln_g, ln_b]
    out = pl.pallas_call(
        functools.partial(_attn_core_kernel, has_cache=has_cache),
        grid=(n_seq, tiles),
        in_specs=in_specs,
        out_specs=tile_spec(d),
        out_shape=jax.ShapeDtypeStruct((n_seq, seq_len, d), F32),
        compiler_params=pltpu.CompilerParams(vmem_limit_bytes=VMEM_LIMIT),
        name="attn_core_cache" if has_cache else "attn_core",
    )(*args)
    return out.reshape(n_seq * seq_len, d)


def _rec_proj_kernel(x_ref, mod_ref, w_ref, lb_ref, q_out, v_out, g_out, lf_fw, lf_bw, k_fw, k_bw):
    d = D_MODEL
    mod = mod_ref[...]
    shift, scale = mod[:, :d], mod[:, d:2 * d]
    h = (x_ref[...] * (1.0 + scale) + shift).astype(BF16)
    u = _dot(h, w_ref[:, 0:REC_WIDTH])
    q_out[...] = (u * _sigmoid(u)).astype(q_out.dtype)
    for direction, (lf_out, k_out) in enumerate(((lf_fw, k_fw), (lf_bw, k_bw))):
        z = _dot(h, w_ref[:, (1 + direction) * REC_WIDTH:(2 + direction) * REC_WIDTH])
        lb = lb_ref[direction:direction + 1, :]
        sig = _sigmoid(z)
        forget = lb + (1.0 - lb) * sig
        lf_out[...] = jnp.log(forget)
        k_out[...] = ((1.0 - lb) * (1.0 - sig)).astype(k_out.dtype)
    u = _dot(h, w_ref[:, 3 * REC_WIDTH:4 * REC_WIDTH])
    v_out[...] = u.astype(v_out.dtype)
    u = _dot(h, w_ref[:, 4 * REC_WIDTH:5 * REC_WIDTH])
    g_out[...] = (u * _sigmoid(u)).astype(g_out.dtype)


def _rec_proj(x2, mods4, w_bf, lb, *, seq_len, mod_row0, per_seq_mod):
    n_tok = x2.shape[0]
    d = D_MODEL
    tiles_per_seq = seq_len // ROW_TILE

    def mod_map(i):
        row = mod_row0 + (i // tiles_per_seq if per_seq_mod else 0)
        return (1, row, 0, 0)

    row_spec = pl.BlockSpec((ROW_TILE, REC_WIDTH), lambda i: (i, 0))
    full = lambda i: (0, 0)
    out_dtypes = (BF16, BF16, BF16, F32, F32, BF16, BF16)
    return pl.pallas_call(
        _rec_proj_kernel,
        grid=(n_tok // ROW_TILE,),
        in_specs=[
            pl.BlockSpec((ROW_TILE, d), lambda i: (i, 0)),
            pl.BlockSpec((None, None, 1, 3 * d), mod_map),
            pl.BlockSpec(w_bf.shape, full),
            pl.BlockSpec(lb.shape, full),
        ],
        out_specs=tuple(row_spec for _ in out_dtypes),
        out_shape=tuple(jax.ShapeDtypeStruct((n_tok, REC_WIDTH), t) for t in out_dtypes),
        compiler_params=pltpu.CompilerParams(vmem_limit_bytes=VMEM_LIMIT),
        name="rec_proj",
    )(x2, mods4, w_bf, lb)


def _small_level_exponents(b, backward):
    t_rows = b.shape[0]
    pos = lax.broadcasted_iota(jnp.int32, b.shape, 0)
    even = jnp.where(pos % 2 == 0, b, pltpu.roll(b, 1, 0))
    odd = jnp.where(pos % 2 == 1, b, pltpu.roll(b, t_rows - 1, 0))
    if backward:
        c1 = odd
        c2 = jnp.where(pos % 4 >= 2, even, pltpu.roll(even, t_rows - 2, 0))
    else:
        c1 = even
        c2 = jnp.where(pos % 4 < 2, odd, pltpu.roll(odd, 2, 0))
    return {1: _neg_abs(b - c1), 2: _neg_abs(b - c2)}


def _large_level_exponent(b, edges, half, backward):
    t_rows, lanes = b.shape
    g = half // SUBLANES
    n_blocks = t_rows // (2 * half)
    b5 = b.reshape(n_blocks, 2, g, SUBLANES, lanes)
    e5 = edges.reshape(n_blocks, 2, g, SUBLANES, lanes)
    centre = (e5[:, 1:2, 0:1] if backward else e5[:, 0:1, g - 1:g])
    first, second = b5[:, 0:1], b5[:, 1:2]
    if backward:
        parts = [first - centre, centre - second]
    else:
        parts = [centre - first, second - centre]
    return jnp.concatenate(parts, axis=1).reshape(t_rows, lanes)


def _neg_abs(x):
    bits = lax.bitcast_convert_type(x, jnp.int32) | jnp.int32(-2 ** 31)
    return lax.bitcast_convert_type(bits, F32)


def _block_diag(a, b):
    za = jnp.zeros(a.shape, a.dtype)
    return jnp.concatenate(
        [jnp.concatenate([a, za], axis=1), jnp.concatenate([za, b], axis=1)], axis=0)


def _pair_scores(a, c):
    return _dot_nt(a, _block_diag(c[:, :REC_DK], c[:, REC_DK:]))


def _gla_units_step(units):
    t = GLA_TILE
    dk = REC_DK
    for u in units:
        lf2 = u["lf"]() * LOG2_E
        hi = lf2.astype(BF16)
        lo = (lf2 - hi.astype(F32)).astype(BF16)
        sums = _dot(u["sums_ref"][...], jnp.concatenate([hi, lo], axis=0))
        u["b"], u["edges"], u["e4"] = sums[:t], sums[t:2 * t], sums[2 * t:]
        u["scores"] = u["masks_ref"][0] * _pair_scores(u["q"](), u["k"]()).astype(BF16)

    def level(u, li, e):
        x = jnp.exp2(e).astype(BF16)
        z = _pair_scores(u["q"]() * x, u["k"]() * x).astype(BF16)
        u["scores"] = u["scores"] + u["masks_ref"][1 + li] * z

    order = [h for h in GLA_LEVELS if h >= SUBLANES // 2] + [h for h in GLA_LEVELS if h < SUBLANES // 2]
    for half in order:
        li = GLA_LEVELS.index(half)
        for u in units:
            if half == SUBLANES // 2:
                e = u["e4"]
            elif half >= SUBLANES:
                e = _large_level_exponent(u["b"], u["edges"], half, u["backward"])
            else:
                if "small" not in u:
                    u["small"] = _small_level_exponents(u["b"], u["backward"])
                e = u["small"][half]
            level(u, li, e)

    for u in units:
        b, st_ref = u["b"], u["st_ref"]
        q, k, v = u["q"](), u["k"](), u["v"]()
        edge = b[0:1, :] if u["backward"] else b[t - 1:t, :]
        o = _dot(u["scores"], _block_diag(v[:, :dk], v[:, dk:]))
        st_a, st_b = st_ref[0], st_ref[1]
        q_in = q * jnp.exp2(b).astype(BF16)
        o = o + _dot_nt(q_in, _block_diag(st_a.astype(BF16), st_b.astype(BF16)))
        k_edge = k * jnp.exp2(edge - b).astype(BF16)
        carry = jnp.exp2(edge)
        st_ref[0] = st_a * carry[:, :dk] + _dot_tn(v[:, :dk], k_edge[:, :dk])
        st_ref[1] = st_b * carry[:, dk:] + _dot_tn(v[:, dk:], k_edge[:, dk:])
        u["store"](o)


def _gla_kernel(*refs, seq_len, has_state, want_state):
    refs = list(refs)
    q_ref, v_ref, lff_ref, lfb_ref, kf_ref, kb_ref = refs[:6]
    pos = 6
    if has_state:
        s0_ref = refs[pos]
        pos += 1
    sums_f_ref, sums_b_ref, mf_ref, mb_ref = refs[pos:pos + 4]
    pos += 4
    o_ref = refs[pos]
    pos += 1
    if want_state:
        s_out_ref = refs[pos]
        pos += 1
    st_ref = refs[pos]

    t = GLA_TILE
    n_tiles = seq_len // t
    pair_lanes = 2 * REC_DK
    o_ref[...] = jnp.zeros(o_ref.shape, o_ref.dtype)
    for direction in range(2):
        for hd in range(GLA_HEADS_PER_STEP):
            if has_state:
                st_ref[direction, hd] = s0_ref[direction, hd].T
            else:
                st_ref[direction, hd] = jnp.zeros((REC_DV, REC_DK), F32)

    def body(i, carry):
        units = []
        for direction, (lf_ref, k_ref, sums_ref, m_ref) in enumerate(
                ((lff_ref, kf_ref, sums_f_ref, mf_ref), (lfb_ref, kb_ref, sums_b_ref, mb_ref))):
            tile = i if direction == 0 else n_tiles - 1 - i
            rows = pl.ds(pl.multiple_of(tile * t, t), t)
            for pair in range(GLA_HEADS_PER_STEP // 2):
                lanes = slice(pair * pair_lanes, (pair + 1) * pair_lanes)

                def store(o, rows=rows, lanes=lanes):
                    o_ref[rows, lanes] += o

                load = lambda ref, rows=rows, lanes=lanes: (lambda: ref[rows, lanes])
                units.append(dict(
                    q=load(q_ref), k=load(k_ref), v=load(v_ref), lf=load(lf_ref),
                    st_ref=st_ref.at[direction, pl.ds(2 * pair, 2)], sums_ref=sums_ref,
                    masks_ref=m_ref, backward=direction == 1, store=store))
        _gla_units_step(units)
        return carry

    lax.fori_loop(0, n_tiles, body, 0)
    if want_state:
        for direction in range(2):
            for hd in range(GLA_HEADS_PER_STEP):
                s_out_ref[direction, hd] = st_ref[direction, hd].T


def _gla(q, v, lf_fw, lf_bw, k_fw, k_bw, s0, consts, *, n_seq, seq_len, want_state):
    has_state = s0 is not None
    width = REC_WIDTH
    hps = GLA_HEADS_PER_STEP
    seq3 = lambda a: a.reshape(n_seq, seq_len, width)
    head_spec = pl.BlockSpec((None, seq_len, hps * REC_DK), lambda b, h: (b, 0, h))
    state_spec = pl.BlockSpec((None, 2, hps, REC_DK, REC_DV), lambda b, h: (b, 0, h, 0, 0))
    in_specs = [head_spec] * 6
    args = [seq3(q), seq3(v), seq3(lf_fw), seq3(lf_bw), seq3(k_fw), seq3(k_bw)]
    if has_state:
        in_specs.append(state_spec)
        args.append(s0)
    for c in consts:
        in_specs.append(pl.BlockSpec(c.shape, lambda b, h, nd=c.ndim: (0,) * nd))
        args.append(c)
    out_shape = [jax.ShapeDtypeStruct((n_seq, seq_len, width), F32)]
    out_specs = [head_spec]
    if want_state:
        out_shape.append(jax.ShapeDtypeStruct((n_seq, 2, N_REC_HEADS, REC_DK, REC_DV), F32))
        out_specs.append(state_spec)
    res = pl.pallas_call(
        functools.partial(_gla_kernel, seq_len=seq_len, has_state=has_state, want_state=want_state),
        grid=(n_seq, N_REC_HEADS // hps),
        in_specs=in_specs,
        out_specs=tuple(out_specs),
        out_shape=tuple(out_shape),
        scratch_shapes=[pltpu.VMEM((2, hps, REC_DV, REC_DK), F32)],
        compiler_params=pltpu.CompilerParams(vmem_limit_bytes=VMEM_LIMIT),
        name="gla_state_in" if has_state else "gla_state_out",
    )(*args)
    o = res[0].reshape(n_seq * seq_len, width)
    return (o, res[1]) if want_state else (o, None)


def _rec_out_kernel(o_ref, sg_ref, x_ref, mod_ref, ng_ref, w_ref, lng_ref, lnb_ref, out_ref):
    d = D_MODEL
    parts = []
    for hd in range(N_REC_HEADS):
        oh = o_ref[:, hd * REC_DV:(hd + 1) * REC_DV]
        ms = jnp.mean(oh * oh, axis=-1, keepdims=True)
        parts.append(oh * lax.rsqrt(ms + NORM_EPS) * ng_ref[...])
    o = jnp.concatenate(parts, axis=1)
    gated = (o * sg_ref[...].astype(F32)).astype(BF16)
    branch = _dot(gated, w_ref[...])
    gate = mod_ref[...][:, 2 * d:]
    y = DEEPNORM_ALPHA * x_ref[...] + gate * branch
    out_ref[...] = _layer_norm(y, lng_ref[...], lnb_ref[...])


def _rec_out(o, sg, x2, mods4, norm_gain, w_out_bf, ln_g, ln_b, *, seq_len, mod_row0, per_seq_mod):
    n_tok = x2.shape[0]
    d = D_MODEL
    tiles_per_seq = seq_len // ROW_TILE

    def mod_map(i):
        row = mod_row0 + (i // tiles_per_seq if per_seq_mod else 0)
        return (1, row, 0, 0)

    row_spec = pl.BlockSpec((ROW_TILE, d), lambda i: (i, 0))
    full = lambda i: (0, 0)
    return pl.pallas_call(
        _rec_out_kernel,
        grid=(n_tok // ROW_TILE,),
        in_specs=[
            row_spec, row_spec, row_spec,
            pl.BlockSpec((None, None, 1, 3 * d), mod_map),
            pl.BlockSpec(norm_gain.shape, full),
            pl.BlockSpec(w_out_bf.shape, full),
            pl.BlockSpec(ln_g.shape, full),
            pl.BlockSpec(ln_b.shape, full),
        ],
        out_specs=row_spec,
        out_shape=jax.ShapeDtypeStruct((n_tok, d), F32),
        compiler_params=pltpu.CompilerParams(vmem_limit_bytes=VMEM_LIMIT),
        name="rec_out",
    )(o, sg, x2, mods4, norm_gain, w_out_bf, ln_g, ln_b)


def _rope_tables(n_tokens):
    n_rows = n_tokens // GRID_W
    rows = jnp.repeat(jnp.arange(n_rows, dtype=F32), GRID_W)
    cols = jnp.tile(jnp.arange(GRID_W, dtype=F32), n_rows)
    inv_freq = 1.0 / (ROPE_THETA ** (jnp.arange(0, AXIS_DIM, 2, dtype=F32) / AXIS_DIM))
    ang_r = rows[:, None] * inv_freq[None, :]
    ang_c = cols[:, None] * inv_freq[None, :]
    ang = jnp.concatenate([ang_r, ang_r, ang_c, ang_c], axis=-1)
    cos, sin = jnp.cos(ang), jnp.sin(ang)
    first = (jnp.arange(HEAD_DIM) % AXIS_DIM) < AXIS_DIM // 2
    sin_a = jnp.where(first[None, :], -sin, 0.0)
    sin_b = jnp.where(first[None, :], 0.0, sin)
    reps = GROUP_LANES // HEAD_DIM
    return tuple(jnp.tile(t, (1, reps)) for t in (cos, sin_a, sin_b))


def _head_mean_matrix():
    idx = jnp.arange(GROUP_LANES) // HEAD_DIM
    return jnp.where(idx[:, None] == idx[None, :], 1.0 / HEAD_DIM, 0.0).astype(BF16)


def _gla_consts():
    t = GLA_TILE
    r = jnp.arange(t)[:, None]
    c = jnp.arange(t)[None, :]
    grp = SUBLANES * (r // SUBLANES)
    mid = SUBLANES // 2
    upper = r % SUBLANES >= mid
    fw = [c <= r,
          c <= grp + SUBLANES - 1,
          jnp.where(upper, (c > grp + mid - 1) & (c <= r), (c > r) & (c <= grp + mid - 1))]
    bw = [c >= r,
          c >= grp,
          jnp.where(upper, (c >= grp + mid) & (c < r), (c >= r) & (c < grp + mid))]
    twice = lambda a: jnp.concatenate([a, a], axis=-1)
    sums_fw = twice(jnp.concatenate(fw, axis=0).astype(BF16))
    sums_bw = twice(jnp.concatenate(bw, axis=0).astype(BF16))
    masks = [(r == c)]
    for half in GLA_LEVELS:
        size = 2 * half
        masks.append((r // size == c // size) & (r % size >= half) & (c % size < half))
    m_fw = jnp.stack(masks).astype(BF16)
    m_bw = jnp.swapaxes(m_fw, 1, 2)
    return sums_fw, sums_bw, twice(m_fw), twice(m_bw)


def kernel(x_prompt, x_sample, cache_k, cache_v, state_rec, c, c_ctx, ada_w, ada_b, attn_w_in,
           attn_q_gain, attn_k_gain, attn_w_out, rec_w_in, rec_lower_bounds, rec_norm_gain,
           rec_w_out, ln_gain, ln_bias):
    d = D_MODEL
    n_p, len_p, _ = x_prompt.shape
    n_s, len_s, _ = x_sample.shape
    past = cache_k.shape[2]

    cond = jnp.zeros((COND_ROWS, d), F32).at[0].set(c_ctx).at[1:1 + n_s].set(c)
    mods4 = _mods(cond, ada_w, ada_b).reshape(DEPTH, COND_ROWS, 1, 3 * d)
    lb_all = _lower_bounds(rec_lower_bounds)

    xp = x_prompt.reshape(n_p * len_p, d)
    xs = x_sample.reshape(n_s * len_s, d)

    w_in = attn_w_in[0].astype(BF16)
    w_out = attn_w_out[0].astype(BF16)
    reps = GROUP_LANES // HEAD_DIM
    qg = (jnp.tile(attn_q_gain[0], reps) * (1.0 / math.sqrt(HEAD_DIM))).reshape(1, GROUP_LANES)
    kg = jnp.tile(attn_k_gain[0], reps).reshape(1, GROUP_LANES)
    pn = _head_mean_matrix()
    ln_g = ln_gain[0].reshape(1, d)
    ln_b = ln_bias[0].reshape(1, d)

    q_p, k_p, v_p, g_p = _attn_proj(xp, mods4, w_in, pn, qg, kg, None, seq_len=len_p,
                                    mod_row0=0, per_seq_mod=False, kv_dtype=F32)
    q_s, k_s, v_s, g_s = _attn_proj(xs, mods4, w_in, pn, qg, kg, _rope_tables(len_s), seq_len=len_s,
                                    mod_row0=1, per_seq_mod=True, kv_dtype=BF16)
    xp1 = _attn_core(q_p, k_p, v_p, None, g_p, xp, mods4, w_out, ln_g, ln_b, n_seq=n_p,
                     seq_len=len_p, q_tile=len_p, mod_row0=0, per_seq_mod=False)
    cache = (cache_k[:, 0].reshape(n_s, past, KV_WIDTH), cache_v[:, 0].reshape(n_s, past, KV_WIDTH))
    xs1 = _attn_core(q_s, k_s, v_s, cache, g_s, xs, mods4, w_out, ln_g, ln_b, n_seq=n_s,
                     seq_len=len_s, q_tile=ROW_TILE, mod_row0=1, per_seq_mod=True)
    new_cache_k = k_p.reshape(n_p, 1, len_p, N_KV_HEADS, HEAD_DIM)
    new_cache_v = v_p.reshape(n_p, 1, len_p, N_KV_HEADS, HEAD_DIM)

    rw_in = rec_w_in[0].astype(BF16)
    rw_out = rec_w_out[0].astype(BF16)
    lb = lb_all[1]
    ng = rec_norm_gain[0].reshape(1, REC_DV)
    ln_g = ln_gain[1].reshape(1, d)
    ln_b = ln_bias[1].reshape(1, d)
    consts = _gla_consts()

    outs = []
    states = None
    for x1, n_seq, seq_len, row0, per_seq, s0 in (
            (xp1, n_p, len_p, 0, False, None),
            (xs1, n_s, len_s, 1, True, state_rec[:, 0])):
        q, v, g, lf_fw, lf_bw, k_fw, k_bw = _rec_proj(x1, mods4, rw_in, lb, seq_len=seq_len,
                                                       mod_row0=row0, per_seq_mod=per_seq)
        o, st = _gla(q, v, lf_fw, lf_bw, k_fw, k_bw, s0, consts, n_seq=n_seq, seq_len=seq_len,
                     want_state=s0 is None)
        if st is not None:
            states = st
        outs.append(_rec_out(o, g, x1, mods4, ng, rw_out, ln_g, ln_b, seq_len=seq_len,
                             mod_row0=row0, per_seq_mod=per_seq))

    y_prompt = outs[0].reshape(n_p, len_p, d)
    y_sample = outs[1].reshape(n_s, len_s, d)
    new_state_rec = states.reshape(n_p, 1, 2, N_REC_HEADS, REC_DK, REC_DV)
    return (y_prompt, y_sample, new_cache_k, new_cache_v, new_state_rec)
```

```python
import functools
import math

import jax
import jax.numpy as jnp
from jax import lax
from jax.experimental import pallas as pl
from jax.experimental.pallas import tpu as pltpu

F32 = jnp.float32
BF16 = jnp.bfloat16

D_MODEL = 1024
DEPTH = 2
GRID_W = 64
N_HEADS = 16
N_KV_HEADS = 4
HEAD_DIM = 64
AXIS_DIM = HEAD_DIM // 2
ATTN_WIDTH = N_HEADS * HEAD_DIM
KV_WIDTH = N_KV_HEADS * HEAD_DIM
ROPE_THETA = 10000.0
N_REC_HEADS = 8
REC_DK = 128
REC_DV = 128
REC_WIDTH = N_REC_HEADS * REC_DK
NORM_EPS = 1e-6
LN_EPS = 1e-5
DEEPNORM_ALPHA = (2.0 * DEPTH) ** 0.25

SUBLANES = 8
COND_ROWS = SUBLANES
ROW_TILE = 256
GROUP_LANES = 256
SLAB_LANES = 128
KPAD_WIDTH = 2 * N_KV_HEADS * SLAB_LANES
KEY_BLOCK = 256
GLA_TILE = 128
GLA_LEVELS = (1, 2, 4, 8, 16, 32, 64)
GLA_HEADS_PER_STEP = 4
LOG2_E = 1.4426950408889634
VMEM_LIMIT = 56 * 1024 * 1024


def _sigmoid(x):
    return 1.0 / (1.0 + jnp.exp(-x))


def _dot(a, b):
    return jnp.dot(a, b, preferred_element_type=F32)


def _dot_nt(a, b):
    return lax.dot_general(a, b, (((1,), (1,)), ((), ())), preferred_element_type=F32)


def _dot_tn(a, b):
    return lax.dot_general(a, b, (((0,), (0,)), ((), ())), preferred_element_type=F32)


def _layer_norm(y, g, b):
    mu = jnp.mean(y, axis=-1, keepdims=True)
    yc = y - mu
    var = jnp.mean(yc * yc, axis=-1, keepdims=True)
    return yc * lax.rsqrt(var + LN_EPS) * g + b


def _mods_kernel(cond_ref, w_ref, b_ref, out_ref):
    c = cond_ref[...]
    s = (c * _sigmoid(c)).astype(BF16)
    out_ref[...] = _dot(s, w_ref[...].astype(BF16)) + b_ref[...]


def _mods(cond, ada_w, ada_b):
    d = D_MODEL
    return pl.pallas_call(
        _mods_kernel,
        grid=(DEPTH, 3),
        in_specs=[
            pl.BlockSpec((COND_ROWS, d), lambda l, j: (0, 0)),
            pl.BlockSpec((None, d, d), lambda l, j: (l, 0, j)),
            pl.BlockSpec((None, 1, d), lambda l, j: (l, 0, j)),
        ],
        out_specs=pl.BlockSpec((None, COND_ROWS, d), lambda l, j: (l, 0, j)),
        out_shape=jax.ShapeDtypeStruct((DEPTH, COND_ROWS, 3 * d), F32),
        compiler_params=pltpu.CompilerParams(vmem_limit_bytes=VMEM_LIMIT),
        name="adaln_mods",
    )(cond, ada_w, ada_b.reshape(DEPTH, 1, 3 * d))


def _lower_bounds_kernel(r_ref, out_ref):
    r = [r_ref[i] for i in range(DEPTH)]
    m = functools.reduce(jnp.maximum, r)
    e = [jnp.exp(x - m) for x in r]
    tot = functools.reduce(lambda a, b: a + b, e)
    soft = [x / tot for x in e]
    acc = soft[0]
    for i in range(DEPTH):
        if i > 0:
            acc = acc + soft[i]
        out_ref[i] = acc - soft[0]


def _lower_bounds(rec_lower_bounds):
    return pl.pallas_call(
        _lower_bounds_kernel,
        out_shape=jax.ShapeDtypeStruct(rec_lower_bounds.shape, F32),
        name="rec_lower_bounds",
    )(rec_lower_bounds)


def _kv_head_slabs(k):
    lane = lax.broadcasted_iota(jnp.int32, (k.shape[0], SLAB_LANES), 1)
    low = lane < HEAD_DIM
    heads_per_slab = SLAB_LANES // HEAD_DIM
    out = []
    for j in range(N_KV_HEADS):
        tile = k[:, (j // heads_per_slab) * SLAB_LANES:(j // heads_per_slab + 1) * SLAB_LANES]
        moved = pltpu.roll(tile, HEAD_DIM, 1)
        at_low, at_high = (tile, moved) if j % heads_per_slab == 0 else (moved, tile)
        out.append(jnp.where(low, at_low, 0.0))
        out.append(jnp.where(low, 0.0, at_high))
    return jnp.concatenate(out, axis=1)


def _attn_proj_kernel(*refs, rope, cache_out):
    refs = list(refs)
    x_ref, mod_ref, w_ref, pn_ref, qg_ref, kg_ref = refs[:6]
    pos = 6
    if rope:
        cos_ref, sa_ref, sb_ref = refs[pos:pos + 3]
        pos += 3
    q_out, kp_out, vt_out, g_out = refs[pos:pos + 4]
    pos += 4
    if cache_out:
        k_out, v_out = refs[pos:pos + 2]
    d = D_MODEL
    mod = mod_ref[...]
    shift, scale = mod[:, :d], mod[:, d:2 * d]
    h = (x_ref[...] * (1.0 + scale) + shift).astype(BF16)
    pn = pn_ref[...]

    def norm_rope(u, gain):
        ms = _dot((u * u).astype(BF16), pn)
        y = u * lax.rsqrt(ms + NORM_EPS) * gain
        if rope:
            y = (y * cos_ref[...]
                 + pltpu.roll(y, GROUP_LANES - AXIS_DIM // 2, 1) * sa_ref[...]
                 + pltpu.roll(y, AXIS_DIM // 2, 1) * sb_ref[...])
        return y

    for j in range(N_KV_HEADS):
        lo = j * GROUP_LANES
        u = _dot(h, w_ref[:, lo:lo + GROUP_LANES])
        q_out[:, lo:lo + GROUP_LANES] = norm_rope(u, qg_ref[...]).astype(q_out.dtype)
    u = _dot(h, w_ref[:, ATTN_WIDTH:ATTN_WIDTH + KV_WIDTH])
    k = norm_rope(u, kg_ref[...])
    kp_out[...] = _kv_head_slabs(k).astype(kp_out.dtype)
    v = _dot(h, w_ref[:, ATTN_WIDTH + KV_WIDTH:ATTN_WIDTH + 2 * KV_WIDTH])
    vt_out[...] = v.T.astype(vt_out.dtype)
    if cache_out:
        k_out[...] = k
        v_out[...] = v
    for j in range(N_KV_HEADS):
        lo = ATTN_WIDTH + 2 * KV_WIDTH + j * GROUP_LANES
        u = _dot(h, w_ref[:, lo:lo + GROUP_LANES])
        g_out[:, j * GROUP_LANES:(j + 1) * GROUP_LANES] = (u * _sigmoid(u)).astype(g_out.dtype)


def _attn_proj(x2, mods4, w_bf, pn, qg, kg, rope_tabs, *, seq_len, mod_row0, per_seq_mod, cache_out):
    n_tok = x2.shape[0]
    d = D_MODEL
    tiles_per_seq = seq_len // ROW_TILE
    rope = rope_tabs is not None

    def mod_map(i):
        row = mod_row0 + (i // tiles_per_seq if per_seq_mod else 0)
        return (0, row, 0, 0)

    full = lambda i: (0, 0)
    in_specs = [
        pl.BlockSpec((ROW_TILE, d), lambda i: (i, 0)),
        pl.BlockSpec((None, None, 1, 3 * d), mod_map),
        pl.BlockSpec(w_bf.shape, full),
        pl.BlockSpec(pn.shape, full),
        pl.BlockSpec(qg.shape, full),
        pl.BlockSpec(kg.shape, full),
    ]
    args = [x2, mods4, w_bf, pn, qg, kg]
    if rope:
        for t in rope_tabs:
            in_specs.append(pl.BlockSpec((ROW_TILE, GROUP_LANES), lambda i: (i % tiles_per_seq, 0)))
            args.append(t)
    rows = lambda width: pl.BlockSpec((ROW_TILE, width), lambda i: (i, 0))
    out_shape = [
        jax.ShapeDtypeStruct((n_tok, ATTN_WIDTH), BF16),
        jax.ShapeDtypeStruct((n_tok, KPAD_WIDTH), BF16),
        jax.ShapeDtypeStruct((n_tok // seq_len, KV_WIDTH, seq_len), BF16),
        jax.ShapeDtypeStruct((n_tok, ATTN_WIDTH), BF16),
    ]
    out_specs = [
        rows(ATTN_WIDTH),
        rows(KPAD_WIDTH),
        pl.BlockSpec((None, KV_WIDTH, ROW_TILE), lambda i: (i // tiles_per_seq, 0, i % tiles_per_seq)),
        rows(ATTN_WIDTH),
    ]
    if cache_out:
        out_shape += [jax.ShapeDtypeStruct((n_tok, KV_WIDTH), F32)] * 2
        out_specs += [rows(KV_WIDTH)] * 2
    return pl.pallas_call(
        functools.partial(_attn_proj_kernel, rope=rope, cache_out=cache_out),
        grid=(n_tok // ROW_TILE,),
        in_specs=in_specs,
        out_specs=tuple(out_specs),
        out_shape=tuple(out_shape),
        compiler_params=pltpu.CompilerParams(vmem_limit_bytes=VMEM_LIMIT),
        name="attn_proj_rope" if rope else "attn_proj",
    )(*args)


def _attn_core_kernel(*refs, n_new_blocks, has_cache):
    refs = list(refs)
    q_ref, kp_ref, vt_ref = refs[:3]
    pos = 3
    if has_cache:
        kc_ref, vc_ref = refs[pos:pos + 2]
        pos += 2
    sg_ref, x_ref, mod_ref, w_ref, lng_ref, lnb_ref, out_ref, s_scr, ot_scr = refs[pos:]
    d = D_MODEL
    tq = q_ref.shape[0]
    kb = KEY_BLOCK
    group = N_HEADS // N_KV_HEADS
    heads_per_slab = SLAB_LANES // HEAD_DIM

    blocks = []
    if has_cache:
        kc_slabs = _kv_head_slabs(kc_ref[...]).astype(BF16)
        vc_t = vc_ref[...].T.astype(BF16)
        blocks.append((lambda s: kc_slabs[:, s * SLAB_LANES:(s + 1) * SLAB_LANES],
                       lambda j: vc_t[j * HEAD_DIM:(j + 1) * HEAD_DIM, :]))
    for blk in range(n_new_blocks):
        blocks.append((lambda s, blk=blk: kp_ref[blk * kb:(blk + 1) * kb, s * SLAB_LANES:(s + 1) * SLAB_LANES],
                       lambda j, blk=blk: vt_ref[j * HEAD_DIM:(j + 1) * HEAD_DIM, blk * kb:(blk + 1) * kb]))

    def fold_rows(a, op):
        return functools.reduce(op, [a[r * SUBLANES:(r + 1) * SUBLANES, :] for r in range(kb // SUBLANES)])

    def scores_phase(j):
        slot = j % 2
        maxima = []
        for hh in range(group):
            hd = j * group + hh
            slab, where = hd // heads_per_slab, hd % heads_per_slab
            q_slab = q_ref[:, slab * SLAB_LANES:(slab + 1) * SLAB_LANES]
            mx = None
            for bi, (keys, _) in enumerate(blocks):
                s = _dot_nt(keys(heads_per_slab * j + where), q_slab)
                s_scr[slot, hh, bi * kb:(bi + 1) * kb, :] = s
                r = fold_rows(s, jnp.maximum)
                mx = r if mx is None else jnp.maximum(mx, r)
            maxima.append(jnp.max(mx, axis=0, keepdims=True))
        return maxima

    def values_phase(j, maxima):
        slot = j % 2
        acc = None
        sums = [None] * group
        for bi, (_, values) in enumerate(blocks):
            ps = []
            for hh in range(group):
                p = jnp.exp2(s_scr[slot, hh, bi * kb:(bi + 1) * kb, :] - maxima[hh])
                r = fold_rows(p, lambda a, b: a + b)
                sums[hh] = r if sums[hh] is None else sums[hh] + r
                ps.append(p.astype(BF16))
            part = _dot(values(j), jnp.concatenate(ps, axis=1))
            acc = part if acc is None else acc + part
        for hh in range(group):
            hd = j * group + hh
            l = jnp.sum(sums[hh], axis=0, keepdims=True)
            ot_scr[hd * HEAD_DIM:(hd + 1) * HEAD_DIM, :] = acc[:, hh * tq:(hh + 1) * tq] * (1.0 / l)

    pending = scores_phase(0)
    for j in range(N_KV_HEADS):
        nxt = scores_phase(j + 1) if j + 1 < N_KV_HEADS else None
        values_phase(j, pending)
        pending = nxt

    o = ot_scr[...].T
    gated = (o * sg_ref[...].astype(F32)).astype(BF16)
    branch = _dot(gated, w_ref[...])
    gate = mod_ref[...][:, 2 * d:]
    y = DEEPNORM_ALPHA * x_ref[...] + gate * branch
    out_ref[...] = _layer_norm(y, lng_ref[...], lnb_ref[...])


def _attn_core(q, kp, vt, cache, sg, x2, mods4, w_out_bf, ln_g, ln_b, *, n_seq, seq_len, q_tile,
               mod_row0, per_seq_mod):
    d = D_MODEL
    tiles = seq_len // q_tile
    has_cache = cache is not None
    q3 = q.reshape(n_seq, seq_len, ATTN_WIDTH)
    kp3 = kp.reshape(n_seq, seq_len, KPAD_WIDTH)
    sg3 = sg.reshape(n_seq, seq_len, ATTN_WIDTH)
    x3 = x2.reshape(n_seq, seq_len, d)
    n_keys = seq_len + (cache[0].shape[1] if has_cache else 0)

    def mod_map(b, i):
        return (0, mod_row0 + (b if per_seq_mod else 0), 0, 0)

    tile_spec = lambda width: pl.BlockSpec((None, q_tile, width), lambda b, i: (b, i, 0))
    seq_spec = lambda rows, width: pl.BlockSpec((None, rows, width), lambda b, i: (b, 0, 0))
    full2 = lambda b, i: (0, 0)
    in_specs = [tile_spec(ATTN_WIDTH), seq_spec(seq_len, KPAD_WIDTH), seq_spec(KV_WIDTH, seq_len)]
    args = [q3, kp3, vt]
    if has_cache:
        ck, cv = cache
        in_specs += [seq_spec(ck.shape[1], KV_WIDTH), seq_spec(cv.shape[1], KV_WIDTH)]
        args += [ck, cv]
    in_specs += [
        tile_spec(ATTN_WIDTH),
        tile_spec(d),
        pl.BlockSpec((None, None, 1, 3 * d), mod_map),
        pl.BlockSpec(w_out_bf.shape, full2),
        pl.BlockSpec(ln_g.shape, full2),
        pl.BlockSpec(ln_b.shape, full2),
    ]
    args += [sg3, x3, mods4, w_out_bf, ln_g, ln_b]
    out = pl.pallas_call(
        functools.partial(_attn_core_kernel, n_new_blocks=seq_len // KEY_BLOCK, has_cache=has_cache),
        grid=(n_seq, tiles),
        in_specs=in_specs,
        out_specs=tile_spec(d),
        out_shape=jax.ShapeDtypeStruct((n_seq, seq_len, d), F32),
        scratch_shapes=[
            pltpu.VMEM((2, N_HEADS // N_KV_HEADS, n_keys, q_tile), F32),
            pltpu.VMEM((ATTN_WIDTH, q_tile), F32),
        ],
        compiler_params=pltpu.CompilerParams(vmem_limit_bytes=VMEM_LIMIT),
        name="attn_core_cache" if has_cache else "attn_core",
    )(*args)
    return out.reshape(n_seq * seq_len, d)


def _rec_proj_kernel(x_ref, mod_ref, w_ref, lb_ref, q_out, v_out, g_out, lf_fw, lf_bw, k_fw, k_bw):
    d = D_MODEL
    mod = mod_ref[...]
    shift, scale = mod[:, :d], mod[:, d:2 * d]
    h = (x_ref[...] * (1.0 + scale) + shift).astype(BF16)
    u = _dot(h, w_ref[:, 0:REC_WIDTH])
    q_out[...] = (u * _sigmoid(u)).astype(q_out.dtype)
    for direction, (lf_out, k_out) in enumerate(((lf_fw, k_fw), (lf_bw, k_bw))):
        z = _dot(h, w_ref[:, (1 + direction) * REC_WIDTH:(2 + direction) * REC_WIDTH])
        lb = lb_ref[direction:direction + 1, :]
        sig = _sigmoid(z)
        forget = lb + (1.0 - lb) * sig
        lf_out[...] = jnp.log(forget)
        k_out[...] = ((1.0 - lb) * (1.0 - sig)).astype(k_out.dtype)
    u = _dot(h, w_ref[:, 3 * REC_WIDTH:4 * REC_WIDTH])
    v_out[...] = u.astype(v_out.dtype)
    u = _dot(h, w_ref[:, 4 * REC_WIDTH:5 * REC_WIDTH])
    g_out[...] = (u * _sigmoid(u)).astype(g_out.dtype)


def _rec_proj(x2, mods4, w_bf, lb, *, seq_len, mod_row0, per_seq_mod):
    n_tok = x2.shape[0]
    d = D_MODEL
    tiles_per_seq = seq_len // ROW_TILE

    def mod_map(i):
        row = mod_row0 + (i // tiles_per_seq if per_seq_mod else 0)
        return (1, row, 0, 0)

    row_spec = pl.BlockSpec((ROW_TILE, REC_WIDTH), lambda i: (i, 0))
    full = lambda i: (0, 0)
    out_dtypes = (BF16, BF16, BF16, F32, F32, BF16, BF16)
    return pl.pallas_call(
        _rec_proj_kernel,
        grid=(n_tok // ROW_TILE,),
        in_specs=[
            pl.BlockSpec((ROW_TILE, d), lambda i: (i, 0)),
            pl.BlockSpec((None, None, 1, 3 * d), mod_map),
            pl.BlockSpec(w_bf.shape, full),
            pl.BlockSpec(lb.shape, full),
        ],
        out_specs=tuple(row_spec for _ in out_dtypes),
        out_shape=tuple(jax.ShapeDtypeStruct((n_tok, REC_WIDTH), t) for t in out_dtypes),
        compiler_params=pltpu.CompilerParams(vmem_limit_bytes=VMEM_LIMIT),
        name="rec_proj",
    )(x2, mods4, w_bf, lb)


def _small_level_exponents(b, backward):
    t_rows = b.shape[0]
    pos = lax.broadcasted_iota(jnp.int32, b.shape, 0)
    even = jnp.where(pos % 2 == 0, b, pltpu.roll(b, 1, 0))
    odd = jnp.where(pos % 2 == 1, b, pltpu.roll(b, t_rows - 1, 0))
    if backward:
        c1 = odd
        c2 = jnp.where(pos % 4 >= 2, even, pltpu.roll(even, t_rows - 2, 0))
    else:
        c1 = even
        c2 = jnp.where(pos % 4 < 2, odd, pltpu.roll(odd, 2, 0))
    return {1: _neg_abs(b - c1), 2: _neg_abs(b - c2)}


def _large_level_exponent(b, edges, half, backward):
    t_rows, lanes = b.shape
    g = half // SUBLANES
    n_blocks = t_rows // (2 * half)
    b5 = b.reshape(n_blocks, 2, g, SUBLANES, lanes)
    e5 = edges.reshape(n_blocks, 2, g, SUBLANES, lanes)
    centre = (e5[:, 1:2, 0:1] if backward else e5[:, 0:1, g - 1:g])
    first, second = b5[:, 0:1], b5[:, 1:2]
    if backward:
        parts = [first - centre, centre - second]
    else:
        parts = [centre - first, second - centre]
    return jnp.concatenate(parts, axis=1).reshape(t_rows, lanes)


def _neg_abs(x):
    bits = lax.bitcast_convert_type(x, jnp.int32) | jnp.int32(-2 ** 31)
    return lax.bitcast_convert_type(bits, F32)


def _block_diag(a, b):
    za = jnp.zeros(a.shape, a.dtype)
    return jnp.concatenate(
        [jnp.concatenate([a, za], axis=1), jnp.concatenate([za, b], axis=1)], axis=0)


def _pair_scores(a, c):
    return _dot_nt(a, _block_diag(c[:, :REC_DK], c[:, REC_DK:]))


def _gla_units_step(units):
    t = GLA_TILE
    dk = REC_DK
    for u in units:
        lf2 = u["lf"]() * LOG2_E
        hi = lf2.astype(BF16)
        lo = (lf2 - hi.astype(F32)).astype(BF16)
        sums = _dot(u["sums_ref"][...], jnp.concatenate([hi, lo], axis=0))
        u["b"], u["edges"], u["e4"] = sums[:t], sums[t:2 * t], sums[2 * t:]
        u["scores"] = u["masks_ref"][0] * _pair_scores(u["q"](), u["k"]()).astype(BF16)

    def level(u, li, e):
        x = jnp.exp2(e).astype(BF16)
        z = _pair_scores(u["q"]() * x, u["k"]() * x).astype(BF16)
        u["scores"] = u["scores"] + u["masks_ref"][1 + li] * z

    order = [h for h in GLA_LEVELS if h >= SUBLANES // 2] + [h for h in GLA_LEVELS if h < SUBLANES // 2]
    for half in order:
        li = GLA_LEVELS.index(half)
        for u in units:
            if half == SUBLANES // 2:
                e = u["e4"]
            elif half >= SUBLANES:
                e = _large_level_exponent(u["b"], u["edges"], half, u["backward"])
            else:
                if "small" not in u:
                    u["small"] = _small_level_exponents(u["b"], u["backward"])
                e = u["small"][half]
            level(u, li, e)

    for u in units:
        b, st_ref = u["b"], u["st_ref"]
        q, k, v = u["q"](), u["k"](), u["v"]()
        edge = b[0:1, :] if u["backward"] else b[t - 1:t, :]
        o = _dot(u["scores"], _block_diag(v[:, :dk], v[:, dk:]))
        st_a, st_b = st_ref[0], st_ref[1]
        q_in = q * jnp.exp2(b).astype(BF16)
        o = o + _dot_nt(q_in, _block_diag(st_a.astype(BF16), st_b.astype(BF16)))
        k_edge = k * jnp.exp2(edge - b).astype(BF16)
        carry = jnp.exp2(edge)
        st_ref[0] = st_a * carry[:, :dk] + _dot_tn(v[:, :dk], k_edge[:, :dk])
        st_ref[1] = st_b * carry[:, dk:] + _dot_tn(v[:, dk:], k_edge[:, dk:])
        u["store"](o)


def _gla_kernel(*refs, seq_len, has_state, want_state):
    refs = list(refs)
    q_ref, v_ref, lff_ref, lfb_ref, kf_ref, kb_ref = refs[:6]
    pos = 6
    if has_state:
        s0_ref = refs[pos]
        pos += 1
    sums_f_ref, sums_b_ref, mf_ref, mb_ref = refs[pos:pos + 4]
    pos += 4
    o_ref = refs[pos]
    pos += 1
    if want_state:
        s_out_ref = refs[pos]
        pos += 1
    st_ref = refs[pos]

    t = GLA_TILE
    n_tiles = seq_len // t
    pair_lanes = 2 * REC_DK
    o_ref[...] = jnp.zeros(o_ref.shape, o_ref.dtype)
    for direction in range(2):
        for hd in range(GLA_HEADS_PER_STEP):
            if has_state:
                st_ref[direction, hd] = s0_ref[direction, hd].T
            else:
                st_ref[direction, hd] = jnp.zeros((REC_DV, REC_DK), F32)

    def body(i, carry):
        units = []
        for direction, (lf_ref, k_ref, sums_ref, m_ref) in enumerate(
                ((lff_ref, kf_ref, sums_f_ref, mf_ref), (lfb_ref, kb_ref, sums_b_ref, mb_ref))):
            tile = i if direction == 0 else n_tiles - 1 - i
            rows = pl.ds(pl.multiple_of(tile * t, t), t)
            for pair in range(GLA_HEADS_PER_STEP // 2):
                lanes = slice(pair * pair_lanes, (pair + 1) * pair_lanes)

                def store(o, rows=rows, lanes=lanes):
                    o_ref[rows, lanes] += o

                load = lambda ref, rows=rows, lanes=lanes: (lambda: ref[rows, lanes])
                units.append(dict(
                    q=load(q_ref), k=load(k_ref), v=load(v_ref), lf=load(lf_ref),
                    st_ref=st_ref.at[direction, pl.ds(2 * pair, 2)], sums_ref=sums_ref,
                    masks_ref=m_ref, backward=direction == 1, store=store))
        _gla_units_step(units)
        return carry

    lax.fori_loop(0, n_tiles, body, 0)
    if want_state:
        for direction in range(2):
            for hd in range(GLA_HEADS_PER_STEP):
                s_out_ref[direction, hd] = st_ref[direction, hd].T


def _gla(q, v, lf_fw, lf_bw, k_fw, k_bw, s0, consts, *, n_seq, seq_len, want_state):
    has_state = s0 is not None
    width = REC_WIDTH
    hps = GLA_HEADS_PER_STEP
    seq3 = lambda a: a.reshape(n_seq, seq_len, width)
    head_spec = pl.BlockSpec((None, seq_len, hps * REC_DK), lambda b, h: (b, 0, h))
    state_spec = pl.BlockSpec((None, 2, hps, REC_DK, REC_DV), lambda b, h: (b, 0, h, 0, 0))
    in_specs = [head_spec] * 6
    args = [seq3(q), seq3(v), seq3(lf_fw), seq3(lf_bw), seq3(k_fw), seq3(k_bw)]
    if has_state:
        in_specs.append(state_spec)
        args.append(s0)
    for c in consts:
        in_specs.append(pl.BlockSpec(c.shape, lambda b, h, nd=c.ndim: (0,) * nd))
        args.append(c)
    out_shape = [jax.ShapeDtypeStruct((n_seq, seq_len, width), F32)]
    out_specs = [head_spec]
    if want_state:
        out_shape.append(jax.ShapeDtypeStruct((n_seq, 2, N_REC_HEADS, REC_DK, REC_DV), F32))
        out_specs.append(state_spec)
    res = pl.pallas_call(
        functools.partial(_gla_kernel, seq_len=seq_len, has_state=has_state, want_state=want_state),
        grid=(n_seq, N_REC_HEADS // hps),
        in_specs=in_specs,
        out_specs=tuple(out_specs),
        out_shape=tuple(out_shape),
        scratch_shapes=[pltpu.VMEM((2, hps, REC_DV, REC_DK), F32)],
        compiler_params=pltpu.CompilerParams(vmem_limit_bytes=VMEM_LIMIT),
        name="gla_state_in" if has_state else "gla_state_out",
    )(*args)
    o = res[0].reshape(n_seq * seq_len, width)
    return (o, res[1]) if want_state else (o, None)


def _rec_out_kernel(o_ref, sg_ref, x_ref, mod_ref, ng_ref, w_ref, lng_ref, lnb_ref, out_ref):
    d = D_MODEL
    parts = []
    for hd in range(N_REC_HEADS):
        oh = o_ref[:, hd * REC_DV:(hd + 1) * REC_DV]
        ms = jnp.mean(oh * oh, axis=-1, keepdims=True)
        parts.append(oh * lax.rsqrt(ms + NORM_EPS) * ng_ref[...])
    o = jnp.concatenate(parts, axis=1)
    gated = (o * sg_ref[...].astype(F32)).astype(BF16)
    branch = _dot(gated, w_ref[...])
    gate = mod_ref[...][:, 2 * d:]
    y = DEEPNORM_ALPHA * x_ref[...] + gate * branch
    out_ref[...] = _layer_norm(y, lng_ref[...], lnb_ref[...])


def _rec_out(o, sg, x2, mods4, norm_gain, w_out_bf, ln_g, ln_b, *, seq_len, mod_row0, per_seq_mod):
    n_tok = x2.shape[0]
    d = D_MODEL
    tiles_per_seq = seq_len // ROW_TILE

    def mod_map(i):
        row = mod_row0 + (i // tiles_per_seq if per_seq_mod else 0)
        return (1, row, 0, 0)

    row_spec = pl.BlockSpec((ROW_TILE, d), lambda i: (i, 0))
    full = lambda i: (0, 0)
    return pl.pallas_call(
        _rec_out_kernel,
        grid=(n_tok // ROW_TILE,),
        in_specs=[
            row_spec, row_spec, row_spec,
            pl.BlockSpec((None, None, 1, 3 * d), mod_map),
            pl.BlockSpec(norm_gain.shape, full),
            pl.BlockSpec(w_out_bf.shape, full),
            pl.BlockSpec(ln_g.shape, full),
            pl.BlockSpec(ln_b.shape, full),
        ],
        out_specs=row_spec,
        out_shape=jax.ShapeDtypeStruct((n_tok, d), F32),
        compiler_params=pltpu.CompilerParams(vmem_limit_bytes=VMEM_LIMIT),
        name="rec_out",
    )(o, sg, x2, mods4, norm_gain, w_out_bf, ln_g, ln_b)


def _rope_tables(n_tokens):
    n_rows = n_tokens // GRID_W
    rows = jnp.repeat(jnp.arange(n_rows, dtype=F32), GRID_W)
    cols = jnp.tile(jnp.arange(GRID_W, dtype=F32), n_rows)
    inv_freq = 1.0 / (ROPE_THETA ** (jnp.arange(0, AXIS_DIM, 2, dtype=F32) / AXIS_DIM))
    ang_r = rows[:, None] * inv_freq[None, :]
    ang_c = cols[:, None] * inv_freq[None, :]
    ang = jnp.concatenate([ang_r, ang_r, ang_c, ang_c], axis=-1)
    cos, sin = jnp.cos(ang), jnp.sin(ang)
    first = (jnp.arange(HEAD_DIM) % AXIS_DIM) < AXIS_DIM // 2
    sin_a = jnp.where(first[None, :], -sin, 0.0)
    sin_b = jnp.where(first[None, :], 0.0, sin)
    reps = GROUP_LANES // HEAD_DIM
    return tuple(jnp.tile(t, (1, reps)) for t in (cos, sin_a, sin_b))


def _head_mean_matrix():
    idx = jnp.arange(GROUP_LANES) // HEAD_DIM
    return jnp.where(idx[:, None] == idx[None, :], 1.0 / HEAD_DIM, 0.0).astype(BF16)


def _gla_consts():
    t = GLA_TILE
    r = jnp.arange(t)[:, None]
    c = jnp.arange(t)[None, :]
    grp = SUBLANES * (r // SUBLANES)
    mid = SUBLANES // 2
    upper = r % SUBLANES >= mid
    fw = [c <= r,
          c <= grp + SUBLANES - 1,
          jnp.where(upper, (c > grp + mid - 1) & (c <= r), (c > r) & (c <= grp + mid - 1))]
    bw = [c >= r,
          c >= grp,
          jnp.where(upper, (c >= grp + mid) & (c < r), (c >= r) & (c < grp + mid))]
    twice = lambda a: jnp.concatenate([a, a], axis=-1)
    sums_fw = twice(jnp.concatenate(fw, axis=0).astype(BF16))
    sums_bw = twice(jnp.concatenate(bw, axis=0).astype(BF16))
    masks = [(r == c)]
    for half in GLA_LEVELS:
        size = 2 * half
        masks.append((r // size == c // size) & (r % size >= half) & (c % size < half))
    m_fw = jnp.stack(masks).astype(BF16)
    m_bw = jnp.swapaxes(m_fw, 1, 2)
    return sums_fw, sums_bw, twice(m_fw), twice(m_bw)


def kernel(x_prompt, x_sample, cache_k, cache_v, state_rec, c, c_ctx, ada_w, ada_b, attn_w_in,
           attn_q_gain, attn_k_gain, attn_w_out, rec_w_in, rec_lower_bounds, rec_norm_gain,
           rec_w_out, ln_gain, ln_bias):
    d = D_MODEL
    n_p, len_p, _ = x_prompt.shape
    n_s, len_s, _ = x_sample.shape
    past = cache_k.shape[2]
    assert past == KEY_BLOCK and len_p % KEY_BLOCK == 0 and len_s % KEY_BLOCK == 0

    cond = jnp.zeros((COND_ROWS, d), F32).at[0].set(c_ctx).at[1:1 + n_s].set(c)
    mods4 = _mods(cond, ada_w, ada_b).reshape(DEPTH, COND_ROWS, 1, 3 * d)
    lb_all = _lower_bounds(rec_lower_bounds)

    xp = x_prompt.reshape(n_p * len_p, d)
    xs = x_sample.reshape(n_s * len_s, d)

    w_in = attn_w_in[0].astype(BF16)
    w_out = attn_w_out[0].astype(BF16)
    reps = GROUP_LANES // HEAD_DIM
    qg = (jnp.tile(attn_q_gain[0], reps) * (LOG2_E / math.sqrt(HEAD_DIM))).reshape(1, GROUP_LANES)
    kg = jnp.tile(attn_k_gain[0], reps).reshape(1, GROUP_LANES)
    pn = _head_mean_matrix()
    ln_g = ln_gain[0].reshape(1, d)
    ln_b = ln_bias[0].reshape(1, d)

    q_p, kp_p, vt_p, g_p, k_p, v_p = _attn_proj(xp, mods4, w_in, pn, qg, kg, None, seq_len=len_p,
                                                mod_row0=0, per_seq_mod=False, cache_out=True)
    q_s, kp_s, vt_s, g_s = _attn_proj(xs, mods4, w_in, pn, qg, kg, _rope_tables(len_s), seq_len=len_s,
                                      mod_row0=1, per_seq_mod=True, cache_out=False)
    xp1 = _attn_core(q_p, kp_p, vt_p, None, g_p, xp, mods4, w_out, ln_g, ln_b, n_seq=n_p,
                     seq_len=len_p, q_tile=len_p, mod_row0=0, per_seq_mod=False)
    cache = (cache_k[:, 0].reshape(n_s, past, KV_WIDTH), cache_v[:, 0].reshape(n_s, past, KV_WIDTH))
    xs1 = _attn_core(q_s, kp_s, vt_s, cache, g_s, xs, mods4, w_out, ln_g, ln_b, n_seq=n_s,
                     seq_len=len_s, q_tile=ROW_TILE, mod_row0=1, per_seq_mod=True)
    new_cache_k = k_p.reshape(n_p, 1, len_p, N_KV_HEADS, HEAD_DIM)
    new_cache_v = v_p.reshape(n_p, 1, len_p, N_KV_HEADS, HEAD_DIM)

    rw_in = rec_w_in[0].astype(BF16)
    rw_out = rec_w_out[0].astype(BF16)
    lb = lb_all[1]
    ng = rec_norm_gain[0].reshape(1, REC_DV)
    ln_g = ln_gain[1].reshape(1, d)
    ln_b = ln_bias[1].reshape(1, d)
    consts = _gla_consts()

    outs = []
    states = None
    for x1, n_seq, seq_len, row0, per_seq, s0 in (
            (xp1, n_p, len_p, 0, False, None),
            (xs1, n_s, len_s, 1, True, state_rec[:, 0])):
        q, v, g, lf_fw, lf_bw, k_fw, k_bw = _rec_proj(x1, mods4, rw_in, lb, seq_len=seq_len,
                                                       mod_row0=row0, per_seq_mod=per_seq)
        o, st = _gla(q, v, lf_fw, lf_bw, k_fw, k_bw, s0, consts, n_seq=n_seq, seq_len=seq_len,
                     want_state=s0 is None)
        if st is not None:
            states = st
        outs.append(_rec_out(o, g, x1, mods4, ng, rw_out, ln_g, ln_b, seq_len=seq_len,
                             mod_row0=row0, per_seq_mod=per_seq))

    y_prompt = outs[0].reshape(n_p, len_p, d)
    y_sample = outs[1].reshape(n_s, len_s, d)
    new_state_rec = states.reshape(n_p, 1, 2, N_REC_HEADS, REC_DK, REC_DV)
    return (y_prompt, y_sample, new_cache_k, new_cache_v, new_state_rec)
```

```python
import functools
import math

import jax
import jax.numpy as jnp
from jax import lax
from jax.experimental import pallas as pl
from jax.experimental.pallas import tpu as pltpu

F32 = jnp.float32
BF16 = jnp.bfloat16

D_MODEL = 1024
DEPTH = 2
GRID_W = 64
N_HEADS = 16
N_KV_HEADS = 4
HEAD_DIM = 64
AXIS_DIM = HEAD_DIM // 2
ATTN_WIDTH = N_HEADS * HEAD_DIM
KV_WIDTH = N_KV_HEADS * HEAD_DIM
ROPE_THETA = 10000.0
N_REC_HEADS = 8
REC_DK = 128
REC_DV = 128
REC_WIDTH = N_REC_HEADS * REC_DK
NORM_EPS = 1e-6
LN_EPS = 1e-5
DEEPNORM_ALPHA = (2.0 * DEPTH) ** 0.25

SUBLANES = 8
COND_ROWS = SUBLANES
ROW_TILE = 256
GROUP_LANES = 256
SLAB_LANES = 128
KPAD_WIDTH = 2 * N_KV_HEADS * SLAB_LANES
KEY_BLOCK = 256
REC_PROJ_SLAB = 256
GLA_TILE = 128
GLA_LEVELS = (1, 2, 4, 8, 16, 32, 64)
GLA_HEADS_PER_STEP = 4
LOG2_E = 1.4426950408889634
VMEM_LIMIT = 56 * 1024 * 1024


def _sigmoid(x):
    return 1.0 / (1.0 + jnp.exp(-x))


def _dot(a, b):
    return jnp.dot(a, b, preferred_element_type=F32)


def _dot_nt(a, b):
    return lax.dot_general(a, b, (((1,), (1,)), ((), ())), preferred_element_type=F32)


def _dot_tn(a, b):
    return lax.dot_general(a, b, (((0,), (0,)), ((), ())), preferred_element_type=F32)


def _layer_norm(y, g, b):
    mu = jnp.mean(y, axis=-1, keepdims=True)
    yc = y - mu
    var = jnp.mean(yc * yc, axis=-1, keepdims=True)
    return yc * lax.rsqrt(var + LN_EPS) * g + b


def _mods_kernel(cond_ref, w_ref, b_ref, out_ref):
    c = cond_ref[...]
    s = (c * _sigmoid(c)).astype(BF16)
    out_ref[...] = _dot(s, w_ref[...].astype(BF16)) + b_ref[...]


def _mods(cond, ada_w, ada_b):
    d = D_MODEL
    return pl.pallas_call(
        _mods_kernel,
        grid=(DEPTH, 3),
        in_specs=[
            pl.BlockSpec((COND_ROWS, d), lambda l, j: (0, 0)),
            pl.BlockSpec((None, d, d), lambda l, j: (l, 0, j)),
            pl.BlockSpec((None, 1, d), lambda l, j: (l, 0, j)),
        ],
        out_specs=pl.BlockSpec((None, COND_ROWS, d), lambda l, j: (l, 0, j)),
        out_shape=jax.ShapeDtypeStruct((DEPTH, COND_ROWS, 3 * d), F32),
        compiler_params=pltpu.CompilerParams(vmem_limit_bytes=VMEM_LIMIT),
        name="adaln_mods",
    )(cond, ada_w, ada_b.reshape(DEPTH, 1, 3 * d))


def _lower_bounds_kernel(r_ref, out_ref):
    r = [r_ref[i] for i in range(DEPTH)]
    m = functools.reduce(jnp.maximum, r)
    e = [jnp.exp(x - m) for x in r]
    tot = functools.reduce(lambda a, b: a + b, e)
    soft = [x / tot for x in e]
    acc = soft[0]
    for i in range(DEPTH):
        if i > 0:
            acc = acc + soft[i]
        out_ref[i] = acc - soft[0]


def _lower_bounds(rec_lower_bounds):
    return pl.pallas_call(
        _lower_bounds_kernel,
        out_shape=jax.ShapeDtypeStruct(rec_lower_bounds.shape, F32),
        name="rec_lower_bounds",
    )(rec_lower_bounds)


def _kv_head_slabs(k):
    lane = lax.broadcasted_iota(jnp.int32, (k.shape[0], SLAB_LANES), 1)
    low = lane < HEAD_DIM
    heads_per_slab = SLAB_LANES // HEAD_DIM
    out = []
    for j in range(N_KV_HEADS):
        tile = k[:, (j // heads_per_slab) * SLAB_LANES:(j // heads_per_slab + 1) * SLAB_LANES]
        moved = pltpu.roll(tile, HEAD_DIM, 1)
        at_low, at_high = (tile, moved) if j % heads_per_slab == 0 else (moved, tile)
        out.append(jnp.where(low, at_low, 0.0))
        out.append(jnp.where(low, 0.0, at_high))
    return jnp.concatenate(out, axis=1)


def _attn_proj_kernel(*refs, rope, cache_out):
    refs = list(refs)
    x_ref, mod_ref, w_ref, pn_ref, qg_ref, kg_ref = refs[:6]
    pos = 6
    if rope:
        cos_ref, sa_ref, sb_ref = refs[pos:pos + 3]
        pos += 3
    q_out, kp_out, vt_out, g_out = refs[pos:pos + 4]
    pos += 4
    if cache_out:
        k_out, v_out = refs[pos:pos + 2]
    d = D_MODEL
    mod = mod_ref[...]
    shift, scale = mod[:, :d], mod[:, d:2 * d]
    h = (x_ref[...] * (1.0 + scale) + shift).astype(BF16)
    pn = pn_ref[...]

    def norm_rope(u, gain):
        ms = _dot((u * u).astype(BF16), pn)
        y = u * lax.rsqrt(ms + NORM_EPS) * gain
        if rope:
            y = (y * cos_ref[...]
                 + pltpu.roll(y, GROUP_LANES - AXIS_DIM // 2, 1) * sa_ref[...]
                 + pltpu.roll(y, AXIS_DIM // 2, 1) * sb_ref[...])
        return y

    def finish_q(j, u):
        q_out[:, j * GROUP_LANES:(j + 1) * GROUP_LANES] = norm_rope(u, qg_ref[...]).astype(q_out.dtype)

    def finish_k(_, u):
        k = norm_rope(u, kg_ref[...])
        kp_out[...] = _kv_head_slabs(k).astype(kp_out.dtype)
        if cache_out:
            k_out[...] = k

    def finish_v(_, v):
        vt_out[...] = v.T.astype(vt_out.dtype)
        if cache_out:
            v_out[...] = v

    def finish_g(j, u):
        g_out[:, j * GROUP_LANES:(j + 1) * GROUP_LANES] = (u * _sigmoid(u)).astype(g_out.dtype)

    slabs = ([(finish_q, j) for j in range(N_KV_HEADS)] + [(finish_k, 0), (finish_v, 0)]
             + [(finish_g, j) for j in range(N_KV_HEADS)])
    project = lambda i: _dot(h, w_ref[:, i * GROUP_LANES:(i + 1) * GROUP_LANES])
    u_next = project(0)
    for i, (finish, j) in enumerate(slabs):
        u = u_next
        if i + 1 < len(slabs):
            u_next = project(i + 1)
        finish(j, u)


def _attn_proj(x2, mods4, w_bf, pn, qg, kg, rope_tabs, *, seq_len, mod_row0, per_seq_mod, cache_out):
    n_tok = x2.shape[0]
    d = D_MODEL
    tiles_per_seq = seq_len // ROW_TILE
    rope = rope_tabs is not None

    def mod_map(i):
        row = mod_row0 + (i // tiles_per_seq if per_seq_mod else 0)
        return (0, row, 0, 0)

    full = lambda i: (0, 0)
    in_specs = [
        pl.BlockSpec((ROW_TILE, d), lambda i: (i, 0)),
        pl.BlockSpec((None, None, 1, 3 * d), mod_map),
        pl.BlockSpec(w_bf.shape, full),
        pl.BlockSpec(pn.shape, full),
        pl.BlockSpec(qg.shape, full),
        pl.BlockSpec(kg.shape, full),
    ]
    args = [x2, mods4, w_bf, pn, qg, kg]
    if rope:
        for t in rope_tabs:
            in_specs.append(pl.BlockSpec((ROW_TILE, GROUP_LANES), lambda i: (i % tiles_per_seq, 0)))
            args.append(t)
    rows = lambda width: pl.BlockSpec((ROW_TILE, width), lambda i: (i, 0))
    out_shape = [
        jax.ShapeDtypeStruct((n_tok, ATTN_WIDTH), BF16),
        jax.ShapeDtypeStruct((n_tok, KPAD_WIDTH), BF16),
        jax.ShapeDtypeStruct((n_tok // seq_len, KV_WIDTH, seq_len), BF16),
        jax.ShapeDtypeStruct((n_tok, ATTN_WIDTH), BF16),
    ]
    out_specs = [
        rows(ATTN_WIDTH),
        rows(KPAD_WIDTH),
        pl.BlockSpec((None, KV_WIDTH, ROW_TILE), lambda i: (i // tiles_per_seq, 0, i % tiles_per_seq)),
        rows(ATTN_WIDTH),
    ]
    if cache_out:
        out_shape += [jax.ShapeDtypeStruct((n_tok, KV_WIDTH), F32)] * 2
        out_specs += [rows(KV_WIDTH)] * 2
    return pl.pallas_call(
        functools.partial(_attn_proj_kernel, rope=rope, cache_out=cache_out),
        grid=(n_tok // ROW_TILE,),
        in_specs=in_specs,
        out_specs=tuple(out_specs),
        out_shape=tuple(out_shape),
        compiler_params=pltpu.CompilerParams(vmem_limit_bytes=VMEM_LIMIT),
        name="attn_proj_rope" if rope else "attn_proj",
    )(*args)


def _attn_core_kernel(*refs, n_new_blocks, has_cache):
    refs = list(refs)
    q_ref, kp_ref, vt_ref = refs[:3]
    pos = 3
    if has_cache:
        kc_ref, vc_ref = refs[pos:pos + 2]
        pos += 2
    sg_ref, x_ref, mod_ref, w_ref, lng_ref, lnb_ref, out_ref, s_scr, ot_scr = refs[pos:]
    d = D_MODEL
    tq = q_ref.shape[0]
    kb = KEY_BLOCK
    group = N_HEADS // N_KV_HEADS
    heads_per_slab = SLAB_LANES // HEAD_DIM

    blocks = []
    if has_cache:
        kc_slabs = _kv_head_slabs(kc_ref[...]).astype(BF16)
        vc_t = vc_ref[...].T.astype(BF16)
        blocks.append((lambda s: kc_slabs[:, s * SLAB_LANES:(s + 1) * SLAB_LANES],
                       lambda j: vc_t[j * HEAD_DIM:(j + 1) * HEAD_DIM, :]))
    for blk in range(n_new_blocks):
        blocks.append((lambda s, blk=blk: kp_ref[blk * kb:(blk + 1) * kb, s * SLAB_LANES:(s + 1) * SLAB_LANES],
                       lambda j, blk=blk: vt_ref[j * HEAD_DIM:(j + 1) * HEAD_DIM, blk * kb:(blk + 1) * kb]))

    def fold_rows(a, op):
        return functools.reduce(op, [a[r * SUBLANES:(r + 1) * SUBLANES, :] for r in range(kb // SUBLANES)])

    def scores_phase(j):
        slot = j % 2
        maxima = []
        for hh in range(group):
            hd = j * group + hh
            slab, where = hd // heads_per_slab, hd % heads_per_slab
            q_slab = q_ref[:, slab * SLAB_LANES:(slab + 1) * SLAB_LANES]
            mx = None
            for bi, (keys, _) in enumerate(blocks):
                s = _dot_nt(keys(heads_per_slab * j + where), q_slab)
                s_scr[slot, hh, bi * kb:(bi + 1) * kb, :] = s
                r = fold_rows(s, jnp.maximum)
                mx = r if mx is None else jnp.maximum(mx, r)
            maxima.append(jnp.max(mx, axis=0, keepdims=True))
        return maxima

    def values_phase(j, maxima):
        slot = j % 2
        acc = None
        sums = [None] * group
        for bi, (_, values) in enumerate(blocks):
            ps = []
            for hh in range(group):
                p = jnp.exp2(s_scr[slot, hh, bi * kb:(bi + 1) * kb, :] - maxima[hh])
                r = fold_rows(p, lambda a, b: a + b)
                sums[hh] = r if sums[hh] is None else sums[hh] + r
                ps.append(p.astype(BF16))
            part = _dot(values(j), jnp.concatenate(ps, axis=1))
            acc = part if acc is None else acc + part
        for hh in range(group):
            hd = j * group + hh
            l = jnp.sum(sums[hh], axis=0, keepdims=True)
            ot_scr[hd * HEAD_DIM:(hd + 1) * HEAD_DIM, :] = acc[:, hh * tq:(hh + 1) * tq] * (1.0 / l)

    pending = scores_phase(0)
    for j in range(N_KV_HEADS):
        nxt = scores_phase(j + 1) if j + 1 < N_KV_HEADS else None
        values_phase(j, pending)
        pending = nxt

    o = ot_scr[...].T
    gated = (o * sg_ref[...].astype(F32)).astype(BF16)
    branch = _dot(gated, w_ref[...])
    gate = mod_ref[...][:, 2 * d:]
    y = DEEPNORM_ALPHA * x_ref[...] + gate * branch
    out_ref[...] = _layer_norm(y, lng_ref[...], lnb_ref[...])


def _attn_core(q, kp, vt, cache, sg, x2, mods4, w_out_bf, ln_g, ln_b, *, n_seq, seq_len, q_tile,
               mod_row0, per_seq_mod):
    d = D_MODEL
    tiles = seq_len // q_tile
    has_cache = cache is not None
    q3 = q.reshape(n_seq, seq_len, ATTN_WIDTH)
    kp3 = kp.reshape(n_seq, seq_len, KPAD_WIDTH)
    sg3 = sg.reshape(n_seq, seq_len, ATTN_WIDTH)
    x3 = x2.reshape(n_seq, seq_len, d)
    n_keys = seq_len + (cache[0].shape[1] if has_cache else 0)

    def mod_map(b, i):
        return (0, mod_row0 + (b if per_seq_mod else 0), 0, 0)

    tile_spec = lambda width: pl.BlockSpec((None, q_tile, width), lambda b, i: (b, i, 0))
    seq_spec = lambda rows, width: pl.BlockSpec((None, rows, width), lambda b, i: (b, 0, 0))
    full2 = lambda b, i: (0, 0)
    in_specs = [tile_spec(ATTN_WIDTH), seq_spec(seq_len, KPAD_WIDTH), seq_spec(KV_WIDTH, seq_len)]
    args = [q3, kp3, vt]
    if has_cache:
        ck, cv = cache
        in_specs += [seq_spec(ck.shape[1], KV_WIDTH), seq_spec(cv.shape[1], KV_WIDTH)]
        args += [ck, cv]
    in_specs += [
        tile_spec(ATTN_WIDTH),
        tile_spec(d),
        pl.BlockSpec((None, None, 1, 3 * d), mod_map),
        pl.BlockSpec(w_out_bf.shape, full2),
        pl.BlockSpec(ln_g.shape, full2),
        pl.BlockSpec(ln_b.shape, full2),
    ]
    args += [sg3, x3, mods4, w_out_bf, ln_g, ln_b]
    out = pl.pallas_call(
        functools.partial(_attn_core_kernel, n_new_blocks=seq_len // KEY_BLOCK, has_cache=has_cache),
        grid=(n_seq, tiles),
        in_specs=in_specs,
        out_specs=tile_spec(d),
        out_shape=jax.ShapeDtypeStruct((n_seq, seq_len, d), F32),
        scratch_shapes=[
            pltpu.VMEM((2, N_HEADS // N_KV_HEADS, n_keys, q_tile), F32),
            pltpu.VMEM((ATTN_WIDTH, q_tile), F32),
        ],
        compiler_params=pltpu.CompilerParams(vmem_limit_bytes=VMEM_LIMIT),
        name="attn_core_cache" if has_cache else "attn_core",
    )(*args)
    return out.reshape(n_seq * seq_len, d)


def _rec_proj_kernel(x_ref, mod_ref, w_ref, lb_ref, q_out, v_out, g_out, lf_fw, lf_bw, k_fw, k_bw):
    d = D_MODEL
    mod = mod_ref[...]
    shift, scale = mod[:, :d], mod[:, d:2 * d]
    h = (x_ref[...] * (1.0 + scale) + shift).astype(BF16)
    slab = REC_PROJ_SLAB

    def finish_silu(out_ref):
        def finish(cols, u):
            out_ref[:, cols] = (u * _sigmoid(u)).astype(out_ref.dtype)
        return finish

    def finish_gate(direction, lf_out, k_out):
        def finish(cols, z):
            lb = lb_ref[direction:direction + 1, cols]
            sig = _sigmoid(z)
            lf_out[:, cols] = jnp.log(lb + (1.0 - lb) * sig)
            k_out[:, cols] = ((1.0 - lb) * (1.0 - sig)).astype(k_out.dtype)
        return finish

    def finish_v(cols, u):
        v_out[:, cols] = u.astype(v_out.dtype)

    sections = (finish_silu(q_out), finish_gate(0, lf_fw, k_fw), finish_gate(1, lf_bw, k_bw),
                finish_v, finish_silu(g_out))
    per_section = REC_WIDTH // slab
    n_slabs = len(sections) * per_section
    project = lambda i: _dot(h, w_ref[:, i * slab:(i + 1) * slab])
    u_next = project(0)
    for i in range(n_slabs):
        u = u_next
        if i + 1 < n_slabs:
            u_next = project(i + 1)
        within = i % per_section
        sections[i // per_section](slice(within * slab, (within + 1) * slab), u)


def _rec_proj(x2, mods4, w_bf, lb, *, seq_len, mod_row0, per_seq_mod):
    n_tok = x2.shape[0]
    d = D_MODEL
    tiles_per_seq = seq_len // ROW_TILE

    def mod_map(i):
        row = mod_row0 + (i // tiles_per_seq if per_seq_mod else 0)
        return (1, row, 0, 0)

    row_spec = pl.BlockSpec((ROW_TILE, REC_WIDTH), lambda i: (i, 0))
    full = lambda i: (0, 0)
    out_dtypes = (BF16, BF16, BF16, F32, F32, BF16, BF16)
    return pl.pallas_call(
        _rec_proj_kernel,
        grid=(n_tok // ROW_TILE,),
        in_specs=[
            pl.BlockSpec((ROW_TILE, d), lambda i: (i, 0)),
            pl.BlockSpec((None, None, 1, 3 * d), mod_map),
            pl.BlockSpec(w_bf.shape, full),
            pl.BlockSpec(lb.shape, full),
        ],
        out_specs=tuple(row_spec for _ in out_dtypes),
        out_shape=tuple(jax.ShapeDtypeStruct((n_tok, REC_WIDTH), t) for t in out_dtypes),
        compiler_params=pltpu.CompilerParams(vmem_limit_bytes=VMEM_LIMIT),
        name="rec_proj",
    )(x2, mods4, w_bf, lb)


def _small_level_exponents(b, backward):
    t_rows = b.shape[0]
    pos = lax.broadcasted_iota(jnp.int32, b.shape, 0)
    even = jnp.where(pos % 2 == 0, b, pltpu.roll(b, 1, 0))
    odd = jnp.where(pos % 2 == 1, b, pltpu.roll(b, t_rows - 1, 0))
    if backward:
        c1 = odd
        c2 = jnp.where(pos % 4 >= 2, even, pltpu.roll(even, t_rows - 2, 0))
    else:
        c1 = even
        c2 = jnp.where(pos % 4 < 2, odd, pltpu.roll(odd, 2, 0))
    return {1: _neg_abs(b - c1), 2: _neg_abs(b - c2)}


def _large_level_exponent(b, edges, half, backward):
    t_rows, lanes = b.shape
    g = half // SUBLANES
    n_blocks = t_rows // (2 * half)
    b5 = b.reshape(n_blocks, 2, g, SUBLANES, lanes)
    e5 = edges.reshape(n_blocks, 2, g, SUBLANES, lanes)
    centre = (e5[:, 1:2, 0:1] if backward else e5[:, 0:1, g - 1:g])
    first, second = b5[:, 0:1], b5[:, 1:2]
    if backward:
        parts = [first - centre, centre - second]
    else:
        parts = [centre - first, second - centre]
    return jnp.concatenate(parts, axis=1).reshape(t_rows, lanes)


def _neg_abs(x):
    bits = lax.bitcast_convert_type(x, jnp.int32) | jnp.int32(-2 ** 31)
    return lax.bitcast_convert_type(bits, F32)


def _block_diag(a, b):
    za = jnp.zeros(a.shape, a.dtype)
    return jnp.concatenate(
        [jnp.concatenate([a, za], axis=1), jnp.concatenate([za, b], axis=1)], axis=0)


def _pair_scores(a, c):
    return _dot_nt(a, _block_diag(c[:, :REC_DK], c[:, REC_DK:]))


def _gla_units_step(units):
    t = GLA_TILE
    dk = REC_DK
    for u in units:
        lf2 = u["lf"]() * LOG2_E
        hi = lf2.astype(BF16)
        lo = (lf2 - hi.astype(F32)).astype(BF16)
        sums = _dot(u["sums_ref"][...], jnp.concatenate([hi, lo], axis=0))
        u["b"], u["edges"], u["e4"] = sums[:t], sums[t:2 * t], sums[2 * t:]
        u["scores"] = u["masks_ref"][0] * _pair_scores(u["q"](), u["k"]()).astype(BF16)

    def level(u, li, e):
        x = jnp.exp2(e).astype(BF16)
        z = _pair_scores(u["q"]() * x, u["k"]() * x).astype(BF16)
        u["scores"] = u["scores"] + u["masks_ref"][1 + li] * z

    order = [h for h in GLA_LEVELS if h >= SUBLANES // 2] + [h for h in GLA_LEVELS if h < SUBLANES // 2]
    for half in order:
        li = GLA_LEVELS.index(half)
        for u in units:
            if half == SUBLANES // 2:
                e = u["e4"]
            elif half >= SUBLANES:
                e = _large_level_exponent(u["b"], u["edges"], half, u["backward"])
            else:
                if "small" not in u:
                    u["small"] = _small_level_exponents(u["b"], u["backward"])
                e = u["small"][half]
            level(u, li, e)

    for u in units:
        b, st_ref = u["b"], u["st_ref"]
        q, k, v = u["q"](), u["k"](), u["v"]()
        edge = b[0:1, :] if u["backward"] else b[t - 1:t, :]
        o = _dot(u["scores"], _block_diag(v[:, :dk], v[:, dk:]))
        st_a, st_b = st_ref[0], st_ref[1]
        q_in = q * jnp.exp2(b).astype(BF16)
        o = o + _dot_nt(q_in, _block_diag(st_a.astype(BF16), st_b.astype(BF16)))
        k_edge = k * jnp.exp2(edge - b).astype(BF16)
        carry = jnp.exp2(edge)
        st_ref[0] = st_a * carry[:, :dk] + _dot_tn(v[:, :dk], k_edge[:, :dk])
        st_ref[1] = st_b * carry[:, dk:] + _dot_tn(v[:, dk:], k_edge[:, dk:])
        u["store"](o)


def _gla_kernel(*refs, seq_len, has_state, want_state):
    refs = list(refs)
    q_ref, v_ref, lff_ref, lfb_ref, kf_ref, kb_ref = refs[:6]
    pos = 6
    if has_state:
        s0_ref = refs[pos]
        pos += 1
    sums_f_ref, sums_b_ref, mf_ref, mb_ref = refs[pos:pos + 4]
    pos += 4
    o_ref = refs[pos]
    pos += 1
    if want_state:
        s_out_ref = refs[pos]
        pos += 1
    st_ref = refs[pos]

    t = GLA_TILE
    n_tiles = seq_len // t
    pair_lanes = 2 * REC_DK
    o_ref[...] = jnp.zeros(o_ref.shape, o_ref.dtype)
    for direction in range(2):
        for hd in range(GLA_HEADS_PER_STEP):
            if has_state:
                st_ref[direction, hd] = s0_ref[direction, hd].T
            else:
                st_ref[direction, hd] = jnp.zeros((REC_DV, REC_DK), F32)

    def body(i, carry):
        units = []
        for direction, (lf_ref, k_ref, sums_ref, m_ref) in enumerate(
                ((lff_ref, kf_ref, sums_f_ref, mf_ref), (lfb_ref, kb_ref, sums_b_ref, mb_ref))):
            tile = i if direction == 0 else n_tiles - 1 - i
            rows = pl.ds(pl.multiple_of(tile * t, t), t)
            for pair in range(GLA_HEADS_PER_STEP // 2):
                lanes = slice(pair * pair_lanes, (pair + 1) * pair_lanes)

                def store(o, rows=rows, lanes=lanes):
                    o_ref[rows, lanes] += o

                load = lambda ref, rows=rows, lanes=lanes: (lambda: ref[rows, lanes])
                units.append(dict(
                    q=load(q_ref), k=load(k_ref), v=load(v_ref), lf=load(lf_ref),
                    st_ref=st_ref.at[direction, pl.ds(2 * pair, 2)], sums_ref=sums_ref,
                    masks_ref=m_ref, backward=direction == 1, store=store))
        _gla_units_step(units)
        return carry

    lax.fori_loop(0, n_tiles, body, 0)
    if want_state:
        for direction in range(2):
            for hd in range(GLA_HEADS_PER_STEP):
                s_out_ref[direction, hd] = st_ref[direction, hd].T


def _gla(q, v, lf_fw, lf_bw, k_fw, k_bw, s0, consts, *, n_seq, seq_len, want_state):
    has_state = s0 is not None
    width = REC_WIDTH
    hps = GLA_HEADS_PER_STEP
    seq3 = lambda a: a.reshape(n_seq, seq_len, width)
    head_spec = pl.BlockSpec((None, seq_len, hps * REC_DK), lambda b, h: (b, 0, h))
    state_spec = pl.BlockSpec((None, 2, hps, REC_DK, REC_DV), lambda b, h: (b, 0, h, 0, 0))
    in_specs = [head_spec] * 6
    args = [seq3(q), seq3(v), seq3(lf_fw), seq3(lf_bw), seq3(k_fw), seq3(k_bw)]
    if has_state:
        in_specs.append(state_spec)
        args.append(s0)
    for c in consts:
        in_specs.append(pl.BlockSpec(c.shape, lambda b, h, nd=c.ndim: (0,) * nd))
        args.append(c)
    out_shape = [jax.ShapeDtypeStruct((n_seq, seq_len, width), F32)]
    out_specs = [head_spec]
    if want_state:
        out_shape.append(jax.ShapeDtypeStruct((n_seq, 2, N_REC_HEADS, REC_DK, REC_DV), F32))
        out_specs.append(state_spec)
    res = pl.pallas_call(
        functools.partial(_gla_kernel, seq_len=seq_len, has_state=has_state, want_state=want_state),
        grid=(n_seq, N_REC_HEADS // hps),
        in_specs=in_specs,
        out_specs=tuple(out_specs),
        out_shape=tuple(out_shape),
        scratch_shapes=[pltpu.VMEM((2, hps, REC_DV, REC_DK), F32)],
        compiler_params=pltpu.CompilerParams(vmem_limit_bytes=VMEM_LIMIT),
        name="gla_state_in" if has_state else "gla_state_out",
    )(*args)
    o = res[0].reshape(n_seq * seq_len, width)
    return (o, res[1]) if want_state else (o, None)


def _rec_out_kernel(o_ref, sg_ref, x_ref, mod_ref, ng_ref, w_ref, lng_ref, lnb_ref, out_ref):
    d = D_MODEL
    parts = []
    for hd in range(N_REC_HEADS):
        oh = o_ref[:, hd * REC_DV:(hd + 1) * REC_DV]
        ms = jnp.mean(oh * oh, axis=-1, keepdims=True)
        parts.append(oh * lax.rsqrt(ms + NORM_EPS) * ng_ref[...])
    o = jnp.concatenate(parts, axis=1)
    gated = (o * sg_ref[...].astype(F32)).astype(BF16)
    branch = _dot(gated, w_ref[...])
    gate = mod_ref[...][:, 2 * d:]
    y = DEEPNORM_ALPHA * x_ref[...] + gate * branch
    out_ref[...] = _layer_norm(y, lng_ref[...], lnb_ref[...])


def _rec_out(o, sg, x2, mods4, norm_gain, w_out_bf, ln_g, ln_b, *, seq_len, mod_row0, per_seq_mod):
    n_tok = x2.shape[0]
    d = D_MODEL
    tiles_per_seq = seq_len // ROW_TILE

    def mod_map(i):
        row = mod_row0 + (i // tiles_per_seq if per_seq_mod else 0)
        return (1, row, 0, 0)

    row_spec = pl.BlockSpec((ROW_TILE, d), lambda i: (i, 0))
    full = lambda i: (0, 0)
    return pl.pallas_call(
        _rec_out_kernel,
        grid=(n_tok // ROW_TILE,),
        in_specs=[
            row_spec, row_spec, row_spec,
            pl.BlockSpec((None, None, 1, 3 * d), mod_map),
            pl.BlockSpec(norm_gain.shape, full),
            pl.BlockSpec(w_out_bf.shape, full),
            pl.BlockSpec(ln_g.shape, full),
            pl.BlockSpec(ln_b.shape, full),
        ],
        out_specs=row_spec,
        out_shape=jax.ShapeDtypeStruct((n_tok, d), F32),
        compiler_params=pltpu.CompilerParams(vmem_limit_bytes=VMEM_LIMIT),
        name="rec_out",
    )(o, sg, x2, mods4, norm_gain, w_out_bf, ln_g, ln_b)


def _rope_tables(n_tokens):
    n_rows = n_tokens // GRID_W
    rows = jnp.repeat(jnp.arange(n_rows, dtype=F32), GRID_W)
    cols = jnp.tile(jnp.arange(GRID_W, dtype=F32), n_rows)
    inv_freq = 1.0 / (ROPE_THETA ** (jnp.arange(0, AXIS_DIM, 2, dtype=F32) / AXIS_DIM))
    ang_r = rows[:, None] * inv_freq[None, :]
    ang_c = cols[:, None] * inv_freq[None, :]
    ang = jnp.concatenate([ang_r, ang_r, ang_c, ang_c], axis=-1)
    cos, sin = jnp.cos(ang), jnp.sin(ang)
    first = (jnp.arange(HEAD_DIM) % AXIS_DIM) < AXIS_DIM // 2
    sin_a = jnp.where(first[None, :], -sin, 0.0)
    sin_b = jnp.where(first[None, :], 0.0, sin)
    reps = GROUP_LANES // HEAD_DIM
    return tuple(jnp.tile(t, (1, reps)) for t in (cos, sin_a, sin_b))


def _head_mean_matrix():
    idx = jnp.arange(GROUP_LANES) // HEAD_DIM
    return jnp.where(idx[:, None] == idx[None, :], 1.0 / HEAD_DIM, 0.0).astype(BF16)


def _gla_consts():
    t = GLA_TILE
    r = jnp.arange(t)[:, None]
    c = jnp.arange(t)[None, :]
    grp = SUBLANES * (r // SUBLANES)
    mid = SUBLANES // 2
    upper = r % SUBLANES >= mid
    fw = [c <= r,
          c <= grp + SUBLANES - 1,
          jnp.where(upper, (c > grp + mid - 1) & (c <= r), (c > r) & (c <= grp + mid - 1))]
    bw = [c >= r,
          c >= grp,
          jnp.where(upper, (c >= grp + mid) & (c < r), (c >= r) & (c < grp + mid))]
    twice = lambda a: jnp.concatenate([a, a], axis=-1)
    sums_fw = twice(jnp.concatenate(fw, axis=0).astype(BF16))
    sums_bw = twice(jnp.concatenate(bw, axis=0).astype(BF16))
    masks = [(r == c)]
    for half in GLA_LEVELS:
        size = 2 * half
        masks.append((r // size == c // size) & (r % size >= half) & (c % size < half))
    m_fw = jnp.stack(masks).astype(BF16)
    m_bw = jnp.swapaxes(m_fw, 1, 2)
    return sums_fw, sums_bw, twice(m_fw), twice(m_bw)


def kernel(x_prompt, x_sample, cache_k, cache_v, state_rec, c, c_ctx, ada_w, ada_b, attn_w_in,
           attn_q_gain, attn_k_gain, attn_w_out, rec_w_in, rec_lower_bounds, rec_norm_gain,
           rec_w_out, ln_gain, ln_bias):
    d = D_MODEL
    n_p, len_p, _ = x_prompt.shape
    n_s, len_s, _ = x_sample.shape
    past = cache_k.shape[2]
    assert past == KEY_BLOCK and len_p % KEY_BLOCK == 0 and len_s % KEY_BLOCK == 0

    cond = jnp.zeros((COND_ROWS, d), F32).at[0].set(c_ctx).at[1:1 + n_s].set(c)
    mods4 = _mods(cond, ada_w, ada_b).reshape(DEPTH, COND_ROWS, 1, 3 * d)
    lb_all = _lower_bounds(rec_lower_bounds)

    xp = x_prompt.reshape(n_p * len_p, d)
    xs = x_sample.reshape(n_s * len_s, d)

    w_in = attn_w_in[0].astype(BF16)
    w_out = attn_w_out[0].astype(BF16)
    reps = GROUP_LANES // HEAD_DIM
    qg = (jnp.tile(attn_q_gain[0], reps) * (LOG2_E / math.sqrt(HEAD_DIM))).reshape(1, GROUP_LANES)
    kg = jnp.tile(attn_k_gain[0], reps).reshape(1, GROUP_LANES)
    pn = _head_mean_matrix()
    ln_g = ln_gain[0].reshape(1, d)
    ln_b = ln_bias[0].reshape(1, d)

    q_p, kp_p, vt_p, g_p, k_p, v_p = _attn_proj(xp, mods4, w_in, pn, qg, kg, None, seq_len=len_p,
                                                mod_row0=0, per_seq_mod=False, cache_out=True)
    q_s, kp_s, vt_s, g_s = _attn_proj(xs, mods4, w_in, pn, qg, kg, _rope_tables(len_s), seq_len=len_s,
                                      mod_row0=1, per_seq_mod=True, cache_out=False)
    xp1 = _attn_core(q_p, kp_p, vt_p, None, g_p, xp, mods4, w_out, ln_g, ln_b, n_seq=n_p,
                     seq_len=len_p, q_tile=len_p, mod_row0=0, per_seq_mod=False)
    cache = (cache_k[:, 0].reshape(n_s, past, KV_WIDTH), cache_v[:, 0].reshape(n_s, past, KV_WIDTH))
    xs1 = _attn_core(q_s, kp_s, vt_s, cache, g_s, xs, mods4, w_out, ln_g, ln_b, n_seq=n_s,
                     seq_len=len_s, q_tile=ROW_TILE, mod_row0=1, per_seq_mod=True)
    new_cache_k = k_p.reshape(n_p, 1, len_p, N_KV_HEADS, HEAD_DIM)
    new_cache_v = v_p.reshape(n_p, 1, len_p, N_KV_HEADS, HEAD_DIM)

    rw_in = rec_w_in[0].astype(BF16)
    rw_out = rec_w_out[0].astype(BF16)
    lb = lb_all[1]
    ng = rec_norm_gain[0].reshape(1, REC_DV)
    ln_g = ln_gain[1].reshape(1, d)
    ln_b = ln_bias[1].reshape(1, d)
    consts = _gla_consts()

    outs = []
    states = None
    for x1, n_seq, seq_len, row0, per_seq, s0 in (
            (xp1, n_p, len_p, 0, False, None),
            (xs1, n_s, len_s, 1, True, state_rec[:, 0])):
        q, v, g, lf_fw, lf_bw, k_fw, k_bw = _rec_proj(x1, mods4, rw_in, lb, seq_len=seq_len,
                                                       mod_row0=row0, per_seq_mod=per_seq)
        o, st = _gla(q, v, lf_fw, lf_bw, k_fw, k_bw, s0, consts, n_seq=n_seq, seq_len=seq_len,
                     want_state=s0 is None)
        if st is not None:
            states = st
        outs.append(_rec_out(o, g, x1, mods4, ng, rw_out, ln_g, ln_b, seq_len=seq_len,
                             mod_row0=row0, per_seq_mod=per_seq))

    y_prompt = outs[0].reshape(n_p, len_p, d)
    y_sample = outs[1].reshape(n_s, len_s, d)
    new_state_rec = states.reshape(n_p, 1, 2, N_REC_HEADS, REC_DK, REC_DV)
    return (y_prompt, y_sample, new_cache_k, new_cache_v, new_state_rec)
```

```python
import functools
import math

import jax
import jax.numpy as jnp
from jax import lax
from jax.experimental import pallas as pl
from jax.experimental.pallas import tpu as pltpu

F32 = jnp.float32
BF16 = jnp.bfloat16

D_MODEL = 1024
DEPTH = 2
GRID_W = 64
N_HEADS = 16
N_KV_HEADS = 4
HEAD_DIM = 64
AXIS_DIM = HEAD_DIM // 2
ATTN_WIDTH = N_HEADS * HEAD_DIM
KV_WIDTH = N_KV_HEADS * HEAD_DIM
ROPE_THETA = 10000.0
N_REC_HEADS = 8
REC_DK = 128
REC_DV = 128
REC_WIDTH = N_REC_HEADS * REC_DK
NORM_EPS = 1e-6
LN_EPS = 1e-5
DEEPNORM_ALPHA = (2.0 * DEPTH) ** 0.25

SUBLANES = 8
COND_ROWS = SUBLANES
ROW_TILE = 256
GROUP_LANES = 256
SLAB_LANES = 128
KPAD_WIDTH = 2 * N_KV_HEADS * SLAB_LANES
KEY_BLOCK = 256
REC_PROJ_SLAB = 256
GLA_TILE = 128
GLA_LEVELS = (1, 2, 4, 8, 16, 32, 64)
GLA_HEADS_PER_STEP = 4
GLA_ITEMS_PER_STEP = 8
LOG2_E = 1.4426950408889634
VMEM_LIMIT = 56 * 1024 * 1024


def _sigmoid(x):
    return 1.0 / (1.0 + jnp.exp(-x))


def _dot(a, b):
    return jnp.dot(a, b, preferred_element_type=F32)


def _dot_nt(a, b):
    return lax.dot_general(a, b, (((1,), (1,)), ((), ())), preferred_element_type=F32)


def _dot_tn(a, b):
    return lax.dot_general(a, b, (((0,), (0,)), ((), ())), preferred_element_type=F32)


def _layer_norm(y, g, b):
    mu = jnp.mean(y, axis=-1, keepdims=True)
    yc = y - mu
    var = jnp.mean(yc * yc, axis=-1, keepdims=True)
    return yc * lax.rsqrt(var + LN_EPS) * g + b


def _mods_kernel(cond_ref, w_ref, b_ref, out_ref):
    c = cond_ref[...]
    s = (c * _sigmoid(c)).astype(BF16)
    out_ref[...] = _dot(s, w_ref[...].astype(BF16)) + b_ref[...]


def _mods(cond, ada_w, ada_b):
    d = D_MODEL
    return pl.pallas_call(
        _mods_kernel,
        grid=(DEPTH, 3),
        in_specs=[
            pl.BlockSpec((COND_ROWS, d), lambda l, j: (0, 0)),
            pl.BlockSpec((None, d, d), lambda l, j: (l, 0, j)),
            pl.BlockSpec((None, 1, d), lambda l, j: (l, 0, j)),
        ],
        out_specs=pl.BlockSpec((None, COND_ROWS, d), lambda l, j: (l, 0, j)),
        out_shape=jax.ShapeDtypeStruct((DEPTH, COND_ROWS, 3 * d), F32),
        compiler_params=pltpu.CompilerParams(vmem_limit_bytes=VMEM_LIMIT),
        name="adaln_mods",
    )(cond, ada_w, ada_b.reshape(DEPTH, 1, 3 * d))


def _lower_bounds_kernel(r_ref, out_ref):
    r = [r_ref[i] for i in range(DEPTH)]
    m = functools.reduce(jnp.maximum, r)
    e = [jnp.exp(x - m) for x in r]
    tot = functools.reduce(lambda a, b: a + b, e)
    soft = [x / tot for x in e]
    acc = soft[0]
    for i in range(DEPTH):
        if i > 0:
            acc = acc + soft[i]
        out_ref[i] = acc - soft[0]


def _lower_bounds(rec_lower_bounds):
    return pl.pallas_call(
        _lower_bounds_kernel,
        out_shape=jax.ShapeDtypeStruct(rec_lower_bounds.shape, F32),
        name="rec_lower_bounds",
    )(rec_lower_bounds)


def _kv_head_slabs(k):
    lane = lax.broadcasted_iota(jnp.int32, (k.shape[0], SLAB_LANES), 1)
    low = lane < HEAD_DIM
    heads_per_slab = SLAB_LANES // HEAD_DIM
    out = []
    for j in range(N_KV_HEADS):
        tile = k[:, (j // heads_per_slab) * SLAB_LANES:(j // heads_per_slab + 1) * SLAB_LANES]
        moved = pltpu.roll(tile, HEAD_DIM, 1)
        at_low, at_high = (tile, moved) if j % heads_per_slab == 0 else (moved, tile)
        out.append(jnp.where(low, at_low, 0.0))
        out.append(jnp.where(low, 0.0, at_high))
    return jnp.concatenate(out, axis=1)


def _attn_proj_kernel(*refs, rope, cache_out):
    refs = list(refs)
    x_ref, mod_ref, w_ref, pn_ref, qg_ref, kg_ref = refs[:6]
    pos = 6
    if rope:
        cos_ref, sa_ref, sb_ref = refs[pos:pos + 3]
        pos += 3
    q_out, kp_out, vt_out, g_out = refs[pos:pos + 4]
    pos += 4
    if cache_out:
        k_out, v_out = refs[pos:pos + 2]
    d = D_MODEL
    mod = mod_ref[...]
    shift, scale = mod[:, :d], mod[:, d:2 * d]
    h = (x_ref[...] * (1.0 + scale) + shift).astype(BF16)
    pn = pn_ref[...]

    def norm_rope(u, gain):
        ms = _dot((u * u).astype(BF16), pn)
        y = u * lax.rsqrt(ms + NORM_EPS) * gain
        if rope:
            y = (y * cos_ref[...]
                 + pltpu.roll(y, GROUP_LANES - AXIS_DIM // 2, 1) * sa_ref[...]
                 + pltpu.roll(y, AXIS_DIM // 2, 1) * sb_ref[...])
        return y

    def finish_q(j, u):
        q_out[:, j * GROUP_LANES:(j + 1) * GROUP_LANES] = norm_rope(u, qg_ref[...]).astype(q_out.dtype)

    def finish_k(_, u):
        k = norm_rope(u, kg_ref[...])
        kp_out[...] = _kv_head_slabs(k).astype(kp_out.dtype)
        if cache_out:
            k_out[...] = k

    def finish_v(_, v):
        vt_out[...] = v.T.astype(vt_out.dtype)
        if cache_out:
            v_out[...] = v

    def finish_g(j, u):
        g_out[:, j * GROUP_LANES:(j + 1) * GROUP_LANES] = (u * _sigmoid(u)).astype(g_out.dtype)

    slabs = ([(finish_q, j) for j in range(N_KV_HEADS)] + [(finish_k, 0), (finish_v, 0)]
             + [(finish_g, j) for j in range(N_KV_HEADS)])
    project = lambda i: _dot(h, w_ref[:, i * GROUP_LANES:(i + 1) * GROUP_LANES])
    u_next = project(0)
    for i, (finish, j) in enumerate(slabs):
        u = u_next
        if i + 1 < len(slabs):
            u_next = project(i + 1)
        finish(j, u)


def _attn_proj(x2, mods4, w_bf, pn, qg, kg, rope_tabs, *, seq_len, mod_row0, per_seq_mod, cache_out):
    n_tok = x2.shape[0]
    d = D_MODEL
    tiles_per_seq = seq_len // ROW_TILE
    rope = rope_tabs is not None

    def mod_map(i):
        row = mod_row0 + (i // tiles_per_seq if per_seq_mod else 0)
        return (0, row, 0, 0)

    full = lambda i: (0, 0)
    in_specs = [
        pl.BlockSpec((ROW_TILE, d), lambda i: (i, 0)),
        pl.BlockSpec((None, None, 1, 3 * d), mod_map),
        pl.BlockSpec(w_bf.shape, full),
        pl.BlockSpec(pn.shape, full),
        pl.BlockSpec(qg.shape, full),
        pl.BlockSpec(kg.shape, full),
    ]
    args = [x2, mods4, w_bf, pn, qg, kg]
    if rope:
        for t in rope_tabs:
            in_specs.append(pl.BlockSpec((ROW_TILE, GROUP_LANES), lambda i: (i % tiles_per_seq, 0)))
            args.append(t)
    rows = lambda width: pl.BlockSpec((ROW_TILE, width), lambda i: (i, 0))
    out_shape = [
        jax.ShapeDtypeStruct((n_tok, ATTN_WIDTH), BF16),
        jax.ShapeDtypeStruct((n_tok, KPAD_WIDTH), BF16),
        jax.ShapeDtypeStruct((n_tok // seq_len, KV_WIDTH, seq_len), BF16),
        jax.ShapeDtypeStruct((n_tok, ATTN_WIDTH), BF16),
    ]
    out_specs = [
        rows(ATTN_WIDTH),
        rows(KPAD_WIDTH),
        pl.BlockSpec((None, KV_WIDTH, ROW_TILE), lambda i: (i // tiles_per_seq, 0, i % tiles_per_seq)),
        rows(ATTN_WIDTH),
    ]
    if cache_out:
        out_shape += [jax.ShapeDtypeStruct((n_tok, KV_WIDTH), F32)] * 2
        out_specs += [rows(KV_WIDTH)] * 2
    return pl.pallas_call(
        functools.partial(_attn_proj_kernel, rope=rope, cache_out=cache_out),
        grid=(n_tok // ROW_TILE,),
        in_specs=in_specs,
        out_specs=tuple(out_specs),
        out_shape=tuple(out_shape),
        compiler_params=pltpu.CompilerParams(vmem_limit_bytes=VMEM_LIMIT),
        name="attn_proj_rope" if rope else "attn_proj",
    )(*args)


def _attn_core_kernel(*refs, n_new_blocks, has_cache):
    refs = list(refs)
    q_ref, kp_ref, vt_ref = refs[:3]
    pos = 3
    if has_cache:
        kc_ref, vc_ref = refs[pos:pos + 2]
        pos += 2
    sg_ref, x_ref, mod_ref, w_ref, lng_ref, lnb_ref, out_ref, s_scr, ot_scr = refs[pos:]
    d = D_MODEL
    tq = q_ref.shape[0]
    kb = KEY_BLOCK
    group = N_HEADS // N_KV_HEADS
    heads_per_slab = SLAB_LANES // HEAD_DIM

    blocks = []
    if has_cache:
        kc_slabs = _kv_head_slabs(kc_ref[...]).astype(BF16)
        vc_t = vc_ref[...].T.astype(BF16)
        blocks.append((lambda s: kc_slabs[:, s * SLAB_LANES:(s + 1) * SLAB_LANES],
                       lambda j: vc_t[j * HEAD_DIM:(j + 1) * HEAD_DIM, :]))
    for blk in range(n_new_blocks):
        blocks.append((lambda s, blk=blk: kp_ref[blk * kb:(blk + 1) * kb, s * SLAB_LANES:(s + 1) * SLAB_LANES],
                       lambda j, blk=blk: vt_ref[j * HEAD_DIM:(j + 1) * HEAD_DIM, blk * kb:(blk + 1) * kb]))

    def fold_rows(a, op):
        return functools.reduce(op, [a[r * SUBLANES:(r + 1) * SUBLANES, :] for r in range(kb // SUBLANES)])

    def scores_phase(j):
        slot = j % 2
        maxima = []
        for hh in range(group):
            hd = j * group + hh
            slab, where = hd // heads_per_slab, hd % heads_per_slab
            q_slab = q_ref[:, slab * SLAB_LANES:(slab + 1) * SLAB_LANES]
            mx = None
            for bi, (keys, _) in enumerate(blocks):
                s = _dot_nt(keys(heads_per_slab * j + where), q_slab)
                s_scr[slot, hh, bi * kb:(bi + 1) * kb, :] = s
                r = fold_rows(s, jnp.maximum)
                mx = r if mx is None else jnp.maximum(mx, r)
            maxima.append(jnp.max(mx, axis=0, keepdims=True))
        return maxima

    def values_phase(j, maxima):
        slot = j % 2
        acc = None
        sums = [None] * group
        for bi, (_, values) in enumerate(blocks):
            ps = []
            for hh in range(group):
                p = jnp.exp2(s_scr[slot, hh, bi * kb:(bi + 1) * kb, :] - maxima[hh])
                r = fold_rows(p, lambda a, b: a + b)
                sums[hh] = r if sums[hh] is None else sums[hh] + r
                ps.append(p.astype(BF16))
            part = _dot(values(j), jnp.concatenate(ps, axis=1))
            acc = part if acc is None else acc + part
        for hh in range(group):
            hd = j * group + hh
            l = jnp.sum(sums[hh], axis=0, keepdims=True)
            ot_scr[hd * HEAD_DIM:(hd + 1) * HEAD_DIM, :] = acc[:, hh * tq:(hh + 1) * tq] * (1.0 / l)

    pending = scores_phase(0)
    for j in range(N_KV_HEADS):
        nxt = scores_phase(j + 1) if j + 1 < N_KV_HEADS else None
        values_phase(j, pending)
        pending = nxt

    o = ot_scr[...].T
    gated = (o * sg_ref[...].astype(F32)).astype(BF16)
    branch = _dot(gated, w_ref[...])
    gate = mod_ref[...][:, 2 * d:]
    y = DEEPNORM_ALPHA * x_ref[...] + gate * branch
    out_ref[...] = _layer_norm(y, lng_ref[...], lnb_ref[...])


def _attn_core(q, kp, vt, cache, sg, x2, mods4, w_out_bf, ln_g, ln_b, *, n_seq, seq_len, q_tile,
               mod_row0, per_seq_mod):
    d = D_MODEL
    tiles = seq_len // q_tile
    has_cache = cache is not None
    q3 = q.reshape(n_seq, seq_len, ATTN_WIDTH)
    kp3 = kp.reshape(n_seq, seq_len, KPAD_WIDTH)
    sg3 = sg.reshape(n_seq, seq_len, ATTN_WIDTH)
    x3 = x2.reshape(n_seq, seq_len, d)
    n_keys = seq_len + (cache[0].shape[1] if has_cache else 0)

    def mod_map(b, i):
        return (0, mod_row0 + (b if per_seq_mod else 0), 0, 0)

    tile_spec = lambda width: pl.BlockSpec((None, q_tile, width), lambda b, i: (b, i, 0))
    seq_spec = lambda rows, width: pl.BlockSpec((None, rows, width), lambda b, i: (b, 0, 0))
    full2 = lambda b, i: (0, 0)
    in_specs = [tile_spec(ATTN_WIDTH), seq_spec(seq_len, KPAD_WIDTH), seq_spec(KV_WIDTH, seq_len)]
    args = [q3, kp3, vt]
    if has_cache:
        ck, cv = cache
        in_specs += [seq_spec(ck.shape[1], KV_WIDTH), seq_spec(cv.shape[1], KV_WIDTH)]
        args += [ck, cv]
    in_specs += [
        tile_spec(ATTN_WIDTH),
        tile_spec(d),
        pl.BlockSpec((None, None, 1, 3 * d), mod_map),
        pl.BlockSpec(w_out_bf.shape, full2),
        pl.BlockSpec(ln_g.shape, full2),
        pl.BlockSpec(ln_b.shape, full2),
    ]
    args += [sg3, x3, mods4, w_out_bf, ln_g, ln_b]
    out = pl.pallas_call(
        functools.partial(_attn_core_kernel, n_new_blocks=seq_len // KEY_BLOCK, has_cache=has_cache),
        grid=(n_seq, tiles),
        in_specs=in_specs,
        out_specs=tile_spec(d),
        out_shape=jax.ShapeDtypeStruct((n_seq, seq_len, d), F32),
        scratch_shapes=[
            pltpu.VMEM((2, N_HEADS // N_KV_HEADS, n_keys, q_tile), F32),
            pltpu.VMEM((ATTN_WIDTH, q_tile), F32),
        ],
        compiler_params=pltpu.CompilerParams(vmem_limit_bytes=VMEM_LIMIT),
        name="attn_core_cache" if has_cache else "attn_core",
    )(*args)
    return out.reshape(n_seq * seq_len, d)


def _rec_proj_kernel(x_ref, mod_ref, w_ref, lb_ref, q_out, v_out, g_out, lf_fw, lf_bw, k_fw, k_bw):
    d = D_MODEL
    mod = mod_ref[...]
    shift, scale = mod[:, :d], mod[:, d:2 * d]
    h = (x_ref[...] * (1.0 + scale) + shift).astype(BF16)
    slab = REC_PROJ_SLAB

    def finish_silu(out_ref):
        def finish(cols, u):
            out_ref[:, cols] = (u * _sigmoid(u)).astype(out_ref.dtype)
        return finish

    def finish_gate(direction, lf_out, k_out):
        def finish(cols, z):
            lb = lb_ref[direction:direction + 1, cols]
            sig = _sigmoid(z)
            lf_out[:, cols] = jnp.log(lb + (1.0 - lb) * sig)
            k_out[:, cols] = ((1.0 - lb) * (1.0 - sig)).astype(k_out.dtype)
        return finish

    def finish_v(cols, u):
        v_out[:, cols] = u.astype(v_out.dtype)

    sections = (finish_silu(q_out), finish_gate(0, lf_fw, k_fw), finish_gate(1, lf_bw, k_bw),
                finish_v, finish_silu(g_out))
    per_section = REC_WIDTH // slab
    n_slabs = len(sections) * per_section
    project = lambda i: _dot(h, w_ref[:, i * slab:(i + 1) * slab])
    u_next = project(0)
    for i in range(n_slabs):
        u = u_next
        if i + 1 < n_slabs:
            u_next = project(i + 1)
        within = i % per_section
        sections[i // per_section](slice(within * slab, (within + 1) * slab), u)


def _rec_proj(x2, mods4, w_bf, lb, *, seq_len, mod_row0, per_seq_mod):
    n_tok = x2.shape[0]
    d = D_MODEL
    tiles_per_seq = seq_len // ROW_TILE

    def mod_map(i):
        row = mod_row0 + (i // tiles_per_seq if per_seq_mod else 0)
        return (1, row, 0, 0)

    row_spec = pl.BlockSpec((ROW_TILE, REC_WIDTH), lambda i: (i, 0))
    full = lambda i: (0, 0)
    out_dtypes = (BF16, BF16, BF16, F32, F32, BF16, BF16)
    return pl.pallas_call(
        _rec_proj_kernel,
        grid=(n_tok // ROW_TILE,),
        in_specs=[
            pl.BlockSpec((ROW_TILE, d), lambda i: (i, 0)),
            pl.BlockSpec((None, None, 1, 3 * d), mod_map),
            pl.BlockSpec(w_bf.shape, full),
            pl.BlockSpec(lb.shape, full),
        ],
        out_specs=tuple(row_spec for _ in out_dtypes),
        out_shape=tuple(jax.ShapeDtypeStruct((n_tok, REC_WIDTH), t) for t in out_dtypes),
        compiler_params=pltpu.CompilerParams(vmem_limit_bytes=VMEM_LIMIT),
        name="rec_proj",
    )(x2, mods4, w_bf, lb)


def _large_level_exponent(b, edges, half, backward):
    t_rows, lanes = b.shape
    g = half // SUBLANES
    n_blocks = t_rows // (2 * half)
    b5 = b.reshape(n_blocks, 2, g, SUBLANES, lanes)
    e5 = edges.reshape(n_blocks, 2, g, SUBLANES, lanes)
    centre = (e5[:, 1:2, 0:1] if backward else e5[:, 0:1, g - 1:g])
    first, second = b5[:, 0:1], b5[:, 1:2]
    if backward:
        parts = [first - centre, centre - second]
    else:
        parts = [centre - first, second - centre]
    return jnp.concatenate(parts, axis=1).reshape(t_rows, lanes)


def _block_diag(a, b):
    za = jnp.zeros(a.shape, a.dtype)
    return jnp.concatenate(
        [jnp.concatenate([a, za], axis=1), jnp.concatenate([za, b], axis=1)], axis=0)


def _pair_scores(a, c):
    return _dot_nt(a, _block_diag(c[:, :REC_DK], c[:, REC_DK:]))


def _gla_scores_stage(units, between_levels=()):
    t = GLA_TILE
    pending = list(between_levels)
    small = [h for h in GLA_LEVELS if h < SUBLANES]
    for u in units:
        lf2 = u["lf"]() * LOG2_E
        hi = lf2.astype(BF16)
        lo = (lf2 - hi.astype(F32)).astype(BF16)
        sums = _dot(u["sums_ref"][...], jnp.concatenate([hi, lo], axis=0))
        u["b"], u["edges"] = sums[:t], sums[t:2 * t]
        u["small"] = {h: sums[(2 + i) * t:(3 + i) * t] for i, h in enumerate(small)}
        u["scores"] = u["masks_ref"][0] * _pair_scores(u["q"](), u["k"]()).astype(BF16)

    def level(u, li, e):
        x = jnp.exp2(e).astype(BF16)
        z = _pair_scores(u["q"]() * x, u["k"]() * x).astype(BF16)
        u["scores"] = u["scores"] + u["masks_ref"][1 + li] * z

    for li, half in enumerate(GLA_LEVELS):
        for u in units:
            if half < SUBLANES:
                e = u["small"][half]
            else:
                e = _large_level_exponent(u["b"], u["edges"], half, u["backward"])
            level(u, li, e)
        if pending:
            pending.pop(0)()
    for thunk in pending:
        thunk()
    for u in units:
        u["save"](u["scores"], u["b"])


def _gla_state_stage(u):
    t = GLA_TILE
    dk = REC_DK
    b, st_ref = u["load_b"](), u["st_ref"]
    q, k, v = u["q"](), u["k"](), u["v"]()
    edge = b[0:1, :] if u["backward"] else b[t - 1:t, :]
    o = _dot(u["load_scores"](), _block_diag(v[:, :dk], v[:, dk:]))
    st_a, st_b = st_ref[0], st_ref[1]
    q_in = q * jnp.exp2(b).astype(BF16)
    o = o + _dot_nt(q_in, _block_diag(st_a.astype(BF16), st_b.astype(BF16)))
    k_edge = k * jnp.exp2(edge - b).astype(BF16)
    carry = jnp.exp2(edge)
    st_ref[0] = st_a * carry[:, :dk] + _dot_tn(v[:, :dk], k_edge[:, :dk])
    st_ref[1] = st_b * carry[:, dk:] + _dot_tn(v[:, dk:], k_edge[:, dk:])
    u["store"](o)


def _gla_kernel(*refs, n_tiles, has_state, want_state):
    refs = list(refs)
    q_ref, v_ref, lff_ref, lfb_ref, kf_ref, kb_ref = refs[:6]
    pos = 6
    if has_state:
        s0_ref = refs[pos]
        pos += 1
    sums_f_ref, sums_b_ref, mf_ref, mb_ref = refs[pos:pos + 4]
    pos += 4
    o_ref = refs[pos]
    pos += 1
    if want_state:
        s_out_ref = refs[pos]
        pos += 1
    st_ref, pipe_s, pipe_b = refs[pos:pos + 3]

    t = GLA_TILE
    n_seqs = q_ref.shape[0]
    n_items = n_seqs * n_tiles
    pair_lanes = 2 * REC_DK
    o_ref[...] = jnp.zeros(o_ref.shape, o_ref.dtype)
    for seq in range(n_seqs):
        for direction in range(2):
            for hd in range(GLA_HEADS_PER_STEP):
                if has_state:
                    st_ref[seq, direction, hd] = s0_ref[seq, direction, hd].T
                else:
                    st_ref[seq, direction, hd] = jnp.zeros((REC_DV, REC_DK), F32)

    def units_of(item, slot):
        seq, step = item // n_tiles, item % n_tiles
        units = []
        for direction, (lf_ref, k_ref, sums_ref, m_ref) in enumerate(
                ((lff_ref, kf_ref, sums_f_ref, mf_ref), (lfb_ref, kb_ref, sums_b_ref, mb_ref))):
            tile = step if direction == 0 else n_tiles - 1 - step
            rows = pl.ds(pl.multiple_of(tile * t, t), t)
            for pair in range(GLA_HEADS_PER_STEP // 2):
                lanes = slice(pair * pair_lanes, (pair + 1) * pair_lanes)
                ui = len(units)

                def store(o, rows=rows, lanes=lanes):
                    o_ref[seq, rows, lanes] += o

                def save(scores, b, ui=ui):
                    pipe_s[slot, ui] = scores
                    pipe_b[slot, ui] = b

                load = lambda ref, rows=rows, lanes=lanes: (lambda: ref[seq, rows, lanes])
                units.append(dict(
                    q=load(q_ref), k=load(k_ref), v=load(v_ref), lf=load(lf_ref),
                    st_ref=st_ref.at[seq, direction, pl.ds(2 * pair, 2)], sums_ref=sums_ref,
                    masks_ref=m_ref, backward=direction == 1, store=store, save=save,
                    load_scores=lambda ui=ui: pipe_s[slot, ui],
                    load_b=lambda ui=ui: pipe_b[slot, ui]))
        return units

    def body(item, carry):
        units = units_of(item, 0)
        _gla_scores_stage(units)
        for u in units:
            _gla_state_stage(u)
        return carry

    lax.fori_loop(0, n_items, body, 0)
    if want_state:
        for seq in range(n_seqs):
            for direction in range(2):
                for hd in range(GLA_HEADS_PER_STEP):
                    s_out_ref[seq, direction, hd] = st_ref[seq, direction, hd].T


def _gla(q, v, lf_fw, lf_bw, k_fw, k_bw, s0, consts, *, n_seq, seq_len, want_state):
    has_state = s0 is not None
    width = REC_WIDTH
    hps = GLA_HEADS_PER_STEP
    n_tiles = seq_len // GLA_TILE
    seqs = min(n_seq, max(1, GLA_ITEMS_PER_STEP // n_tiles))
    assert n_seq % seqs == 0
    n_units = hps
    seq3 = lambda a: a.reshape(n_seq, seq_len, width)
    head_spec = pl.BlockSpec((seqs, seq_len, hps * REC_DK), lambda b, h: (b, 0, h))
    state_spec = pl.BlockSpec((seqs, 2, hps, REC_DK, REC_DV), lambda b, h: (b, 0, h, 0, 0))
    in_specs = [head_spec] * 6
    args = [seq3(q), seq3(v), seq3(lf_fw), seq3(lf_bw), seq3(k_fw), seq3(k_bw)]
    if has_state:
        in_specs.append(state_spec)
        args.append(s0)
    for c in consts:
        in_specs.append(pl.BlockSpec(c.shape, lambda b, h, nd=c.ndim: (0,) * nd))
        args.append(c)
    out_shape = [jax.ShapeDtypeStruct((n_seq, seq_len, width), F32)]
    out_specs = [head_spec]
    if want_state:
        out_shape.append(jax.ShapeDtypeStruct((n_seq, 2, N_REC_HEADS, REC_DK, REC_DV), F32))
        out_specs.append(state_spec)
    res = pl.pallas_call(
        functools.partial(_gla_kernel, n_tiles=n_tiles, has_state=has_state, want_state=want_state),
        grid=(n_seq // seqs, N_REC_HEADS // hps),
        in_specs=in_specs,
        out_specs=tuple(out_specs),
        out_shape=tuple(out_shape),
        scratch_shapes=[
            pltpu.VMEM((seqs, 2, hps, REC_DV, REC_DK), F32),
            pltpu.VMEM((2, n_units, GLA_TILE, 2 * GLA_TILE), BF16),
            pltpu.VMEM((2, n_units, GLA_TILE, 2 * REC_DK), F32),
        ],
        compiler_params=pltpu.CompilerParams(vmem_limit_bytes=VMEM_LIMIT),
        name="gla_state_in" if has_state else "gla_state_out",
    )(*args)
    o = res[0].reshape(n_seq * seq_len, width)
    return (o, res[1]) if want_state else (o, None)


def _rec_out_kernel(o_ref, sg_ref, x_ref, mod_ref, ng_ref, w_ref, lng_ref, lnb_ref, out_ref):
    d = D_MODEL
    parts = []
    for hd in range(N_REC_HEADS):
        oh = o_ref[:, hd * REC_DV:(hd + 1) * REC_DV]
        ms = jnp.mean(oh * oh, axis=-1, keepdims=True)
        parts.append(oh * lax.rsqrt(ms + NORM_EPS) * ng_ref[...])
    o = jnp.concatenate(parts, axis=1)
    gated = (o * sg_ref[...].astype(F32)).astype(BF16)
    branch = _dot(gated, w_ref[...])
    gate = mod_ref[...][:, 2 * d:]
    y = DEEPNORM_ALPHA * x_ref[...] + gate * branch
    out_ref[...] = _layer_norm(y, lng_ref[...], lnb_ref[...])


def _rec_out(o, sg, x2, mods4, norm_gain, w_out_bf, ln_g, ln_b, *, seq_len, mod_row0, per_seq_mod):
    n_tok = x2.shape[0]
    d = D_MODEL
    tiles_per_seq = seq_len // ROW_TILE

    def mod_map(i):
        row = mod_row0 + (i // tiles_per_seq if per_seq_mod else 0)
        return (1, row, 0, 0)

    row_spec = pl.BlockSpec((ROW_TILE, d), lambda i: (i, 0))
    full = lambda i: (0, 0)
    return pl.pallas_call(
        _rec_out_kernel,
        grid=(n_tok // ROW_TILE,),
        in_specs=[
            row_spec, row_spec, row_spec,
            pl.BlockSpec((None, None, 1, 3 * d), mod_map),
            pl.BlockSpec(norm_gain.shape, full),
            pl.BlockSpec(w_out_bf.shape, full),
            pl.BlockSpec(ln_g.shape, full),
            pl.BlockSpec(ln_b.shape, full),
        ],
        out_specs=row_spec,
        out_shape=jax.ShapeDtypeStruct((n_tok, d), F32),
        compiler_params=pltpu.CompilerParams(vmem_limit_bytes=VMEM_LIMIT),
        name="rec_out",
    )(o, sg, x2, mods4, norm_gain, w_out_bf, ln_g, ln_b)


def _rope_tables(n_tokens):
    n_rows = n_tokens // GRID_W
    rows = jnp.repeat(jnp.arange(n_rows, dtype=F32), GRID_W)
    cols = jnp.tile(jnp.arange(GRID_W, dtype=F32), n_rows)
    inv_freq = 1.0 / (ROPE_THETA ** (jnp.arange(0, AXIS_DIM, 2, dtype=F32) / AXIS_DIM))
    ang_r = rows[:, None] * inv_freq[None, :]
    ang_c = cols[:, None] * inv_freq[None, :]
    ang = jnp.concatenate([ang_r, ang_r, ang_c, ang_c], axis=-1)
    cos, sin = jnp.cos(ang), jnp.sin(ang)
    first = (jnp.arange(HEAD_DIM) % AXIS_DIM) < AXIS_DIM // 2
    sin_a = jnp.where(first[None, :], -sin, 0.0)
    sin_b = jnp.where(first[None, :], 0.0, sin)
    reps = GROUP_LANES // HEAD_DIM
    return tuple(jnp.tile(t, (1, reps)) for t in (cos, sin_a, sin_b))


def _head_mean_matrix():
    idx = jnp.arange(GROUP_LANES) // HEAD_DIM
    return jnp.where(idx[:, None] == idx[None, :], 1.0 / HEAD_DIM, 0.0).astype(BF16)


def _gla_consts():
    t = GLA_TILE
    r = jnp.arange(t)[:, None]
    c = jnp.arange(t)[None, :]
    grp = SUBLANES * (r // SUBLANES)
    fw = [c <= r, c <= grp + SUBLANES - 1]
    bw = [c >= r, c >= grp]
    for half in GLA_LEVELS:
        if half >= SUBLANES:
            continue
        start = 2 * half * (r // (2 * half))
        upper = r - start >= half
        last_low, first_up = start + half - 1, start + half
        fw.append(jnp.where(upper, (c > last_low) & (c <= r), (c > r) & (c <= last_low)))
        bw.append(jnp.where(upper, (c >= first_up) & (c < r), (c >= r) & (c < first_up)))
    twice = lambda a: jnp.concatenate([a, a], axis=-1)
    sums_fw = twice(jnp.concatenate(fw, axis=0).astype(BF16))
    sums_bw = twice(jnp.concatenate(bw, axis=0).astype(BF16))
    masks = [(r == c)]
    for half in GLA_LEVELS:
        size = 2 * half
        masks.append((r // size == c // size) & (r % size >= half) & (c % size < half))
    m_fw = jnp.stack(masks).astype(BF16)
    m_bw = jnp.swapaxes(m_fw, 1, 2)
    return sums_fw, sums_bw, twice(m_fw), twice(m_bw)


def kernel(x_prompt, x_sample, cache_k, cache_v, state_rec, c, c_ctx, ada_w, ada_b, attn_w_in,
           attn_q_gain, attn_k_gain, attn_w_out, rec_w_in, rec_lower_bounds, rec_norm_gain,
           rec_w_out, ln_gain, ln_bias):
    d = D_MODEL
    n_p, len_p, _ = x_prompt.shape
    n_s, len_s, _ = x_sample.shape
    past = cache_k.shape[2]
    assert past == KEY_BLOCK and len_p % KEY_BLOCK == 0 and len_s % KEY_BLOCK == 0

    cond = jnp.zeros((COND_ROWS, d), F32).at[0].set(c_ctx).at[1:1 + n_s].set(c)
    mods4 = _mods(cond, ada_w, ada_b).reshape(DEPTH, COND_ROWS, 1, 3 * d)
    lb_all = _lower_bounds(rec_lower_bounds)

    xp = x_prompt.reshape(n_p * len_p, d)
    xs = x_sample.reshape(n_s * len_s, d)

    w_in = attn_w_in[0].astype(BF16)
    w_out = attn_w_out[0].astype(BF16)
    reps = GROUP_LANES // HEAD_DIM
    qg = (jnp.tile(attn_q_gain[0], reps) * (LOG2_E / math.sqrt(HEAD_DIM))).reshape(1, GROUP_LANES)
    kg = jnp.tile(attn_k_gain[0], reps).reshape(1, GROUP_LANES)
    pn = _head_mean_matrix()
    ln_g = ln_gain[0].reshape(1, d)
    ln_b = ln_bias[0].reshape(1, d)

    q_p, kp_p, vt_p, g_p, k_p, v_p = _attn_proj(xp, mods4, w_in, pn, qg, kg, None, seq_len=len_p,
                                                mod_row0=0, per_seq_mod=False, cache_out=True)
    q_s, kp_s, vt_s, g_s = _attn_proj(xs, mods4, w_in, pn, qg, kg, _rope_tables(len_s), seq_len=len_s,
                                      mod_row0=1, per_seq_mod=True, cache_out=False)
    xp1 = _attn_core(q_p, kp_p, vt_p, None, g_p, xp, mods4, w_out, ln_g, ln_b, n_seq=n_p,
                     seq_len=len_p, q_tile=len_p, mod_row0=0, per_seq_mod=False)
    cache = (cache_k[:, 0].reshape(n_s, past, KV_WIDTH), cache_v[:, 0].reshape(n_s, past, KV_WIDTH))
    xs1 = _attn_core(q_s, kp_s, vt_s, cache, g_s, xs, mods4, w_out, ln_g, ln_b, n_seq=n_s,
                     seq_len=len_s, q_tile=ROW_TILE, mod_row0=1, per_seq_mod=True)
    new_cache_k = k_p.reshape(n_p, 1, len_p, N_KV_HEADS, HEAD_DIM)
    new_cache_v = v_p.reshape(n_p, 1, len_p, N_KV_HEADS, HEAD_DIM)

    rw_in = rec_w_in[0].astype(BF16)
    rw_out = rec_w_out[0].astype(BF16)
    lb = lb_all[1]
    ng = rec_norm_gain[0].reshape(1, REC_DV)
    ln_g = ln_gain[1].reshape(1, d)
    ln_b = ln_bias[1].reshape(1, d)
    consts = _gla_consts()

    outs = []
    states = None
    for x1, n_seq, seq_len, row0, per_seq, s0 in (
            (xp1, n_p, len_p, 0, False, None),
            (xs1, n_s, len_s, 1, True, state_rec[:, 0])):
        q, v, g, lf_fw, lf_bw, k_fw, k_bw = _rec_proj(x1, mods4, rw_in, lb, seq_len=seq_len,
                                                       mod_row0=row0, per_seq_mod=per_seq)
        o, st = _gla(q, v, lf_fw, lf_bw, k_fw, k_bw, s0, consts, n_seq=n_seq, seq_len=seq_len,
                     want_state=s0 is None)
        if st is not None:
            states = st
        outs.append(_rec_out(o, g, x1, mods4, ng, rw_out, ln_g, ln_b, seq_len=seq_len,
                             mod_row0=row0, per_seq_mod=per_seq))

    y_prompt = outs[0].reshape(n_p, len_p, d)
    y_sample = outs[1].reshape(n_s, len_s, d)
    new_state_rec = states.reshape(n_p, 1, 2, N_REC_HEADS, REC_DK, REC_DV)
    return (y_prompt, y_sample, new_cache_k, new_cache_v, new_state_rec)
```

```python
import functools
import math

import jax
import jax.numpy as jnp
from jax import lax
from jax.experimental import pallas as pl
from jax.experimental.pallas import tpu as pltpu

F32 = jnp.float32
BF16 = jnp.bfloat16

D_MODEL = 1024
DEPTH = 2
GRID_W = 64
N_HEADS = 16
N_KV_HEADS = 4
HEAD_DIM = 64
AXIS_DIM = HEAD_DIM // 2
ATTN_WIDTH = N_HEADS * HEAD_DIM
KV_WIDTH = N_KV_HEADS * HEAD_DIM
ROPE_THETA = 10000.0
N_REC_HEADS = 8
REC_DK = 128
REC_DV = 128
REC_WIDTH = N_REC_HEADS * REC_DK
NORM_EPS = 1e-6
LN_EPS = 1e-5
DEEPNORM_ALPHA = (2.0 * DEPTH) ** 0.25

SUBLANES = 8
BF16_ROWS = 16
COND_ROWS = SUBLANES
ROW_TILE = 256
GROUP_LANES = 256
SLAB_LANES = 128
KPAD_WIDTH = 2 * N_KV_HEADS * SLAB_LANES
REC_PROJ_SLAB = 256
GLA_TILE = 128
GLA_LEVELS = (1, 2, 4, 8, 16, 32, 64)
GLA_HEADS_PER_STEP = 4
GLA_ITEMS_PER_STEP = 8
LOG2_E = 1.4426950408889634
VMEM_LIMIT = 56 * 1024 * 1024


def _sigmoid(x):
    return 1.0 / (1.0 + jnp.exp(-x))


def _dot(a, b):
    return jnp.dot(a, b, preferred_element_type=F32)


def _dot_nt(a, b):
    return lax.dot_general(a, b, (((1,), (1,)), ((), ())), preferred_element_type=F32)


def _dot_tn(a, b):
    return lax.dot_general(a, b, (((0,), (0,)), ((), ())), preferred_element_type=F32)


def _layer_norm(y, g, b):
    mu = jnp.mean(y, axis=-1, keepdims=True)
    yc = y - mu
    var = jnp.mean(yc * yc, axis=-1, keepdims=True)
    return yc * lax.rsqrt(var + LN_EPS) * g + b


def _mods_kernel(cond_ref, w_ref, b_ref, out_ref):
    c = cond_ref[...]
    s = (c * _sigmoid(c)).astype(BF16)
    out_ref[...] = _dot(s, w_ref[...].astype(BF16)) + b_ref[...]


def _mods(cond, ada_w, ada_b):
    d = D_MODEL
    return pl.pallas_call(
        _mods_kernel,
        grid=(DEPTH, 3),
        in_specs=[
            pl.BlockSpec((COND_ROWS, d), lambda l, j: (0, 0)),
            pl.BlockSpec((None, d, d), lambda l, j: (l, 0, j)),
            pl.BlockSpec((None, 1, d), lambda l, j: (l, 0, j)),
        ],
        out_specs=pl.BlockSpec((None, COND_ROWS, d), lambda l, j: (l, 0, j)),
        out_shape=jax.ShapeDtypeStruct((DEPTH, COND_ROWS, 3 * d), F32),
        compiler_params=pltpu.CompilerParams(vmem_limit_bytes=VMEM_LIMIT),
        name="adaln_mods",
    )(cond, ada_w, ada_b.reshape(DEPTH, 1, 3 * d))


def _lower_bounds_kernel(r_ref, out_ref):
    r = [r_ref[i] for i in range(DEPTH)]
    m = functools.reduce(jnp.maximum, r)
    e = [jnp.exp(x - m) for x in r]
    tot = functools.reduce(lambda a, b: a + b, e)
    soft = [x / tot for x in e]
    acc = soft[0]
    for i in range(DEPTH):
        if i > 0:
            acc = acc + soft[i]
        out_ref[i] = acc - soft[0]


def _lower_bounds(rec_lower_bounds):
    return pl.pallas_call(
        _lower_bounds_kernel,
        out_shape=jax.ShapeDtypeStruct(rec_lower_bounds.shape, F32),
        name="rec_lower_bounds",
    )(rec_lower_bounds)


def _kv_head_slabs(k):
    lane = lax.broadcasted_iota(jnp.int32, (k.shape[0], SLAB_LANES), 1)
    low = lane < HEAD_DIM
    heads_per_slab = SLAB_LANES // HEAD_DIM
    out = []
    for j in range(N_KV_HEADS):
        tile = k[:, (j // heads_per_slab) * SLAB_LANES:(j // heads_per_slab + 1) * SLAB_LANES]
        moved = pltpu.roll(tile, HEAD_DIM, 1)
        at_low, at_high = (tile, moved) if j % heads_per_slab == 0 else (moved, tile)
        out.append(jnp.where(low, at_low, 0.0))
        out.append(jnp.where(low, 0.0, at_high))
    return jnp.concatenate(out, axis=1)


def _attn_proj_kernel(*refs, rope, cache_out):
    refs = list(refs)
    x_ref, mod_ref, w_ref, pn_ref, qg_ref, kg_ref = refs[:6]
    pos = 6
    if rope:
        cos_ref, sa_ref, sb_ref = refs[pos:pos + 3]
        pos += 3
    q_out, kp_out, vt_out, g_out = refs[pos:pos + 4]
    pos += 4
    if cache_out:
        k_out, v_out = refs[pos:pos + 2]
    d = D_MODEL
    mod = mod_ref[...]
    shift, scale = mod[:, :d], mod[:, d:2 * d]
    h = (x_ref[...] * (1.0 + scale) + shift).astype(BF16)
    pn = pn_ref[...]

    def norm_rope(u, gain):
        ms = _dot((u * u).astype(BF16), pn)
        y = u * lax.rsqrt(ms + NORM_EPS) * gain
        if rope:
            y = (y * cos_ref[...]
                 + pltpu.roll(y, GROUP_LANES - AXIS_DIM // 2, 1) * sa_ref[...]
                 + pltpu.roll(y, AXIS_DIM // 2, 1) * sb_ref[...])
        return y

    def finish_q(j, u):
        q_out[:, j * GROUP_LANES:(j + 1) * GROUP_LANES] = norm_rope(u, qg_ref[...]).astype(q_out.dtype)

    def finish_k(_, u):
        k = norm_rope(u, kg_ref[...])
        kp_out[...] = _kv_head_slabs(k).astype(kp_out.dtype)
        if cache_out:
            k_out[...] = k

    def finish_v(_, v):
        vt_out[...] = v.T.astype(vt_out.dtype)
        if cache_out:
            v_out[...] = v

    def finish_g(j, u):
        g_out[:, j * GROUP_LANES:(j + 1) * GROUP_LANES] = (u * _sigmoid(u)).astype(g_out.dtype)

    slabs = ([(finish_q, j) for j in range(N_KV_HEADS)] + [(finish_k, 0), (finish_v, 0)]
             + [(finish_g, j) for j in range(N_KV_HEADS)])
    project = lambda i: _dot(h, w_ref[:, i * GROUP_LANES:(i + 1) * GROUP_LANES])
    u_next = project(0)
    for i, (finish, j) in enumerate(slabs):
        u = u_next
        if i + 1 < len(slabs):
            u_next = project(i + 1)
        finish(j, u)


def _attn_proj(x2, mods4, w_bf, pn, qg, kg, rope_tabs, *, seq_len, mod_row0, per_seq_mod, cache_out):
    n_tok = x2.shape[0]
    d = D_MODEL
    tiles_per_seq = seq_len // ROW_TILE
    rope = rope_tabs is not None

    def mod_map(i):
        row = mod_row0 + (i // tiles_per_seq if per_seq_mod else 0)
        return (0, row, 0, 0)

    full = lambda i: (0, 0)
    in_specs = [
        pl.BlockSpec((ROW_TILE, d), lambda i: (i, 0)),
        pl.BlockSpec((None, None, 1, 3 * d), mod_map),
        pl.BlockSpec(w_bf.shape, full),
        pl.BlockSpec(pn.shape, full),
        pl.BlockSpec(qg.shape, full),
        pl.BlockSpec(kg.shape, full),
    ]
    args = [x2, mods4, w_bf, pn, qg, kg]
    if rope:
        for t in rope_tabs:
            in_specs.append(pl.BlockSpec((ROW_TILE, GROUP_LANES), lambda i: (i % tiles_per_seq, 0)))
            args.append(t)
    rows = lambda width: pl.BlockSpec((ROW_TILE, width), lambda i: (i, 0))
    out_shape = [
        jax.ShapeDtypeStruct((n_tok, ATTN_WIDTH), BF16),
        jax.ShapeDtypeStruct((n_tok, KPAD_WIDTH), BF16),
        jax.ShapeDtypeStruct((n_tok // seq_len, KV_WIDTH, seq_len), BF16),
        jax.ShapeDtypeStruct((n_tok, ATTN_WIDTH), BF16),
    ]
    out_specs = [
        rows(ATTN_WIDTH),
        rows(KPAD_WIDTH),
        pl.BlockSpec((None, KV_WIDTH, ROW_TILE), lambda i: (i // tiles_per_seq, 0, i % tiles_per_seq)),
        rows(ATTN_WIDTH),
    ]
    if cache_out:
        out_shape += [jax.ShapeDtypeStruct((n_tok, KV_WIDTH), F32)] * 2
        out_specs += [rows(KV_WIDTH)] * 2
    return pl.pallas_call(
        functools.partial(_attn_proj_kernel, rope=rope, cache_out=cache_out),
        grid=(n_tok // ROW_TILE,),
        in_specs=in_specs,
        out_specs=tuple(out_specs),
        out_shape=tuple(out_shape),
        compiler_params=pltpu.CompilerParams(vmem_limit_bytes=VMEM_LIMIT),
        name="attn_proj_rope" if rope else "attn_proj",
    )(*args)


def _attn_core_kernel(*refs, has_cache):
    refs = list(refs)
    q_ref, kp_ref, vt_ref = refs[:3]
    pos = 3
    if has_cache:
        kc_ref, vc_ref = refs[pos:pos + 2]
        pos += 2
    sg_ref, x_ref, mod_ref, w_ref, lng_ref, lnb_ref, out_ref, s_scr, p_scr, ot_scr = refs[pos:]
    d = D_MODEL
    n_new = kp_ref.shape[0]
    n_old = kc_ref.shape[0] if has_cache else 0
    n_keys = n_old + n_new
    group = N_HEADS // N_KV_HEADS
    heads_per_slab = SLAB_LANES // HEAD_DIM
    if has_cache:
        kc_slabs = _kv_head_slabs(kc_ref[...]).astype(BF16)
        vc_t = vc_ref[...].T.astype(BF16)

    def fold_rows(a, op):
        return functools.reduce(
            op, [a[r * SUBLANES:(r + 1) * SUBLANES, :] for r in range(a.shape[0] // SUBLANES)])

    def scores_phase(j):
        slot = j % 2
        maxima = []
        for hh in range(group):
            hd = j * group + hh
            slab, where = hd // heads_per_slab, hd % heads_per_slab
            q_slab = q_ref[:, slab * SLAB_LANES:(slab + 1) * SLAB_LANES]
            k_slab = heads_per_slab * j + where
            lanes = slice(k_slab * SLAB_LANES, (k_slab + 1) * SLAB_LANES)
            s = _dot_nt(kp_ref[:, lanes], q_slab)
            s_scr[slot, hh, n_old:, :] = s
            mx = fold_rows(s, jnp.maximum)
            if has_cache:
                s = _dot_nt(kc_slabs[:, lanes], q_slab)
                s_scr[slot, hh, :n_old, :] = s
                mx = jnp.maximum(mx, fold_rows(s, jnp.maximum))
            maxima.append(jnp.max(mx, axis=0, keepdims=True))
        return maxima

    def values_phase(j, maxima):
        slot = j % 2
        rows = slice(j * HEAD_DIM, (j + 1) * HEAD_DIM)
        v_t = vt_ref[rows, :]
        if has_cache:
            v_t = jnp.concatenate([vc_t[rows, :], v_t], axis=1)
        v_ext = jnp.concatenate([v_t, jnp.ones((BF16_ROWS, n_keys), BF16)], axis=0)
        for hh in range(group):
            hd = j * group + hh
            p_scr[hh % 2] = jnp.exp2(s_scr[slot, hh] - maxima[hh]).astype(BF16)
            acc = _dot(v_ext, p_scr[hh % 2])
            l = acc[HEAD_DIM:HEAD_DIM + 1, :]
            ot_scr[hd * HEAD_DIM:(hd + 1) * HEAD_DIM, :] = acc[:HEAD_DIM, :] * (1.0 / l)

    pending = scores_phase(0)
    for j in range(N_KV_HEADS):
        nxt = scores_phase(j + 1) if j + 1 < N_KV_HEADS else None
        values_phase(j, pending)
        pending = nxt

    o = ot_scr[...].T
    gated = (o * sg_ref[...].astype(F32)).astype(BF16)
    branch = _dot(gated, w_ref[...])
    gate = mod_ref[...][:, 2 * d:]
    y = DEEPNORM_ALPHA * x_ref[...] + gate * branch
    out_ref[...] = _layer_norm(y, lng_ref[...], lnb_ref[...])


def _attn_core(q, kp, vt, cache, sg, x2, mods4, w_out_bf, ln_g, ln_b, *, n_seq, seq_len, q_tile,
               mod_row0, per_seq_mod):
    d = D_MODEL
    tiles = seq_len // q_tile
    has_cache = cache is not None
    q3 = q.reshape(n_seq, seq_len, ATTN_WIDTH)
    kp3 = kp.reshape(n_seq, seq_len, KPAD_WIDTH)
    sg3 = sg.reshape(n_seq, seq_len, ATTN_WIDTH)
    x3 = x2.reshape(n_seq, seq_len, d)
    n_keys = seq_len + (cache[0].shape[1] if has_cache else 0)

    def mod_map(b, i):
        return (0, mod_row0 + (b if per_seq_mod else 0), 0, 0)

    tile_spec = lambda width: pl.BlockSpec((None, q_tile, width), lambda b, i: (b, i, 0))
    seq_spec = lambda rows, width: pl.BlockSpec((None, rows, width), lambda b, i: (b, 0, 0))
    full2 = lambda b, i: (0, 0)
    in_specs = [tile_spec(ATTN_WIDTH), seq_spec(seq_len, KPAD_WIDTH), seq_spec(KV_WIDTH, seq_len)]
    args = [q3, kp3, vt]
    if has_cache:
        ck, cv = cache
        in_specs += [seq_spec(ck.shape[1], KV_WIDTH), seq_spec(cv.shape[1], KV_WIDTH)]
        args += [ck, cv]
    in_specs += [
        tile_spec(ATTN_WIDTH),
        tile_spec(d),
        pl.BlockSpec((None, None, 1, 3 * d), mod_map),
        pl.BlockSpec(w_out_bf.shape, full2),
        pl.BlockSpec(ln_g.shape, full2),
        pl.BlockSpec(ln_b.shape, full2),
    ]
    args += [sg3, x3, mods4, w_out_bf, ln_g, ln_b]
    out = pl.pallas_call(
        functools.partial(_attn_core_kernel, has_cache=has_cache),
        grid=(n_seq, tiles),
        in_specs=in_specs,
        out_specs=tile_spec(d),
        out_shape=jax.ShapeDtypeStruct((n_seq, seq_len, d), F32),
        scratch_shapes=[
            pltpu.VMEM((2, N_HEADS // N_KV_HEADS, n_keys, q_tile), F32),
            pltpu.VMEM((2, n_keys, q_tile), BF16),
            pltpu.VMEM((ATTN_WIDTH, q_tile), F32),
        ],
        compiler_params=pltpu.CompilerParams(vmem_limit_bytes=VMEM_LIMIT),
        name="attn_core_cache" if has_cache else "attn_core",
    )(*args)
    return out.reshape(n_seq * seq_len, d)


def _rec_proj_kernel(x_ref, mod_ref, w_ref, lb_ref, q_out, v_out, g_out, lf_fw, lf_bw, k_fw, k_bw):
    d = D_MODEL
    mod = mod_ref[...]
    shift, scale = mod[:, :d], mod[:, d:2 * d]
    h = (x_ref[...] * (1.0 + scale) + shift).astype(BF16)
    slab = REC_PROJ_SLAB

    def finish_silu(out_ref):
        def finish(cols, u):
            out_ref[:, cols] = (u * _sigmoid(u)).astype(out_ref.dtype)
        return finish

    def finish_gate(direction, lf_out, k_out):
        def finish(cols, z):
            lb = lb_ref[direction:direction + 1, cols]
            sig = _sigmoid(z)
            lf_out[:, cols] = jnp.log(lb + (1.0 - lb) * sig)
            k_out[:, cols] = ((1.0 - lb) * (1.0 - sig)).astype(k_out.dtype)
        return finish

    def finish_v(cols, u):
        v_out[:, cols] = u.astype(v_out.dtype)

    sections = (finish_silu(q_out), finish_gate(0, lf_fw, k_fw), finish_gate(1, lf_bw, k_bw),
                finish_v, finish_silu(g_out))
    per_section = REC_WIDTH // slab
    n_slabs = len(sections) * per_section
    project = lambda i: _dot(h, w_ref[:, i * slab:(i + 1) * slab])
    u_next = project(0)
    for i in range(n_slabs):
        u = u_next
        if i + 1 < n_slabs:
            u_next = project(i + 1)
        within = i % per_section
        sections[i // per_section](slice(within * slab, (within + 1) * slab), u)


def _rec_proj(x2, mods4, w_bf, lb, *, seq_len, mod_row0, per_seq_mod):
    n_tok = x2.shape[0]
    d = D_MODEL
    tiles_per_seq = seq_len // ROW_TILE

    def mod_map(i):
        row = mod_row0 + (i // tiles_per_seq if per_seq_mod else 0)
        return (1, row, 0, 0)

    row_spec = pl.BlockSpec((ROW_TILE, REC_WIDTH), lambda i: (i, 0))
    full = lambda i: (0, 0)
    out_dtypes = (BF16, BF16, BF16, F32, F32, BF16, BF16)
    return pl.pallas_call(
        _rec_proj_kernel,
        grid=(n_tok // ROW_TILE,),
        in_specs=[
            pl.BlockSpec((ROW_TILE, d), lambda i: (i, 0)),
            pl.BlockSpec((None, None, 1, 3 * d), mod_map),
            pl.BlockSpec(w_bf.shape, full),
            pl.BlockSpec(lb.shape, full),
        ],
        out_specs=tuple(row_spec for _ in out_dtypes),
        out_shape=tuple(jax.ShapeDtypeStruct((n_tok, REC_WIDTH), t) for t in out_dtypes),
        compiler_params=pltpu.CompilerParams(vmem_limit_bytes=VMEM_LIMIT),
        name="rec_proj",
    )(x2, mods4, w_bf, lb)


def _large_level_exponent(b, edges, half, backward):
    t_rows, lanes = b.shape
    g = half // SUBLANES
    n_blocks = t_rows // (2 * half)
    b5 = b.reshape(n_blocks, 2, g, SUBLANES, lanes)
    e5 = edges.reshape(n_blocks, 2, g, SUBLANES, lanes)
    centre = (e5[:, 1:2, 0:1] if backward else e5[:, 0:1, g - 1:g])
    first, second = b5[:, 0:1], b5[:, 1:2]
    if backward:
        parts = [first - centre, centre - second]
    else:
        parts = [centre - first, second - centre]
    return jnp.concatenate(parts, axis=1).reshape(t_rows, lanes)


def _block_diag(a, b):
    za = jnp.zeros(a.shape, a.dtype)
    return jnp.concatenate(
        [jnp.concatenate([a, za], axis=1), jnp.concatenate([za, b], axis=1)], axis=0)


def _pair_scores(a, c):
    return _dot_nt(a, _block_diag(c[:, :REC_DK], c[:, REC_DK:]))


def _gla_scores_stage(units, between_levels=()):
    t = GLA_TILE
    pending = list(between_levels)
    small = [h for h in GLA_LEVELS if h < SUBLANES]
    for u in units:
        lf2 = u["lf"]() * LOG2_E
        hi = lf2.astype(BF16)
        lo = (lf2 - hi.astype(F32)).astype(BF16)
        sums = _dot(u["sums_ref"][...], jnp.concatenate([hi, lo], axis=0))
        u["b"], u["edges"] = sums[:t], sums[t:2 * t]
        u["small"] = {h: sums[(2 + i) * t:(3 + i) * t] for i, h in enumerate(small)}
        u["scores"] = u["masks_ref"][0] * _pair_scores(u["q"](), u["k"]()).astype(BF16)

    def level(u, li, e):
        x = jnp.exp2(e).astype(BF16)
        z = _pair_scores(u["q"]() * x, u["k"]() * x).astype(BF16)
        u["scores"] = u["scores"] + u["masks_ref"][1 + li] * z

    for li, half in enumerate(GLA_LEVELS):
        for u in units:
            if half < SUBLANES:
                e = u["small"][half]
            else:
                e = _large_level_exponent(u["b"], u["edges"], half, u["backward"])
            level(u, li, e)
        if pending:
            pending.pop(0)()
    for thunk in pending:
        thunk()
    for u in units:
        u["save"](u["scores"], u["b"])


def _gla_state_stage(u):
    t = GLA_TILE
    dk = REC_DK
    b, st_ref = u["load_b"](), u["st_ref"]
    q, k, v = u["q"](), u["k"](), u["v"]()
    edge = b[0:1, :] if u["backward"] else b[t - 1:t, :]
    o = _dot(u["load_scores"](), _block_diag(v[:, :dk], v[:, dk:]))
    st_a, st_b = st_ref[0], st_ref[1]
    q_in = q * jnp.exp2(b).astype(BF16)
    o = o + _dot_nt(q_in, _block_diag(st_a.astype(BF16), st_b.astype(BF16)))
    k_edge = k * jnp.exp2(edge - b).astype(BF16)
    carry = jnp.exp2(edge)
    st_ref[0] = st_a * carry[:, :dk] + _dot_tn(v[:, :dk], k_edge[:, :dk])
    st_ref[1] = st_b * carry[:, dk:] + _dot_tn(v[:, dk:], k_edge[:, dk:])
    u["store"](o)


def _gla_kernel(*refs, n_tiles, has_state, want_state):
    refs = list(refs)
    q_ref, v_ref, lff_ref, lfb_ref, kf_ref, kb_ref = refs[:6]
    pos = 6
    if has_state:
        s0_ref = refs[pos]
        pos += 1
    sums_f_ref, sums_b_ref, mf_ref, mb_ref = refs[pos:pos + 4]
    pos += 4
    o_ref = refs[pos]
    pos += 1
    if want_state:
        s_out_ref = refs[pos]
        pos += 1
    st_ref, pipe_s, pipe_b = refs[pos:pos + 3]

    t = GLA_TILE
    n_seqs = q_ref.shape[0]
    n_items = n_seqs * n_tiles
    pair_lanes = 2 * REC_DK
    o_ref[...] = jnp.zeros(o_ref.shape, o_ref.dtype)
    for seq in range(n_seqs):
        for direction in range(2):
            for hd in range(GLA_HEADS_PER_STEP):
                if has_state:
                    st_ref[seq, direction, hd] = s0_ref[seq, direction, hd].T
                else:
                    st_ref[seq, direction, hd] = jnp.zeros((REC_DV, REC_DK), F32)

    def units_of(item, slot):
        seq, step = item // n_tiles, item % n_tiles
        units = []
        for direction, (lf_ref, k_ref, sums_ref, m_ref) in enumerate(
                ((lff_ref, kf_ref, sums_f_ref, mf_ref), (lfb_ref, kb_ref, sums_b_ref, mb_ref))):
            tile = step if direction == 0 else n_tiles - 1 - step
            rows = pl.ds(pl.multiple_of(tile * t, t), t)
            for pair in range(GLA_HEADS_PER_STEP // 2):
                lanes = slice(pair * pair_lanes, (pair + 1) * pair_lanes)
                ui = len(units)

                def store(o, rows=rows, lanes=lanes):
                    o_ref[seq, rows, lanes] += o

                def save(scores, b, ui=ui):
                    pipe_s[slot, ui] = scores
                    pipe_b[slot, ui] = b

                load = lambda ref, rows=rows, lanes=lanes: (lambda: ref[seq, rows, lanes])
                units.append(dict(
                    q=load(q_ref), k=load(k_ref), v=load(v_ref), lf=load(lf_ref),
                    st_ref=st_ref.at[seq, direction, pl.ds(2 * pair, 2)], sums_ref=sums_ref,
                    masks_ref=m_ref, backward=direction == 1, store=store, save=save,
                    load_scores=lambda ui=ui: pipe_s[slot, ui],
                    load_b=lambda ui=ui: pipe_b[slot, ui]))
        return units

    def body(item, carry):
        units = units_of(item, 0)
        _gla_scores_stage(units)
        for u in units:
            _gla_state_stage(u)
        return carry

    lax.fori_loop(0, n_items, body, 0)
    if want_state:
        for seq in range(n_seqs):
            for direction in range(2):
                for hd in range(GLA_HEADS_PER_STEP):
                    s_out_ref[seq, direction, hd] = st_ref[seq, direction, hd].T


def _gla(q, v, lf_fw, lf_bw, k_fw, k_bw, s0, consts, *, n_seq, seq_len, want_state):
    has_state = s0 is not None
    width = REC_WIDTH
    hps = GLA_HEADS_PER_STEP
    n_tiles = seq_len // GLA_TILE
    seqs = min(n_seq, max(1, GLA_ITEMS_PER_STEP // n_tiles))
    assert n_seq % seqs == 0
    n_units = hps
    seq3 = lambda a: a.reshape(n_seq, seq_len, width)
    head_spec = pl.BlockSpec((seqs, seq_len, hps * REC_DK), lambda b, h: (b, 0, h))
    state_spec = pl.BlockSpec((seqs, 2, hps, REC_DK, REC_DV), lambda b, h: (b, 0, h, 0, 0))
    in_specs = [head_spec] * 6
    args = [seq3(q), seq3(v), seq3(lf_fw), seq3(lf_bw), seq3(k_fw), seq3(k_bw)]
    if has_state:
        in_specs.append(state_spec)
        args.append(s0)
    for c in consts:
        in_specs.append(pl.BlockSpec(c.shape, lambda b, h, nd=c.ndim: (0,) * nd))
        args.append(c)
    out_shape = [jax.ShapeDtypeStruct((n_seq, seq_len, width), F32)]
    out_specs = [head_spec]
    if want_state:
        out_shape.append(jax.ShapeDtypeStruct((n_seq, 2, N_REC_HEADS, REC_DK, REC_DV), F32))
        out_specs.append(state_spec)
    res = pl.pallas_call(
        functools.partial(_gla_kernel, n_tiles=n_tiles, has_state=has_state, want_state=want_state),
        grid=(n_seq // seqs, N_REC_HEADS // hps),
        in_specs=in_specs,
        out_specs=tuple(out_specs),
        out_shape=tuple(out_shape),
        scratch_shapes=[
            pltpu.VMEM((seqs, 2, hps, REC_DV, REC_DK), F32),
            pltpu.VMEM((2, n_units, GLA_TILE, 2 * GLA_TILE), BF16),
            pltpu.VMEM((2, n_units, GLA_TILE, 2 * REC_DK), F32),
        ],
        compiler_params=pltpu.CompilerParams(vmem_limit_bytes=VMEM_LIMIT),
        name="gla_state_in" if has_state else "gla_state_out",
    )(*args)
    o = res[0].reshape(n_seq * seq_len, width)
    return (o, res[1]) if want_state else (o, None)


def _rec_out_kernel(o_ref, sg_ref, x_ref, mod_ref, ng_ref, w_ref, lng_ref, lnb_ref, out_ref):
    d = D_MODEL
    parts = []
    for hd in range(N_REC_HEADS):
        oh = o_ref[:, hd * REC_DV:(hd + 1) * REC_DV]
        ms = jnp.mean(oh * oh, axis=-1, keepdims=True)
        parts.append(oh * lax.rsqrt(ms + NORM_EPS) * ng_ref[...])
    o = jnp.concatenate(parts, axis=1)
    gated = (o * sg_ref[...].astype(F32)).astype(BF16)
    branch = _dot(gated, w_ref[...])
    gate = mod_ref[...][:, 2 * d:]
    y = DEEPNORM_ALPHA * x_ref[...] + gate * branch
    out_ref[...] = _layer_norm(y, lng_ref[...], lnb_ref[...])


def _rec_out(o, sg, x2, mods4, norm_gain, w_out_bf, ln_g, ln_b, *, seq_len, mod_row0, per_seq_mod):
    n_tok = x2.shape[0]
    d = D_MODEL
    tiles_per_seq = seq_len // ROW_TILE

    def mod_map(i):
        row = mod_row0 + (i // tiles_per_seq if per_seq_mod else 0)
        return (1, row, 0, 0)

    row_spec = pl.BlockSpec((ROW_TILE, d), lambda i: (i, 0))
    full = lambda i: (0, 0)
    return pl.pallas_call(
        _rec_out_kernel,
        grid=(n_tok // ROW_TILE,),
        in_specs=[
            row_spec, row_spec, row_spec,
            pl.BlockSpec((None, None, 1, 3 * d), mod_map),
            pl.BlockSpec(norm_gain.shape, full),
            pl.BlockSpec(w_out_bf.shape, full),
            pl.BlockSpec(ln_g.shape, full),
            pl.BlockSpec(ln_b.shape, full),
        ],
        out_specs=row_spec,
        out_shape=jax.ShapeDtypeStruct((n_tok, d), F32),
        compiler_params=pltpu.CompilerParams(vmem_limit_bytes=VMEM_LIMIT),
        name="rec_out",
    )(o, sg, x2, mods4, norm_gain, w_out_bf, ln_g, ln_b)


def _rope_tables(n_tokens):
    n_rows = n_tokens // GRID_W
    rows = jnp.repeat(jnp.arange(n_rows, dtype=F32), GRID_W)
    cols = jnp.tile(jnp.arange(GRID_W, dtype=F32), n_rows)
    inv_freq = 1.0 / (ROPE_THETA ** (jnp.arange(0, AXIS_DIM, 2, dtype=F32) / AXIS_DIM))
    ang_r = rows[:, None] * inv_freq[None, :]
    ang_c = cols[:, None] * inv_freq[None, :]
    ang = jnp.concatenate([ang_r, ang_r, ang_c, ang_c], axis=-1)
    cos, sin = jnp.cos(ang), jnp.sin(ang)
    first = (jnp.arange(HEAD_DIM) % AXIS_DIM) < AXIS_DIM // 2
    sin_a = jnp.where(first[None, :], -sin, 0.0)
    sin_b = jnp.where(first[None, :], 0.0, sin)
    reps = GROUP_LANES // HEAD_DIM
    return tuple(jnp.tile(t, (1, reps)) for t in (cos, sin_a, sin_b))


def _head_mean_matrix():
    idx = jnp.arange(GROUP_LANES) // HEAD_DIM
    return jnp.where(idx[:, None] == idx[None, :], 1.0 / HEAD_DIM, 0.0).astype(BF16)


def _gla_consts():
    t = GLA_TILE
    r = jnp.arange(t)[:, None]
    c = jnp.arange(t)[None, :]
    grp = SUBLANES * (r // SUBLANES)
    fw = [c <= r, c <= grp + SUBLANES - 1]
    bw = [c >= r, c >= grp]
    for half in GLA_LEVELS:
        if half >= SUBLANES:
            continue
        start = 2 * half * (r // (2 * half))
        upper = r - start >= half
        last_low, first_up = start + half - 1, start + half
        fw.append(jnp.where(upper, (c > last_low) & (c <= r), (c > r) & (c <= last_low)))
        bw.append(jnp.where(upper, (c >= first_up) & (c < r), (c >= r) & (c < first_up)))
    twice = lambda a: jnp.concatenate([a, a], axis=-1)
    sums_fw = twice(jnp.concatenate(fw, axis=0).astype(BF16))
    sums_bw = twice(jnp.concatenate(bw, axis=0).astype(BF16))
    masks = [(r == c)]
    for half in GLA_LEVELS:
        size = 2 * half
        masks.append((r // size == c // size) & (r % size >= half) & (c % size < half))
    m_fw = jnp.stack(masks).astype(BF16)
    m_bw = jnp.swapaxes(m_fw, 1, 2)
    return sums_fw, sums_bw, twice(m_fw), twice(m_bw)


def kernel(x_prompt, x_sample, cache_k, cache_v, state_rec, c, c_ctx, ada_w, ada_b, attn_w_in,
           attn_q_gain, attn_k_gain, attn_w_out, rec_w_in, rec_lower_bounds, rec_norm_gain,
           rec_w_out, ln_gain, ln_bias):
    d = D_MODEL
    n_p, len_p, _ = x_prompt.shape
    n_s, len_s, _ = x_sample.shape
    past = cache_k.shape[2]
    assert past % SLAB_LANES == 0 and len_p % SLAB_LANES == 0 and len_s % SLAB_LANES == 0

    cond = jnp.zeros((COND_ROWS, d), F32).at[0].set(c_ctx).at[1:1 + n_s].set(c)
    mods4 = _mods(cond, ada_w, ada_b).reshape(DEPTH, COND_ROWS, 1, 3 * d)
    lb_all = _lower_bounds(rec_lower_bounds)

    xp = x_prompt.reshape(n_p * len_p, d)
    xs = x_sample.reshape(n_s * len_s, d)

    w_in = attn_w_in[0].astype(BF16)
    w_out = attn_w_out[0].astype(BF16)
    reps = GROUP_LANES // HEAD_DIM
    qg = (jnp.tile(attn_q_gain[0], reps) * (LOG2_E / math.sqrt(HEAD_DIM))).reshape(1, GROUP_LANES)
    kg = jnp.tile(attn_k_gain[0], reps).reshape(1, GROUP_LANES)
    pn = _head_mean_matrix()
    ln_g = ln_gain[0].reshape(1, d)
    ln_b = ln_bias[0].reshape(1, d)

    q_p, kp_p, vt_p, g_p, k_p, v_p = _attn_proj(xp, mods4, w_in, pn, qg, kg, None, seq_len=len_p,
                                                mod_row0=0, per_seq_mod=False, cache_out=True)
    q_s, kp_s, vt_s, g_s = _attn_proj(xs, mods4, w_in, pn, qg, kg, _rope_tables(len_s), seq_len=len_s,
                                      mod_row0=1, per_seq_mod=True, cache_out=False)
    xp1 = _attn_core(q_p, kp_p, vt_p, None, g_p, xp, mods4, w_out, ln_g, ln_b, n_seq=n_p,
                     seq_len=len_p, q_tile=len_p, mod_row0=0, per_seq_mod=False)
    cache = (cache_k[:, 0].reshape(n_s, past, KV_WIDTH), cache_v[:, 0].reshape(n_s, past, KV_WIDTH))
    xs1 = _attn_core(q_s, kp_s, vt_s, cache, g_s, xs, mods4, w_out, ln_g, ln_b, n_seq=n_s,
                     seq_len=len_s, q_tile=ROW_TILE, mod_row0=1, per_seq_mod=True)
    new_cache_k = k_p.reshape(n_p, 1, len_p, N_KV_HEADS, HEAD_DIM)
    new_cache_v = v_p.reshape(n_p, 1, len_p, N_KV_HEADS, HEAD_DIM)

    rw_in = rec_w_in[0].astype(BF16)
    rw_out = rec_w_out[0].astype(BF16)
    lb = lb_all[1]
    ng = rec_norm_gain[0].reshape(1, REC_DV)
    ln_g = ln_gain[1].reshape(1, d)
    ln_b = ln_bias[1].reshape(1, d)
    consts = _gla_consts()

    outs = []
    states = None
    for x1, n_seq, seq_len, row0, per_seq, s0 in (
            (xp1, n_p, len_p, 0, False, None),
            (xs1, n_s, len_s, 1, True, state_rec[:, 0])):
        q, v, g, lf_fw, lf_bw, k_fw, k_bw = _rec_proj(x1, mods4, rw_in, lb, seq_len=seq_len,
                                                       mod_row0=row0, per_seq_mod=per_seq)
        o, st = _gla(q, v, lf_fw, lf_bw, k_fw, k_bw, s0, consts, n_seq=n_seq, seq_len=seq_len,
                     want_state=s0 is None)
        if st is not None:
            states = st
        outs.append(_rec_out(o, g, x1, mods4, ng, rw_out, ln_g, ln_b, seq_len=seq_len,
                             mod_row0=row0, per_seq_mod=per_seq))

    y_prompt = outs[0].reshape(n_p, len_p, d)
    y_sample = outs[1].reshape(n_s, len_s, d)
    new_state_rec = states.reshape(n_p, 1, 2, N_REC_HEADS, REC_DK, REC_DV)
    return (y_prompt, y_sample, new_cache_k, new_cache_v, new_state_rec)
```

```python
import functools
import math

import jax
import jax.numpy as jnp
from jax import lax
from jax.experimental import pallas as pl
from jax.experimental.pallas import tpu as pltpu

F32 = jnp.float32
BF16 = jnp.bfloat16

D_MODEL = 1024
DEPTH = 2
GRID_W = 64
N_HEADS = 16
N_KV_HEADS = 4
HEAD_DIM = 64
AXIS_DIM = HEAD_DIM // 2
ATTN_WIDTH = N_HEADS * HEAD_DIM
KV_WIDTH = N_KV_HEADS * HEAD_DIM
ROPE_THETA = 10000.0
N_REC_HEADS = 8
REC_DK = 128
REC_DV = 128
REC_WIDTH = N_REC_HEADS * REC_DK
NORM_EPS = 1e-6
LN_EPS = 1e-5
DEEPNORM_ALPHA = (2.0 * DEPTH) ** 0.25

SUBLANES = 8
BF16_ROWS = 16
COND_ROWS = SUBLANES
ROW_TILE = 256
GROUP_LANES = 256
SLAB_LANES = 128
KPAD_WIDTH = 2 * N_KV_HEADS * SLAB_LANES
REC_PROJ_SLAB = 256
GLA_TILE = 128
GLA_LEVELS = (1, 2, 4, 8, 16, 32, 64)
GLA_HEADS_PER_STEP = 4
GLA_ITEMS_PER_STEP = 8
LOG2_E = 1.4426950408889634
VMEM_LIMIT = 56 * 1024 * 1024


def _sigmoid(x):
    return 1.0 / (1.0 + jnp.exp(-x))


def _dot(a, b):
    return jnp.dot(a, b, preferred_element_type=F32)


def _dot_nt(a, b):
    return lax.dot_general(a, b, (((1,), (1,)), ((), ())), preferred_element_type=F32)


def _dot_tn(a, b):
    return lax.dot_general(a, b, (((0,), (0,)), ((), ())), preferred_element_type=F32)


def _layer_norm(y, g, b):
    mu = jnp.mean(y, axis=-1, keepdims=True)
    yc = y - mu
    var = jnp.mean(yc * yc, axis=-1, keepdims=True)
    return yc * lax.rsqrt(var + LN_EPS) * g + b


def _mods_kernel(cond_ref, w_ref, b_ref, out_ref):
    c = cond_ref[...]
    s = (c * _sigmoid(c)).astype(BF16)
    out_ref[...] = _dot(s, w_ref[...].astype(BF16)) + b_ref[...]


def _mods(cond, ada_w, ada_b):
    d = D_MODEL
    return pl.pallas_call(
        _mods_kernel,
        grid=(DEPTH, 3),
        in_specs=[
            pl.BlockSpec((COND_ROWS, d), lambda l, j: (0, 0)),
            pl.BlockSpec((None, d, d), lambda l, j: (l, 0, j)),
            pl.BlockSpec((None, 1, d), lambda l, j: (l, 0, j)),
        ],
        out_specs=pl.BlockSpec((None, COND_ROWS, d), lambda l, j: (l, 0, j)),
        out_shape=jax.ShapeDtypeStruct((DEPTH, COND_ROWS, 3 * d), F32),
        compiler_params=pltpu.CompilerParams(vmem_limit_bytes=VMEM_LIMIT),
        name="adaln_mods",
    )(cond, ada_w, ada_b.reshape(DEPTH, 1, 3 * d))


def _lower_bounds_kernel(r_ref, out_ref):
    r = [r_ref[i] for i in range(DEPTH)]
    m = functools.reduce(jnp.maximum, r)
    e = [jnp.exp(x - m) for x in r]
    tot = functools.reduce(lambda a, b: a + b, e)
    soft = [x / tot for x in e]
    acc = soft[0]
    for i in range(DEPTH):
        if i > 0:
            acc = acc + soft[i]
        out_ref[i] = acc - soft[0]


def _lower_bounds(rec_lower_bounds):
    return pl.pallas_call(
        _lower_bounds_kernel,
        out_shape=jax.ShapeDtypeStruct(rec_lower_bounds.shape, F32),
        name="rec_lower_bounds",
    )(rec_lower_bounds)


def _kv_head_slabs(k):
    lane = lax.broadcasted_iota(jnp.int32, (k.shape[0], SLAB_LANES), 1)
    low = lane < HEAD_DIM
    heads_per_slab = SLAB_LANES // HEAD_DIM
    out = []
    for j in range(N_KV_HEADS):
        tile = k[:, (j // heads_per_slab) * SLAB_LANES:(j // heads_per_slab + 1) * SLAB_LANES]
        moved = pltpu.roll(tile, HEAD_DIM, 1)
        at_low, at_high = (tile, moved) if j % heads_per_slab == 0 else (moved, tile)
        out.append(jnp.where(low, at_low, 0.0))
        out.append(jnp.where(low, 0.0, at_high))
    return jnp.concatenate(out, axis=1)


def _attn_proj_kernel(*refs, rope, cache_out):
    refs = list(refs)
    x_ref, mod_ref, w_ref, pn_ref, qg_ref, kg_ref = refs[:6]
    pos = 6
    if rope:
        cos_ref, sa_ref, sb_ref = refs[pos:pos + 3]
        pos += 3
    q_out, kp_out, vt_out, g_out = refs[pos:pos + 4]
    pos += 4
    if cache_out:
        k_out, v_out = refs[pos:pos + 2]
    d = D_MODEL
    mod = mod_ref[...]
    shift, scale = mod[:, :d], mod[:, d:2 * d]
    h = (x_ref[...] * (1.0 + scale) + shift).astype(BF16)
    pn = pn_ref[...]

    def norm_rope(u, gain):
        ms = _dot((u * u).astype(BF16), pn)
        y = u * lax.rsqrt(ms + NORM_EPS) * gain
        if rope:
            y = (y * cos_ref[...]
                 + pltpu.roll(y, GROUP_LANES - AXIS_DIM // 2, 1) * sa_ref[...]
                 + pltpu.roll(y, AXIS_DIM // 2, 1) * sb_ref[...])
        return y

    def finish_q(j, u):
        q_out[:, j * GROUP_LANES:(j + 1) * GROUP_LANES] = norm_rope(u, qg_ref[...]).astype(q_out.dtype)

    def finish_k(_, u):
        k = norm_rope(u, kg_ref[...])
        kp_out[...] = _kv_head_slabs(k).astype(kp_out.dtype)
        if cache_out:
            k_out[...] = k

    def finish_v(_, v):
        vt_out[...] = v.T.astype(vt_out.dtype)
        if cache_out:
            v_out[...] = v

    def finish_g(j, u):
        g_out[:, j * GROUP_LANES:(j + 1) * GROUP_LANES] = (u * _sigmoid(u)).astype(g_out.dtype)

    slabs = ([(finish_q, j) for j in range(N_KV_HEADS)] + [(finish_k, 0), (finish_v, 0)]
             + [(finish_g, j) for j in range(N_KV_HEADS)])
    project = lambda i: _dot(h, w_ref[:, i * GROUP_LANES:(i + 1) * GROUP_LANES])
    u_next = project(0)
    for i, (finish, j) in enumerate(slabs):
        u = u_next
        if i + 1 < len(slabs):
            u_next = project(i + 1)
        finish(j, u)


def _attn_proj(x2, mods4, w_bf, pn, qg, kg, rope_tabs, *, seq_len, mod_row0, per_seq_mod, cache_out):
    n_tok = x2.shape[0]
    d = D_MODEL
    tiles_per_seq = seq_len // ROW_TILE
    rope = rope_tabs is not None

    def mod_map(i):
        row = mod_row0 + (i // tiles_per_seq if per_seq_mod else 0)
        return (0, row, 0, 0)

    full = lambda i: (0, 0)
    in_specs = [
        pl.BlockSpec((ROW_TILE, d), lambda i: (i, 0)),
        pl.BlockSpec((None, None, 1, 3 * d), mod_map),
        pl.BlockSpec(w_bf.shape, full),
        pl.BlockSpec(pn.shape, full),
        pl.BlockSpec(qg.shape, full),
        pl.BlockSpec(kg.shape, full),
    ]
    args = [x2, mods4, w_bf, pn, qg, kg]
    if rope:
        for t in rope_tabs:
            in_specs.append(pl.BlockSpec((ROW_TILE, GROUP_LANES), lambda i: (i % tiles_per_seq, 0)))
            args.append(t)
    rows = lambda width: pl.BlockSpec((ROW_TILE, width), lambda i: (i, 0))
    out_shape = [
        jax.ShapeDtypeStruct((n_tok, ATTN_WIDTH), BF16),
        jax.ShapeDtypeStruct((n_tok, KPAD_WIDTH), BF16),
        jax.ShapeDtypeStruct((n_tok // seq_len, KV_WIDTH, seq_len), BF16),
        jax.ShapeDtypeStruct((n_tok, ATTN_WIDTH), BF16),
    ]
    out_specs = [
        rows(ATTN_WIDTH),
        rows(KPAD_WIDTH),
        pl.BlockSpec((None, KV_WIDTH, ROW_TILE), lambda i: (i // tiles_per_seq, 0, i % tiles_per_seq)),
        rows(ATTN_WIDTH),
    ]
    if cache_out:
        out_shape += [jax.ShapeDtypeStruct((n_tok, KV_WIDTH), F32)] * 2
        out_specs += [rows(KV_WIDTH)] * 2
    return pl.pallas_call(
        functools.partial(_attn_proj_kernel, rope=rope, cache_out=cache_out),
        grid=(n_tok // ROW_TILE,),
        in_specs=in_specs,
        out_specs=tuple(out_specs),
        out_shape=tuple(out_shape),
        compiler_params=pltpu.CompilerParams(vmem_limit_bytes=VMEM_LIMIT),
        name="attn_proj_rope" if rope else "attn_proj",
    )(*args)


def _attn_core_kernel(*refs, has_cache):
    refs = list(refs)
    q_ref, kp_ref, vt_ref = refs[:3]
    pos = 3
    if has_cache:
        kc_ref, vc_ref = refs[pos:pos + 2]
        pos += 2
    sg_ref, x_ref, mod_ref, w_ref, lng_ref, lnb_ref, out_ref, s_scr, p_scr, ot_scr = refs[pos:]
    d = D_MODEL
    n_new = kp_ref.shape[0]
    n_old = kc_ref.shape[0] if has_cache else 0
    n_keys = n_old + n_new
    group = N_HEADS // N_KV_HEADS
    heads_per_slab = SLAB_LANES // HEAD_DIM
    if has_cache:
        kc_slabs = _kv_head_slabs(kc_ref[...]).astype(BF16)
        vc_t = vc_ref[...].T.astype(BF16)

    def fold_rows(a, op):
        return functools.reduce(
            op, [a[r * SUBLANES:(r + 1) * SUBLANES, :] for r in range(a.shape[0] // SUBLANES)])

    def scores_phase(j):
        slot = j % 2
        maxima = []
        for hh in range(group):
            hd = j * group + hh
            slab, where = hd // heads_per_slab, hd % heads_per_slab
            q_slab = q_ref[:, slab * SLAB_LANES:(slab + 1) * SLAB_LANES]
            k_slab = heads_per_slab * j + where
            lanes = slice(k_slab * SLAB_LANES, (k_slab + 1) * SLAB_LANES)
            s = _dot_nt(kp_ref[:, lanes], q_slab)
            s_scr[slot, hh, n_old:, :] = s
            mx = fold_rows(s, jnp.maximum)
            if has_cache:
                s = _dot_nt(kc_slabs[:, lanes], q_slab)
                s_scr[slot, hh, :n_old, :] = s
                mx = jnp.maximum(mx, fold_rows(s, jnp.maximum))
            maxima.append(jnp.max(mx, axis=0, keepdims=True))
        return maxima

    def values_phase(j, maxima):
        slot = j % 2
        rows = slice(j * HEAD_DIM, (j + 1) * HEAD_DIM)
        v_t = vt_ref[rows, :]
        if has_cache:
            v_t = jnp.concatenate([vc_t[rows, :], v_t], axis=1)
        v_ext = jnp.concatenate([v_t, jnp.ones((BF16_ROWS, n_keys), BF16)], axis=0)
        for hh in range(group):
            hd = j * group + hh
            p_scr[hh % 2] = jnp.exp2(s_scr[slot, hh] - maxima[hh]).astype(BF16)
            acc = _dot(v_ext, p_scr[hh % 2])
            l = acc[HEAD_DIM:HEAD_DIM + 1, :]
            ot_scr[hd * HEAD_DIM:(hd + 1) * HEAD_DIM, :] = acc[:HEAD_DIM, :] * (1.0 / l)

    pending = scores_phase(0)
    for j in range(N_KV_HEADS):
        nxt = scores_phase(j + 1) if j + 1 < N_KV_HEADS else None
        values_phase(j, pending)
        pending = nxt

    o = ot_scr[...].T
    gated = (o * sg_ref[...].astype(F32)).astype(BF16)
    branch = _dot(gated, w_ref[...])
    gate = mod_ref[...][:, 2 * d:]
    y = DEEPNORM_ALPHA * x_ref[...] + gate * branch
    out_ref[...] = _layer_norm(y, lng_ref[...], lnb_ref[...])


def _attn_core(q, kp, vt, cache, sg, x2, mods4, w_out_bf, ln_g, ln_b, *, n_seq, seq_len, q_tile,
               mod_row0, per_seq_mod):
    d = D_MODEL
    tiles = seq_len // q_tile
    has_cache = cache is not None
    q3 = q.reshape(n_seq, seq_len, ATTN_WIDTH)
    kp3 = kp.reshape(n_seq, seq_len, KPAD_WIDTH)
    sg3 = sg.reshape(n_seq, seq_len, ATTN_WIDTH)
    x3 = x2.reshape(n_seq, seq_len, d)
    n_keys = seq_len + (cache[0].shape[1] if has_cache else 0)

    def mod_map(b, i):
        return (0, mod_row0 + (b if per_seq_mod else 0), 0, 0)

    tile_spec = lambda width: pl.BlockSpec((None, q_tile, width), lambda b, i: (b, i, 0))
    seq_spec = lambda rows, width: pl.BlockSpec((None, rows, width), lambda b, i: (b, 0, 0))
    full2 = lambda b, i: (0, 0)
    in_specs = [tile_spec(ATTN_WIDTH), seq_spec(seq_len, KPAD_WIDTH), seq_spec(KV_WIDTH, seq_len)]
    args = [q3, kp3, vt]
    if has_cache:
        ck, cv = cache
        in_specs += [seq_spec(ck.shape[1], KV_WIDTH), seq_spec(cv.shape[1], KV_WIDTH)]
        args += [ck, cv]
    in_specs += [
        tile_spec(ATTN_WIDTH),
        tile_spec(d),
        pl.BlockSpec((None, None, 1, 3 * d), mod_map),
        pl.BlockSpec(w_out_bf.shape, full2),
        pl.BlockSpec(ln_g.shape, full2),
        pl.BlockSpec(ln_b.shape, full2),
    ]
    args += [sg3, x3, mods4, w_out_bf, ln_g, ln_b]
    out = pl.pallas_call(
        functools.partial(_attn_core_kernel, has_cache=has_cache),
        grid=(n_seq, tiles),
        in_specs=in_specs,
        out_specs=tile_spec(d),
        out_shape=jax.ShapeDtypeStruct((n_seq, seq_len, d), F32),
        scratch_shapes=[
            pltpu.VMEM((2, N_HEADS // N_KV_HEADS, n_keys, q_tile), F32),
            pltpu.VMEM((2, n_keys, q_tile), BF16),
            pltpu.VMEM((ATTN_WIDTH, q_tile), F32),
        ],
        compiler_params=pltpu.CompilerParams(vmem_limit_bytes=VMEM_LIMIT),
        name="attn_core_cache" if has_cache else "attn_core",
    )(*args)
    return out.reshape(n_seq * seq_len, d)


def _rec_proj_kernel(x_ref, mod_ref, w_ref, lb_ref, q_out, v_out, g_out, lf_fw, lf_bw, k_fw, k_bw):
    d = D_MODEL
    mod = mod_ref[...]
    shift, scale = mod[:, :d], mod[:, d:2 * d]
    h = (x_ref[...] * (1.0 + scale) + shift).astype(BF16)
    slab = REC_PROJ_SLAB

    def store_channel_major(out_ref, cols, val):
        val_t = val.T.astype(out_ref.dtype)
        for i in range(val.shape[0] // GLA_TILE):
            out_ref[i, cols, :] = val_t[:, i * GLA_TILE:(i + 1) * GLA_TILE]

    def finish_q(cols, u):
        store_channel_major(q_out, cols, u * _sigmoid(u))

    def finish_gate(direction, lf_out, k_out):
        def finish(cols, z):
            lb = lb_ref[direction:direction + 1, cols]
            sig = _sigmoid(z)
            lf_out[:, cols] = jnp.log(lb + (1.0 - lb) * sig)
            store_channel_major(k_out, cols, (1.0 - lb) * (1.0 - sig))
        return finish

    def finish_v(cols, u):
        v_out[:, cols] = u.astype(v_out.dtype)

    def finish_g(cols, u):
        g_out[:, cols] = (u * _sigmoid(u)).astype(g_out.dtype)

    sections = (finish_q, finish_gate(0, lf_fw, k_fw), finish_gate(1, lf_bw, k_bw),
                finish_v, finish_g)
    per_section = REC_WIDTH // slab
    n_slabs = len(sections) * per_section
    project = lambda i: _dot(h, w_ref[:, i * slab:(i + 1) * slab])
    u_next = project(0)
    for i in range(n_slabs):
        u = u_next
        if i + 1 < n_slabs:
            u_next = project(i + 1)
        within = i % per_section
        sections[i // per_section](slice(within * slab, (within + 1) * slab), u)


def _rec_proj(x2, mods4, w_bf, lb, *, seq_len, mod_row0, per_seq_mod):
    n_tok = x2.shape[0]
    d = D_MODEL
    tiles_per_seq = seq_len // ROW_TILE

    def mod_map(i):
        row = mod_row0 + (i // tiles_per_seq if per_seq_mod else 0)
        return (1, row, 0, 0)

    row_spec = pl.BlockSpec((ROW_TILE, REC_WIDTH), lambda i: (i, 0))
    tiles = ROW_TILE // GLA_TILE
    tile_spec = pl.BlockSpec((None, tiles, REC_WIDTH, GLA_TILE),
                             lambda i: (i // tiles_per_seq, i % tiles_per_seq, 0, 0))
    full = lambda i: (0, 0)
    row_out = lambda t: (row_spec, jax.ShapeDtypeStruct((n_tok, REC_WIDTH), t))
    tile_out = (tile_spec, jax.ShapeDtypeStruct(
        (n_tok // seq_len, seq_len // GLA_TILE, REC_WIDTH, GLA_TILE), BF16))
    outs = (tile_out, row_out(BF16), row_out(BF16), row_out(F32), row_out(F32), tile_out, tile_out)
    return pl.pallas_call(
        _rec_proj_kernel,
        grid=(n_tok // ROW_TILE,),
        in_specs=[
            pl.BlockSpec((ROW_TILE, d), lambda i: (i, 0)),
            pl.BlockSpec((None, None, 1, 3 * d), mod_map),
            pl.BlockSpec(w_bf.shape, full),
            pl.BlockSpec(lb.shape, full),
        ],
        out_specs=tuple(spec for spec, _ in outs),
        out_shape=tuple(shape for _, shape in outs),
        compiler_params=pltpu.CompilerParams(vmem_limit_bytes=VMEM_LIMIT),
        name="rec_proj",
    )(x2, mods4, w_bf, lb)


def _block_diag(a, b):
    za = jnp.zeros(a.shape, a.dtype)
    return jnp.concatenate(
        [jnp.concatenate([a, za], axis=1), jnp.concatenate([za, b], axis=1)], axis=0)


def _pair_scores(a_t, c_t):
    return _dot_tn(a_t, _block_diag(c_t[:REC_DK, :], c_t[REC_DK:, :]))


def _gla_scores_stage(units):
    t = GLA_TILE
    for u in units:
        lf2 = u["lf"]() * LOG2_E
        hi = lf2.astype(BF16)
        lo = (lf2 - hi.astype(F32)).astype(BF16)
        u["sums"] = _dot_tn(jnp.concatenate([hi, lo], axis=0), u["sums_ref"][...])
        u["scores"] = u["masks_ref"][0] * _pair_scores(u["q_t"](), u["k_t"]()).astype(BF16)
    for li in range(len(GLA_LEVELS)):
        for u in units:
            x = jnp.exp2(u["sums"][:, (1 + li) * t:(2 + li) * t]).astype(BF16)
            z = _pair_scores(u["q_t"]() * x, u["k_t"]() * x).astype(BF16)
            u["scores"] = u["scores"] + u["masks_ref"][1 + li] * z
    for u in units:
        u["save"](u["scores"], u["sums"][:, :t])


def _gla_state_stage(u):
    t = GLA_TILE
    dk = REC_DK
    b_t, st_ref = u["load_b"](), u["st_ref"]
    q_t, k_t, v = u["q_t"](), u["k_t"](), u["v"]()
    edge = b_t[:, 0:1] if u["backward"] else b_t[:, t - 1:t]
    o = _dot(u["load_scores"](), _block_diag(v[:, :dk], v[:, dk:]))
    st_a, st_b = st_ref[0], st_ref[1]
    q_in = q_t * jnp.exp2(b_t).astype(BF16)
    o = o + _dot_tn(q_in, _block_diag(st_a.astype(BF16), st_b.astype(BF16)))
    k_edge = k_t * jnp.exp2(edge - b_t).astype(BF16)
    carry = jnp.exp2(edge)
    st_ref[0] = st_a * carry[:dk, :] + _dot(k_edge[:dk, :], v[:, :dk])
    st_ref[1] = st_b * carry[dk:, :] + _dot(k_edge[dk:, :], v[:, dk:])
    u["store"](o)


def _gla_kernel(*refs, n_tiles, has_state, want_state):
    refs = list(refs)
    q_ref, v_ref, lff_ref, lfb_ref, kf_ref, kb_ref = refs[:6]
    pos = 6
    if has_state:
        s0_ref = refs[pos]
        pos += 1
    sums_f_ref, sums_b_ref, mf_ref, mb_ref = refs[pos:pos + 4]
    pos += 4
    o_ref = refs[pos]
    pos += 1
    if want_state:
        s_out_ref = refs[pos]
        pos += 1
    st_ref, pipe_s, pipe_b = refs[pos:pos + 3]

    t = GLA_TILE
    n_seqs = v_ref.shape[0]
    n_items = n_seqs * n_tiles
    pair_lanes = 2 * REC_DK
    o_ref[...] = jnp.zeros(o_ref.shape, o_ref.dtype)
    if has_state:
        st_ref[...] = s0_ref[...]
    else:
        st_ref[...] = jnp.zeros(st_ref.shape, st_ref.dtype)

    def units_of(item):
        seq, step = item // n_tiles, item % n_tiles
        units = []
        for direction, (lf_ref, k_ref, sums_ref, m_ref) in enumerate(
                ((lff_ref, kf_ref, sums_f_ref, mf_ref), (lfb_ref, kb_ref, sums_b_ref, mb_ref))):
            tile = step if direction == 0 else n_tiles - 1 - step
            rows = pl.ds(pl.multiple_of(tile * t, t), t)
            for pair in range(GLA_HEADS_PER_STEP // 2):
                lanes = slice(pair * pair_lanes, (pair + 1) * pair_lanes)
                ui = len(units)

                def store(o, rows=rows, lanes=lanes):
                    o_ref[seq, rows, lanes] += o

                def save(scores, b_t, ui=ui):
                    pipe_s[ui] = scores
                    pipe_b[ui] = b_t

                rows_of = lambda ref, rows=rows, lanes=lanes: (lambda: ref[seq, rows, lanes])
                channels_of = lambda ref, tile=tile, lanes=lanes: (lambda: ref[seq, tile, lanes, :])
                units.append(dict(
                    q_t=channels_of(q_ref), k_t=channels_of(k_ref), v=rows_of(v_ref),
                    lf=rows_of(lf_ref), st_ref=st_ref.at[seq, direction, pl.ds(2 * pair, 2)],
                    sums_ref=sums_ref, masks_ref=m_ref, backward=direction == 1, store=store,
                    save=save, load_scores=lambda ui=ui: pipe_s[ui], load_b=lambda ui=ui: pipe_b[ui]))
        return units

    def body(item, carry):
        units = units_of(item)
        _gla_scores_stage(units)
        for u in units:
            _gla_state_stage(u)
        return carry

    lax.fori_loop(0, n_items, body, 0)
    if want_state:
        s_out_ref[...] = st_ref[...]


def _gla(q, v, lf_fw, lf_bw, k_fw, k_bw, s0, consts, *, n_seq, seq_len, want_state):
    has_state = s0 is not None
    width = REC_WIDTH
    hps = GLA_HEADS_PER_STEP
    n_tiles = seq_len // GLA_TILE
    seqs = min(n_seq, max(1, GLA_ITEMS_PER_STEP // n_tiles))
    assert n_seq % seqs == 0
    n_units = hps
    seq3 = lambda a: a.reshape(n_seq, seq_len, width)
    head_spec = pl.BlockSpec((seqs, seq_len, hps * REC_DK), lambda b, h: (b, 0, h))
    tile_spec = pl.BlockSpec((seqs, n_tiles, hps * REC_DK, GLA_TILE), lambda b, h: (b, 0, h, 0))
    state_spec = pl.BlockSpec((seqs, 2, hps, REC_DK, REC_DV), lambda b, h: (b, 0, h, 0, 0))
    in_specs = [tile_spec, head_spec, head_spec, head_spec, tile_spec, tile_spec]
    args = [q, seq3(v), seq3(lf_fw), seq3(lf_bw), k_fw, k_bw]
    if has_state:
        in_specs.append(state_spec)
        args.append(s0)
    for c in consts:
        in_specs.append(pl.BlockSpec(c.shape, lambda b, h, nd=c.ndim: (0,) * nd))
        args.append(c)
    out_shape = [jax.ShapeDtypeStruct((n_seq, seq_len, width), F32)]
    out_specs = [head_spec]
    if want_state:
        out_shape.append(jax.ShapeDtypeStruct((n_seq, 2, N_REC_HEADS, REC_DK, REC_DV), F32))
        out_specs.append(state_spec)
    res = pl.pallas_call(
        functools.partial(_gla_kernel, n_tiles=n_tiles, has_state=has_state, want_state=want_state),
        grid=(n_seq // seqs, N_REC_HEADS // hps),
        in_specs=in_specs,
        out_specs=tuple(out_specs),
        out_shape=tuple(out_shape),
        scratch_shapes=[
            pltpu.VMEM((seqs, 2, hps, REC_DK, REC_DV), F32),
            pltpu.VMEM((n_units, GLA_TILE, 2 * GLA_TILE), BF16),
            pltpu.VMEM((n_units, 2 * REC_DK, GLA_TILE), F32),
        ],
        compiler_params=pltpu.CompilerParams(vmem_limit_bytes=VMEM_LIMIT),
        name="gla_state_in" if has_state else "gla_state_out",
    )(*args)
    o = res[0].reshape(n_seq * seq_len, width)
    return (o, res[1]) if want_state else (o, None)


def _rec_out_kernel(o_ref, sg_ref, x_ref, mod_ref, ng_ref, w_ref, lng_ref, lnb_ref, out_ref):
    d = D_MODEL
    parts = []
    for hd in range(N_REC_HEADS):
        oh = o_ref[:, hd * REC_DV:(hd + 1) * REC_DV]
        ms = jnp.mean(oh * oh, axis=-1, keepdims=True)
        parts.append(oh * lax.rsqrt(ms + NORM_EPS) * ng_ref[...])
    o = jnp.concatenate(parts, axis=1)
    gated = (o * sg_ref[...].astype(F32)).astype(BF16)
    branch = _dot(gated, w_ref[...])
    gate = mod_ref[...][:, 2 * d:]
    y = DEEPNORM_ALPHA * x_ref[...] + gate * branch
    out_ref[...] = _layer_norm(y, lng_ref[...], lnb_ref[...])


def _rec_out(o, sg, x2, mods4, norm_gain, w_out_bf, ln_g, ln_b, *, seq_len, mod_row0, per_seq_mod):
    n_tok = x2.shape[0]
    d = D_MODEL
    tiles_per_seq = seq_len // ROW_TILE

    def mod_map(i):
        row = mod_row0 + (i // tiles_per_seq if per_seq_mod else 0)
        return (1, row, 0, 0)

    row_spec = pl.BlockSpec((ROW_TILE, d), lambda i: (i, 0))
    full = lambda i: (0, 0)
    return pl.pallas_call(
        _rec_out_kernel,
        grid=(n_tok // ROW_TILE,),
        in_specs=[
            row_spec, row_spec, row_spec,
            pl.BlockSpec((None, None, 1, 3 * d), mod_map),
            pl.BlockSpec(norm_gain.shape, full),
            pl.BlockSpec(w_out_bf.shape, full),
            pl.BlockSpec(ln_g.shape, full),
            pl.BlockSpec(ln_b.shape, full),
        ],
        out_specs=row_spec,
        out_shape=jax.ShapeDtypeStruct((n_tok, d), F32),
        compiler_params=pltpu.CompilerParams(vmem_limit_bytes=VMEM_LIMIT),
        name="rec_out",
    )(o, sg, x2, mods4, norm_gain, w_out_bf, ln_g, ln_b)


def _rope_tables(n_tokens):
    n_rows = n_tokens // GRID_W
    rows = jnp.repeat(jnp.arange(n_rows, dtype=F32), GRID_W)
    cols = jnp.tile(jnp.arange(GRID_W, dtype=F32), n_rows)
    inv_freq = 1.0 / (ROPE_THETA ** (jnp.arange(0, AXIS_DIM, 2, dtype=F32) / AXIS_DIM))
    ang_r = rows[:, None] * inv_freq[None, :]
    ang_c = cols[:, None] * inv_freq[None, :]
    ang = jnp.concatenate([ang_r, ang_r, ang_c, ang_c], axis=-1)
    cos, sin = jnp.cos(ang), jnp.sin(ang)
    first = (jnp.arange(HEAD_DIM) % AXIS_DIM) < AXIS_DIM // 2
    sin_a = jnp.where(first[None, :], -sin, 0.0)
    sin_b = jnp.where(first[None, :], 0.0, sin)
    reps = GROUP_LANES // HEAD_DIM
    return tuple(jnp.tile(t, (1, reps)) for t in (cos, sin_a, sin_b))


def _head_mean_matrix():
    idx = jnp.arange(GROUP_LANES) // HEAD_DIM
    return jnp.where(idx[:, None] == idx[None, :], 1.0 / HEAD_DIM, 0.0).astype(BF16)


def _gla_consts():
    t = GLA_TILE
    r = jnp.arange(t)[:, None]
    c = jnp.arange(t)[None, :]
    fw = [c <= r]
    bw = [c >= r]
    for half in GLA_LEVELS:
        start = 2 * half * (r // (2 * half))
        upper = r - start >= half
        last_low, first_up = start + half - 1, start + half
        fw.append(jnp.where(upper, (c > last_low) & (c <= r), (c > r) & (c <= last_low)))
        bw.append(jnp.where(upper, (c >= first_up) & (c < r), (c >= r) & (c < first_up)))
    twice = lambda a: jnp.concatenate([a, a], axis=-1)

    def summation(weights):
        cols = jnp.concatenate([w.T for w in weights], axis=1).astype(BF16)
        return jnp.concatenate([cols, cols], axis=0)

    sums_fw, sums_bw = summation(fw), summation(bw)
    masks = [(r == c)]
    for half in GLA_LEVELS:
        size = 2 * half
        masks.append((r // size == c // size) & (r % size >= half) & (c % size < half))
    m_fw = jnp.stack(masks).astype(BF16)
    m_bw = jnp.swapaxes(m_fw, 1, 2)
    return sums_fw, sums_bw, twice(m_fw), twice(m_bw)


def kernel(x_prompt, x_sample, cache_k, cache_v, state_rec, c, c_ctx, ada_w, ada_b, attn_w_in,
           attn_q_gain, attn_k_gain, attn_w_out, rec_w_in, rec_lower_bounds, rec_norm_gain,
           rec_w_out, ln_gain, ln_bias):
    d = D_MODEL
    n_p, len_p, _ = x_prompt.shape
    n_s, len_s, _ = x_sample.shape
    past = cache_k.shape[2]
    assert past % SLAB_LANES == 0 and len_p % SLAB_LANES == 0 and len_s % SLAB_LANES == 0

    cond = jnp.zeros((COND_ROWS, d), F32).at[0].set(c_ctx).at[1:1 + n_s].set(c)
    mods4 = _mods(cond, ada_w, ada_b).reshape(DEPTH, COND_ROWS, 1, 3 * d)
    lb_all = _lower_bounds(rec_lower_bounds)

    xp = x_prompt.reshape(n_p * len_p, d)
    xs = x_sample.reshape(n_s * len_s, d)

    w_in = attn_w_in[0].astype(BF16)
    w_out = attn_w_out[0].astype(BF16)
    reps = GROUP_LANES // HEAD_DIM
    qg = (jnp.tile(attn_q_gain[0], reps) * (LOG2_E / math.sqrt(HEAD_DIM))).reshape(1, GROUP_LANES)
    kg = jnp.tile(attn_k_gain[0], reps).reshape(1, GROUP_LANES)
    pn = _head_mean_matrix()
    ln_g = ln_gain[0].reshape(1, d)
    ln_b = ln_bias[0].reshape(1, d)

    q_p, kp_p, vt_p, g_p, k_p, v_p = _attn_proj(xp, mods4, w_in, pn, qg, kg, None, seq_len=len_p,
                                                mod_row0=0, per_seq_mod=False, cache_out=True)
    q_s, kp_s, vt_s, g_s = _attn_proj(xs, mods4, w_in, pn, qg, kg, _rope_tables(len_s), seq_len=len_s,
                                      mod_row0=1, per_seq_mod=True, cache_out=False)
    xp1 = _attn_core(q_p, kp_p, vt_p, None, g_p, xp, mods4, w_out, ln_g, ln_b, n_seq=n_p,
                     seq_len=len_p, q_tile=len_p, mod_row0=0, per_seq_mod=False)
    cache = (cache_k[:, 0].reshape(n_s, past, KV_WIDTH), cache_v[:, 0].reshape(n_s, past, KV_WIDTH))
    xs1 = _attn_core(q_s, kp_s, vt_s, cache, g_s, xs, mods4, w_out, ln_g, ln_b, n_seq=n_s,
                     seq_len=len_s, q_tile=ROW_TILE, mod_row0=1, per_seq_mod=True)
    new_cache_k = k_p.reshape(n_p, 1, len_p, N_KV_HEADS, HEAD_DIM)
    new_cache_v = v_p.reshape(n_p, 1, len_p, N_KV_HEADS, HEAD_DIM)

    rw_in = rec_w_in[0].astype(BF16)
    rw_out = rec_w_out[0].astype(BF16)
    lb = lb_all[1]
    ng = rec_norm_gain[0].reshape(1, REC_DV)
    ln_g = ln_gain[1].reshape(1, d)
    ln_b = ln_bias[1].reshape(1, d)
    consts = _gla_consts()

    outs = []
    states = None
    for x1, n_seq, seq_len, row0, per_seq, s0 in (
            (xp1, n_p, len_p, 0, False, None),
            (xs1, n_s, len_s, 1, True, state_rec[:, 0])):
        q, v, g, lf_fw, lf_bw, k_fw, k_bw = _rec_proj(x1, mods4, rw_in, lb, seq_len=seq_len,
                                                       mod_row0=row0, per_seq_mod=per_seq)
        o, st = _gla(q, v, lf_fw, lf_bw, k_fw, k_bw, s0, consts, n_seq=n_seq, seq_len=seq_len,
                     want_state=s0 is None)
        if st is not None:
            states = st
        outs.append(_rec_out(o, g, x1, mods4, ng, rw_out, ln_g, ln_b, seq_len=seq_len,
                             mod_row0=row0, per_seq_mod=per_seq))

    y_prompt = outs[0].reshape(n_p, len_p, d)
    y_sample = outs[1].reshape(n_s, len_s, d)
    new_state_rec = states.reshape(n_p, 1, 2, N_REC_HEADS, REC_DK, REC_DV)
    return (y_prompt, y_sample, new_cache_k, new_cache_v, new_state_rec)
```

```python
import functools
import math

import jax
import jax.numpy as jnp
from jax import lax
from jax.experimental import pallas as pl
from jax.experimental.pallas import tpu as pltpu

F32 = jnp.float32
BF16 = jnp.bfloat16

D_MODEL = 1024
DEPTH = 2
GRID_W = 64
N_HEADS = 16
N_KV_HEADS = 4
HEAD_DIM = 64
AXIS_DIM = HEAD_DIM // 2
ATTN_WIDTH = N_HEADS * HEAD_DIM
KV_WIDTH = N_KV_HEADS * HEAD_DIM
ROPE_THETA = 10000.0
N_REC_HEADS = 8
REC_DK = 128
REC_DV = 128
REC_WIDTH = N_REC_HEADS * REC_DK
NORM_EPS = 1e-6
LN_EPS = 1e-5
DEEPNORM_ALPHA = (2.0 * DEPTH) ** 0.25

SUBLANES = 8
BF16_ROWS = 16
COND_ROWS = SUBLANES
ROW_TILE = 256
OUT_ROW_TILE = 512
OUT_ROW_CHUNK = 256
GROUP_LANES = 256
SLAB_LANES = 128
KPAD_WIDTH = 2 * N_KV_HEADS * SLAB_LANES
REC_PROJ_SLAB = 256
GLA_TILE = 128
GLA_LEVELS = (1, 2, 4, 8, 16, 32, 64)
GLA_HEADS_PER_STEP = 4
GLA_ITEMS_PER_STEP = 8
LOG2_E = 1.4426950408889634
VMEM_LIMIT = 56 * 1024 * 1024


def _sigmoid(x):
    return 1.0 / (1.0 + jnp.exp(-x))


def _dot(a, b):
    return jnp.dot(a, b, preferred_element_type=F32)


def _dot_nt(a, b):
    return lax.dot_general(a, b, (((1,), (1,)), ((), ())), preferred_element_type=F32)


def _dot_tn(a, b):
    return lax.dot_general(a, b, (((0,), (0,)), ((), ())), preferred_element_type=F32)


def _layer_norm(y, g, b):
    mu = jnp.mean(y, axis=-1, keepdims=True)
    yc = y - mu
    var = jnp.mean(yc * yc, axis=-1, keepdims=True)
    return yc * lax.rsqrt(var + LN_EPS) * g + b


def _mods_kernel(cond_ref, w_ref, b_ref, out_ref):
    c = cond_ref[...]
    s = (c * _sigmoid(c)).astype(BF16)
    out_ref[...] = _dot(s, w_ref[...].astype(BF16)) + b_ref[...]


def _mods(cond, ada_w, ada_b):
    d = D_MODEL
    return pl.pallas_call(
        _mods_kernel,
        grid=(DEPTH, 3),
        in_specs=[
            pl.BlockSpec((COND_ROWS, d), lambda l, j: (0, 0)),
            pl.BlockSpec((None, d, d), lambda l, j: (l, 0, j)),
            pl.BlockSpec((None, 1, d), lambda l, j: (l, 0, j)),
        ],
        out_specs=pl.BlockSpec((None, COND_ROWS, d), lambda l, j: (l, 0, j)),
        out_shape=jax.ShapeDtypeStruct((DEPTH, COND_ROWS, 3 * d), F32),
        compiler_params=pltpu.CompilerParams(vmem_limit_bytes=VMEM_LIMIT),
        name="adaln_mods",
    )(cond, ada_w, ada_b.reshape(DEPTH, 1, 3 * d))


def _lower_bounds_kernel(r_ref, out_ref):
    r = [r_ref[i] for i in range(DEPTH)]
    m = functools.reduce(jnp.maximum, r)
    e = [jnp.exp(x - m) for x in r]
    tot = functools.reduce(lambda a, b: a + b, e)
    soft = [x / tot for x in e]
    acc = soft[0]
    for i in range(DEPTH):
        if i > 0:
            acc = acc + soft[i]
        out_ref[i] = acc - soft[0]


def _lower_bounds(rec_lower_bounds):
    return pl.pallas_call(
        _lower_bounds_kernel,
        out_shape=jax.ShapeDtypeStruct(rec_lower_bounds.shape, F32),
        name="rec_lower_bounds",
    )(rec_lower_bounds)


def _kv_head_slabs(k):
    lane = lax.broadcasted_iota(jnp.int32, (k.shape[0], SLAB_LANES), 1)
    low = lane < HEAD_DIM
    heads_per_slab = SLAB_LANES // HEAD_DIM
    out = []
    for j in range(N_KV_HEADS):
        tile = k[:, (j // heads_per_slab) * SLAB_LANES:(j // heads_per_slab + 1) * SLAB_LANES]
        moved = pltpu.roll(tile, HEAD_DIM, 1)
        at_low, at_high = (tile, moved) if j % heads_per_slab == 0 else (moved, tile)
        out.append(jnp.where(low, at_low, 0.0))
        out.append(jnp.where(low, 0.0, at_high))
    return jnp.concatenate(out, axis=1)


def _attn_proj_kernel(*refs, rope, cache_out):
    refs = list(refs)
    x_ref, mod_ref, w_ref, pn_ref, qg_ref, kg_ref = refs[:6]
    pos = 6
    if rope:
        cos_ref, sa_ref, sb_ref = refs[pos:pos + 3]
        pos += 3
    q_out, kp_out, vt_out, g_out = refs[pos:pos + 4]
    pos += 4
    if cache_out:
        k_out, v_out = refs[pos:pos + 2]
    d = D_MODEL
    mod = mod_ref[...]
    shift, scale = mod[:, :d], mod[:, d:2 * d]
    h = (x_ref[...] * (1.0 + scale) + shift).astype(BF16)
    pn = pn_ref[...]

    def norm_rope(u, gain):
        ms = _dot((u * u).astype(BF16), pn)
        y = u * lax.rsqrt(ms + NORM_EPS) * gain
        if rope:
            y = (y * cos_ref[...]
                 + pltpu.roll(y, GROUP_LANES - AXIS_DIM // 2, 1) * sa_ref[...]
                 + pltpu.roll(y, AXIS_DIM // 2, 1) * sb_ref[...])
        return y

    def finish_q(j, u):
        q_out[:, j * GROUP_LANES:(j + 1) * GROUP_LANES] = norm_rope(u, qg_ref[...]).astype(q_out.dtype)

    def finish_k(_, u):
        k = norm_rope(u, kg_ref[...])
        kp_out[...] = _kv_head_slabs(k).astype(kp_out.dtype)
        if cache_out:
            k_out[...] = k

    def finish_v(_, v):
        vt_out[...] = v.T.astype(vt_out.dtype)
        if cache_out:
            v_out[...] = v

    def finish_g(j, u):
        g_out[:, j * GROUP_LANES:(j + 1) * GROUP_LANES] = (u * _sigmoid(u)).astype(g_out.dtype)

    slabs = ([(finish_q, j) for j in range(N_KV_HEADS)] + [(finish_k, 0), (finish_v, 0)]
             + [(finish_g, j) for j in range(N_KV_HEADS)])
    project = lambda i: _dot(h, w_ref[:, i * GROUP_LANES:(i + 1) * GROUP_LANES])
    v_slab = N_KV_HEADS + 1
    order = [i for i in range(len(slabs)) if i != v_slab] + [v_slab]
    u_next = project(order[0])
    for n, i in enumerate(order):
        u = u_next
        if n + 1 < len(order):
            u_next = project(order[n + 1])
        finish, j = slabs[i]
        finish(j, u)


def _attn_proj(x2, mods4, w_bf, pn, qg, kg, rope_tabs, *, seq_len, mod_row0, per_seq_mod, cache_out):
    n_tok = x2.shape[0]
    d = D_MODEL
    tiles_per_seq = seq_len // ROW_TILE
    rope = rope_tabs is not None

    def mod_map(i):
        row = mod_row0 + (i // tiles_per_seq if per_seq_mod else 0)
        return (0, row, 0, 0)

    full = lambda i: (0, 0)
    in_specs = [
        pl.BlockSpec((ROW_TILE, d), lambda i: (i, 0)),
        pl.BlockSpec((None, None, 1, 3 * d), mod_map),
        pl.BlockSpec(w_bf.shape, full),
        pl.BlockSpec(pn.shape, full),
        pl.BlockSpec(qg.shape, full),
        pl.BlockSpec(kg.shape, full),
    ]
    args = [x2, mods4, w_bf, pn, qg, kg]
    if rope:
        for t in rope_tabs:
            in_specs.append(pl.BlockSpec((ROW_TILE, GROUP_LANES), lambda i: (i % tiles_per_seq, 0)))
            args.append(t)
    rows = lambda width: pl.BlockSpec((ROW_TILE, width), lambda i: (i, 0))
    out_shape = [
        jax.ShapeDtypeStruct((n_tok, ATTN_WIDTH), BF16),
        jax.ShapeDtypeStruct((n_tok, KPAD_WIDTH), BF16),
        jax.ShapeDtypeStruct((n_tok // seq_len, KV_WIDTH, seq_len), BF16),
        jax.ShapeDtypeStruct((n_tok, ATTN_WIDTH), BF16),
    ]
    out_specs = [
        rows(ATTN_WIDTH),
        rows(KPAD_WIDTH),
        pl.BlockSpec((None, KV_WIDTH, ROW_TILE), lambda i: (i // tiles_per_seq, 0, i % tiles_per_seq)),
        rows(ATTN_WIDTH),
    ]
    if cache_out:
        out_shape += [jax.ShapeDtypeStruct((n_tok, KV_WIDTH), F32)] * 2
        out_specs += [rows(KV_WIDTH)] * 2
    return pl.pallas_call(
        functools.partial(_attn_proj_kernel, rope=rope, cache_out=cache_out),
        grid=(n_tok // ROW_TILE,),
        in_specs=in_specs,
        out_specs=tuple(out_specs),
        out_shape=tuple(out_shape),
        compiler_params=pltpu.CompilerParams(vmem_limit_bytes=VMEM_LIMIT),
        name="attn_proj_rope" if rope else "attn_proj",
    )(*args)


def _attn_core_kernel(*refs, has_cache):
    refs = list(refs)
    q_ref, kp_ref, vt_ref = refs[:3]
    pos = 3
    if has_cache:
        kc_ref, vc_ref = refs[pos:pos + 2]
        pos += 2
    sg_ref, x_ref, mod_ref, w_ref, lng_ref, lnb_ref, out_ref, s_scr, p_scr, ot_scr = refs[pos:]
    d = D_MODEL
    n_new = kp_ref.shape[0]
    n_old = kc_ref.shape[0] if has_cache else 0
    n_keys = n_old + n_new
    group = N_HEADS // N_KV_HEADS
    heads_per_slab = SLAB_LANES // HEAD_DIM
    if has_cache:
        kc_slabs = _kv_head_slabs(kc_ref[...]).astype(BF16)
        vc_t = vc_ref[...].T.astype(BF16)

    def fold_rows(a, op):
        return functools.reduce(
            op, [a[r * SUBLANES:(r + 1) * SUBLANES, :] for r in range(a.shape[0] // SUBLANES)])

    def scores_phase(j):
        slot = j % 2
        maxima = []
        for hh in range(group):
            hd = j * group + hh
            slab, where = hd // heads_per_slab, hd % heads_per_slab
            q_slab = q_ref[:, slab * SLAB_LANES:(slab + 1) * SLAB_LANES]
            k_slab = heads_per_slab * j + where
            lanes = slice(k_slab * SLAB_LANES, (k_slab + 1) * SLAB_LANES)
            s = _dot_nt(kp_ref[:, lanes], q_slab)
            s_scr[slot, hh, n_old:, :] = s
            mx = fold_rows(s, jnp.maximum)
            if has_cache:
                s = _dot_nt(kc_slabs[:, lanes], q_slab)
                s_scr[slot, hh, :n_old, :] = s
                mx = jnp.maximum(mx, fold_rows(s, jnp.maximum))
            maxima.append(jnp.max(mx, axis=0, keepdims=True))
        return maxima

    def values_phase(j, maxima):
        slot = j % 2
        rows = slice(j * HEAD_DIM, (j + 1) * HEAD_DIM)
        v_t = vt_ref[rows, :]
        if has_cache:
            v_t = jnp.concatenate([vc_t[rows, :], v_t], axis=1)
        v_ext = jnp.concatenate([v_t, jnp.ones((BF16_ROWS, n_keys), BF16)], axis=0)
        for hh in range(group):
            hd = j * group + hh
            p_scr[hh % 2] = jnp.exp2(s_scr[slot, hh] - maxima[hh]).astype(BF16)
            acc = _dot(v_ext, p_scr[hh % 2])
            l = acc[HEAD_DIM:HEAD_DIM + 1, :]
            ot_scr[hd * HEAD_DIM:(hd + 1) * HEAD_DIM, :] = acc[:HEAD_DIM, :] * (1.0 / l)

    pending = scores_phase(0)
    for j in range(N_KV_HEADS):
        nxt = scores_phase(j + 1) if j + 1 < N_KV_HEADS else None
        values_phase(j, pending)
        pending = nxt

    o = ot_scr[...].T
    gated = (o * sg_ref[...].astype(F32)).astype(BF16)
    branch = _dot(gated, w_ref[...])
    gate = mod_ref[...][:, 2 * d:]
    y = DEEPNORM_ALPHA * x_ref[...] + gate * branch
    out_ref[...] = _layer_norm(y, lng_ref[...], lnb_ref[...])


def _attn_core(q, kp, vt, cache, sg, x2, mods4, w_out_bf, ln_g, ln_b, *, n_seq, seq_len, q_tile,
               mod_row0, per_seq_mod):
    d = D_MODEL
    tiles = seq_len // q_tile
    has_cache = cache is not None
    q3 = q.reshape(n_seq, seq_len, ATTN_WIDTH)
    kp3 = kp.reshape(n_seq, seq_len, KPAD_WIDTH)
    sg3 = sg.reshape(n_seq, seq_len, ATTN_WIDTH)
    x3 = x2.reshape(n_seq, seq_len, d)
    n_keys = seq_len + (cache[0].shape[1] if has_cache else 0)

    def mod_map(b, i):
        return (0, mod_row0 + (b if per_seq_mod else 0), 0, 0)

    tile_spec = lambda width: pl.BlockSpec((None, q_tile, width), lambda b, i: (b, i, 0))
    seq_spec = lambda rows, width: pl.BlockSpec((None, rows, width), lambda b, i: (b, 0, 0))
    full2 = lambda b, i: (0, 0)
    in_specs = [tile_spec(ATTN_WIDTH), seq_spec(seq_len, KPAD_WIDTH), seq_spec(KV_WIDTH, seq_len)]
    args = [q3, kp3, vt]
    if has_cache:
        ck, cv = cache
        in_specs += [seq_spec(ck.shape[1], KV_WIDTH), seq_spec(cv.shape[1], KV_WIDTH)]
        args += [ck, cv]
    in_specs += [
        tile_spec(ATTN_WIDTH),
        tile_spec(d),
        pl.BlockSpec((None, None, 1, 3 * d), mod_map),
        pl.BlockSpec(w_out_bf.shape, full2),
        pl.BlockSpec(ln_g.shape, full2),
        pl.BlockSpec(ln_b.shape, full2),
    ]
    args += [sg3, x3, mods4, w_out_bf, ln_g, ln_b]
    out = pl.pallas_call(
        functools.partial(_attn_core_kernel, has_cache=has_cache),
        grid=(n_seq, tiles),
        in_specs=in_specs,
        out_specs=tile_spec(d),
        out_shape=jax.ShapeDtypeStruct((n_seq, seq_len, d), F32),
        scratch_shapes=[
            pltpu.VMEM((2, N_HEADS // N_KV_HEADS, n_keys, q_tile), F32),
            pltpu.VMEM((2, n_keys, q_tile), BF16),
            pltpu.VMEM((ATTN_WIDTH, q_tile), F32),
        ],
        compiler_params=pltpu.CompilerParams(vmem_limit_bytes=VMEM_LIMIT),
        name="attn_core_cache" if has_cache else "attn_core",
    )(*args)
    return out.reshape(n_seq * seq_len, d)


def _rec_proj_kernel(x_ref, mod_ref, w_ref, lb_ref, q_out, v_out, g_out, lf_fw, lf_bw, k_fw, k_bw):
    d = D_MODEL
    mod = mod_ref[...]
    shift, scale = mod[:, :d], mod[:, d:2 * d]
    h = (x_ref[...] * (1.0 + scale) + shift).astype(BF16)
    slab = REC_PROJ_SLAB

    def store_channel_major(out_ref, cols, val):
        val_t = val.T.astype(out_ref.dtype)
        for i in range(val.shape[0] // GLA_TILE):
            out_ref[i, cols, :] = val_t[:, i * GLA_TILE:(i + 1) * GLA_TILE]

    def finish_q(cols, u):
        store_channel_major(q_out, cols, u * _sigmoid(u))

    def finish_gate(direction, lf_out, k_out):
        def finish(cols, z):
            lb = lb_ref[direction:direction + 1, cols]
            sig = _sigmoid(z)
            lf_out[:, cols] = jnp.log(lb + (1.0 - lb) * sig)
            store_channel_major(k_out, cols, (1.0 - lb) * (1.0 - sig))
        return finish

    def finish_v(cols, u):
        v_out[:, cols] = u.astype(v_out.dtype)

    def finish_g(cols, u):
        g_out[:, cols] = (u * _sigmoid(u)).astype(g_out.dtype)

    sections = (finish_q, finish_gate(0, lf_fw, k_fw), finish_gate(1, lf_bw, k_bw),
                finish_v, finish_g)
    per_section = REC_WIDTH // slab
    n_slabs = len(sections) * per_section
    project = lambda i: _dot(h, w_ref[:, i * slab:(i + 1) * slab])
    v_section = sections.index(finish_v)
    order = ([i for i in range(n_slabs) if i // per_section != v_section]
             + [i for i in range(n_slabs) if i // per_section == v_section])
    u_next = project(order[0])
    for n, i in enumerate(order):
        u = u_next
        if n + 1 < n_slabs:
            u_next = project(order[n + 1])
        within = i % per_section
        sections[i // per_section](slice(within * slab, (within + 1) * slab), u)


def _rec_proj(x2, mods4, w_bf, lb, *, seq_len, mod_row0, per_seq_mod):
    n_tok = x2.shape[0]
    d = D_MODEL
    tiles_per_seq = seq_len // ROW_TILE

    def mod_map(i):
        row = mod_row0 + (i // tiles_per_seq if per_seq_mod else 0)
        return (1, row, 0, 0)

    row_spec = pl.BlockSpec((ROW_TILE, REC_WIDTH), lambda i: (i, 0))
    tiles = ROW_TILE // GLA_TILE
    tile_spec = pl.BlockSpec((None, tiles, REC_WIDTH, GLA_TILE),
                             lambda i: (i // tiles_per_seq, i % tiles_per_seq, 0, 0))
    full = lambda i: (0, 0)
    row_out = lambda t: (row_spec, jax.ShapeDtypeStruct((n_tok, REC_WIDTH), t))
    tile_out = (tile_spec, jax.ShapeDtypeStruct(
        (n_tok // seq_len, seq_len // GLA_TILE, REC_WIDTH, GLA_TILE), BF16))
    outs = (tile_out, row_out(BF16), row_out(BF16), row_out(F32), row_out(F32), tile_out, tile_out)
    return pl.pallas_call(
        _rec_proj_kernel,
        grid=(n_tok // ROW_TILE,),
        in_specs=[
            pl.BlockSpec((ROW_TILE, d), lambda i: (i, 0)),
            pl.BlockSpec((None, None, 1, 3 * d), mod_map),
            pl.BlockSpec(w_bf.shape, full),
            pl.BlockSpec(lb.shape, full),
        ],
        out_specs=tuple(spec for spec, _ in outs),
        out_shape=tuple(shape for _, shape in outs),
        compiler_params=pltpu.CompilerParams(vmem_limit_bytes=VMEM_LIMIT),
        name="rec_proj",
    )(x2, mods4, w_bf, lb)


def _block_diag(a, b):
    za = jnp.zeros(a.shape, a.dtype)
    return jnp.concatenate(
        [jnp.concatenate([a, za], axis=1), jnp.concatenate([za, b], axis=1)], axis=0)


def _pair_scores(a_t, c_t):
    return _dot_tn(a_t, _block_diag(c_t[:REC_DK, :], c_t[REC_DK:, :]))


def _gla_scores_stage(units):
    t = GLA_TILE
    for u in units:
        lf2 = u["lf"]() * LOG2_E
        hi = lf2.astype(BF16)
        lo = (lf2 - hi.astype(F32)).astype(BF16)
        u["sums"] = _dot_tn(jnp.concatenate([hi, lo], axis=0), u["sums_ref"][...])
        u["scores"] = u["masks_ref"][0] * _pair_scores(u["q_t"](), u["k_t"]()).astype(BF16)
    for li in range(len(GLA_LEVELS)):
        for u in units:
            x = jnp.exp2(u["sums"][:, (1 + li) * t:(2 + li) * t]).astype(BF16)
            z = _pair_scores(u["q_t"]() * x, u["k_t"]() * x).astype(BF16)
            u["scores"] = u["scores"] + u["masks_ref"][1 + li] * z
    for u in units:
        u["save"](u["scores"], u["sums"][:, :t])


def _gla_state_stage(u):
    t = GLA_TILE
    dk = REC_DK
    b_t, st_ref = u["load_b"](), u["st_ref"]
    q_t, k_t, v = u["q_t"](), u["k_t"](), u["v"]()
    edge = b_t[:, 0:1] if u["backward"] else b_t[:, t - 1:t]
    o = _dot(u["load_scores"](), _block_diag(v[:, :dk], v[:, dk:]))
    st_a, st_b = st_ref[0], st_ref[1]
    q_in = q_t * jnp.exp2(b_t).astype(BF16)
    o = o + _dot_tn(q_in, _block_diag(st_a.astype(BF16), st_b.astype(BF16)))
    k_edge = k_t * jnp.exp2(edge - b_t).astype(BF16)
    carry = jnp.exp2(edge)
    st_ref[0] = st_a * carry[:dk, :] + _dot(k_edge[:dk, :], v[:, :dk])
    st_ref[1] = st_b * carry[dk:, :] + _dot(k_edge[dk:, :], v[:, dk:])
    u["store"](o)


def _gla_kernel(*refs, n_tiles, has_state, want_state):
    refs = list(refs)
    q_ref, v_ref, lff_ref, lfb_ref, kf_ref, kb_ref = refs[:6]
    pos = 6
    if has_state:
        s0_ref = refs[pos]
        pos += 1
    sums_f_ref, sums_b_ref, mf_ref, mb_ref = refs[pos:pos + 4]
    pos += 4
    o_ref = refs[pos]
    pos += 1
    if want_state:
        s_out_ref = refs[pos]
        pos += 1
    st_ref, pipe_s, pipe_b = refs[pos:pos + 3]

    t = GLA_TILE
    n_seqs = v_ref.shape[0]
    n_items = n_seqs * n_tiles
    pair_lanes = 2 * REC_DK
    o_ref[...] = jnp.zeros(o_ref.shape, o_ref.dtype)
    if has_state:
        st_ref[...] = s0_ref[...]
    else:
        st_ref[...] = jnp.zeros(st_ref.shape, st_ref.dtype)

    def units_of(item):
        seq, step = item // n_tiles, item % n_tiles
        units = []
        for direction, (lf_ref, k_ref, sums_ref, m_ref) in enumerate(
                ((lff_ref, kf_ref, sums_f_ref, mf_ref), (lfb_ref, kb_ref, sums_b_ref, mb_ref))):
            tile = step if direction == 0 else n_tiles - 1 - step
            rows = pl.ds(pl.multiple_of(tile * t, t), t)
            for pair in range(GLA_HEADS_PER_STEP // 2):
                lanes = slice(pair * pair_lanes, (pair + 1) * pair_lanes)
                ui = len(units)

                def store(o, rows=rows, lanes=lanes):
                    o_ref[seq, rows, lanes] += o

                def save(scores, b_t, ui=ui):
                    pipe_s[ui] = scores
                    pipe_b[ui] = b_t

                rows_of = lambda ref, rows=rows, lanes=lanes: (lambda: ref[seq, rows, lanes])
                channels_of = lambda ref, tile=tile, lanes=lanes: (lambda: ref[seq, tile, lanes, :])
                units.append(dict(
                    q_t=channels_of(q_ref), k_t=channels_of(k_ref), v=rows_of(v_ref),
                    lf=rows_of(lf_ref), st_ref=st_ref.at[seq, direction, pl.ds(2 * pair, 2)],
                    sums_ref=sums_ref, masks_ref=m_ref, backward=direction == 1, store=store,
                    save=save, load_scores=lambda ui=ui: pipe_s[ui], load_b=lambda ui=ui: pipe_b[ui]))
        return units

    def body(item, carry):
        units = units_of(item)
        _gla_scores_stage(units)
        for u in units:
            _gla_state_stage(u)
        return carry

    lax.fori_loop(0, n_items, body, 0)
    if want_state:
        s_out_ref[...] = st_ref[...]


def _gla(q, v, lf_fw, lf_bw, k_fw, k_bw, s0, consts, *, n_seq, seq_len, want_state):
    has_state = s0 is not None
    width = REC_WIDTH
    hps = GLA_HEADS_PER_STEP
    n_tiles = seq_len // GLA_TILE
    seqs = min(n_seq, max(1, GLA_ITEMS_PER_STEP // n_tiles))
    assert n_seq % seqs == 0
    n_units = hps
    seq3 = lambda a: a.reshape(n_seq, seq_len, width)
    head_spec = pl.BlockSpec((seqs, seq_len, hps * REC_DK), lambda b, h: (b, 0, h))
    tile_spec = pl.BlockSpec((seqs, n_tiles, hps * REC_DK, GLA_TILE), lambda b, h: (b, 0, h, 0))
    state_spec = pl.BlockSpec((seqs, 2, hps, REC_DK, REC_DV), lambda b, h: (b, 0, h, 0, 0))
    in_specs = [tile_spec, head_spec, head_spec, head_spec, tile_spec, tile_spec]
    args = [q, seq3(v), seq3(lf_fw), seq3(lf_bw), k_fw, k_bw]
    if has_state:
        in_specs.append(state_spec)
        args.append(s0)
    for c in consts:
        in_specs.append(pl.BlockSpec(c.shape, lambda b, h, nd=c.ndim: (0,) * nd))
        args.append(c)
    out_shape = [jax.ShapeDtypeStruct((n_seq, seq_len, width), F32)]
    out_specs = [head_spec]
    if want_state:
        out_shape.append(jax.ShapeDtypeStruct((n_seq, 2, N_REC_HEADS, REC_DK, REC_DV), F32))
        out_specs.append(state_spec)
    res = pl.pallas_call(
        functools.partial(_gla_kernel, n_tiles=n_tiles, has_state=has_state, want_state=want_state),
        grid=(n_seq // seqs, N_REC_HEADS // hps),
        in_specs=in_specs,
        out_specs=tuple(out_specs),
        out_shape=tuple(out_shape),
        scratch_shapes=[
            pltpu.VMEM((seqs, 2, hps, REC_DK, REC_DV), F32),
            pltpu.VMEM((n_units, GLA_TILE, 2 * GLA_TILE), BF16),
            pltpu.VMEM((n_units, 2 * REC_DK, GLA_TILE), F32),
        ],
        compiler_params=pltpu.CompilerParams(vmem_limit_bytes=VMEM_LIMIT),
        name="gla_state_in" if has_state else "gla_state_out",
    )(*args)
    o = res[0].reshape(n_seq * seq_len, width)
    return (o, res[1]) if want_state else (o, None)


def _rec_out_kernel(o_ref, sg_ref, x_ref, mod_ref, ng_ref, w_ref, lng_ref, lnb_ref, out_ref):
    d = D_MODEL
    gate = mod_ref[...][:, 2 * d:]

    def branch_of(rows):
        parts = []
        for hd in range(N_REC_HEADS):
            oh = o_ref[rows, hd * REC_DV:(hd + 1) * REC_DV]
            ms = jnp.mean(oh * oh, axis=-1, keepdims=True)
            parts.append(oh * lax.rsqrt(ms + NORM_EPS) * ng_ref[...])
        o = jnp.concatenate(parts, axis=1)
        gated = (o * sg_ref[rows, :].astype(F32)).astype(BF16)
        return _dot(gated, w_ref[...])

    chunks = [pl.ds(c * OUT_ROW_CHUNK, OUT_ROW_CHUNK) for c in range(o_ref.shape[0] // OUT_ROW_CHUNK)]
    branches = [branch_of(rows) for rows in chunks]
    for rows, branch in zip(chunks, branches):
        y = DEEPNORM_ALPHA * x_ref[rows, :] + gate * branch
        out_ref[rows, :] = _layer_norm(y, lng_ref[...], lnb_ref[...])


def _rec_out(o, sg, x2, mods4, norm_gain, w_out_bf, ln_g, ln_b, *, seq_len, mod_row0, per_seq_mod):
    n_tok = x2.shape[0]
    d = D_MODEL
    assert not per_seq_mod or seq_len % OUT_ROW_TILE == 0
    tiles_per_seq = max(1, seq_len // OUT_ROW_TILE)

    def mod_map(i):
        row = mod_row0 + (i // tiles_per_seq if per_seq_mod else 0)
        return (1, row, 0, 0)

    row_spec = pl.BlockSpec((OUT_ROW_TILE, d), lambda i: (i, 0))
    full = lambda i: (0, 0)
    return pl.pallas_call(
        _rec_out_kernel,
        grid=(n_tok // OUT_ROW_TILE,),
        in_specs=[
            row_spec, row_spec, row_spec,
            pl.BlockSpec((None, None, 1, 3 * d), mod_map),
            pl.BlockSpec(norm_gain.shape, full),
            pl.BlockSpec(w_out_bf.shape, full),
            pl.BlockSpec(ln_g.shape, full),
            pl.BlockSpec(ln_b.shape, full),
        ],
        out_specs=row_spec,
        out_shape=jax.ShapeDtypeStruct((n_tok, d), F32),
        compiler_params=pltpu.CompilerParams(vmem_limit_bytes=VMEM_LIMIT),
        name="rec_out",
    )(o, sg, x2, mods4, norm_gain, w_out_bf, ln_g, ln_b)


def _rope_tables(n_tokens):
    n_rows = n_tokens // GRID_W
    rows = jnp.repeat(jnp.arange(n_rows, dtype=F32), GRID_W)
    cols = jnp.tile(jnp.arange(GRID_W, dtype=F32), n_rows)
    inv_freq = 1.0 / (ROPE_THETA ** (jnp.arange(0, AXIS_DIM, 2, dtype=F32) / AXIS_DIM))
    ang_r = rows[:, None] * inv_freq[None, :]
    ang_c = cols[:, None] * inv_freq[None, :]
    ang = jnp.concatenate([ang_r, ang_r, ang_c, ang_c], axis=-1)
    cos, sin = jnp.cos(ang), jnp.sin(ang)
    first = (jnp.arange(HEAD_DIM) % AXIS_DIM) < AXIS_DIM // 2
    sin_a = jnp.where(first[None, :], -sin, 0.0)
    sin_b = jnp.where(first[None, :], 0.0, sin)
    reps = GROUP_LANES // HEAD_DIM
    return tuple(jnp.tile(t, (1, reps)) for t in (cos, sin_a, sin_b))


def _head_mean_matrix():
    idx = jnp.arange(GROUP_LANES) // HEAD_DIM
    return jnp.where(idx[:, None] == idx[None, :], 1.0 / HEAD_DIM, 0.0).astype(BF16)


def _gla_consts():
    t = GLA_TILE
    r = jnp.arange(t)[:, None]
    c = jnp.arange(t)[None, :]
    fw = [c <= r]
    bw = [c >= r]
    for half in GLA_LEVELS:
        start = 2 * half * (r // (2 * half))
        upper = r - start >= half
        last_low, first_up = start + half - 1, start + half
        fw.append(jnp.where(upper, (c > last_low) & (c <= r), (c > r) & (c <= last_low)))
        bw.append(jnp.where(upper, (c >= first_up) & (c < r), (c >= r) & (c < first_up)))
    twice = lambda a: jnp.concatenate([a, a], axis=-1)

    def summation(weights):
        cols = jnp.concatenate([w.T for w in weights], axis=1).astype(BF16)
        return jnp.concatenate([cols, cols], axis=0)

    sums_fw, sums_bw = summation(fw), summation(bw)
    masks = [(r == c)]
    for half in GLA_LEVELS:
        size = 2 * half
        masks.append((r // size == c // size) & (r % size >= half) & (c % size < half))
    m_fw = jnp.stack(masks).astype(BF16)
    m_bw = jnp.swapaxes(m_fw, 1, 2)
    return sums_fw, sums_bw, twice(m_fw), twice(m_bw)


def kernel(x_prompt, x_sample, cache_k, cache_v, state_rec, c, c_ctx, ada_w, ada_b, attn_w_in,
           attn_q_gain, attn_k_gain, attn_w_out, rec_w_in, rec_lower_bounds, rec_norm_gain,
           rec_w_out, ln_gain, ln_bias):
    d = D_MODEL
    n_p, len_p, _ = x_prompt.shape
    n_s, len_s, _ = x_sample.shape
    past = cache_k.shape[2]
    assert past % SLAB_LANES == 0 and len_p % SLAB_LANES == 0 and len_s % SLAB_LANES == 0

    cond = jnp.zeros((COND_ROWS, d), F32).at[0].set(c_ctx).at[1:1 + n_s].set(c)
    mods4 = _mods(cond, ada_w, ada_b).reshape(DEPTH, COND_ROWS, 1, 3 * d)
    lb_all = _lower_bounds(rec_lower_bounds)

    xp = x_prompt.reshape(n_p * len_p, d)
    xs = x_sample.reshape(n_s * len_s, d)

    w_in = attn_w_in[0].astype(BF16)
    w_out = attn_w_out[0].astype(BF16)
    reps = GROUP_LANES // HEAD_DIM
    qg = (jnp.tile(attn_q_gain[0], reps) * (LOG2_E / math.sqrt(HEAD_DIM))).reshape(1, GROUP_LANES)
    kg = jnp.tile(attn_k_gain[0], reps).reshape(1, GROUP_LANES)
    pn = _head_mean_matrix()
    ln_g = ln_gain[0].reshape(1, d)
    ln_b = ln_bias[0].reshape(1, d)

    q_p, kp_p, vt_p, g_p, k_p, v_p = _attn_proj(xp, mods4, w_in, pn, qg, kg, None, seq_len=len_p,
                                                mod_row0=0, per_seq_mod=False, cache_out=True)
    q_s, kp_s, vt_s, g_s = _attn_proj(xs, mods4, w_in, pn, qg, kg, _rope_tables(len_s), seq_len=len_s,
                                      mod_row0=1, per_seq_mod=True, cache_out=False)
    xp1 = _attn_core(q_p, kp_p, vt_p, None, g_p, xp, mods4, w_out, ln_g, ln_b, n_seq=n_p,
                     seq_len=len_p, q_tile=len_p, mod_row0=0, per_seq_mod=False)
    cache = (cache_k[:, 0].reshape(n_s, past, KV_WIDTH), cache_v[:, 0].reshape(n_s, past, KV_WIDTH))
    xs1 = _attn_core(q_s, kp_s, vt_s, cache, g_s, xs, mods4, w_out, ln_g, ln_b, n_seq=n_s,
                     seq_len=len_s, q_tile=ROW_TILE, mod_row0=1, per_seq_mod=True)
    new_cache_k = k_p.reshape(n_p, 1, len_p, N_KV_HEADS, HEAD_DIM)
    new_cache_v = v_p.reshape(n_p, 1, len_p, N_KV_HEADS, HEAD_DIM)

    rw_in = rec_w_in[0].astype(BF16)
    rw_out = rec_w_out[0].astype(BF16)
    lb = lb_all[1]
    ng = rec_norm_gain[0].reshape(1, REC_DV)
    ln_g = ln_gain[1].reshape(1, d)
    ln_b = ln_bias[1].reshape(1, d)
    consts = _gla_consts()

    outs = []
    states = None
    for x1, n_seq, seq_len, row0, per_seq, s0 in (
            (xp1, n_p, len_p, 0, False, None),
            (xs1, n_s, len_s, 1, True, state_rec[:, 0])):
        q, v, g, lf_fw, lf_bw, k_fw, k_bw = _rec_proj(x1, mods4, rw_in, lb, seq_len=seq_len,
                                                       mod_row0=row0, per_seq_mod=per_seq)
        o, st = _gla(q, v, lf_fw, lf_bw, k_fw, k_bw, s0, consts, n_seq=n_seq, seq_len=seq_len,
                     want_state=s0 is None)
        if st is not None:
            states = st
        outs.append(_rec_out(o, g, x1, mods4, ng, rw_out, ln_g, ln_b, seq_len=seq_len,
                             mod_row0=row0, per_seq_mod=per_seq))

    y_prompt = outs[0].reshape(n_p, len_p, d)
    y_sample = outs[1].reshape(n_s, len_s, d)
    new_state_rec = states.reshape(n_p, 1, 2, N_REC_HEADS, REC_DK, REC_DV)
    return (y_prompt, y_sample, new_cache_k, new_cache_v, new_state_rec)
```

```python
import functools
import math

import jax
import jax.numpy as jnp
from jax import lax
from jax.experimental import pallas as pl
from jax.experimental.pallas import tpu as pltpu

F32 = jnp.float32
BF16 = jnp.bfloat16

D_MODEL = 1024
DEPTH = 2
GRID_W = 64
N_HEADS = 16
N_KV_HEADS = 4
HEAD_DIM = 64
AXIS_DIM = HEAD_DIM // 2
ATTN_WIDTH = N_HEADS * HEAD_DIM
KV_WIDTH = N_KV_HEADS * HEAD_DIM
ROPE_THETA = 10000.0
N_REC_HEADS = 8
REC_DK = 128
REC_DV = 128
REC_WIDTH = N_REC_HEADS * REC_DK
NORM_EPS = 1e-6
LN_EPS = 1e-5
DEEPNORM_ALPHA = (2.0 * DEPTH) ** 0.25

SUBLANES = 8
BF16_ROWS = 16
COND_ROWS = SUBLANES
ROW_TILE = 256
OUT_ROW_TILE = 512
OUT_ROW_CHUNK = 256
GROUP_LANES = 256
SLAB_LANES = 128
KPAD_WIDTH = 2 * N_KV_HEADS * SLAB_LANES
REC_PROJ_SLAB = 256
GLA_TILE = 128
GLA_LEVELS = (1, 2, 4, 8, 16, 32, 64)
GLA_HEADS_PER_STEP = 4
GLA_ITEMS_PER_STEP = 8
LOG2_E = 1.4426950408889634
VMEM_LIMIT = 56 * 1024 * 1024


def _sigmoid(x):
    return 1.0 / (1.0 + jnp.exp(-x))


def _dot(a, b):
    return jnp.dot(a, b, preferred_element_type=F32)


def _dot_nt(a, b):
    return lax.dot_general(a, b, (((1,), (1,)), ((), ())), preferred_element_type=F32)


def _dot_tn(a, b):
    return lax.dot_general(a, b, (((0,), (0,)), ((), ())), preferred_element_type=F32)


def _layer_norm(y, g, b):
    mu = jnp.mean(y, axis=-1, keepdims=True)
    yc = y - mu
    var = jnp.mean(yc * yc, axis=-1, keepdims=True)
    return yc * lax.rsqrt(var + LN_EPS) * g + b


def _mods_kernel(cond_ref, w_ref, b_ref, out_ref):
    c = cond_ref[...]
    s = (c * _sigmoid(c)).astype(BF16)
    out_ref[...] = _dot(s, w_ref[...].astype(BF16)) + b_ref[...]


def _mods(cond, ada_w, ada_b):
    d = D_MODEL
    return pl.pallas_call(
        _mods_kernel,
        grid=(DEPTH, 3),
        in_specs=[
            pl.BlockSpec((COND_ROWS, d), lambda l, j: (0, 0)),
            pl.BlockSpec((None, d, d), lambda l, j: (l, 0, j)),
            pl.BlockSpec((None, 1, d), lambda l, j: (l, 0, j)),
        ],
        out_specs=pl.BlockSpec((None, COND_ROWS, d), lambda l, j: (l, 0, j)),
        out_shape=jax.ShapeDtypeStruct((DEPTH, COND_ROWS, 3 * d), F32),
        compiler_params=pltpu.CompilerParams(vmem_limit_bytes=VMEM_LIMIT),
        name="adaln_mods",
    )(cond, ada_w, ada_b.reshape(DEPTH, 1, 3 * d))


def _lower_bounds_kernel(r_ref, out_ref):
    r = [r_ref[i] for i in range(DEPTH)]
    m = functools.reduce(jnp.maximum, r)
    e = [jnp.exp(x - m) for x in r]
    tot = functools.reduce(lambda a, b: a + b, e)
    soft = [x / tot for x in e]
    acc = soft[0]
    for i in range(DEPTH):
        if i > 0:
            acc = acc + soft[i]
        out_ref[i] = acc - soft[0]


def _lower_bounds(rec_lower_bounds):
    return pl.pallas_call(
        _lower_bounds_kernel,
        out_shape=jax.ShapeDtypeStruct(rec_lower_bounds.shape, F32),
        name="rec_lower_bounds",
    )(rec_lower_bounds)


def _kv_head_slabs(k):
    lane = lax.broadcasted_iota(jnp.int32, (k.shape[0], SLAB_LANES), 1)
    low = lane < HEAD_DIM
    heads_per_slab = SLAB_LANES // HEAD_DIM
    out = []
    for j in range(N_KV_HEADS):
        tile = k[:, (j // heads_per_slab) * SLAB_LANES:(j // heads_per_slab + 1) * SLAB_LANES]
        moved = pltpu.roll(tile, HEAD_DIM, 1)
        at_low, at_high = (tile, moved) if j % heads_per_slab == 0 else (moved, tile)
        out.append(jnp.where(low, at_low, 0.0))
        out.append(jnp.where(low, 0.0, at_high))
    return jnp.concatenate(out, axis=1)


def _attn_proj_kernel(*refs, rope, cache_out):
    refs = list(refs)
    x_ref, mod_ref, w_ref, pn_ref, qg_ref, kg_ref = refs[:6]
    pos = 6
    if rope:
        cos_ref, sa_ref, sb_ref = refs[pos:pos + 3]
        pos += 3
    q_out, kp_out, vt_out, g_out = refs[pos:pos + 4]
    pos += 4
    if cache_out:
        k_out, v_out = refs[pos:pos + 2]
    d = D_MODEL
    mod = mod_ref[...]
    shift, scale = mod[:, :d], mod[:, d:2 * d]
    h = (x_ref[...] * (1.0 + scale) + shift).astype(BF16)
    pn = pn_ref[...]

    def norm_rope(u, gain):
        ms = _dot((u * u).astype(BF16), pn)
        y = u * lax.rsqrt(ms + NORM_EPS) * gain
        if rope:
            y = (y * cos_ref[...]
                 + pltpu.roll(y, GROUP_LANES - AXIS_DIM // 2, 1) * sa_ref[...]
                 + pltpu.roll(y, AXIS_DIM // 2, 1) * sb_ref[...])
        return y

    def finish_q(j, u):
        q_out[:, j * GROUP_LANES:(j + 1) * GROUP_LANES] = norm_rope(u, qg_ref[...]).astype(q_out.dtype)

    def finish_k(_, u):
        k = norm_rope(u, kg_ref[...])
        kp_out[...] = _kv_head_slabs(k).astype(kp_out.dtype)
        if cache_out:
            k_out[...] = k

    def finish_v(_, v):
        vt_out[...] = v.T.astype(vt_out.dtype)
        if cache_out:
            v_out[...] = v

    def finish_g(j, u):
        g_out[:, j * GROUP_LANES:(j + 1) * GROUP_LANES] = (u * _sigmoid(u)).astype(g_out.dtype)

    slabs = ([(finish_q, j) for j in range(N_KV_HEADS)] + [(finish_k, 0), (finish_v, 0)]
             + [(finish_g, j) for j in range(N_KV_HEADS)])
    project = lambda i: _dot(h, w_ref[:, i * GROUP_LANES:(i + 1) * GROUP_LANES])
    v_slab = N_KV_HEADS + 1
    order = [i for i in range(len(slabs)) if i != v_slab] + [v_slab]
    u_next = project(order[0])
    for n, i in enumerate(order):
        u = u_next
        if n + 1 < len(order):
            u_next = project(order[n + 1])
        finish, j = slabs[i]
        finish(j, u)


def _attn_proj(x2, mods4, w_bf, pn, qg, kg, rope_tabs, *, seq_len, mod_row0, per_seq_mod, cache_out):
    n_tok = x2.shape[0]
    d = D_MODEL
    tiles_per_seq = seq_len // ROW_TILE
    rope = rope_tabs is not None

    def mod_map(i):
        row = mod_row0 + (i // tiles_per_seq if per_seq_mod else 0)
        return (0, row, 0, 0)

    full = lambda i: (0, 0)
    in_specs = [
        pl.BlockSpec((ROW_TILE, d), lambda i: (i, 0)),
        pl.BlockSpec((None, None, 1, 3 * d), mod_map),
        pl.BlockSpec(w_bf.shape, full),
        pl.BlockSpec(pn.shape, full),
        pl.BlockSpec(qg.shape, full),
        pl.BlockSpec(kg.shape, full),
    ]
    args = [x2, mods4, w_bf, pn, qg, kg]
    if rope:
        for t in rope_tabs:
            in_specs.append(pl.BlockSpec((ROW_TILE, GROUP_LANES), lambda i: (i % tiles_per_seq, 0)))
            args.append(t)
    rows = lambda width: pl.BlockSpec((ROW_TILE, width), lambda i: (i, 0))
    out_shape = [
        jax.ShapeDtypeStruct((n_tok, ATTN_WIDTH), BF16),
        jax.ShapeDtypeStruct((n_tok, KPAD_WIDTH), BF16),
        jax.ShapeDtypeStruct((n_tok // seq_len, KV_WIDTH, seq_len), BF16),
        jax.ShapeDtypeStruct((n_tok, ATTN_WIDTH), BF16),
    ]
    out_specs = [
        rows(ATTN_WIDTH),
        rows(KPAD_WIDTH),
        pl.BlockSpec((None, KV_WIDTH, ROW_TILE), lambda i: (i // tiles_per_seq, 0, i % tiles_per_seq)),
        rows(ATTN_WIDTH),
    ]
    if cache_out:
        out_shape += [jax.ShapeDtypeStruct((n_tok, KV_WIDTH), F32)] * 2
        out_specs += [rows(KV_WIDTH)] * 2
    return pl.pallas_call(
        functools.partial(_attn_proj_kernel, rope=rope, cache_out=cache_out),
        grid=(n_tok // ROW_TILE,),
        in_specs=in_specs,
        out_specs=tuple(out_specs),
        out_shape=tuple(out_shape),
        compiler_params=pltpu.CompilerParams(vmem_limit_bytes=VMEM_LIMIT),
        name="attn_proj_rope" if rope else "attn_proj",
    )(*args)


def _attention_phases(q_ref, kp_ref, vt_ref, cache, s_scr, p_scr, ot_scr, between=()):
    n_new = kp_ref.shape[0]
    n_old = cache[0].shape[0] if cache is not None else 0
    n_keys = n_old + n_new
    group = N_HEADS // N_KV_HEADS
    heads_per_slab = SLAB_LANES // HEAD_DIM
    pending_work = list(between)

    def fold_rows(a, op):
        return functools.reduce(
            op, [a[r * SUBLANES:(r + 1) * SUBLANES, :] for r in range(a.shape[0] // SUBLANES)])

    def scores_phase(j):
        slot = j % 2
        maxima = []
        for hh in range(group):
            hd = j * group + hh
            slab, where = hd // heads_per_slab, hd % heads_per_slab
            q_slab = q_ref[:, slab * SLAB_LANES:(slab + 1) * SLAB_LANES]
            k_slab = heads_per_slab * j + where
            lanes = slice(k_slab * SLAB_LANES, (k_slab + 1) * SLAB_LANES)
            s = _dot_nt(kp_ref[:, lanes], q_slab)
            s_scr[slot, hh, n_old:, :] = s
            mx = fold_rows(s, jnp.maximum)
            if cache is not None:
                s = _dot_nt(cache[0][:, lanes], q_slab)
                s_scr[slot, hh, :n_old, :] = s
                mx = jnp.maximum(mx, fold_rows(s, jnp.maximum))
            maxima.append(jnp.max(mx, axis=0, keepdims=True))
        return maxima

    def values_phase(j, maxima):
        slot = j % 2
        rows = slice(j * HEAD_DIM, (j + 1) * HEAD_DIM)
        v_t = vt_ref[rows, :]
        if cache is not None:
            v_t = jnp.concatenate([cache[1][rows, :], v_t], axis=1)
        v_ext = jnp.concatenate([v_t, jnp.ones((BF16_ROWS, n_keys), BF16)], axis=0)
        for hh in range(group):
            hd = j * group + hh
            p_scr[hh % 2] = jnp.exp2(s_scr[slot, hh] - maxima[hh]).astype(BF16)
            acc = _dot(v_ext, p_scr[hh % 2])
            l = acc[HEAD_DIM:HEAD_DIM + 1, :]
            ot_scr[hd * HEAD_DIM:(hd + 1) * HEAD_DIM, :] = acc[:HEAD_DIM, :] * (1.0 / l)

    pending = scores_phase(0)
    for j in range(N_KV_HEADS):
        nxt = scores_phase(j + 1) if j + 1 < N_KV_HEADS else None
        if pending_work:
            pending_work.pop(0)()
        values_phase(j, pending)
        pending = nxt
    for thunk in pending_work:
        thunk()


def _branch_epilogue(ot_scr, sg, x_ref, mod_ref, w_ref, lng_ref, lnb_ref, out_ref):
    d = D_MODEL
    o = ot_scr[...].T
    gated = (o * sg.astype(F32)).astype(BF16)
    branch = _dot(gated, w_ref[...])
    gate = mod_ref[...][:, 2 * d:]
    y = DEEPNORM_ALPHA * x_ref[...] + gate * branch
    out_ref[...] = _layer_norm(y, lng_ref[...], lnb_ref[...])


def _attn_core_kernel(q_ref, kp_ref, vt_ref, kc_ref, vc_ref, sg_ref, x_ref, mod_ref, w_ref,
                      lng_ref, lnb_ref, out_ref, s_scr, p_scr, ot_scr):
    cache = (_kv_head_slabs(kc_ref[...]).astype(BF16), vc_ref[...].T.astype(BF16))
    _attention_phases(q_ref, kp_ref, vt_ref, cache, s_scr, p_scr, ot_scr)
    _branch_epilogue(ot_scr, sg_ref[...], x_ref, mod_ref, w_ref, lng_ref, lnb_ref, out_ref)


def _attn_seq_kernel(x_ref, mod_ref, w_in_ref, pn_ref, qg_ref, kg_ref, w_out_ref, lng_ref, lnb_ref,
                     out_ref, k_out, v_out,
                     h_scr, q_scr, kp_scr, vt_scr, sg_scr, s_scr, p_scr, ot_scr):
    d = D_MODEL
    mod = mod_ref[...]
    shift, scale = mod[:, :d], mod[:, d:2 * d]
    h_scr[...] = (x_ref[...] * (1.0 + scale) + shift).astype(BF16)
    pn = pn_ref[...]

    def head_norm(u, gain):
        ms = _dot((u * u).astype(BF16), pn)
        return u * lax.rsqrt(ms + NORM_EPS) * gain

    def finish_q(j, u):
        q_scr[:, j * GROUP_LANES:(j + 1) * GROUP_LANES] = head_norm(u, qg_ref[...]).astype(BF16)

    def finish_k(_, u):
        k = head_norm(u, kg_ref[...])
        kp_scr[...] = _kv_head_slabs(k).astype(BF16)
        k_out[...] = k

    def finish_v(_, v):
        vt_scr[...] = v.T.astype(BF16)
        v_out[...] = v

    def finish_g(j, u):
        sg_scr[:, j * GROUP_LANES:(j + 1) * GROUP_LANES] = (u * _sigmoid(u)).astype(BF16)

    project = lambda i: _dot(h_scr[...], w_in_ref[:, i * GROUP_LANES:(i + 1) * GROUP_LANES])
    slabs = [(finish_q, j) for j in range(N_KV_HEADS)] + [(finish_k, 0), (finish_v, 0)]
    u_next = project(0)
    for i, (finish, j) in enumerate(slabs):
        u = u_next
        if i + 1 < len(slabs):
            u_next = project(i + 1)
        finish(j, u)
    gate_slabs = [functools.partial(lambda j: finish_g(j, project(len(slabs) + j)), j)
                  for j in range(N_KV_HEADS)]
    _attention_phases(q_scr, kp_scr, vt_scr, None, s_scr, p_scr, ot_scr, between=gate_slabs)
    _branch_epilogue(ot_scr, sg_scr[...], x_ref, mod_ref, w_out_ref, lng_ref, lnb_ref, out_ref)


def _attn_seq(x2, mods4, w_in_bf, pn, qg, kg, w_out_bf, ln_g, ln_b, *, n_seq, seq_len, mod_row):
    d = D_MODEL
    n_tok = n_seq * seq_len
    full = lambda b: (0, 0)
    rows = lambda width: pl.BlockSpec((seq_len, width), lambda b: (b, 0))
    consts = [w_in_bf, pn, qg, kg, w_out_bf, ln_g, ln_b]
    group = N_HEADS // N_KV_HEADS
    return pl.pallas_call(
        _attn_seq_kernel,
        grid=(n_seq,),
        in_specs=[rows(d), pl.BlockSpec((None, None, 1, 3 * d), lambda b: (0, mod_row, 0, 0))]
        + [pl.BlockSpec(c.shape, full) for c in consts],
        out_specs=(rows(d), rows(KV_WIDTH), rows(KV_WIDTH)),
        out_shape=(jax.ShapeDtypeStruct((n_tok, d), F32),
                   jax.ShapeDtypeStruct((n_tok, KV_WIDTH), F32),
                   jax.ShapeDtypeStruct((n_tok, KV_WIDTH), F32)),
        scratch_shapes=[
            pltpu.VMEM((seq_len, d), BF16),
            pltpu.VMEM((seq_len, ATTN_WIDTH), BF16),
            pltpu.VMEM((seq_len, KPAD_WIDTH), BF16),
            pltpu.VMEM((KV_WIDTH, seq_len), BF16),
            pltpu.VMEM((seq_len, ATTN_WIDTH), BF16),
            pltpu.VMEM((2, group, seq_len, seq_len), F32),
            pltpu.VMEM((2, seq_len, seq_len), BF16),
            pltpu.VMEM((ATTN_WIDTH, seq_len), F32),
        ],
        compiler_params=pltpu.CompilerParams(vmem_limit_bytes=VMEM_LIMIT),
        name="attn_seq",
    )(x2, mods4, *consts)


def _attn_core(q, kp, vt, cache, sg, x2, mods4, w_out_bf, ln_g, ln_b, *, n_seq, seq_len, q_tile,
               mod_row0):
    d = D_MODEL
    tiles = seq_len // q_tile
    ck, cv = cache
    n_keys = seq_len + ck.shape[1]
    q3 = q.reshape(n_seq, seq_len, ATTN_WIDTH)
    kp3 = kp.reshape(n_seq, seq_len, KPAD_WIDTH)
    sg3 = sg.reshape(n_seq, seq_len, ATTN_WIDTH)
    x3 = x2.reshape(n_seq, seq_len, d)
    tile_spec = lambda width: pl.BlockSpec((None, q_tile, width), lambda b, i: (b, i, 0))
    seq_spec = lambda rows, width: pl.BlockSpec((None, rows, width), lambda b, i: (b, 0, 0))
    full2 = lambda b, i: (0, 0)
    out = pl.pallas_call(
        _attn_core_kernel,
        grid=(n_seq, tiles),
        in_specs=[
            tile_spec(ATTN_WIDTH), seq_spec(seq_len, KPAD_WIDTH), seq_spec(KV_WIDTH, seq_len),
            seq_spec(ck.shape[1], KV_WIDTH), seq_spec(cv.shape[1], KV_WIDTH),
            tile_spec(ATTN_WIDTH), tile_spec(d),
            pl.BlockSpec((None, None, 1, 3 * d), lambda b, i: (0, mod_row0 + b, 0, 0)),
            pl.BlockSpec(w_out_bf.shape, full2),
            pl.BlockSpec(ln_g.shape, full2),
            pl.BlockSpec(ln_b.shape, full2),
        ],
        out_specs=tile_spec(d),
        out_shape=jax.ShapeDtypeStruct((n_seq, seq_len, d), F32),
        scratch_shapes=[
            pltpu.VMEM((2, N_HEADS // N_KV_HEADS, n_keys, q_tile), F32),
            pltpu.VMEM((2, n_keys, q_tile), BF16),
            pltpu.VMEM((ATTN_WIDTH, q_tile), F32),
        ],
        compiler_params=pltpu.CompilerParams(vmem_limit_bytes=VMEM_LIMIT),
        name="attn_core_cache",
    )(q3, kp3, vt, ck, cv, sg3, x3, mods4, w_out_bf, ln_g, ln_b)
    return out.reshape(n_seq * seq_len, d)


def _rec_proj_kernel(x_ref, mod_ref, w_ref, lb_ref, q_out, v_out, g_out, lf_fw, lf_bw, k_fw, k_bw):
    d = D_MODEL
    mod = mod_ref[...]
    shift, scale = mod[:, :d], mod[:, d:2 * d]
    h = (x_ref[...] * (1.0 + scale) + shift).astype(BF16)
    slab = REC_PROJ_SLAB

    def store_channel_major(out_ref, cols, val):
        val_t = val.T.astype(out_ref.dtype)
        for i in range(val.shape[0] // GLA_TILE):
            out_ref[i, cols, :] = val_t[:, i * GLA_TILE:(i + 1) * GLA_TILE]

    def finish_q(cols, u):
        store_channel_major(q_out, cols, u * _sigmoid(u))

    def finish_gate(direction, lf_out, k_out):
        def finish(cols, z):
            lb = lb_ref[direction:direction + 1, cols]
            sig = _sigmoid(z)
            lf_out[:, cols] = jnp.log(lb + (1.0 - lb) * sig)
            store_channel_major(k_out, cols, (1.0 - lb) * (1.0 - sig))
        return finish

    def finish_v(cols, u):
        v_out[:, cols] = u.astype(v_out.dtype)

    def finish_g(cols, u):
        g_out[:, cols] = (u * _sigmoid(u)).astype(g_out.dtype)

    sections = (finish_q, finish_gate(0, lf_fw, k_fw), finish_gate(1, lf_bw, k_bw),
                finish_v, finish_g)
    per_section = REC_WIDTH // slab
    n_slabs = len(sections) * per_section
    project = lambda i: _dot(h, w_ref[:, i * slab:(i + 1) * slab])
    v_section = sections.index(finish_v)
    order = ([i for i in range(n_slabs) if i // per_section != v_section]
             + [i for i in range(n_slabs) if i // per_section == v_section])
    u_next = project(order[0])
    for n, i in enumerate(order):
        u = u_next
        if n + 1 < n_slabs:
            u_next = project(order[n + 1])
        within = i % per_section
        sections[i // per_section](slice(within * slab, (within + 1) * slab), u)


def _rec_proj(x2, mods4, w_bf, lb, *, seq_len, mod_row0, per_seq_mod):
    n_tok = x2.shape[0]
    d = D_MODEL
    tiles_per_seq = seq_len // ROW_TILE

    def mod_map(i):
        row = mod_row0 + (i // tiles_per_seq if per_seq_mod else 0)
        return (1, row, 0, 0)

    row_spec = pl.BlockSpec((ROW_TILE, REC_WIDTH), lambda i: (i, 0))
    tiles = ROW_TILE // GLA_TILE
    tile_spec = pl.BlockSpec((None, tiles, REC_WIDTH, GLA_TILE),
                             lambda i: (i // tiles_per_seq, i % tiles_per_seq, 0, 0))
    full = lambda i: (0, 0)
    row_out = lambda t: (row_spec, jax.ShapeDtypeStruct((n_tok, REC_WIDTH), t))
    tile_out = (tile_spec, jax.ShapeDtypeStruct(
        (n_tok // seq_len, seq_len // GLA_TILE, REC_WIDTH, GLA_TILE), BF16))
    outs = (tile_out, row_out(BF16), row_out(BF16), row_out(F32), row_out(F32), tile_out, tile_out)
    return pl.pallas_call(
        _rec_proj_kernel,
        grid=(n_tok // ROW_TILE,),
        in_specs=[
            pl.BlockSpec((ROW_TILE, d), lambda i: (i, 0)),
            pl.BlockSpec((None, None, 1, 3 * d), mod_map),
            pl.BlockSpec(w_bf.shape, full),
            pl.BlockSpec(lb.shape, full),
        ],
        out_specs=tuple(spec for spec, _ in outs),
        out_shape=tuple(shape for _, shape in outs),
        compiler_params=pltpu.CompilerParams(vmem_limit_bytes=VMEM_LIMIT),
        name="rec_proj",
    )(x2, mods4, w_bf, lb)


def _block_diag(a, b):
    za = jnp.zeros(a.shape, a.dtype)
    return jnp.concatenate(
        [jnp.concatenate([a, za], axis=1), jnp.concatenate([za, b], axis=1)], axis=0)


def _pair_scores(a_t, c_t):
    return _dot_tn(a_t, _block_diag(c_t[:REC_DK, :], c_t[REC_DK:, :]))


def _gla_scores_stage(units):
    t = GLA_TILE
    for u in units:
        lf2 = u["lf"]() * LOG2_E
        hi = lf2.astype(BF16)
        lo = (lf2 - hi.astype(F32)).astype(BF16)
        u["sums"] = _dot_tn(jnp.concatenate([hi, lo], axis=0), u["sums_ref"][...])
        u["scores"] = u["masks_ref"][0] * _pair_scores(u["q_t"](), u["k_t"]()).astype(BF16)
    for li in range(len(GLA_LEVELS)):
        for u in units:
            x = jnp.exp2(u["sums"][:, (1 + li) * t:(2 + li) * t]).astype(BF16)
            z = _pair_scores(u["q_t"]() * x, u["k_t"]() * x).astype(BF16)
            u["scores"] = u["scores"] + u["masks_ref"][1 + li] * z
    for u in units:
        u["save"](u["scores"], u["sums"][:, :t])


def _gla_state_stage(u):
    t = GLA_TILE
    dk = REC_DK
    b_t, st_ref = u["load_b"](), u["st_ref"]
    q_t, k_t, v = u["q_t"](), u["k_t"](), u["v"]()
    edge = b_t[:, 0:1] if u["backward"] else b_t[:, t - 1:t]
    o = _dot(u["load_scores"](), _block_diag(v[:, :dk], v[:, dk:]))
    st_a, st_b = st_ref[0], st_ref[1]
    q_in = q_t * jnp.exp2(b_t).astype(BF16)
    o = o + _dot_tn(q_in, _block_diag(st_a.astype(BF16), st_b.astype(BF16)))
    k_edge = k_t * jnp.exp2(edge - b_t).astype(BF16)
    carry = jnp.exp2(edge)
    st_ref[0] = st_a * carry[:dk, :] + _dot(k_edge[:dk, :], v[:, :dk])
    st_ref[1] = st_b * carry[dk:, :] + _dot(k_edge[dk:, :], v[:, dk:])
    u["store"](o)


def _gla_kernel(*refs, n_tiles, has_state, want_state):
    refs = list(refs)
    q_ref, v_ref, lff_ref, lfb_ref, kf_ref, kb_ref = refs[:6]
    pos = 6
    if has_state:
        s0_ref = refs[pos]
        pos += 1
    sums_f_ref, sums_b_ref, mf_ref, mb_ref = refs[pos:pos + 4]
    pos += 4
    o_ref = refs[pos]
    pos += 1
    if want_state:
        s_out_ref = refs[pos]
        pos += 1
    st_ref, pipe_s, pipe_b = refs[pos:pos + 3]

    t = GLA_TILE
    n_seqs = v_ref.shape[0]
    n_items = n_seqs * n_tiles
    pair_lanes = 2 * REC_DK
    o_ref[...] = jnp.zeros(o_ref.shape, o_ref.dtype)
    if has_state:
        st_ref[...] = s0_ref[...]
    else:
        st_ref[...] = jnp.zeros(st_ref.shape, st_ref.dtype)

    def units_of(item):
        seq, step = item // n_tiles, item % n_tiles
        units = []
        for direction, (lf_ref, k_ref, sums_ref, m_ref) in enumerate(
                ((lff_ref, kf_ref, sums_f_ref, mf_ref), (lfb_ref, kb_ref, sums_b_ref, mb_ref))):
            tile = step if direction == 0 else n_tiles - 1 - step
            rows = pl.ds(pl.multiple_of(tile * t, t), t)
            for pair in range(GLA_HEADS_PER_STEP // 2):
                lanes = slice(pair * pair_lanes, (pair + 1) * pair_lanes)
                ui = len(units)

                def store(o, rows=rows, lanes=lanes):
                    o_ref[seq, rows, lanes] += o

                def save(scores, b_t, ui=ui):
                    pipe_s[ui] = scores
                    pipe_b[ui] = b_t

                rows_of = lambda ref, rows=rows, lanes=lanes: (lambda: ref[seq, rows, lanes])
                channels_of = lambda ref, tile=tile, lanes=lanes: (lambda: ref[seq, tile, lanes, :])
                units.append(dict(
                    q_t=channels_of(q_ref), k_t=channels_of(k_ref), v=rows_of(v_ref),
                    lf=rows_of(lf_ref), st_ref=st_ref.at[seq, direction, pl.ds(2 * pair, 2)],
                    sums_ref=sums_ref, masks_ref=m_ref, backward=direction == 1, store=store,
                    save=save, load_scores=lambda ui=ui: pipe_s[ui], load_b=lambda ui=ui: pipe_b[ui]))
        return units

    def body(item, carry):
        units = units_of(item)
        _gla_scores_stage(units)
        for u in units:
            _gla_state_stage(u)
        return carry

    lax.fori_loop(0, n_items, body, 0)
    if want_state:
        s_out_ref[...] = st_ref[...]


def _gla(q, v, lf_fw, lf_bw, k_fw, k_bw, s0, consts, *, n_seq, seq_len, want_state):
    has_state = s0 is not None
    width = REC_WIDTH
    hps = GLA_HEADS_PER_STEP
    n_tiles = seq_len // GLA_TILE
    seqs = min(n_seq, max(1, GLA_ITEMS_PER_STEP // n_tiles))
    assert n_seq % seqs == 0
    n_units = hps
    seq3 = lambda a: a.reshape(n_seq, seq_len, width)
    head_spec = pl.BlockSpec((seqs, seq_len, hps * REC_DK), lambda b, h: (b, 0, h))
    tile_spec = pl.BlockSpec((seqs, n_tiles, hps * REC_DK, GLA_TILE), lambda b, h: (b, 0, h, 0))
    state_spec = pl.BlockSpec((seqs, 2, hps, REC_DK, REC_DV), lambda b, h: (b, 0, h, 0, 0))
    in_specs = [tile_spec, head_spec, head_spec, head_spec, tile_spec, tile_spec]
    args = [q, seq3(v), seq3(lf_fw), seq3(lf_bw), k_fw, k_bw]
    if has_state:
        in_specs.append(state_spec)
        args.append(s0)
    for c in consts:
        in_specs.append(pl.BlockSpec(c.shape, lambda b, h, nd=c.ndim: (0,) * nd))
        args.append(c)
    out_shape = [jax.ShapeDtypeStruct((n_seq, seq_len, width), F32)]
    out_specs = [head_spec]
    if want_state:
        out_shape.append(jax.ShapeDtypeStruct((n_seq, 2, N_REC_HEADS, REC_DK, REC_DV), F32))
        out_specs.append(state_spec)
    res = pl.pallas_call(
        functools.partial(_gla_kernel, n_tiles=n_tiles, has_state=has_state, want_state=want_state),
        grid=(n_seq // seqs, N_REC_HEADS // hps),
        in_specs=in_specs,
        out_specs=tuple(out_specs),
        out_shape=tuple(out_shape),
        scratch_shapes=[
            pltpu.VMEM((seqs, 2, hps, REC_DK, REC_DV), F32),
            pltpu.VMEM((n_units, GLA_TILE, 2 * GLA_TILE), BF16),
            pltpu.VMEM((n_units, 2 * REC_DK, GLA_TILE), F32),
        ],
        compiler_params=pltpu.CompilerParams(vmem_limit_bytes=VMEM_LIMIT),
        name="gla_state_in" if has_state else "gla_state_out",
    )(*args)
    o = res[0].reshape(n_seq * seq_len, width)
    return (o, res[1]) if want_state else (o, None)


def _rec_out_kernel(o_ref, sg_ref, x_ref, mod_ref, ng_ref, w_ref, lng_ref, lnb_ref, out_ref):
    d = D_MODEL
    gate = mod_ref[...][:, 2 * d:]

    def branch_of(rows):
        parts = []
        for hd in range(N_REC_HEADS):
            oh = o_ref[rows, hd * REC_DV:(hd + 1) * REC_DV]
            ms = jnp.mean(oh * oh, axis=-1, keepdims=True)
            parts.append(oh * lax.rsqrt(ms + NORM_EPS) * ng_ref[...])
        o = jnp.concatenate(parts, axis=1)
        gated = (o * sg_ref[rows, :].astype(F32)).astype(BF16)
        return _dot(gated, w_ref[...])

    chunks = [pl.ds(c * OUT_ROW_CHUNK, OUT_ROW_CHUNK) for c in range(o_ref.shape[0] // OUT_ROW_CHUNK)]
    branches = [branch_of(rows) for rows in chunks]
    for rows, branch in zip(chunks, branches):
        y = DEEPNORM_ALPHA * x_ref[rows, :] + gate * branch
        out_ref[rows, :] = _layer_norm(y, lng_ref[...], lnb_ref[...])


def _rec_out(o, sg, x2, mods4, norm_gain, w_out_bf, ln_g, ln_b, *, seq_len, mod_row0, per_seq_mod):
    n_tok = x2.shape[0]
    d = D_MODEL
    assert not per_seq_mod or seq_len % OUT_ROW_TILE == 0
    tiles_per_seq = max(1, seq_len // OUT_ROW_TILE)

    def mod_map(i):
        row = mod_row0 + (i // tiles_per_seq if per_seq_mod else 0)
        return (1, row, 0, 0)

    row_spec = pl.BlockSpec((OUT_ROW_TILE, d), lambda i: (i, 0))
    full = lambda i: (0, 0)
    return pl.pallas_call(
        _rec_out_kernel,
        grid=(n_tok // OUT_ROW_TILE,),
        in_specs=[
            row_spec, row_spec, row_spec,
            pl.BlockSpec((None, None, 1, 3 * d), mod_map),
            pl.BlockSpec(norm_gain.shape, full),
            pl.BlockSpec(w_out_bf.shape, full),
            pl.BlockSpec(ln_g.shape, full),
            pl.BlockSpec(ln_b.shape, full),
        ],
        out_specs=row_spec,
        out_shape=jax.ShapeDtypeStruct((n_tok, d), F32),
        compiler_params=pltpu.CompilerParams(vmem_limit_bytes=VMEM_LIMIT),
        name="rec_out",
    )(o, sg, x2, mods4, norm_gain, w_out_bf, ln_g, ln_b)


def _rope_tables(n_tokens):
    n_rows = n_tokens // GRID_W
    rows = jnp.repeat(jnp.arange(n_rows, dtype=F32), GRID_W)
    cols = jnp.tile(jnp.arange(GRID_W, dtype=F32), n_rows)
    inv_freq = 1.0 / (ROPE_THETA ** (jnp.arange(0, AXIS_DIM, 2, dtype=F32) / AXIS_DIM))
    ang_r = rows[:, None] * inv_freq[None, :]
    ang_c = cols[:, None] * inv_freq[None, :]
    ang = jnp.concatenate([ang_r, ang_r, ang_c, ang_c], axis=-1)
    cos, sin = jnp.cos(ang), jnp.sin(ang)
    first = (jnp.arange(HEAD_DIM) % AXIS_DIM) < AXIS_DIM // 2
    sin_a = jnp.where(first[None, :], -sin, 0.0)
    sin_b = jnp.where(first[None, :], 0.0, sin)
    reps = GROUP_LANES // HEAD_DIM
    return tuple(jnp.tile(t, (1, reps)) for t in (cos, sin_a, sin_b))


def _head_mean_matrix():
    idx = jnp.arange(GROUP_LANES) // HEAD_DIM
    return jnp.where(idx[:, None] == idx[None, :], 1.0 / HEAD_DIM, 0.0).astype(BF16)


def _gla_consts():
    t = GLA_TILE
    r = jnp.arange(t)[:, None]
    c = jnp.arange(t)[None, :]
    fw = [c <= r]
    bw = [c >= r]
    for half in GLA_LEVELS:
        start = 2 * half * (r // (2 * half))
        upper = r - start >= half
        last_low, first_up = start + half - 1, start + half
        fw.append(jnp.where(upper, (c > last_low) & (c <= r), (c > r) & (c <= last_low)))
        bw.append(jnp.where(upper, (c >= first_up) & (c < r), (c >= r) & (c < first_up)))
    twice = lambda a: jnp.concatenate([a, a], axis=-1)

    def summation(weights):
        cols = jnp.concatenate([w.T for w in weights], axis=1).astype(BF16)
        return jnp.concatenate([cols, cols], axis=0)

    sums_fw, sums_bw = summation(fw), summation(bw)
    masks = [(r == c)]
    for half in GLA_LEVELS:
        size = 2 * half
        masks.append((r // size == c // size) & (r % size >= half) & (c % size < half))
    m_fw = jnp.stack(masks).astype(BF16)
    m_bw = jnp.swapaxes(m_fw, 1, 2)
    return sums_fw, sums_bw, twice(m_fw), twice(m_bw)


def kernel(x_prompt, x_sample, cache_k, cache_v, state_rec, c, c_ctx, ada_w, ada_b, attn_w_in,
           attn_q_gain, attn_k_gain, attn_w_out, rec_w_in, rec_lower_bounds, rec_norm_gain,
           rec_w_out, ln_gain, ln_bias):
    d = D_MODEL
    n_p, len_p, _ = x_prompt.shape
    n_s, len_s, _ = x_sample.shape
    past = cache_k.shape[2]
    assert past % SLAB_LANES == 0 and len_p % SLAB_LANES == 0 and len_s % SLAB_LANES == 0

    cond = jnp.zeros((COND_ROWS, d), F32).at[0].set(c_ctx).at[1:1 + n_s].set(c)
    mods4 = _mods(cond, ada_w, ada_b).reshape(DEPTH, COND_ROWS, 1, 3 * d)
    lb_all = _lower_bounds(rec_lower_bounds)

    xp = x_prompt.reshape(n_p * len_p, d)
    xs = x_sample.reshape(n_s * len_s, d)

    w_in = attn_w_in[0].astype(BF16)
    w_out = attn_w_out[0].astype(BF16)
    reps = GROUP_LANES // HEAD_DIM
    qg = (jnp.tile(attn_q_gain[0], reps) * (LOG2_E / math.sqrt(HEAD_DIM))).reshape(1, GROUP_LANES)
    kg = jnp.tile(attn_k_gain[0], reps).reshape(1, GROUP_LANES)
    pn = _head_mean_matrix()
    ln_g = ln_gain[0].reshape(1, d)
    ln_b = ln_bias[0].reshape(1, d)

    xp1, k_p, v_p = _attn_seq(xp, mods4, w_in, pn, qg, kg, w_out, ln_g, ln_b, n_seq=n_p,
                              seq_len=len_p, mod_row=0)
    q_s, kp_s, vt_s, g_s = _attn_proj(xs, mods4, w_in, pn, qg, kg, _rope_tables(len_s), seq_len=len_s,
                                      mod_row0=1, per_seq_mod=True, cache_out=False)
    cache = (cache_k[:, 0].reshape(n_s, past, KV_WIDTH), cache_v[:, 0].reshape(n_s, past, KV_WIDTH))
    xs1 = _attn_core(q_s, kp_s, vt_s, cache, g_s, xs, mods4, w_out, ln_g, ln_b, n_seq=n_s,
                     seq_len=len_s, q_tile=ROW_TILE, mod_row0=1)
    new_cache_k = k_p.reshape(n_p, 1, len_p, N_KV_HEADS, HEAD_DIM)
    new_cache_v = v_p.reshape(n_p, 1, len_p, N_KV_HEADS, HEAD_DIM)

    rw_in = rec_w_in[0].astype(BF16)
    rw_out = rec_w_out[0].astype(BF16)
    lb = lb_all[1]
    ng = rec_norm_gain[0].reshape(1, REC_DV)
    ln_g = ln_gain[1].reshape(1, d)
    ln_b = ln_bias[1].reshape(1, d)
    consts = _gla_consts()

    outs = []
    states = None
    for x1, n_seq, seq_len, row0, per_seq, s0 in (
            (xp1, n_p, len_p, 0, False, None),
            (xs1, n_s, len_s, 1, True, state_rec[:, 0])):
        q, v, g, lf_fw, lf_bw, k_fw, k_bw = _rec_proj(x1, mods4, rw_in, lb, seq_len=seq_len,
                                                       mod_row0=row0, per_seq_mod=per_seq)
        o, st = _gla(q, v, lf_fw, lf_bw, k_fw, k_bw, s0, consts, n_seq=n_seq, seq_len=seq_len,
                     want_state=s0 is None)
        if st is not None:
            states = st
        outs.append(_rec_out(o, g, x1, mods4, ng, rw_out, ln_g, ln_b, seq_len=seq_len,
                             mod_row0=row0, per_seq_mod=per_seq))

    y_prompt = outs[0].reshape(n_p, len_p, d)
    y_sample = outs[1].reshape(n_s, len_s, d)
    new_state_rec = states.reshape(n_p, 1, 2, N_REC_HEADS, REC_DK, REC_DV)
    return (y_prompt, y_sample, new_cache_k, new_cache_v, new_state_rec)
```

```python
import functools
import math

import jax
import jax.numpy as jnp
from jax import lax
from jax.experimental import pallas as pl
from jax.experimental.pallas import tpu as pltpu

F32 = jnp.float32
BF16 = jnp.bfloat16

D_MODEL = 1024
DEPTH = 2
GRID_W = 64
N_HEADS = 16
N_KV_HEADS = 4
HEAD_DIM = 64
AXIS_DIM = HEAD_DIM // 2
ATTN_WIDTH = N_HEADS * HEAD_DIM
KV_WIDTH = N_KV_HEADS * HEAD_DIM
ROPE_THETA = 10000.0
N_REC_HEADS = 8
REC_DK = 128
REC_DV = 128
REC_WIDTH = N_REC_HEADS * REC_DK
NORM_EPS = 1e-6
LN_EPS = 1e-5
DEEPNORM_ALPHA = (2.0 * DEPTH) ** 0.25

SUBLANES = 8
BF16_ROWS = 16
COND_ROWS = SUBLANES
ROW_TILE = 256
OUT_ROW_TILE = 512
OUT_ROW_CHUNK = 256
GROUP_LANES = 256
SLAB_LANES = 128
KPAD_WIDTH = 2 * N_KV_HEADS * SLAB_LANES
REC_PROJ_SLAB = 256
GLA_TILE = 128
GLA_LEVELS = (1, 2, 4, 8, 16, 32, 64)
GLA_HEADS_PER_STEP = 4
GLA_ITEMS_PER_STEP = 8
LOG2_E = 1.4426950408889634
VMEM_LIMIT = 56 * 1024 * 1024


def _sigmoid(x):
    return 1.0 / (1.0 + jnp.exp(-x))


def _dot(a, b):
    return jnp.dot(a, b, preferred_element_type=F32)


def _dot_nt(a, b):
    return lax.dot_general(a, b, (((1,), (1,)), ((), ())), preferred_element_type=F32)


def _dot_tn(a, b):
    return lax.dot_general(a, b, (((0,), (0,)), ((), ())), preferred_element_type=F32)


def _layer_norm(y, g, b):
    mu = jnp.mean(y, axis=-1, keepdims=True)
    yc = y - mu
    var = jnp.mean(yc * yc, axis=-1, keepdims=True)
    return yc * lax.rsqrt(var + LN_EPS) * g + b


def _mods_kernel(cond_ref, w_ref, b_ref, out_ref):
    c = cond_ref[...]
    s = (c * _sigmoid(c)).astype(BF16)
    out_ref[...] = _dot(s, w_ref[...].astype(BF16)) + b_ref[...]


def _mods(cond, ada_w, ada_b):
    d = D_MODEL
    return pl.pallas_call(
        _mods_kernel,
        grid=(DEPTH, 3),
        in_specs=[
            pl.BlockSpec((COND_ROWS, d), lambda l, j: (0, 0)),
            pl.BlockSpec((None, d, d), lambda l, j: (l, 0, j)),
            pl.BlockSpec((None, 1, d), lambda l, j: (l, 0, j)),
        ],
        out_specs=pl.BlockSpec((None, COND_ROWS, d), lambda l, j: (l, 0, j)),
        out_shape=jax.ShapeDtypeStruct((DEPTH, COND_ROWS, 3 * d), F32),
        compiler_params=pltpu.CompilerParams(vmem_limit_bytes=VMEM_LIMIT),
        name="adaln_mods",
    )(cond, ada_w, ada_b.reshape(DEPTH, 1, 3 * d))


def _lower_bounds_kernel(r_ref, out_ref):
    r = [r_ref[i] for i in range(DEPTH)]
    m = functools.reduce(jnp.maximum, r)
    e = [jnp.exp(x - m) for x in r]
    tot = functools.reduce(lambda a, b: a + b, e)
    soft = [x / tot for x in e]
    acc = soft[0]
    for i in range(DEPTH):
        if i > 0:
            acc = acc + soft[i]
        out_ref[i] = acc - soft[0]


def _lower_bounds(rec_lower_bounds):
    return pl.pallas_call(
        _lower_bounds_kernel,
        out_shape=jax.ShapeDtypeStruct(rec_lower_bounds.shape, F32),
        name="rec_lower_bounds",
    )(rec_lower_bounds)


def _kv_head_slabs(k):
    lane = lax.broadcasted_iota(jnp.int32, (k.shape[0], SLAB_LANES), 1)
    low = lane < HEAD_DIM
    heads_per_slab = SLAB_LANES // HEAD_DIM
    out = []
    for j in range(N_KV_HEADS):
        tile = k[:, (j // heads_per_slab) * SLAB_LANES:(j // heads_per_slab + 1) * SLAB_LANES]
        moved = pltpu.roll(tile, HEAD_DIM, 1)
        at_low, at_high = (tile, moved) if j % heads_per_slab == 0 else (moved, tile)
        out.append(jnp.where(low, at_low, 0.0))
        out.append(jnp.where(low, 0.0, at_high))
    return jnp.concatenate(out, axis=1)


def _attn_proj_kernel(*refs, rope, cache_out):
    refs = list(refs)
    x_ref, mod_ref, w_ref, pn_ref, qg_ref, kg_ref = refs[:6]
    pos = 6
    if rope:
        cos_ref, sa_ref, sb_ref = refs[pos:pos + 3]
        pos += 3
    q_out, kp_out, vt_out, g_out = refs[pos:pos + 4]
    pos += 4
    if cache_out:
        k_out, v_out = refs[pos:pos + 2]
    d = D_MODEL
    mod = mod_ref[...]
    shift, scale = mod[:, :d], mod[:, d:2 * d]
    h = (x_ref[...] * (1.0 + scale) + shift).astype(BF16)
    pn = pn_ref[...]

    def norm_rope(u, gain):
        ms = _dot((u * u).astype(BF16), pn)
        y = u * lax.rsqrt(ms + NORM_EPS) * gain
        if rope:
            y = (y * cos_ref[...]
                 + pltpu.roll(y, GROUP_LANES - AXIS_DIM // 2, 1) * sa_ref[...]
                 + pltpu.roll(y, AXIS_DIM // 2, 1) * sb_ref[...])
        return y

    def finish_q(j, u):
        q_out[:, j * GROUP_LANES:(j + 1) * GROUP_LANES] = norm_rope(u, qg_ref[...]).astype(q_out.dtype)

    def finish_k(_, u):
        k = norm_rope(u, kg_ref[...])
        kp_out[...] = _kv_head_slabs(k).astype(kp_out.dtype)
        if cache_out:
            k_out[...] = k

    def finish_v(_, v):
        vt_out[...] = v.T.astype(vt_out.dtype)
        if cache_out:
            v_out[...] = v

    def finish_g(j, u):
        g_out[:, j * GROUP_LANES:(j + 1) * GROUP_LANES] = (u * _sigmoid(u)).astype(g_out.dtype)

    slabs = ([(finish_q, j) for j in range(N_KV_HEADS)] + [(finish_k, 0), (finish_v, 0)]
             + [(finish_g, j) for j in range(N_KV_HEADS)])
    project = lambda i: _dot(h, w_ref[:, i * GROUP_LANES:(i + 1) * GROUP_LANES])
    v_slab = N_KV_HEADS + 1
    order = [i for i in range(len(slabs)) if i != v_slab] + [v_slab]
    u_next = project(order[0])
    for n, i in enumerate(order):
        u = u_next
        if n + 1 < len(order):
            u_next = project(order[n + 1])
        finish, j = slabs[i]
        finish(j, u)


def _attn_proj(x2, mods4, w_bf, pn, qg, kg, rope_tabs, *, seq_len, mod_row0, per_seq_mod, cache_out):
    n_tok = x2.shape[0]
    d = D_MODEL
    tiles_per_seq = seq_len // ROW_TILE
    rope = rope_tabs is not None

    def mod_map(i):
        row = mod_row0 + (i // tiles_per_seq if per_seq_mod else 0)
        return (0, row, 0, 0)

    full = lambda i: (0, 0)
    in_specs = [
        pl.BlockSpec((ROW_TILE, d), lambda i: (i, 0)),
        pl.BlockSpec((None, None, 1, 3 * d), mod_map),
        pl.BlockSpec(w_bf.shape, full),
        pl.BlockSpec(pn.shape, full),
        pl.BlockSpec(qg.shape, full),
        pl.BlockSpec(kg.shape, full),
    ]
    args = [x2, mods4, w_bf, pn, qg, kg]
    if rope:
        for t in rope_tabs:
            in_specs.append(pl.BlockSpec((ROW_TILE, GROUP_LANES), lambda i: (i % tiles_per_seq, 0)))
            args.append(t)
    rows = lambda width: pl.BlockSpec((ROW_TILE, width), lambda i: (i, 0))
    out_shape = [
        jax.ShapeDtypeStruct((n_tok, ATTN_WIDTH), BF16),
        jax.ShapeDtypeStruct((n_tok, KPAD_WIDTH), BF16),
        jax.ShapeDtypeStruct((n_tok // seq_len, KV_WIDTH, seq_len), BF16),
        jax.ShapeDtypeStruct((n_tok, ATTN_WIDTH), BF16),
    ]
    out_specs = [
        rows(ATTN_WIDTH),
        rows(KPAD_WIDTH),
        pl.BlockSpec((None, KV_WIDTH, ROW_TILE), lambda i: (i // tiles_per_seq, 0, i % tiles_per_seq)),
        rows(ATTN_WIDTH),
    ]
    if cache_out:
        out_shape += [jax.ShapeDtypeStruct((n_tok, KV_WIDTH), F32)] * 2
        out_specs += [rows(KV_WIDTH)] * 2
    return pl.pallas_call(
        functools.partial(_attn_proj_kernel, rope=rope, cache_out=cache_out),
        grid=(n_tok // ROW_TILE,),
        in_specs=in_specs,
        out_specs=tuple(out_specs),
        out_shape=tuple(out_shape),
        compiler_params=pltpu.CompilerParams(vmem_limit_bytes=VMEM_LIMIT),
        name="attn_proj_rope" if rope else "attn_proj",
    )(*args)


def _attention_phases(q_ref, kp_ref, vt_ref, cache, s_scr, p_scr, ot_scr, between=()):
    n_new = kp_ref.shape[0]
    n_old = cache[0].shape[0] if cache is not None else 0
    n_keys = n_old + n_new
    group = N_HEADS // N_KV_HEADS
    heads_per_slab = SLAB_LANES // HEAD_DIM
    pending_work = list(between)

    def fold_rows(a, op):
        return functools.reduce(
            op, [a[r * SUBLANES:(r + 1) * SUBLANES, :] for r in range(a.shape[0] // SUBLANES)])

    def scores_phase(j):
        slot = j % 2
        maxima = []
        for hh in range(group):
            hd = j * group + hh
            slab, where = hd // heads_per_slab, hd % heads_per_slab
            q_slab = q_ref[:, slab * SLAB_LANES:(slab + 1) * SLAB_LANES]
            k_slab = heads_per_slab * j + where
            lanes = slice(k_slab * SLAB_LANES, (k_slab + 1) * SLAB_LANES)
            s = _dot_nt(kp_ref[:, lanes], q_slab)
            s_scr[slot, hh, n_old:, :] = s
            mx = fold_rows(s, jnp.maximum)
            if cache is not None:
                s = _dot_nt(cache[0][:, lanes], q_slab)
                s_scr[slot, hh, :n_old, :] = s
                mx = jnp.maximum(mx, fold_rows(s, jnp.maximum))
            maxima.append(jnp.max(mx, axis=0, keepdims=True))
        return maxima

    def values_phase(j, maxima):
        slot = j % 2
        rows = slice(j * HEAD_DIM, (j + 1) * HEAD_DIM)
        v_t = vt_ref[rows, :]
        if cache is not None:
            v_t = jnp.concatenate([cache[1][rows, :], v_t], axis=1)
        v_ext = jnp.concatenate([v_t, jnp.ones((BF16_ROWS, n_keys), BF16)], axis=0)
        for hh in range(group):
            hd = j * group + hh
            p_scr[hh % 2] = jnp.exp2(s_scr[slot, hh] - maxima[hh]).astype(BF16)
            acc = _dot(v_ext, p_scr[hh % 2])
            l = acc[HEAD_DIM:HEAD_DIM + 1, :]
            ot_scr[hd * HEAD_DIM:(hd + 1) * HEAD_DIM, :] = acc[:HEAD_DIM, :] * (1.0 / l)

    pending = scores_phase(0)
    for j in range(N_KV_HEADS):
        nxt = scores_phase(j + 1) if j + 1 < N_KV_HEADS else None
        if pending_work:
            pending_work.pop(0)()
        values_phase(j, pending)
        pending = nxt
    for thunk in pending_work:
        thunk()


def _branch_epilogue(ot_scr, sg, x_ref, mod_ref, w_ref, lng_ref, lnb_ref, out_ref):
    d = D_MODEL
    o = ot_scr[...].T
    gated = (o * sg.astype(F32)).astype(BF16)
    branch = _dot(gated, w_ref[...])
    gate = mod_ref[...][:, 2 * d:]
    y = DEEPNORM_ALPHA * x_ref[...] + gate * branch
    out_ref[...] = _layer_norm(y, lng_ref[...], lnb_ref[...])


def _attn_core_kernel(q_ref, kp_ref, vt_ref, kc_ref, vc_ref, sg_ref, x_ref, mod_ref, w_ref,
                      lng_ref, lnb_ref, out_ref, s_scr, p_scr, ot_scr):
    cache = (_kv_head_slabs(kc_ref[...]).astype(BF16), vc_ref[...].T.astype(BF16))
    _attention_phases(q_ref, kp_ref, vt_ref, cache, s_scr, p_scr, ot_scr)
    _branch_epilogue(ot_scr, sg_ref[...], x_ref, mod_ref, w_ref, lng_ref, lnb_ref, out_ref)


def _attn_seq_kernel(x_ref, mod_ref, w_in_ref, pn_ref, qg_ref, kg_ref, w_out_ref, lng_ref, lnb_ref,
                     out_ref, k_out, v_out,
                     h_scr, q_scr, kp_scr, vt_scr, sg_scr, s_scr, p_scr, ot_scr):
    d = D_MODEL
    mod = mod_ref[...]
    shift, scale = mod[:, :d], mod[:, d:2 * d]
    h_scr[...] = (x_ref[...] * (1.0 + scale) + shift).astype(BF16)
    pn = pn_ref[...]

    def head_norm(u, gain):
        ms = _dot((u * u).astype(BF16), pn)
        return u * lax.rsqrt(ms + NORM_EPS) * gain

    def finish_q(j, u):
        q_scr[:, j * GROUP_LANES:(j + 1) * GROUP_LANES] = head_norm(u, qg_ref[...]).astype(BF16)

    def finish_k(_, u):
        k = head_norm(u, kg_ref[...])
        kp_scr[...] = _kv_head_slabs(k).astype(BF16)
        k_out[...] = k.reshape(k_out.shape)

    def finish_v(_, v):
        vt_scr[...] = v.T.astype(BF16)
        v_out[...] = v.reshape(v_out.shape)

    def finish_g(j, u):
        sg_scr[:, j * GROUP_LANES:(j + 1) * GROUP_LANES] = (u * _sigmoid(u)).astype(BF16)

    project = lambda i: _dot(h_scr[...], w_in_ref[:, i * GROUP_LANES:(i + 1) * GROUP_LANES])
    slabs = [(finish_q, j) for j in range(N_KV_HEADS)] + [(finish_k, 0), (finish_v, 0)]
    u_next = project(0)
    for i, (finish, j) in enumerate(slabs):
        u = u_next
        if i + 1 < len(slabs):
            u_next = project(i + 1)
        finish(j, u)
    gate_slabs = [functools.partial(lambda j: finish_g(j, project(len(slabs) + j)), j)
                  for j in range(N_KV_HEADS)]
    _attention_phases(q_scr, kp_scr, vt_scr, None, s_scr, p_scr, ot_scr, between=gate_slabs)
    _branch_epilogue(ot_scr, sg_scr[...], x_ref, mod_ref, w_out_ref, lng_ref, lnb_ref, out_ref)


def _attn_seq(x2, mods4, w_in_bf, pn, qg, kg, w_out_bf, ln_g, ln_b, *, n_seq, seq_len, mod_row):
    d = D_MODEL
    n_tok = n_seq * seq_len
    full = lambda b: (0, 0)
    rows = lambda width: pl.BlockSpec((seq_len, width), lambda b: (b, 0))
    heads = pl.BlockSpec((None, seq_len, N_KV_HEADS, HEAD_DIM), lambda b: (b, 0, 0, 0))
    consts = [w_in_bf, pn, qg, kg, w_out_bf, ln_g, ln_b]
    group = N_HEADS // N_KV_HEADS
    return pl.pallas_call(
        _attn_seq_kernel,
        grid=(n_seq,),
        in_specs=[rows(d), pl.BlockSpec((None, None, 1, 3 * d), lambda b: (0, mod_row, 0, 0))]
        + [pl.BlockSpec(c.shape, full) for c in consts],
        out_specs=(rows(d), heads, heads),
        out_shape=(jax.ShapeDtypeStruct((n_tok, d), F32),
                   jax.ShapeDtypeStruct((n_seq, seq_len, N_KV_HEADS, HEAD_DIM), F32),
                   jax.ShapeDtypeStruct((n_seq, seq_len, N_KV_HEADS, HEAD_DIM), F32)),
        scratch_shapes=[
            pltpu.VMEM((seq_len, d), BF16),
            pltpu.VMEM((seq_len, ATTN_WIDTH), BF16),
            pltpu.VMEM((seq_len, KPAD_WIDTH), BF16),
            pltpu.VMEM((KV_WIDTH, seq_len), BF16),
            pltpu.VMEM((seq_len, ATTN_WIDTH), BF16),
            pltpu.VMEM((2, group, seq_len, seq_len), F32),
            pltpu.VMEM((2, seq_len, seq_len), BF16),
            pltpu.VMEM((ATTN_WIDTH, seq_len), F32),
        ],
        compiler_params=pltpu.CompilerParams(vmem_limit_bytes=VMEM_LIMIT),
        name="attn_seq",
    )(x2, mods4, *consts)


def _attn_core(q, kp, vt, cache, sg, x2, mods4, w_out_bf, ln_g, ln_b, *, n_seq, seq_len, q_tile,
               mod_row0):
    d = D_MODEL
    tiles = seq_len // q_tile
    ck, cv = cache
    n_keys = seq_len + ck.shape[1]
    q3 = q.reshape(n_seq, seq_len, ATTN_WIDTH)
    kp3 = kp.reshape(n_seq, seq_len, KPAD_WIDTH)
    sg3 = sg.reshape(n_seq, seq_len, ATTN_WIDTH)
    x3 = x2.reshape(n_seq, seq_len, d)
    tile_spec = lambda width: pl.BlockSpec((None, q_tile, width), lambda b, i: (b, i, 0))
    seq_spec = lambda rows, width: pl.BlockSpec((None, rows, width), lambda b, i: (b, 0, 0))
    full2 = lambda b, i: (0, 0)
    out = pl.pallas_call(
        _attn_core_kernel,
        grid=(n_seq, tiles),
        in_specs=[
            tile_spec(ATTN_WIDTH), seq_spec(seq_len, KPAD_WIDTH), seq_spec(KV_WIDTH, seq_len),
            seq_spec(ck.shape[1], KV_WIDTH), seq_spec(cv.shape[1], KV_WIDTH),
            tile_spec(ATTN_WIDTH), tile_spec(d),
            pl.BlockSpec((None, None, 1, 3 * d), lambda b, i: (0, mod_row0 + b, 0, 0)),
            pl.BlockSpec(w_out_bf.shape, full2),
            pl.BlockSpec(ln_g.shape, full2),
            pl.BlockSpec(ln_b.shape, full2),
        ],
        out_specs=tile_spec(d),
        out_shape=jax.ShapeDtypeStruct((n_seq, seq_len, d), F32),
        scratch_shapes=[
            pltpu.VMEM((2, N_HEADS // N_KV_HEADS, n_keys, q_tile), F32),
            pltpu.VMEM((2, n_keys, q_tile), BF16),
            pltpu.VMEM((ATTN_WIDTH, q_tile), F32),
        ],
        compiler_params=pltpu.CompilerParams(vmem_limit_bytes=VMEM_LIMIT),
        name="attn_core_cache",
    )(q3, kp3, vt, ck, cv, sg3, x3, mods4, w_out_bf, ln_g, ln_b)
    return out.reshape(n_seq * seq_len, d)


def _rec_proj_kernel(x_ref, mod_ref, w_ref, lb_ref, q_out, v_out, g_out, lf_fw, lf_bw, k_fw, k_bw):
    d = D_MODEL
    mod = mod_ref[...]
    shift, scale = mod[:, :d], mod[:, d:2 * d]
    h = (x_ref[...] * (1.0 + scale) + shift).astype(BF16)
    slab = REC_PROJ_SLAB

    def store_channel_major(out_ref, cols, val):
        val_t = val.T.astype(out_ref.dtype)
        for i in range(val.shape[0] // GLA_TILE):
            out_ref[i, cols, :] = val_t[:, i * GLA_TILE:(i + 1) * GLA_TILE]

    def finish_q(cols, u):
        store_channel_major(q_out, cols, u * _sigmoid(u))

    def finish_gate(direction, lf_out, k_out):
        def finish(cols, z):
            lb = lb_ref[direction:direction + 1, cols]
            sig = _sigmoid(z)
            lf_out[:, cols] = jnp.log(lb + (1.0 - lb) * sig)
            store_channel_major(k_out, cols, (1.0 - lb) * (1.0 - sig))
        return finish

    def finish_v(cols, u):
        v_out[:, cols] = u.astype(v_out.dtype)

    def finish_g(cols, u):
        g_out[:, cols] = (u * _sigmoid(u)).astype(g_out.dtype)

    sections = (finish_q, finish_gate(0, lf_fw, k_fw), finish_gate(1, lf_bw, k_bw),
                finish_v, finish_g)
    per_section = REC_WIDTH // slab
    n_slabs = len(sections) * per_section
    project = lambda i: _dot(h, w_ref[:, i * slab:(i + 1) * slab])
    v_section = sections.index(finish_v)
    order = ([i for i in range(n_slabs) if i // per_section != v_section]
             + [i for i in range(n_slabs) if i // per_section == v_section])
    u_next = project(order[0])
    for n, i in enumerate(order):
        u = u_next
        if n + 1 < n_slabs:
            u_next = project(order[n + 1])
        within = i % per_section
        sections[i // per_section](slice(within * slab, (within + 1) * slab), u)


def _rec_proj(x2, mods4, w_bf, lb, *, seq_len, mod_row0, per_seq_mod):
    n_tok = x2.shape[0]
    d = D_MODEL
    tiles_per_seq = seq_len // ROW_TILE

    def mod_map(i):
        row = mod_row0 + (i // tiles_per_seq if per_seq_mod else 0)
        return (1, row, 0, 0)

    row_spec = pl.BlockSpec((ROW_TILE, REC_WIDTH), lambda i: (i, 0))
    tiles = ROW_TILE // GLA_TILE
    tile_spec = pl.BlockSpec((None, tiles, REC_WIDTH, GLA_TILE),
                             lambda i: (i // tiles_per_seq, i % tiles_per_seq, 0, 0))
    full = lambda i: (0, 0)
    row_out = lambda t: (row_spec, jax.ShapeDtypeStruct((n_tok, REC_WIDTH), t))
    tile_out = (tile_spec, jax.ShapeDtypeStruct(
        (n_tok // seq_len, seq_len // GLA_TILE, REC_WIDTH, GLA_TILE), BF16))
    outs = (tile_out, row_out(BF16), row_out(BF16), row_out(F32), row_out(F32), tile_out, tile_out)
    return pl.pallas_call(
        _rec_proj_kernel,
        grid=(n_tok // ROW_TILE,),
        in_specs=[
            pl.BlockSpec((ROW_TILE, d), lambda i: (i, 0)),
            pl.BlockSpec((None, None, 1, 3 * d), mod_map),
            pl.BlockSpec(w_bf.shape, full),
            pl.BlockSpec(lb.shape, full),
        ],
        out_specs=tuple(spec for spec, _ in outs),
        out_shape=tuple(shape for _, shape in outs),
        compiler_params=pltpu.CompilerParams(vmem_limit_bytes=VMEM_LIMIT),
        name="rec_proj",
    )(x2, mods4, w_bf, lb)


def _block_diag(a, b):
    za = jnp.zeros(a.shape, a.dtype)
    return jnp.concatenate(
        [jnp.concatenate([a, za], axis=1), jnp.concatenate([za, b], axis=1)], axis=0)


def _pair_scores(a_t, c_t):
    return _dot_tn(a_t, _block_diag(c_t[:REC_DK, :], c_t[REC_DK:, :]))


def _gla_scores_stage(units):
    t = GLA_TILE
    for u in units:
        lf2 = u["lf"]() * LOG2_E
        hi = lf2.astype(BF16)
        lo = (lf2 - hi.astype(F32)).astype(BF16)
        u["sums"] = _dot_tn(jnp.concatenate([hi, lo], axis=0), u["sums_ref"][...])
        u["scores"] = u["masks_ref"][0] * _pair_scores(u["q_t"](), u["k_t"]()).astype(BF16)
    for li in range(len(GLA_LEVELS)):
        for u in units:
            x = jnp.exp2(u["sums"][:, (1 + li) * t:(2 + li) * t]).astype(BF16)
            z = _pair_scores(u["q_t"]() * x, u["k_t"]() * x).astype(BF16)
            u["scores"] = u["scores"] + u["masks_ref"][1 + li] * z
    for u in units:
        u["save"](u["scores"], u["sums"][:, :t])


def _gla_state_stage(u):
    t = GLA_TILE
    dk = REC_DK
    b_t, st_ref = u["load_b"](), u["st_ref"]
    q_t, k_t, v = u["q_t"](), u["k_t"](), u["v"]()
    edge = b_t[:, 0:1] if u["backward"] else b_t[:, t - 1:t]
    o = _dot(u["load_scores"](), _block_diag(v[:, :dk], v[:, dk:]))
    st_a, st_b = st_ref[0], st_ref[1]
    q_in = q_t * jnp.exp2(b_t).astype(BF16)
    o = o + _dot_tn(q_in, _block_diag(st_a.astype(BF16), st_b.astype(BF16)))
    k_edge = k_t * jnp.exp2(edge - b_t).astype(BF16)
    carry = jnp.exp2(edge)
    st_ref[0] = st_a * carry[:dk, :] + _dot(k_edge[:dk, :], v[:, :dk])
    st_ref[1] = st_b * carry[dk:, :] + _dot(k_edge[dk:, :], v[:, dk:])
    u["store"](o)


def _gla_kernel(*refs, n_tiles, has_state, want_state):
    refs = list(refs)
    q_ref, v_ref, lff_ref, lfb_ref, kf_ref, kb_ref = refs[:6]
    pos = 6
    if has_state:
        s0_ref = refs[pos]
        pos += 1
    sums_f_ref, sums_b_ref, mf_ref, mb_ref = refs[pos:pos + 4]
    pos += 4
    o_ref = refs[pos]
    pos += 1
    if want_state:
        s_out_ref = refs[pos]
        pos += 1
    st_ref, pipe_s, pipe_b = refs[pos:pos + 3]

    t = GLA_TILE
    n_seqs = v_ref.shape[0]
    n_items = n_seqs * n_tiles
    pair_lanes = 2 * REC_DK
    o_ref[...] = jnp.zeros(o_ref.shape, o_ref.dtype)
    if has_state:
        st_ref[...] = s0_ref[...]
    else:
        st_ref[...] = jnp.zeros(st_ref.shape, st_ref.dtype)

    def units_of(item):
        seq, step = item // n_tiles, item % n_tiles
        units = []
        for direction, (lf_ref, k_ref, sums_ref, m_ref) in enumerate(
                ((lff_ref, kf_ref, sums_f_ref, mf_ref), (lfb_ref, kb_ref, sums_b_ref, mb_ref))):
            tile = step if direction == 0 else n_tiles - 1 - step
            rows = pl.ds(pl.multiple_of(tile * t, t), t)
            for pair in range(GLA_HEADS_PER_STEP // 2):
                lanes = slice(pair * pair_lanes, (pair + 1) * pair_lanes)
                ui = len(units)

                def store(o, rows=rows, lanes=lanes):
                    o_ref[seq, rows, lanes] += o

                def save(scores, b_t, ui=ui):
                    pipe_s[ui] = scores
                    pipe_b[ui] = b_t

                rows_of = lambda ref, rows=rows, lanes=lanes: (lambda: ref[seq, rows, lanes])
                channels_of = lambda ref, tile=tile, lanes=lanes: (lambda: ref[seq, tile, lanes, :])
                units.append(dict(
                    q_t=channels_of(q_ref), k_t=channels_of(k_ref), v=rows_of(v_ref),
                    lf=rows_of(lf_ref), st_ref=st_ref.at[seq, direction, pl.ds(2 * pair, 2)],
                    sums_ref=sums_ref, masks_ref=m_ref, backward=direction == 1, store=store,
                    save=save, load_scores=lambda ui=ui: pipe_s[ui], load_b=lambda ui=ui: pipe_b[ui]))
        return units

    def body(item, carry):
        units = units_of(item)
        _gla_scores_stage(units)
        for u in units:
            _gla_state_stage(u)
        return carry

    lax.fori_loop(0, n_items, body, 0)
    if want_state:
        s_out_ref[...] = st_ref[...]


def _gla(q, v, lf_fw, lf_bw, k_fw, k_bw, s0, consts, *, n_seq, seq_len, want_state):
    has_state = s0 is not None
    width = REC_WIDTH
    hps = GLA_HEADS_PER_STEP
    n_tiles = seq_len // GLA_TILE
    seqs = min(n_seq, max(1, GLA_ITEMS_PER_STEP // n_tiles))
    assert n_seq % seqs == 0
    n_units = hps
    seq3 = lambda a: a.reshape(n_seq, seq_len, width)
    head_spec = pl.BlockSpec((seqs, seq_len, hps * REC_DK), lambda b, h: (b, 0, h))
    tile_spec = pl.BlockSpec((seqs, n_tiles, hps * REC_DK, GLA_TILE), lambda b, h: (b, 0, h, 0))
    state_spec = pl.BlockSpec((seqs, 2, hps, REC_DK, REC_DV), lambda b, h: (b, 0, h, 0, 0))
    in_specs = [tile_spec, head_spec, head_spec, head_spec, tile_spec, tile_spec]
    args = [q, seq3(v), seq3(lf_fw), seq3(lf_bw), k_fw, k_bw]
    if has_state:
        in_specs.append(state_spec)
        args.append(s0)
    for c in consts:
        in_specs.append(pl.BlockSpec(c.shape, lambda b, h, nd=c.ndim: (0,) * nd))
        args.append(c)
    out_shape = [jax.ShapeDtypeStruct((n_seq, seq_len, width), F32)]
    out_specs = [head_spec]
    if want_state:
        out_shape.append(jax.ShapeDtypeStruct((n_seq, 2, N_REC_HEADS, REC_DK, REC_DV), F32))
        out_specs.append(state_spec)
    res = pl.pallas_call(
        functools.partial(_gla_kernel, n_tiles=n_tiles, has_state=has_state, want_state=want_state),
        grid=(n_seq // seqs, N_REC_HEADS // hps),
        in_specs=in_specs,
        out_specs=tuple(out_specs),
        out_shape=tuple(out_shape),
        scratch_shapes=[
            pltpu.VMEM((seqs, 2, hps, REC_DK, REC_DV), F32),
            pltpu.VMEM((n_units, GLA_TILE, 2 * GLA_TILE), BF16),
            pltpu.VMEM((n_units, 2 * REC_DK, GLA_TILE), F32),
        ],
        compiler_params=pltpu.CompilerParams(vmem_limit_bytes=VMEM_LIMIT),
        name="gla_state_in" if has_state else "gla_state_out",
    )(*args)
    o = res[0].reshape(n_seq * seq_len, width)
    return (o, res[1]) if want_state else (o, None)


def _rec_out_kernel(o_ref, sg_ref, x_ref, mod_ref, ng_ref, w_ref, lng_ref, lnb_ref, out_ref):
    d = D_MODEL
    gate = mod_ref[...][:, 2 * d:]

    def branch_of(rows):
        parts = []
        for hd in range(N_REC_HEADS):
            oh = o_ref[rows, hd * REC_DV:(hd + 1) * REC_DV]
            ms = jnp.mean(oh * oh, axis=-1, keepdims=True)
            parts.append(oh * lax.rsqrt(ms + NORM_EPS) * ng_ref[...])
        o = jnp.concatenate(parts, axis=1)
        gated = (o * sg_ref[rows, :].astype(F32)).astype(BF16)
        return _dot(gated, w_ref[...])

    chunks = [pl.ds(c * OUT_ROW_CHUNK, OUT_ROW_CHUNK) for c in range(o_ref.shape[0] // OUT_ROW_CHUNK)]
    branches = [branch_of(rows) for rows in chunks]
    for rows, branch in zip(chunks, branches):
        y = DEEPNORM_ALPHA * x_ref[rows, :] + gate * branch
        out_ref[rows, :] = _layer_norm(y, lng_ref[...], lnb_ref[...])


def _rec_out(o, sg, x2, mods4, norm_gain, w_out_bf, ln_g, ln_b, *, seq_len, mod_row0, per_seq_mod):
    n_tok = x2.shape[0]
    d = D_MODEL
    assert not per_seq_mod or seq_len % OUT_ROW_TILE == 0
    tiles_per_seq = max(1, seq_len // OUT_ROW_TILE)

    def mod_map(i):
        row = mod_row0 + (i // tiles_per_seq if per_seq_mod else 0)
        return (1, row, 0, 0)

    row_spec = pl.BlockSpec((OUT_ROW_TILE, d), lambda i: (i, 0))
    full = lambda i: (0, 0)
    return pl.pallas_call(
        _rec_out_kernel,
        grid=(n_tok // OUT_ROW_TILE,),
        in_specs=[
            row_spec, row_spec, row_spec,
            pl.BlockSpec((None, None, 1, 3 * d), mod_map),
            pl.BlockSpec(norm_gain.shape, full),
            pl.BlockSpec(w_out_bf.shape, full),
            pl.BlockSpec(ln_g.shape, full),
            pl.BlockSpec(ln_b.shape, full),
        ],
        out_specs=row_spec,
        out_shape=jax.ShapeDtypeStruct((n_tok, d), F32),
        compiler_params=pltpu.CompilerParams(vmem_limit_bytes=VMEM_LIMIT),
        name="rec_out",
    )(o, sg, x2, mods4, norm_gain, w_out_bf, ln_g, ln_b)


def _rope_tables(n_tokens):
    n_rows = n_tokens // GRID_W
    rows = jnp.repeat(jnp.arange(n_rows, dtype=F32), GRID_W)
    cols = jnp.tile(jnp.arange(GRID_W, dtype=F32), n_rows)
    inv_freq = 1.0 / (ROPE_THETA ** (jnp.arange(0, AXIS_DIM, 2, dtype=F32) / AXIS_DIM))
    ang_r = rows[:, None] * inv_freq[None, :]
    ang_c = cols[:, None] * inv_freq[None, :]
    ang = jnp.concatenate([ang_r, ang_r, ang_c, ang_c], axis=-1)
    cos, sin = jnp.cos(ang), jnp.sin(ang)
    first = (jnp.arange(HEAD_DIM) % AXIS_DIM) < AXIS_DIM // 2
    sin_a = jnp.where(first[None, :], -sin, 0.0)
    sin_b = jnp.where(first[None, :], 0.0, sin)
    reps = GROUP_LANES // HEAD_DIM
    return tuple(jnp.tile(t, (1, reps)) for t in (cos, sin_a, sin_b))


def _head_mean_matrix():
    idx = jnp.arange(GROUP_LANES) // HEAD_DIM
    return jnp.where(idx[:, None] == idx[None, :], 1.0 / HEAD_DIM, 0.0).astype(BF16)


def _gla_consts():
    t = GLA_TILE
    r = jnp.arange(t)[:, None]
    c = jnp.arange(t)[None, :]
    fw = [c <= r]
    bw = [c >= r]
    for half in GLA_LEVELS:
        start = 2 * half * (r // (2 * half))
        upper = r - start >= half
        last_low, first_up = start + half - 1, start + half
        fw.append(jnp.where(upper, (c > last_low) & (c <= r), (c > r) & (c <= last_low)))
        bw.append(jnp.where(upper, (c >= first_up) & (c < r), (c >= r) & (c < first_up)))
    twice = lambda a: jnp.concatenate([a, a], axis=-1)

    def summation(weights):
        cols = jnp.concatenate([w.T for w in weights], axis=1).astype(BF16)
        return jnp.concatenate([cols, cols], axis=0)

    sums_fw, sums_bw = summation(fw), summation(bw)
    masks = [(r == c)]
    for half in GLA_LEVELS:
        size = 2 * half
        masks.append((r // size == c // size) & (r % size >= half) & (c % size < half))
    m_fw = jnp.stack(masks).astype(BF16)
    m_bw = jnp.swapaxes(m_fw, 1, 2)
    return sums_fw, sums_bw, twice(m_fw), twice(m_bw)


def kernel(x_prompt, x_sample, cache_k, cache_v, state_rec, c, c_ctx, ada_w, ada_b, attn_w_in,
           attn_q_gain, attn_k_gain, attn_w_out, rec_w_in, rec_lower_bounds, rec_norm_gain,
           rec_w_out, ln_gain, ln_bias):
    d = D_MODEL
    n_p, len_p, _ = x_prompt.shape
    n_s, len_s, _ = x_sample.shape
    past = cache_k.shape[2]
    assert past % SLAB_LANES == 0 and len_p % SLAB_LANES == 0 and len_s % SLAB_LANES == 0

    cond = jnp.zeros((COND_ROWS, d), F32).at[0].set(c_ctx).at[1:1 + n_s].set(c)
    mods4 = _mods(cond, ada_w, ada_b).reshape(DEPTH, COND_ROWS, 1, 3 * d)
    lb_all = _lower_bounds(rec_lower_bounds)

    xp = x_prompt.reshape(n_p * len_p, d)
    xs = x_sample.reshape(n_s * len_s, d)

    w_in = attn_w_in[0].astype(BF16)
    w_out = attn_w_out[0].astype(BF16)
    reps = GROUP_LANES // HEAD_DIM
    qg = (jnp.tile(attn_q_gain[0], reps) * (LOG2_E / math.sqrt(HEAD_DIM))).reshape(1, GROUP_LANES)
    kg = jnp.tile(attn_k_gain[0], reps).reshape(1, GROUP_LANES)
    pn = _head_mean_matrix()
    ln_g = ln_gain[0].reshape(1, d)
    ln_b = ln_bias[0].reshape(1, d)

    xp1, k_p, v_p = _attn_seq(xp, mods4, w_in, pn, qg, kg, w_out, ln_g, ln_b, n_seq=n_p,
                              seq_len=len_p, mod_row=0)
    q_s, kp_s, vt_s, g_s = _attn_proj(xs, mods4, w_in, pn, qg, kg, _rope_tables(len_s), seq_len=len_s,
                                      mod_row0=1, per_seq_mod=True, cache_out=False)
    cache = (cache_k[:, 0].reshape(n_s, past, KV_WIDTH), cache_v[:, 0].reshape(n_s, past, KV_WIDTH))
    xs1 = _attn_core(q_s, kp_s, vt_s, cache, g_s, xs, mods4, w_out, ln_g, ln_b, n_seq=n_s,
                     seq_len=len_s, q_tile=ROW_TILE, mod_row0=1)
    new_cache_k = k_p.reshape(n_p, 1, len_p, N_KV_HEADS, HEAD_DIM)
    new_cache_v = v_p.reshape(n_p, 1, len_p, N_KV_HEADS, HEAD_DIM)

    rw_in = rec_w_in[0].astype(BF16)
    rw_out = rec_w_out[0].astype(BF16)
    lb = lb_all[1]
    ng = rec_norm_gain[0].reshape(1, REC_DV)
    ln_g = ln_gain[1].reshape(1, d)
    ln_b = ln_bias[1].reshape(1, d)
    consts = _gla_consts()

    outs = []
    states = None
    for x1, n_seq, seq_len, row0, per_seq, s0 in (
            (xp1, n_p, len_p, 0, False, None),
            (xs1, n_s, len_s, 1, True, state_rec[:, 0])):
        q, v, g, lf_fw, lf_bw, k_fw, k_bw = _rec_proj(x1, mods4, rw_in, lb, seq_len=seq_len,
                                                       mod_row0=row0, per_seq_mod=per_seq)
        o, st = _gla(q, v, lf_fw, lf_bw, k_fw, k_bw, s0, consts, n_seq=n_seq, seq_len=seq_len,
                     want_state=s0 is None)
        if st is not None:
            states = st
        outs.append(_rec_out(o, g, x1, mods4, ng, rw_out, ln_g, ln_b, seq_len=seq_len,
                             mod_row0=row0, per_seq_mod=per_seq))

    y_prompt = outs[0].reshape(n_p, len_p, d)
    y_sample = outs[1].reshape(n_s, len_s, d)
    new_state_rec = states.reshape(n_p, 1, 2, N_REC_HEADS, REC_DK, REC_DV)
    return (y_prompt, y_sample, new_cache_k, new_cache_v, new_state_rec)
```

```python
import functools
import math

import jax
import jax.numpy as jnp
from jax import lax
from jax.experimental import pallas as pl
from jax.experimental.pallas import tpu as pltpu

F32 = jnp.float32
BF16 = jnp.bfloat16

D_MODEL = 1024
DEPTH = 2
GRID_W = 64
N_HEADS = 16
N_KV_HEADS = 4
HEAD_DIM = 64
AXIS_DIM = HEAD_DIM // 2
ATTN_WIDTH = N_HEADS * HEAD_DIM
KV_WIDTH = N_KV_HEADS * HEAD_DIM
ROPE_THETA = 10000.0
N_REC_HEADS = 8
REC_DK = 128
REC_DV = 128
REC_WIDTH = N_REC_HEADS * REC_DK
NORM_EPS = 1e-6
LN_EPS = 1e-5
DEEPNORM_ALPHA = (2.0 * DEPTH) ** 0.25

SUBLANES = 8
BF16_ROWS = 16
COND_ROWS = SUBLANES
ROW_TILE = 256
OUT_ROW_TILE = 512
OUT_ROW_CHUNK = 256
GROUP_LANES = 256
SLAB_LANES = 128
KPAD_WIDTH = 2 * N_KV_HEADS * SLAB_LANES
REC_PROJ_SLAB = 256
GLA_TILE = 128
GLA_LEVELS = (1, 2, 4, 8, 16, 32, 64)
GLA_HEADS_PER_STEP = 4
GLA_ITEMS_PER_STEP = 8
LOG2_E = 1.4426950408889634
VMEM_LIMIT = 56 * 1024 * 1024


def _sigmoid(x):
    return 1.0 / (1.0 + jnp.exp(-x))


def _dot(a, b):
    return jnp.dot(a, b, preferred_element_type=F32)


def _dot_nt(a, b):
    return lax.dot_general(a, b, (((1,), (1,)), ((), ())), preferred_element_type=F32)


def _dot_tn(a, b):
    return lax.dot_general(a, b, (((0,), (0,)), ((), ())), preferred_element_type=F32)


def _layer_norm(y, g, b):
    mu = jnp.mean(y, axis=-1, keepdims=True)
    yc = y - mu
    var = jnp.mean(yc * yc, axis=-1, keepdims=True)
    return yc * lax.rsqrt(var + LN_EPS) * g + b


def _mods_kernel(cond_ref, w_ref, b_ref, out_ref):
    c = cond_ref[...]
    s = (c * _sigmoid(c)).astype(BF16)
    out_ref[...] = _dot(s, w_ref[...].astype(BF16)) + b_ref[...]


def _mods(cond, ada_w, ada_b):
    d = D_MODEL
    return pl.pallas_call(
        _mods_kernel,
        grid=(DEPTH, 3),
        in_specs=[
            pl.BlockSpec((COND_ROWS, d), lambda l, j: (0, 0)),
            pl.BlockSpec((None, d, d), lambda l, j: (l, 0, j)),
            pl.BlockSpec((None, 1, d), lambda l, j: (l, 0, j)),
        ],
        out_specs=pl.BlockSpec((None, COND_ROWS, d), lambda l, j: (l, 0, j)),
        out_shape=jax.ShapeDtypeStruct((DEPTH, COND_ROWS, 3 * d), F32),
        compiler_params=pltpu.CompilerParams(vmem_limit_bytes=VMEM_LIMIT),
        name="adaln_mods",
    )(cond, ada_w, ada_b.reshape(DEPTH, 1, 3 * d))


def _lower_bounds_kernel(r_ref, out_ref):
    r = [r_ref[i] for i in range(DEPTH)]
    m = functools.reduce(jnp.maximum, r)
    e = [jnp.exp(x - m) for x in r]
    tot = functools.reduce(lambda a, b: a + b, e)
    soft = [x / tot for x in e]
    acc = soft[0]
    for i in range(DEPTH):
        if i > 0:
            acc = acc + soft[i]
        out_ref[i] = acc - soft[0]


def _lower_bounds(rec_lower_bounds):
    return pl.pallas_call(
        _lower_bounds_kernel,
        out_shape=jax.ShapeDtypeStruct(rec_lower_bounds.shape, F32),
        name="rec_lower_bounds",
    )(rec_lower_bounds)


def _kv_head_slabs(k):
    lane = lax.broadcasted_iota(jnp.int32, (k.shape[0], SLAB_LANES), 1)
    low = lane < HEAD_DIM
    heads_per_slab = SLAB_LANES // HEAD_DIM
    out = []
    for j in range(N_KV_HEADS):
        tile = k[:, (j // heads_per_slab) * SLAB_LANES:(j // heads_per_slab + 1) * SLAB_LANES]
        moved = pltpu.roll(tile, HEAD_DIM, 1)
        at_low, at_high = (tile, moved) if j % heads_per_slab == 0 else (moved, tile)
        out.append(jnp.where(low, at_low, 0.0))
        out.append(jnp.where(low, 0.0, at_high))
    return jnp.concatenate(out, axis=1)


def _head_norm(u, pn, gain):
    ms = _dot((u * u).astype(BF16), pn)
    return u * lax.rsqrt(ms + NORM_EPS) * gain


def _attn_proj_kernel(x_ref, mod_ref, w_ref, pn_ref, qg_ref, kg_ref, cos_ref, sa_ref, sb_ref,
                      q_out, kp_out, vt_out, g_out):
    d = D_MODEL
    mod = mod_ref[...]
    shift, scale = mod[:, :d], mod[:, d:2 * d]
    h = (x_ref[...] * (1.0 + scale) + shift).astype(BF16)
    pn = pn_ref[...]

    def norm_rope(u, gain):
        y = _head_norm(u, pn, gain)
        return (y * cos_ref[...]
                + pltpu.roll(y, GROUP_LANES - AXIS_DIM // 2, 1) * sa_ref[...]
                + pltpu.roll(y, AXIS_DIM // 2, 1) * sb_ref[...])

    def finish_q(j, u):
        q_out[:, j * GROUP_LANES:(j + 1) * GROUP_LANES] = norm_rope(u, qg_ref[...]).astype(q_out.dtype)

    def finish_k(_, u):
        kp_out[...] = _kv_head_slabs(norm_rope(u, kg_ref[...])).astype(kp_out.dtype)

    def finish_v(_, v):
        vt_out[...] = v.T.astype(vt_out.dtype)

    def finish_g(j, u):
        g_out[:, j * GROUP_LANES:(j + 1) * GROUP_LANES] = (u * _sigmoid(u)).astype(g_out.dtype)

    slabs = ([(finish_q, j) for j in range(N_KV_HEADS)] + [(finish_k, 0), (finish_v, 0)]
             + [(finish_g, j) for j in range(N_KV_HEADS)])
    project = lambda i: _dot(h, w_ref[:, i * GROUP_LANES:(i + 1) * GROUP_LANES])
    v_slab = N_KV_HEADS + 1
    order = [i for i in range(len(slabs)) if i != v_slab] + [v_slab]
    u_next = project(order[0])
    for n, i in enumerate(order):
        u = u_next
        if n + 1 < len(order):
            u_next = project(order[n + 1])
        finish, j = slabs[i]
        finish(j, u)


def _attn_proj(x2, mods4, w_bf, pn, qg, kg, rope_tabs, *, seq_len, mod_row0):
    n_tok = x2.shape[0]
    d = D_MODEL
    tiles_per_seq = seq_len // ROW_TILE
    full = lambda i: (0, 0)
    rows = lambda width: pl.BlockSpec((ROW_TILE, width), lambda i: (i, 0))
    table = pl.BlockSpec((ROW_TILE, GROUP_LANES), lambda i: (i % tiles_per_seq, 0))
    consts = [w_bf, pn, qg, kg]
    return pl.pallas_call(
        _attn_proj_kernel,
        grid=(n_tok // ROW_TILE,),
        in_specs=[rows(d),
                  pl.BlockSpec((None, None, 1, 3 * d), lambda i: (0, mod_row0 + i // tiles_per_seq, 0, 0))]
        + [pl.BlockSpec(c.shape, full) for c in consts] + [table] * len(rope_tabs),
        out_specs=(
            rows(ATTN_WIDTH),
            rows(KPAD_WIDTH),
            pl.BlockSpec((None, KV_WIDTH, ROW_TILE), lambda i: (i // tiles_per_seq, 0, i % tiles_per_seq)),
            rows(ATTN_WIDTH),
        ),
        out_shape=(
            jax.ShapeDtypeStruct((n_tok, ATTN_WIDTH), BF16),
            jax.ShapeDtypeStruct((n_tok, KPAD_WIDTH), BF16),
            jax.ShapeDtypeStruct((n_tok // seq_len, KV_WIDTH, seq_len), BF16),
            jax.ShapeDtypeStruct((n_tok, ATTN_WIDTH), BF16),
        ),
        compiler_params=pltpu.CompilerParams(vmem_limit_bytes=VMEM_LIMIT),
        name="attn_proj_rope",
    )(x2, mods4, *consts, *rope_tabs)


def _attention_phases(q_ref, kp_ref, vt_ref, cache, s_scr, p_scr, ot_scr, between=()):
    n_new = kp_ref.shape[0]
    n_old = cache[0].shape[0] if cache is not None else 0
    n_keys = n_old + n_new
    group = N_HEADS // N_KV_HEADS
    heads_per_slab = SLAB_LANES // HEAD_DIM
    pending_work = list(between)

    def fold_rows(a, op):
        return functools.reduce(
            op, [a[r * SUBLANES:(r + 1) * SUBLANES, :] for r in range(a.shape[0] // SUBLANES)])

    def scores_phase(j):
        slot = j % 2
        maxima = []
        for hh in range(group):
            hd = j * group + hh
            slab, where = hd // heads_per_slab, hd % heads_per_slab
            q_slab = q_ref[:, slab * SLAB_LANES:(slab + 1) * SLAB_LANES]
            k_slab = heads_per_slab * j + where
            lanes = slice(k_slab * SLAB_LANES, (k_slab + 1) * SLAB_LANES)
            s = _dot_nt(kp_ref[:, lanes], q_slab)
            s_scr[slot, hh, n_old:, :] = s
            mx = fold_rows(s, jnp.maximum)
            if cache is not None:
                s = _dot_nt(cache[0][:, lanes], q_slab)
                s_scr[slot, hh, :n_old, :] = s
                mx = jnp.maximum(mx, fold_rows(s, jnp.maximum))
            maxima.append(jnp.max(mx, axis=0, keepdims=True))
        return maxima

    def values_phase(j, maxima):
        slot = j % 2
        rows = slice(j * HEAD_DIM, (j + 1) * HEAD_DIM)
        v_t = vt_ref[rows, :]
        if cache is not None:
            v_t = jnp.concatenate([cache[1][rows, :], v_t], axis=1)
        v_ext = jnp.concatenate([v_t, jnp.ones((BF16_ROWS, n_keys), BF16)], axis=0)
        for hh in range(group):
            hd = j * group + hh
            p_scr[hh % 2] = jnp.exp2(s_scr[slot, hh] - maxima[hh]).astype(BF16)
            acc = _dot(v_ext, p_scr[hh % 2])
            l = acc[HEAD_DIM:HEAD_DIM + 1, :]
            ot_scr[hd * HEAD_DIM:(hd + 1) * HEAD_DIM, :] = acc[:HEAD_DIM, :] * (1.0 / l)

    pending = scores_phase(0)
    for j in range(N_KV_HEADS):
        nxt = scores_phase(j + 1) if j + 1 < N_KV_HEADS else None
        if pending_work:
            pending_work.pop(0)()
        values_phase(j, pending)
        pending = nxt
    for thunk in pending_work:
        thunk()


def _branch_epilogue(ot_scr, sg, x_ref, mod_ref, w_ref, lng_ref, lnb_ref, out_ref):
    d = D_MODEL
    o = ot_scr[...].T
    gated = (o * sg.astype(F32)).astype(BF16)
    branch = _dot(gated, w_ref[...])
    gate = mod_ref[...][:, 2 * d:]
    y = DEEPNORM_ALPHA * x_ref[...] + gate * branch
    out_ref[...] = _layer_norm(y, lng_ref[...], lnb_ref[...])


def _attn_core_kernel(q_ref, kp_ref, vt_ref, kc_ref, vc_ref, sg_ref, x_ref, mod_ref, w_ref,
                      lng_ref, lnb_ref, out_ref, s_scr, p_scr, ot_scr):
    cache = (_kv_head_slabs(kc_ref[...]).astype(BF16), vc_ref[...].T.astype(BF16))
    _attention_phases(q_ref, kp_ref, vt_ref, cache, s_scr, p_scr, ot_scr)
    _branch_epilogue(ot_scr, sg_ref[...], x_ref, mod_ref, w_ref, lng_ref, lnb_ref, out_ref)


def _attn_seq_kernel(x_ref, mod_ref, w_in_ref, pn_ref, qg_ref, kg_ref, w_out_ref, lng_ref, lnb_ref,
                     out_ref, k_out, v_out,
                     h_scr, q_scr, kp_scr, vt_scr, sg_scr, s_scr, p_scr, ot_scr):
    d = D_MODEL
    mod = mod_ref[...]
    shift, scale = mod[:, :d], mod[:, d:2 * d]
    h_scr[...] = (x_ref[...] * (1.0 + scale) + shift).astype(BF16)
    pn = pn_ref[...]

    def finish_q(j, u):
        q_scr[:, j * GROUP_LANES:(j + 1) * GROUP_LANES] = _head_norm(u, pn, qg_ref[...]).astype(BF16)

    def finish_k(_, u):
        k = _head_norm(u, pn, kg_ref[...])
        kp_scr[...] = _kv_head_slabs(k).astype(BF16)
        k_out[...] = k.reshape(k_out.shape)

    def finish_v(_, v):
        vt_scr[...] = v.T.astype(BF16)
        v_out[...] = v.reshape(v_out.shape)

    def finish_g(j, u):
        sg_scr[:, j * GROUP_LANES:(j + 1) * GROUP_LANES] = (u * _sigmoid(u)).astype(BF16)

    project = lambda i: _dot(h_scr[...], w_in_ref[:, i * GROUP_LANES:(i + 1) * GROUP_LANES])
    slabs = [(finish_q, j) for j in range(N_KV_HEADS)] + [(finish_k, 0), (finish_v, 0)]
    u_next = project(0)
    for i, (finish, j) in enumerate(slabs):
        u = u_next
        if i + 1 < len(slabs):
            u_next = project(i + 1)
        finish(j, u)
    gate_slabs = [functools.partial(lambda j: finish_g(j, project(len(slabs) + j)), j)
                  for j in range(N_KV_HEADS)]
    _attention_phases(q_scr, kp_scr, vt_scr, None, s_scr, p_scr, ot_scr, between=gate_slabs)
    _branch_epilogue(ot_scr, sg_scr[...], x_ref, mod_ref, w_out_ref, lng_ref, lnb_ref, out_ref)


def _attn_seq(x2, mods4, w_in_bf, pn, qg, kg, w_out_bf, ln_g, ln_b, *, n_seq, seq_len, mod_row):
    d = D_MODEL
    n_tok = n_seq * seq_len
    full = lambda b: (0, 0)
    rows = lambda width: pl.BlockSpec((seq_len, width), lambda b: (b, 0))
    heads = pl.BlockSpec((None, seq_len, N_KV_HEADS, HEAD_DIM), lambda b: (b, 0, 0, 0))
    consts = [w_in_bf, pn, qg, kg, w_out_bf, ln_g, ln_b]
    group = N_HEADS // N_KV_HEADS
    return pl.pallas_call(
        _attn_seq_kernel,
        grid=(n_seq,),
        in_specs=[rows(d), pl.BlockSpec((None, None, 1, 3 * d), lambda b: (0, mod_row, 0, 0))]
        + [pl.BlockSpec(c.shape, full) for c in consts],
        out_specs=(rows(d), heads, heads),
        out_shape=(jax.ShapeDtypeStruct((n_tok, d), F32),
                   jax.ShapeDtypeStruct((n_seq, seq_len, N_KV_HEADS, HEAD_DIM), F32),
                   jax.ShapeDtypeStruct((n_seq, seq_len, N_KV_HEADS, HEAD_DIM), F32)),
        scratch_shapes=[
            pltpu.VMEM((seq_len, d), BF16),
            pltpu.VMEM((seq_len, ATTN_WIDTH), BF16),
            pltpu.VMEM((seq_len, KPAD_WIDTH), BF16),
            pltpu.VMEM((KV_WIDTH, seq_len), BF16),
            pltpu.VMEM((seq_len, ATTN_WIDTH), BF16),
            pltpu.VMEM((2, group, seq_len, seq_len), F32),
            pltpu.VMEM((2, seq_len, seq_len), BF16),
            pltpu.VMEM((ATTN_WIDTH, seq_len), F32),
        ],
        compiler_params=pltpu.CompilerParams(vmem_limit_bytes=VMEM_LIMIT),
        name="attn_seq",
    )(x2, mods4, *consts)


def _attn_core(q, kp, vt, cache, sg, x2, mods4, w_out_bf, ln_g, ln_b, *, n_seq, seq_len, q_tile,
               mod_row0):
    d = D_MODEL
    tiles = seq_len // q_tile
    ck, cv = cache
    n_keys = seq_len + ck.shape[1]
    q3 = q.reshape(n_seq, seq_len, ATTN_WIDTH)
    kp3 = kp.reshape(n_seq, seq_len, KPAD_WIDTH)
    sg3 = sg.reshape(n_seq, seq_len, ATTN_WIDTH)
    x3 = x2.reshape(n_seq, seq_len, d)
    tile_spec = lambda width: pl.BlockSpec((None, q_tile, width), lambda b, i: (b, i, 0))
    seq_spec = lambda rows, width: pl.BlockSpec((None, rows, width), lambda b, i: (b, 0, 0))
    full2 = lambda b, i: (0, 0)
    out = pl.pallas_call(
        _attn_core_kernel,
        grid=(n_seq, tiles),
        in_specs=[
            tile_spec(ATTN_WIDTH), seq_spec(seq_len, KPAD_WIDTH), seq_spec(KV_WIDTH, seq_len),
            seq_spec(ck.shape[1], KV_WIDTH), seq_spec(cv.shape[1], KV_WIDTH),
            tile_spec(ATTN_WIDTH), tile_spec(d),
            pl.BlockSpec((None, None, 1, 3 * d), lambda b, i: (0, mod_row0 + b, 0, 0)),
            pl.BlockSpec(w_out_bf.shape, full2),
            pl.BlockSpec(ln_g.shape, full2),
            pl.BlockSpec(ln_b.shape, full2),
        ],
        out_specs=tile_spec(d),
        out_shape=jax.ShapeDtypeStruct((n_seq, seq_len, d), F32),
        scratch_shapes=[
            pltpu.VMEM((2, N_HEADS // N_KV_HEADS, n_keys, q_tile), F32),
            pltpu.VMEM((2, n_keys, q_tile), BF16),
            pltpu.VMEM((ATTN_WIDTH, q_tile), F32),
        ],
        compiler_params=pltpu.CompilerParams(vmem_limit_bytes=VMEM_LIMIT),
        name="attn_core_cache",
    )(q3, kp3, vt, ck, cv, sg3, x3, mods4, w_out_bf, ln_g, ln_b)
    return out.reshape(n_seq * seq_len, d)


def _rec_proj_kernel(x_ref, mod_ref, w_ref, lb_ref, q_out, v_out, g_out, lf_fw, lf_bw, k_fw, k_bw):
    d = D_MODEL
    mod = mod_ref[...]
    shift, scale = mod[:, :d], mod[:, d:2 * d]
    h = (x_ref[...] * (1.0 + scale) + shift).astype(BF16)
    slab = REC_PROJ_SLAB

    def store_channel_major(out_ref, cols, val):
        val_t = val.T.astype(out_ref.dtype)
        for i in range(val.shape[0] // GLA_TILE):
            out_ref[i, cols, :] = val_t[:, i * GLA_TILE:(i + 1) * GLA_TILE]

    def finish_q(cols, u):
        store_channel_major(q_out, cols, u * _sigmoid(u))

    def finish_gate(direction, lf_out, k_out):
        def finish(cols, z):
            lb = lb_ref[direction:direction + 1, cols]
            sig = _sigmoid(z)
            lf_out[:, cols] = jnp.log(lb + (1.0 - lb) * sig)
            store_channel_major(k_out, cols, (1.0 - lb) * (1.0 - sig))
        return finish

    def finish_v(cols, u):
        v_out[:, cols] = u.astype(v_out.dtype)

    def finish_g(cols, u):
        g_out[:, cols] = (u * _sigmoid(u)).astype(g_out.dtype)

    sections = (finish_q, finish_gate(0, lf_fw, k_fw), finish_gate(1, lf_bw, k_bw),
                finish_v, finish_g)
    per_section = REC_WIDTH // slab
    n_slabs = len(sections) * per_section
    project = lambda i: _dot(h, w_ref[:, i * slab:(i + 1) * slab])
    v_section = sections.index(finish_v)
    order = ([i for i in range(n_slabs) if i // per_section != v_section]
             + [i for i in range(n_slabs) if i // per_section == v_section])
    u_next = project(order[0])
    for n, i in enumerate(order):
        u = u_next
        if n + 1 < n_slabs:
            u_next = project(order[n + 1])
        within = i % per_section
        sections[i // per_section](slice(within * slab, (within + 1) * slab), u)


def _rec_proj(x2, mods4, w_bf, lb, *, seq_len, mod_row0, per_seq_mod):
    n_tok = x2.shape[0]
    d = D_MODEL
    tiles_per_seq = seq_len // ROW_TILE

    def mod_map(i):
        row = mod_row0 + (i // tiles_per_seq if per_seq_mod else 0)
        return (1, row, 0, 0)

    row_spec = pl.BlockSpec((ROW_TILE, REC_WIDTH), lambda i: (i, 0))
    tiles = ROW_TILE // GLA_TILE
    tile_spec = pl.BlockSpec((None, tiles, REC_WIDTH, GLA_TILE),
                             lambda i: (i // tiles_per_seq, i % tiles_per_seq, 0, 0))
    full = lambda i: (0, 0)
    row_out = lambda t: (row_spec, jax.ShapeDtypeStruct((n_tok, REC_WIDTH), t))
    tile_out = (tile_spec, jax.ShapeDtypeStruct(
        (n_tok // seq_len, seq_len // GLA_TILE, REC_WIDTH, GLA_TILE), BF16))
    outs = (tile_out, row_out(BF16), row_out(BF16), row_out(F32), row_out(F32), tile_out, tile_out)
    return pl.pallas_call(
        _rec_proj_kernel,
        grid=(n_tok // ROW_TILE,),
        in_specs=[
            pl.BlockSpec((ROW_TILE, d), lambda i: (i, 0)),
            pl.BlockSpec((None, None, 1, 3 * d), mod_map),
            pl.BlockSpec(w_bf.shape, full),
            pl.BlockSpec(lb.shape, full),
        ],
        out_specs=tuple(spec for spec, _ in outs),
        out_shape=tuple(shape for _, shape in outs),
        compiler_params=pltpu.CompilerParams(vmem_limit_bytes=VMEM_LIMIT),
        name="rec_proj",
    )(x2, mods4, w_bf, lb)


def _block_diag(a, b):
    za = jnp.zeros(a.shape, a.dtype)
    return jnp.concatenate(
        [jnp.concatenate([a, za], axis=1), jnp.concatenate([za, b], axis=1)], axis=0)


def _pair_scores(a_t, c_t):
    return _dot_tn(a_t, _block_diag(c_t[:REC_DK, :], c_t[REC_DK:, :]))


def _gla_scores_stage(units):
    t = GLA_TILE
    for u in units:
        lf2 = u["lf"]() * LOG2_E
        hi = lf2.astype(BF16)
        lo = (lf2 - hi.astype(F32)).astype(BF16)
        u["sums"] = _dot_tn(jnp.concatenate([hi, lo], axis=0), u["sums_ref"][...])
        u["scores"] = u["masks_ref"][0] * _pair_scores(u["q_t"](), u["k_t"]()).astype(BF16)
    for li in range(len(GLA_LEVELS)):
        for u in units:
            x = jnp.exp2(u["sums"][:, (1 + li) * t:(2 + li) * t]).astype(BF16)
            z = _pair_scores(u["q_t"]() * x, u["k_t"]() * x).astype(BF16)
            u["scores"] = u["scores"] + u["masks_ref"][1 + li] * z
    for u in units:
        u["save"](u["scores"], u["sums"][:, :t])


def _gla_state_stage(u):
    t = GLA_TILE
    dk = REC_DK
    b_t, st_ref = u["load_b"](), u["st_ref"]
    q_t, k_t, v = u["q_t"](), u["k_t"](), u["v"]()
    edge = b_t[:, 0:1] if u["backward"] else b_t[:, t - 1:t]
    o = _dot(u["load_scores"](), _block_diag(v[:, :dk], v[:, dk:]))
    st_a, st_b = st_ref[0], st_ref[1]
    q_in = q_t * jnp.exp2(b_t).astype(BF16)
    o = o + _dot_tn(q_in, _block_diag(st_a.astype(BF16), st_b.astype(BF16)))
    k_edge = k_t * jnp.exp2(edge - b_t).astype(BF16)
    carry = jnp.exp2(edge)
    st_ref[0] = st_a * carry[:dk, :] + _dot(k_edge[:dk, :], v[:, :dk])
    st_ref[1] = st_b * carry[dk:, :] + _dot(k_edge[dk:, :], v[:, dk:])
    u["store"](o)


def _gla_kernel(*refs, n_tiles, has_state, want_state):
    refs = list(refs)
    q_ref, v_ref, lff_ref, lfb_ref, kf_ref, kb_ref = refs[:6]
    pos = 6
    if has_state:
        s0_ref = refs[pos]
        pos += 1
    sums_f_ref, sums_b_ref, mf_ref, mb_ref = refs[pos:pos + 4]
    pos += 4
    o_ref = refs[pos]
    pos += 1
    if want_state:
        s_out_ref = refs[pos]
        pos += 1
    st_ref, pipe_s, pipe_b, o_acc = refs[pos:pos + 4]

    t = GLA_TILE
    n_seqs = v_ref.shape[0]
    n_items = n_seqs * n_tiles
    pair_lanes = 2 * REC_DK
    o_acc[...] = jnp.zeros(o_acc.shape, o_acc.dtype)
    if has_state:
        st_ref[...] = s0_ref[...]
    else:
        st_ref[...] = jnp.zeros(st_ref.shape, st_ref.dtype)

    def units_of(item):
        seq, step = item // n_tiles, item % n_tiles
        units = []
        for direction, (lf_ref, k_ref, sums_ref, m_ref) in enumerate(
                ((lff_ref, kf_ref, sums_f_ref, mf_ref), (lfb_ref, kb_ref, sums_b_ref, mb_ref))):
            tile = step if direction == 0 else n_tiles - 1 - step
            rows = pl.ds(pl.multiple_of(tile * t, t), t)
            for pair in range(GLA_HEADS_PER_STEP // 2):
                lanes = slice(pair * pair_lanes, (pair + 1) * pair_lanes)
                ui = len(units)

                def store(o, rows=rows, lanes=lanes):
                    o_acc[seq, rows, lanes] += o

                def save(scores, b_t, ui=ui):
                    pipe_s[ui] = scores
                    pipe_b[ui] = b_t

                rows_of = lambda ref, rows=rows, lanes=lanes: (lambda: ref[seq, rows, lanes])
                channels_of = lambda ref, tile=tile, lanes=lanes: (lambda: ref[seq, tile, lanes, :])
                units.append(dict(
                    q_t=channels_of(q_ref), k_t=channels_of(k_ref), v=rows_of(v_ref),
                    lf=rows_of(lf_ref), st_ref=st_ref.at[seq, direction, pl.ds(2 * pair, 2)],
                    sums_ref=sums_ref, masks_ref=m_ref, backward=direction == 1, store=store,
                    save=save, load_scores=lambda ui=ui: pipe_s[ui], load_b=lambda ui=ui: pipe_b[ui]))
        return units

    def body(item, carry):
        units = units_of(item)
        _gla_scores_stage(units)
        for u in units:
            _gla_state_stage(u)
        return carry

    lax.fori_loop(0, n_items, body, 0)
    o_ref[...] = o_acc[...].astype(o_ref.dtype)
    if want_state:
        s_out_ref[...] = st_ref[...]


def _gla(q, v, lf_fw, lf_bw, k_fw, k_bw, s0, consts, *, n_seq, seq_len, want_state):
    has_state = s0 is not None
    width = REC_WIDTH
    hps = GLA_HEADS_PER_STEP
    n_tiles = seq_len // GLA_TILE
    seqs = min(n_seq, max(1, GLA_ITEMS_PER_STEP // n_tiles))
    assert n_seq % seqs == 0
    n_units = hps
    seq3 = lambda a: a.reshape(n_seq, seq_len, width)
    head_spec = pl.BlockSpec((seqs, seq_len, hps * REC_DK), lambda b, h: (b, 0, h))
    tile_spec = pl.BlockSpec((seqs, n_tiles, hps * REC_DK, GLA_TILE), lambda b, h: (b, 0, h, 0))
    state_spec = pl.BlockSpec((seqs, 2, hps, REC_DK, REC_DV), lambda b, h: (b, 0, h, 0, 0))
    in_specs = [tile_spec, head_spec, head_spec, head_spec, tile_spec, tile_spec]
    args = [q, seq3(v), seq3(lf_fw), seq3(lf_bw), k_fw, k_bw]
    if has_state:
        in_specs.append(state_spec)
        args.append(s0)
    for c in consts:
        in_specs.append(pl.BlockSpec(c.shape, lambda b, h, nd=c.ndim: (0,) * nd))
        args.append(c)
    out_shape = [jax.ShapeDtypeStruct((n_seq, seq_len, width), BF16)]
    out_specs = [head_spec]
    if want_state:
        out_shape.append(jax.ShapeDtypeStruct((n_seq, 2, N_REC_HEADS, REC_DK, REC_DV), F32))
        out_specs.append(state_spec)
    res = pl.pallas_call(
        functools.partial(_gla_kernel, n_tiles=n_tiles, has_state=has_state, want_state=want_state),
        grid=(n_seq // seqs, N_REC_HEADS // hps),
        in_specs=in_specs,
        out_specs=tuple(out_specs),
        out_shape=tuple(out_shape),
        scratch_shapes=[
            pltpu.VMEM((seqs, 2, hps, REC_DK, REC_DV), F32),
            pltpu.VMEM((n_units, GLA_TILE, 2 * GLA_TILE), BF16),
            pltpu.VMEM((n_units, 2 * REC_DK, GLA_TILE), F32),
            pltpu.VMEM((seqs, seq_len, hps * REC_DV), F32),
        ],
        compiler_params=pltpu.CompilerParams(vmem_limit_bytes=VMEM_LIMIT),
        name="gla_state_in" if has_state else "gla_state_out",
    )(*args)
    o = res[0].reshape(n_seq * seq_len, width)
    return (o, res[1]) if want_state else (o, None)


def _rec_out_kernel(o_ref, sg_ref, x_ref, mod_ref, ng_ref, w_ref, lng_ref, lnb_ref, out_ref):
    d = D_MODEL
    gate = mod_ref[...][:, 2 * d:]

    def branch_of(rows):
        parts = []
        for hd in range(N_REC_HEADS):
            oh = o_ref[rows, hd * REC_DV:(hd + 1) * REC_DV].astype(F32)
            ms = jnp.mean(oh * oh, axis=-1, keepdims=True)
            parts.append(oh * lax.rsqrt(ms + NORM_EPS) * ng_ref[...])
        o = jnp.concatenate(parts, axis=1)
        gated = (o * sg_ref[rows, :].astype(F32)).astype(BF16)
        return _dot(gated, w_ref[...])

    chunks = [pl.ds(c * OUT_ROW_CHUNK, OUT_ROW_CHUNK) for c in range(o_ref.shape[0] // OUT_ROW_CHUNK)]
    branches = [branch_of(rows) for rows in chunks]
    for rows, branch in zip(chunks, branches):
        y = DEEPNORM_ALPHA * x_ref[rows, :] + gate * branch
        out_ref[rows, :] = _layer_norm(y, lng_ref[...], lnb_ref[...])


def _rec_out(o, sg, x2, mods4, norm_gain, w_out_bf, ln_g, ln_b, *, seq_len, mod_row0, per_seq_mod):
    n_tok = x2.shape[0]
    d = D_MODEL
    assert not per_seq_mod or seq_len % OUT_ROW_TILE == 0
    tiles_per_seq = max(1, seq_len // OUT_ROW_TILE)

    def mod_map(i):
        row = mod_row0 + (i // tiles_per_seq if per_seq_mod else 0)
        return (1, row, 0, 0)

    row_spec = pl.BlockSpec((OUT_ROW_TILE, d), lambda i: (i, 0))
    full = lambda i: (0, 0)
    return pl.pallas_call(
        _rec_out_kernel,
        grid=(n_tok // OUT_ROW_TILE,),
        in_specs=[
            row_spec, row_spec, row_spec,
            pl.BlockSpec((None, None, 1, 3 * d), mod_map),
            pl.BlockSpec(norm_gain.shape, full),
            pl.BlockSpec(w_out_bf.shape, full),
            pl.BlockSpec(ln_g.shape, full),
            pl.BlockSpec(ln_b.shape, full),
        ],
        out_specs=row_spec,
        out_shape=jax.ShapeDtypeStruct((n_tok, d), F32),
        compiler_params=pltpu.CompilerParams(vmem_limit_bytes=VMEM_LIMIT),
        name="rec_out",
    )(o, sg, x2, mods4, norm_gain, w_out_bf, ln_g, ln_b)


def _rope_tables(n_tokens):
    n_rows = n_tokens // GRID_W
    rows = jnp.repeat(jnp.arange(n_rows, dtype=F32), GRID_W)
    cols = jnp.tile(jnp.arange(GRID_W, dtype=F32), n_rows)
    inv_freq = 1.0 / (ROPE_THETA ** (jnp.arange(0, AXIS_DIM, 2, dtype=F32) / AXIS_DIM))
    ang_r = rows[:, None] * inv_freq[None, :]
    ang_c = cols[:, None] * inv_freq[None, :]
    ang = jnp.concatenate([ang_r, ang_r, ang_c, ang_c], axis=-1)
    cos, sin = jnp.cos(ang), jnp.sin(ang)
    first = (jnp.arange(HEAD_DIM) % AXIS_DIM) < AXIS_DIM // 2
    sin_a = jnp.where(first[None, :], -sin, 0.0)
    sin_b = jnp.where(first[None, :], 0.0, sin)
    reps = GROUP_LANES // HEAD_DIM
    return tuple(jnp.tile(t, (1, reps)) for t in (cos, sin_a, sin_b))


def _head_mean_matrix():
    idx = jnp.arange(GROUP_LANES) // HEAD_DIM
    return jnp.where(idx[:, None] == idx[None, :], 1.0 / HEAD_DIM, 0.0).astype(BF16)


def _gla_consts():
    t = GLA_TILE
    r = jnp.arange(t)[:, None]
    c = jnp.arange(t)[None, :]
    fw = [c <= r]
    bw = [c >= r]
    for half in GLA_LEVELS:
        start = 2 * half * (r // (2 * half))
        upper = r - start >= half
        last_low, first_up = start + half - 1, start + half
        fw.append(jnp.where(upper, (c > last_low) & (c <= r), (c > r) & (c <= last_low)))
        bw.append(jnp.where(upper, (c >= first_up) & (c < r), (c >= r) & (c < first_up)))
    twice = lambda a: jnp.concatenate([a, a], axis=-1)

    def summation(weights):
        cols = jnp.concatenate([w.T for w in weights], axis=1).astype(BF16)
        return jnp.concatenate([cols, cols], axis=0)

    sums_fw, sums_bw = summation(fw), summation(bw)
    masks = [(r == c)]
    for half in GLA_LEVELS:
        size = 2 * half
        masks.append((r // size == c // size) & (r % size >= half) & (c % size < half))
    m_fw = jnp.stack(masks).astype(BF16)
    m_bw = jnp.swapaxes(m_fw, 1, 2)
    return sums_fw, sums_bw, twice(m_fw), twice(m_bw)


def kernel(x_prompt, x_sample, cache_k, cache_v, state_rec, c, c_ctx, ada_w, ada_b, attn_w_in,
           attn_q_gain, attn_k_gain, attn_w_out, rec_w_in, rec_lower_bounds, rec_norm_gain,
           rec_w_out, ln_gain, ln_bias):
    d = D_MODEL
    n_p, len_p, _ = x_prompt.shape
    n_s, len_s, _ = x_sample.shape
    past = cache_k.shape[2]
    assert past % SLAB_LANES == 0 and len_p % SLAB_LANES == 0 and len_s % SLAB_LANES == 0

    cond = jnp.zeros((COND_ROWS, d), F32).at[0].set(c_ctx).at[1:1 + n_s].set(c)
    mods4 = _mods(cond, ada_w, ada_b).reshape(DEPTH, COND_ROWS, 1, 3 * d)
    lb_all = _lower_bounds(rec_lower_bounds)

    xp = x_prompt.reshape(n_p * len_p, d)
    xs = x_sample.reshape(n_s * len_s, d)

    w_in = attn_w_in[0].astype(BF16)
    w_out = attn_w_out[0].astype(BF16)
    reps = GROUP_LANES // HEAD_DIM
    qg = (jnp.tile(attn_q_gain[0], reps) * (LOG2_E / math.sqrt(HEAD_DIM))).reshape(1, GROUP_LANES)
    kg = jnp.tile(attn_k_gain[0], reps).reshape(1, GROUP_LANES)
    pn = _head_mean_matrix()
    ln_g = ln_gain[0].reshape(1, d)
    ln_b = ln_bias[0].reshape(1, d)

    xp1, k_p, v_p = _attn_seq(xp, mods4, w_in, pn, qg, kg, w_out, ln_g, ln_b, n_seq=n_p,
                              seq_len=len_p, mod_row=0)
    q_s, kp_s, vt_s, g_s = _attn_proj(xs, mods4, w_in, pn, qg, kg, _rope_tables(len_s), seq_len=len_s,
                                      mod_row0=1)
    cache = (cache_k[:, 0].reshape(n_s, past, KV_WIDTH), cache_v[:, 0].reshape(n_s, past, KV_WIDTH))
    xs1 = _attn_core(q_s, kp_s, vt_s, cache, g_s, xs, mods4, w_out, ln_g, ln_b, n_seq=n_s,
                     seq_len=len_s, q_tile=ROW_TILE, mod_row0=1)
    new_cache_k = k_p.reshape(n_p, 1, len_p, N_KV_HEADS, HEAD_DIM)
    new_cache_v = v_p.reshape(n_p, 1, len_p, N_KV_HEADS, HEAD_DIM)

    rw_in = rec_w_in[0].astype(BF16)
    rw_out = rec_w_out[0].astype(BF16)
    lb = lb_all[1]
    ng = rec_norm_gain[0].reshape(1, REC_DV)
    ln_g = ln_gain[1].reshape(1, d)
    ln_b = ln_bias[1].reshape(1, d)
    consts = _gla_consts()

    outs = []
    states = None
    for x1, n_seq, seq_len, row0, per_seq, s0 in (
            (xp1, n_p, len_p, 0, False, None),
            (xs1, n_s, len_s, 1, True, state_rec[:, 0])):
        q, v, g, lf_fw, lf_bw, k_fw, k_bw = _rec_proj(x1, mods4, rw_in, lb, seq_len=seq_len,
                                                       mod_row0=row0, per_seq_mod=per_seq)
        o, st = _gla(q, v, lf_fw, lf_bw, k_fw, k_bw, s0, consts, n_seq=n_seq, seq_len=seq_len,
                     want_state=s0 is None)
        if st is not None:
            states = st
        outs.append(_rec_out(o, g, x1, mods4, ng, rw_out, ln_g, ln_b, seq_len=seq_len,
                             mod_row0=row0, per_seq_mod=per_seq))

    y_prompt = outs[0].reshape(n_p, len_p, d)
    y_sample = outs[1].reshape(n_s, len_s, d)
    new_state_rec = states.reshape(n_p, 1, 2, N_REC_HEADS, REC_DK, REC_DV)
    return (y_prompt, y_sample, new_cache_k, new_cache_v, new_state_rec)
```

```python
import functools
import math

import jax
import jax.numpy as jnp
from jax import lax
from jax.experimental import pallas as pl
from jax.experimental.pallas import tpu as pltpu

F32 = jnp.float32
BF16 = jnp.bfloat16

D_MODEL = 1024
DEPTH = 2
GRID_W = 64
N_HEADS = 16
N_KV_HEADS = 4
HEAD_DIM = 64
AXIS_DIM = HEAD_DIM // 2
ATTN_WIDTH = N_HEADS * HEAD_DIM
KV_WIDTH = N_KV_HEADS * HEAD_DIM
ROPE_THETA = 10000.0
N_REC_HEADS = 8
REC_DK = 128
REC_DV = 128
REC_WIDTH = N_REC_HEADS * REC_DK
NORM_EPS = 1e-6
LN_EPS = 1e-5
DEEPNORM_ALPHA = (2.0 * DEPTH) ** 0.25

SUBLANES = 8
BF16_ROWS = 16
COND_ROWS = SUBLANES
ATTN_Q_TILE = 256
ATTN_PROJ_ROW_TILE = 512
REC_ROW_TILE = 512
OUT_ROW_TILE = 512
OUT_ROW_CHUNK = 256
ATTN_SEQS_PER_STEP = 2
GROUP_LANES = 256
SLAB_LANES = 128
KPAD_WIDTH = 2 * N_KV_HEADS * SLAB_LANES
REC_PROJ_SLAB = 256
GLA_TILE = 128
GLA_LEVELS = (1, 2, 4, 8, 16, 32, 64)
GLA_HEADS_PER_STEP = 4
GLA_ITEMS_PER_STEP = 8
LOG2_E = 1.4426950408889634
VMEM_LIMIT = 56 * 1024 * 1024


def _sigmoid(x):
    return 1.0 / (1.0 + jnp.exp(-x))


def _dot(a, b):
    return jnp.dot(a, b, preferred_element_type=F32)


def _dot_nt(a, b):
    return lax.dot_general(a, b, (((1,), (1,)), ((), ())), preferred_element_type=F32)


def _dot_tn(a, b):
    return lax.dot_general(a, b, (((0,), (0,)), ((), ())), preferred_element_type=F32)


def _layer_norm(y, g, b):
    mu = jnp.mean(y, axis=-1, keepdims=True)
    yc = y - mu
    var = jnp.mean(yc * yc, axis=-1, keepdims=True)
    return yc * lax.rsqrt(var + LN_EPS) * g + b


def _mods_kernel(cond_ref, w_ref, b_ref, out_ref):
    c = cond_ref[...]
    s = (c * _sigmoid(c)).astype(BF16)
    out_ref[...] = _dot(s, w_ref[...].astype(BF16)) + b_ref[...]


def _mods(cond, ada_w, ada_b):
    d = D_MODEL
    return pl.pallas_call(
        _mods_kernel,
        grid=(DEPTH, 3),
        in_specs=[
            pl.BlockSpec((COND_ROWS, d), lambda l, j: (0, 0)),
            pl.BlockSpec((None, d, d), lambda l, j: (l, 0, j)),
            pl.BlockSpec((None, 1, d), lambda l, j: (l, 0, j)),
        ],
        out_specs=pl.BlockSpec((None, COND_ROWS, d), lambda l, j: (l, 0, j)),
        out_shape=jax.ShapeDtypeStruct((DEPTH, COND_ROWS, 3 * d), F32),
        compiler_params=pltpu.CompilerParams(vmem_limit_bytes=VMEM_LIMIT),
        name="adaln_mods",
    )(cond, ada_w, ada_b.reshape(DEPTH, 1, 3 * d))


def _lower_bounds_kernel(r_ref, out_ref):
    r = [r_ref[i] for i in range(DEPTH)]
    m = functools.reduce(jnp.maximum, r)
    e = [jnp.exp(x - m) for x in r]
    tot = functools.reduce(lambda a, b: a + b, e)
    soft = [x / tot for x in e]
    acc = soft[0]
    for i in range(DEPTH):
        if i > 0:
            acc = acc + soft[i]
        out_ref[i] = acc - soft[0]


def _lower_bounds(rec_lower_bounds):
    return pl.pallas_call(
        _lower_bounds_kernel,
        out_shape=jax.ShapeDtypeStruct(rec_lower_bounds.shape, F32),
        name="rec_lower_bounds",
    )(rec_lower_bounds)


def _kv_head_slabs(k):
    lane = lax.broadcasted_iota(jnp.int32, (k.shape[0], SLAB_LANES), 1)
    low = lane < HEAD_DIM
    heads_per_slab = SLAB_LANES // HEAD_DIM
    out = []
    for j in range(N_KV_HEADS):
        tile = k[:, (j // heads_per_slab) * SLAB_LANES:(j // heads_per_slab + 1) * SLAB_LANES]
        moved = pltpu.roll(tile, HEAD_DIM, 1)
        at_low, at_high = (tile, moved) if j % heads_per_slab == 0 else (moved, tile)
        out.append(jnp.where(low, at_low, 0.0))
        out.append(jnp.where(low, 0.0, at_high))
    return jnp.concatenate(out, axis=1)


def _head_norm(u, pn, gain):
    ms = _dot((u * u).astype(BF16), pn)
    return u * lax.rsqrt(ms + NORM_EPS) * gain


def _attn_proj_kernel(x_ref, mod_ref, w_ref, pn_ref, qg_ref, kg_ref, cos_ref, sa_ref, sb_ref,
                      q_out, kp_out, vt_out, g_out):
    d = D_MODEL
    mod = mod_ref[...]
    shift, scale = mod[:, :d], mod[:, d:2 * d]
    h = (x_ref[...] * (1.0 + scale) + shift).astype(BF16)
    pn = pn_ref[...]

    def norm_rope(u, gain):
        y = _head_norm(u, pn, gain)
        return (y * cos_ref[...]
                + pltpu.roll(y, GROUP_LANES - AXIS_DIM // 2, 1) * sa_ref[...]
                + pltpu.roll(y, AXIS_DIM // 2, 1) * sb_ref[...])

    def finish_q(j, u):
        q_out[:, j * GROUP_LANES:(j + 1) * GROUP_LANES] = norm_rope(u, qg_ref[...]).astype(q_out.dtype)

    def finish_k(_, u):
        kp_out[...] = _kv_head_slabs(norm_rope(u, kg_ref[...])).astype(kp_out.dtype)

    def finish_v(_, v):
        vt_out[...] = v.T.astype(vt_out.dtype)

    def finish_g(j, u):
        g_out[:, j * GROUP_LANES:(j + 1) * GROUP_LANES] = (u * _sigmoid(u)).astype(g_out.dtype)

    slabs = ([(finish_q, j) for j in range(N_KV_HEADS)] + [(finish_k, 0), (finish_v, 0)]
             + [(finish_g, j) for j in range(N_KV_HEADS)])
    project = lambda i: _dot(h, w_ref[:, i * GROUP_LANES:(i + 1) * GROUP_LANES])
    v_slab = N_KV_HEADS + 1
    order = [i for i in range(len(slabs)) if i != v_slab] + [v_slab]
    u_next = project(order[0])
    for n, i in enumerate(order):
        u = u_next
        if n + 1 < len(order):
            u_next = project(order[n + 1])
        finish, j = slabs[i]
        finish(j, u)


def _attn_proj(x2, mods4, w_bf, pn, qg, kg, rope_tabs, *, seq_len, mod_row0):
    n_tok = x2.shape[0]
    d = D_MODEL
    tile = ATTN_PROJ_ROW_TILE
    assert seq_len % tile == 0
    tiles_per_seq = seq_len // tile
    full = lambda i: (0, 0)
    rows = lambda width: pl.BlockSpec((tile, width), lambda i: (i, 0))
    table = pl.BlockSpec((tile, GROUP_LANES), lambda i: (i % tiles_per_seq, 0))
    consts = [w_bf, pn, qg, kg]
    return pl.pallas_call(
        _attn_proj_kernel,
        grid=(n_tok // tile,),
        in_specs=[rows(d),
                  pl.BlockSpec((None, None, 1, 3 * d), lambda i: (0, mod_row0 + i // tiles_per_seq, 0, 0))]
        + [pl.BlockSpec(c.shape, full) for c in consts] + [table] * len(rope_tabs),
        out_specs=(
            rows(ATTN_WIDTH),
            rows(KPAD_WIDTH),
            pl.BlockSpec((None, KV_WIDTH, tile), lambda i: (i // tiles_per_seq, 0, i % tiles_per_seq)),
            rows(ATTN_WIDTH),
        ),
        out_shape=(
            jax.ShapeDtypeStruct((n_tok, ATTN_WIDTH), BF16),
            jax.ShapeDtypeStruct((n_tok, KPAD_WIDTH), BF16),
            jax.ShapeDtypeStruct((n_tok // seq_len, KV_WIDTH, seq_len), BF16),
            jax.ShapeDtypeStruct((n_tok, ATTN_WIDTH), BF16),
        ),
        compiler_params=pltpu.CompilerParams(vmem_limit_bytes=VMEM_LIMIT),
        name="attn_proj_rope",
    )(x2, mods4, *consts, *rope_tabs)


def _attention_phases(q_ref, kp_ref, vt_ref, cache, s_scr, p_scr, ot_scr, between=()):
    n_new = kp_ref.shape[0]
    n_old = cache[0].shape[0] if cache is not None else 0
    n_keys = n_old + n_new
    group = N_HEADS // N_KV_HEADS
    heads_per_slab = SLAB_LANES // HEAD_DIM
    pending_work = list(between)

    def fold_rows(a, op):
        return functools.reduce(
            op, [a[r * SUBLANES:(r + 1) * SUBLANES, :] for r in range(a.shape[0] // SUBLANES)])

    def scores_phase(j):
        slot = j % 2
        maxima = []
        for hh in range(group):
            hd = j * group + hh
            slab, where = hd // heads_per_slab, hd % heads_per_slab
            q_slab = q_ref[:, slab * SLAB_LANES:(slab + 1) * SLAB_LANES]
            k_slab = heads_per_slab * j + where
            lanes = slice(k_slab * SLAB_LANES, (k_slab + 1) * SLAB_LANES)
            s = _dot_nt(kp_ref[:, lanes], q_slab)
            s_scr[slot, hh, n_old:, :] = s
            mx = fold_rows(s, jnp.maximum)
            if cache is not None:
                s = _dot_nt(cache[0][:, lanes], q_slab)
                s_scr[slot, hh, :n_old, :] = s
                mx = jnp.maximum(mx, fold_rows(s, jnp.maximum))
            maxima.append(jnp.max(mx, axis=0, keepdims=True))
        return maxima

    def values_phase(j, maxima):
        slot = j % 2
        rows = slice(j * HEAD_DIM, (j + 1) * HEAD_DIM)
        v_t = vt_ref[rows, :]
        if cache is not None:
            v_t = jnp.concatenate([cache[1][rows, :], v_t], axis=1)
        v_ext = jnp.concatenate([v_t, jnp.ones((BF16_ROWS, n_keys), BF16)], axis=0)
        for hh in range(group):
            hd = j * group + hh
            p_scr[hh % 2] = jnp.exp2(s_scr[slot, hh] - maxima[hh]).astype(BF16)
            acc = _dot(v_ext, p_scr[hh % 2])
            l = acc[HEAD_DIM:HEAD_DIM + 1, :]
            ot_scr[hd * HEAD_DIM:(hd + 1) * HEAD_DIM, :] = acc[:HEAD_DIM, :] * (1.0 / l)

    pending = scores_phase(0)
    for j in range(N_KV_HEADS):
        nxt = scores_phase(j + 1) if j + 1 < N_KV_HEADS else None
        if pending_work:
            pending_work.pop(0)()
        values_phase(j, pending)
        pending = nxt
    for thunk in pending_work:
        thunk()


def _branch_epilogue(ot_scr, sg, x_ref, mod_ref, w_ref, lng_ref, lnb_ref, out_ref):
    d = D_MODEL
    o = ot_scr[...].T
    gated = (o * sg.astype(F32)).astype(BF16)
    branch = _dot(gated, w_ref[...])
    gate = mod_ref[...][:, 2 * d:]
    y = DEEPNORM_ALPHA * x_ref[...] + gate * branch
    out_ref[...] = _layer_norm(y, lng_ref[...], lnb_ref[...])


def _attn_core_kernel(q_ref, kp_ref, vt_ref, kc_ref, vc_ref, sg_ref, x_ref, mod_ref, w_ref,
                      lng_ref, lnb_ref, out_ref, s_scr, p_scr, ot_scr):
    cache = (_kv_head_slabs(kc_ref[...]).astype(BF16), vc_ref[...].T.astype(BF16))
    _attention_phases(q_ref, kp_ref, vt_ref, cache, s_scr, p_scr, ot_scr)
    _branch_epilogue(ot_scr, sg_ref[...], x_ref, mod_ref, w_ref, lng_ref, lnb_ref, out_ref)


def _attn_seq_kernel(x_ref, mod_ref, w_in_ref, pn_ref, qg_ref, kg_ref, w_out_ref, lng_ref, lnb_ref,
                     out_ref, k_out, v_out,
                     h_scr, q_scr, kp_scr, vt_scr, sg_scr, s_scr, p_scr, ot_scr):
    d = D_MODEL
    n_sub = x_ref.shape[0]
    mod = mod_ref[...]
    shift, scale = mod[:, :d], mod[:, d:2 * d]
    pn = pn_ref[...]
    project = lambda i: _dot(h_scr[...], w_in_ref[:, i * GROUP_LANES:(i + 1) * GROUP_LANES])

    def epilogue(s):
        _branch_epilogue(ot_scr.at[s], sg_scr[s], x_ref.at[s], mod_ref, w_out_ref, lng_ref, lnb_ref,
                         out_ref.at[s])

    for s in range(n_sub):
        h_scr[...] = (x_ref[s] * (1.0 + scale) + shift).astype(BF16)

        def finish_q(j, u):
            q_scr[:, j * GROUP_LANES:(j + 1) * GROUP_LANES] = _head_norm(u, pn, qg_ref[...]).astype(BF16)

        def finish_k(_, u, s=s):
            k = _head_norm(u, pn, kg_ref[...])
            kp_scr[...] = _kv_head_slabs(k).astype(BF16)
            k_out[s] = k.reshape(k_out.shape[1:])

        def finish_v(_, v, s=s):
            vt_scr[...] = v.T.astype(BF16)
            v_out[s] = v.reshape(v_out.shape[1:])

        def finish_g(j, u, s=s):
            sg_scr[s, :, j * GROUP_LANES:(j + 1) * GROUP_LANES] = (u * _sigmoid(u)).astype(BF16)

        slabs = [(finish_q, j) for j in range(N_KV_HEADS)] + [(finish_k, 0), (finish_v, 0)]
        u_next = project(0)
        if s > 0:
            epilogue(s - 1)
        for i, (finish, j) in enumerate(slabs):
            u = u_next
            if i + 1 < len(slabs):
                u_next = project(i + 1)
            finish(j, u)
        gate_slabs = [functools.partial(lambda j, g: g(j, project(len(slabs) + j)), j, finish_g)
                      for j in range(N_KV_HEADS)]
        _attention_phases(q_scr, kp_scr, vt_scr, None, s_scr, p_scr, ot_scr.at[s], between=gate_slabs)
    epilogue(n_sub - 1)


def _attn_seq(x2, mods4, w_in_bf, pn, qg, kg, w_out_bf, ln_g, ln_b, *, n_seq, seq_len, mod_row):
    d = D_MODEL
    n_sub = ATTN_SEQS_PER_STEP
    assert n_seq % n_sub == 0
    full = lambda b: (0, 0)
    rows = pl.BlockSpec((n_sub, seq_len, d), lambda b: (b, 0, 0))
    heads = pl.BlockSpec((n_sub, seq_len, N_KV_HEADS, HEAD_DIM), lambda b: (b, 0, 0, 0))
    consts = [w_in_bf, pn, qg, kg, w_out_bf, ln_g, ln_b]
    group = N_HEADS // N_KV_HEADS
    out, k, v = pl.pallas_call(
        _attn_seq_kernel,
        grid=(n_seq // n_sub,),
        in_specs=[rows, pl.BlockSpec((None, None, 1, 3 * d), lambda b: (0, mod_row, 0, 0))]
        + [pl.BlockSpec(c.shape, full) for c in consts],
        out_specs=(rows, heads, heads),
        out_shape=(jax.ShapeDtypeStruct((n_seq, seq_len, d), F32),
                   jax.ShapeDtypeStruct((n_seq, seq_len, N_KV_HEADS, HEAD_DIM), F32),
                   jax.ShapeDtypeStruct((n_seq, seq_len, N_KV_HEADS, HEAD_DIM), F32)),
        scratch_shapes=[
            pltpu.VMEM((seq_len, d), BF16),
            pltpu.VMEM((seq_len, ATTN_WIDTH), BF16),
            pltpu.VMEM((seq_len, KPAD_WIDTH), BF16),
            pltpu.VMEM((KV_WIDTH, seq_len), BF16),
            pltpu.VMEM((n_sub, seq_len, ATTN_WIDTH), BF16),
            pltpu.VMEM((2, group, seq_len, seq_len), F32),
            pltpu.VMEM((2, seq_len, seq_len), BF16),
            pltpu.VMEM((n_sub, ATTN_WIDTH, seq_len), F32),
        ],
        compiler_params=pltpu.CompilerParams(vmem_limit_bytes=VMEM_LIMIT),
        name="attn_seq",
    )(x2.reshape(n_seq, seq_len, d), mods4, *consts)
    return out.reshape(n_seq * seq_len, d), k, v


def _attn_core(q, kp, vt, cache, sg, x2, mods4, w_out_bf, ln_g, ln_b, *, n_seq, seq_len, q_tile,
               mod_row0):
    d = D_MODEL
    tiles = seq_len // q_tile
    ck, cv = cache
    n_keys = seq_len + ck.shape[1]
    q3 = q.reshape(n_seq, seq_len, ATTN_WIDTH)
    kp3 = kp.reshape(n_seq, seq_len, KPAD_WIDTH)
    sg3 = sg.reshape(n_seq, seq_len, ATTN_WIDTH)
    x3 = x2.reshape(n_seq, seq_len, d)
    tile_spec = lambda width: pl.BlockSpec((None, q_tile, width), lambda b, i: (b, i, 0))
    seq_spec = lambda rows, width: pl.BlockSpec((None, rows, width), lambda b, i: (b, 0, 0))
    full2 = lambda b, i: (0, 0)
    out = pl.pallas_call(
        _attn_core_kernel,
        grid=(n_seq, tiles),
        in_specs=[
            tile_spec(ATTN_WIDTH), seq_spec(seq_len, KPAD_WIDTH), seq_spec(KV_WIDTH, seq_len),
            seq_spec(ck.shape[1], KV_WIDTH), seq_spec(cv.shape[1], KV_WIDTH),
            tile_spec(ATTN_WIDTH), tile_spec(d),
            pl.BlockSpec((None, None, 1, 3 * d), lambda b, i: (0, mod_row0 + b, 0, 0)),
            pl.BlockSpec(w_out_bf.shape, full2),
            pl.BlockSpec(ln_g.shape, full2),
            pl.BlockSpec(ln_b.shape, full2),
        ],
        out_specs=tile_spec(d),
        out_shape=jax.ShapeDtypeStruct((n_seq, seq_len, d), F32),
        scratch_shapes=[
            pltpu.VMEM((2, N_HEADS // N_KV_HEADS, n_keys, q_tile), F32),
            pltpu.VMEM((2, n_keys, q_tile), BF16),
            pltpu.VMEM((ATTN_WIDTH, q_tile), F32),
        ],
        compiler_params=pltpu.CompilerParams(vmem_limit_bytes=VMEM_LIMIT),
        name="attn_core_cache",
    )(q3, kp3, vt, ck, cv, sg3, x3, mods4, w_out_bf, ln_g, ln_b)
    return out.reshape(n_seq * seq_len, d)


def _rec_proj_kernel(x_ref, mod_ref, w_ref, lb_ref, q_out, v_out, g_out, lf_fw, lf_bw, k_fw, k_bw):
    d = D_MODEL
    mod = mod_ref[...]
    shift, scale = mod[:, :d], mod[:, d:2 * d]
    h = (x_ref[...] * (1.0 + scale) + shift).astype(BF16)
    slab = REC_PROJ_SLAB

    def store_channel_major(out_ref, cols, val):
        val_t = val.T.astype(out_ref.dtype)
        tiles = out_ref.shape[1]
        for i in range(val.shape[0] // GLA_TILE):
            out_ref[i // tiles, i % tiles, cols, :] = val_t[:, i * GLA_TILE:(i + 1) * GLA_TILE]

    def finish_q(cols, u):
        store_channel_major(q_out, cols, u * _sigmoid(u))

    def finish_gate(direction, lf_out, k_out):
        def finish(cols, z):
            lb = lb_ref[direction:direction + 1, cols]
            sig = _sigmoid(z)
            lf_out[:, cols] = jnp.log(lb + (1.0 - lb) * sig)
            store_channel_major(k_out, cols, (1.0 - lb) * (1.0 - sig))
        return finish

    def finish_v(cols, u):
        v_out[:, cols] = u.astype(v_out.dtype)

    def finish_g(cols, u):
        g_out[:, cols] = (u * _sigmoid(u)).astype(g_out.dtype)

    sections = (finish_q, finish_gate(0, lf_fw, k_fw), finish_gate(1, lf_bw, k_bw),
                finish_v, finish_g)
    per_section = REC_WIDTH // slab
    n_slabs = len(sections) * per_section
    project = lambda i: _dot(h, w_ref[:, i * slab:(i + 1) * slab])
    v_section = sections.index(finish_v)
    order = ([i for i in range(n_slabs) if i // per_section != v_section]
             + [i for i in range(n_slabs) if i // per_section == v_section])
    u_next = project(order[0])
    for n, i in enumerate(order):
        u = u_next
        if n + 1 < n_slabs:
            u_next = project(order[n + 1])
        within = i % per_section
        sections[i // per_section](slice(within * slab, (within + 1) * slab), u)


def _rec_proj(x2, mods4, w_bf, lb, *, seq_len, mod_row0, per_seq_mod):
    n_tok = x2.shape[0]
    d = D_MODEL
    rows = REC_ROW_TILE
    steps_per_seq = max(1, seq_len // rows)
    seqs = max(1, rows // seq_len)
    assert rows % GLA_TILE == 0 and (seq_len % rows == 0 or rows % seq_len == 0)
    assert not per_seq_mod or seqs == 1

    def mod_map(i):
        row = mod_row0 + (i // steps_per_seq if per_seq_mod else 0)
        return (1, row, 0, 0)

    row_spec = pl.BlockSpec((rows, REC_WIDTH), lambda i: (i, 0))
    tile_spec = pl.BlockSpec((seqs, rows // seqs // GLA_TILE, REC_WIDTH, GLA_TILE),
                             lambda i: (i // steps_per_seq, i % steps_per_seq, 0, 0))
    full = lambda i: (0, 0)
    row_out = lambda t: (row_spec, jax.ShapeDtypeStruct((n_tok, REC_WIDTH), t))
    tile_out = (tile_spec, jax.ShapeDtypeStruct(
        (n_tok // seq_len, seq_len // GLA_TILE, REC_WIDTH, GLA_TILE), BF16))
    outs = (tile_out, row_out(BF16), row_out(BF16), row_out(F32), row_out(F32), tile_out, tile_out)
    return pl.pallas_call(
        _rec_proj_kernel,
        grid=(n_tok // rows,),
        in_specs=[
            pl.BlockSpec((rows, d), lambda i: (i, 0)),
            pl.BlockSpec((None, None, 1, 3 * d), mod_map),
            pl.BlockSpec(w_bf.shape, full),
            pl.BlockSpec(lb.shape, full),
        ],
        out_specs=tuple(spec for spec, _ in outs),
        out_shape=tuple(shape for _, shape in outs),
        compiler_params=pltpu.CompilerParams(vmem_limit_bytes=VMEM_LIMIT),
        name="rec_proj",
    )(x2, mods4, w_bf, lb)


def _block_diag(a, b):
    za = jnp.zeros(a.shape, a.dtype)
    return jnp.concatenate(
        [jnp.concatenate([a, za], axis=1), jnp.concatenate([za, b], axis=1)], axis=0)


def _pair_scores(a_t, c_t):
    return _dot_tn(a_t, _block_diag(c_t[:REC_DK, :], c_t[REC_DK:, :]))


def _gla_scores_stage(units):
    t = GLA_TILE
    for u in units:
        lf2 = u["lf"]() * LOG2_E
        hi = lf2.astype(BF16)
        lo = (lf2 - hi.astype(F32)).astype(BF16)
        u["sums"] = _dot_tn(jnp.concatenate([hi, lo], axis=0), u["sums_ref"][...])
        u["scores"] = u["masks_ref"][0] * _pair_scores(u["q_t"](), u["k_t"]()).astype(BF16)
    for li in range(len(GLA_LEVELS)):
        for u in units:
            x = jnp.exp2(u["sums"][:, (1 + li) * t:(2 + li) * t]).astype(BF16)
            z = _pair_scores(u["q_t"]() * x, u["k_t"]() * x).astype(BF16)
            u["scores"] = u["scores"] + u["masks_ref"][1 + li] * z
    for u in units:
        u["save"](u["scores"], u["sums"][:, :t])


def _gla_state_stage(u):
    t = GLA_TILE
    dk = REC_DK
    b_t, st_ref = u["load_b"](), u["st_ref"]
    q_t, k_t, v = u["q_t"](), u["k_t"](), u["v"]()
    edge = b_t[:, 0:1] if u["backward"] else b_t[:, t - 1:t]
    o = _dot(u["load_scores"](), _block_diag(v[:, :dk], v[:, dk:]))
    st_a, st_b = st_ref[0], st_ref[1]
    q_in = q_t * jnp.exp2(b_t).astype(BF16)
    o = o + _dot_tn(q_in, _block_diag(st_a.astype(BF16), st_b.astype(BF16)))
    k_edge = k_t * jnp.exp2(edge - b_t).astype(BF16)
    carry = jnp.exp2(edge)
    st_ref[0] = st_a * carry[:dk, :] + _dot(k_edge[:dk, :], v[:, :dk])
    st_ref[1] = st_b * carry[dk:, :] + _dot(k_edge[dk:, :], v[:, dk:])
    u["store"](o)


def _gla_kernel(*refs, n_tiles, has_state, want_state):
    refs = list(refs)
    q_ref, v_ref, lff_ref, lfb_ref, kf_ref, kb_ref = refs[:6]
    pos = 6
    if has_state:
        s0_ref = refs[pos]
        pos += 1
    sums_f_ref, sums_b_ref, mf_ref, mb_ref = refs[pos:pos + 4]
    pos += 4
    o_ref = refs[pos]
    pos += 1
    if want_state:
        s_out_ref = refs[pos]
        pos += 1
    st_ref, pipe_s, pipe_b = refs[pos:pos + 3]

    t = GLA_TILE
    n_seqs = v_ref.shape[0]
    n_items = n_seqs * n_tiles
    pair_lanes = 2 * REC_DK
    o_ref[...] = jnp.zeros(o_ref.shape, o_ref.dtype)
    if has_state:
        st_ref[...] = s0_ref[...]
    else:
        st_ref[...] = jnp.zeros(st_ref.shape, st_ref.dtype)

    def units_of(item):
        seq, step = item // n_tiles, item % n_tiles
        units = []
        for direction, (lf_ref, k_ref, sums_ref, m_ref) in enumerate(
                ((lff_ref, kf_ref, sums_f_ref, mf_ref), (lfb_ref, kb_ref, sums_b_ref, mb_ref))):
            tile = step if direction == 0 else n_tiles - 1 - step
            rows = pl.ds(pl.multiple_of(tile * t, t), t)
            for pair in range(GLA_HEADS_PER_STEP // 2):
                lanes = slice(pair * pair_lanes, (pair + 1) * pair_lanes)
                ui = len(units)

                def store(o, rows=rows, lanes=lanes):
                    o_ref[seq, rows, lanes] += o

                def save(scores, b_t, ui=ui):
                    pipe_s[ui] = scores
                    pipe_b[ui] = b_t

                rows_of = lambda ref, rows=rows, lanes=lanes: (lambda: ref[seq, rows, lanes])
                channels_of = lambda ref, tile=tile, lanes=lanes: (lambda: ref[seq, tile, lanes, :])
                units.append(dict(
                    q_t=channels_of(q_ref), k_t=channels_of(k_ref), v=rows_of(v_ref),
                    lf=rows_of(lf_ref), st_ref=st_ref.at[seq, direction, pl.ds(2 * pair, 2)],
                    sums_ref=sums_ref, masks_ref=m_ref, backward=direction == 1, store=store,
                    save=save, load_scores=lambda ui=ui: pipe_s[ui], load_b=lambda ui=ui: pipe_b[ui]))
        return units

    def body(item, carry):
        units = units_of(item)
        _gla_scores_stage(units)
        for u in units:
            _gla_state_stage(u)
        return carry

    lax.fori_loop(0, n_items, body, 0)
    if want_state:
        s_out_ref[...] = st_ref[...]


def _gla(q, v, lf_fw, lf_bw, k_fw, k_bw, s0, consts, *, n_seq, seq_len, want_state):
    has_state = s0 is not None
    width = REC_WIDTH
    hps = GLA_HEADS_PER_STEP
    n_tiles = seq_len // GLA_TILE
    seqs = min(n_seq, max(1, GLA_ITEMS_PER_STEP // n_tiles))
    assert n_seq % seqs == 0
    n_units = hps
    seq3 = lambda a: a.reshape(n_seq, seq_len, width)
    head_spec = pl.BlockSpec((seqs, seq_len, hps * REC_DK), lambda b, h: (b, 0, h))
    tile_spec = pl.BlockSpec((seqs, n_tiles, hps * REC_DK, GLA_TILE), lambda b, h: (b, 0, h, 0))
    state_spec = pl.BlockSpec((seqs, 2, hps, REC_DK, REC_DV), lambda b, h: (b, 0, h, 0, 0))
    in_specs = [tile_spec, head_spec, head_spec, head_spec, tile_spec, tile_spec]
    args = [q, seq3(v), seq3(lf_fw), seq3(lf_bw), k_fw, k_bw]
    if has_state:
        in_specs.append(state_spec)
        args.append(s0)
    for c in consts:
        in_specs.append(pl.BlockSpec(c.shape, lambda b, h, nd=c.ndim: (0,) * nd))
        args.append(c)
    out_shape = [jax.ShapeDtypeStruct((n_seq, seq_len, width), F32)]
    out_specs = [head_spec]
    if want_state:
        out_shape.append(jax.ShapeDtypeStruct((n_seq, 2, N_REC_HEADS, REC_DK, REC_DV), F32))
        out_specs.append(state_spec)
    res = pl.pallas_call(
        functools.partial(_gla_kernel, n_tiles=n_tiles, has_state=has_state, want_state=want_state),
        grid=(n_seq // seqs, N_REC_HEADS // hps),
        in_specs=in_specs,
        out_specs=tuple(out_specs),
        out_shape=tuple(out_shape),
        scratch_shapes=[
            pltpu.VMEM((seqs, 2, hps, REC_DK, REC_DV), F32),
            pltpu.VMEM((n_units, GLA_TILE, 2 * GLA_TILE), BF16),
            pltpu.VMEM((n_units, 2 * REC_DK, GLA_TILE), F32),
        ],
        compiler_params=pltpu.CompilerParams(vmem_limit_bytes=VMEM_LIMIT),
        name="gla_state_in" if has_state else "gla_state_out",
    )(*args)
    o = res[0].reshape(n_seq * seq_len, width)
    return (o, res[1]) if want_state else (o, None)


def _rec_out_kernel(o_ref, sg_ref, x_ref, mod_ref, ng_ref, w_ref, lng_ref, lnb_ref, out_ref):
    d = D_MODEL
    gate = mod_ref[...][:, 2 * d:]

    def branch_of(rows):
        parts = []
        for hd in range(N_REC_HEADS):
            oh = o_ref[rows, hd * REC_DV:(hd + 1) * REC_DV]
            ms = jnp.mean(oh * oh, axis=-1, keepdims=True)
            parts.append(oh * lax.rsqrt(ms + NORM_EPS) * ng_ref[...])
        o = jnp.concatenate(parts, axis=1)
        gated = (o * sg_ref[rows, :].astype(F32)).astype(BF16)
        return _dot(gated, w_ref[...])

    chunks = [pl.ds(c * OUT_ROW_CHUNK, OUT_ROW_CHUNK) for c in range(o_ref.shape[0] // OUT_ROW_CHUNK)]
    branches = [branch_of(rows) for rows in chunks]
    for rows, branch in zip(chunks, branches):
        y = DEEPNORM_ALPHA * x_ref[rows, :] + gate * branch
        out_ref[rows, :] = _layer_norm(y, lng_ref[...], lnb_ref[...])


def _rec_out(o, sg, x2, mods4, norm_gain, w_out_bf, ln_g, ln_b, *, seq_len, mod_row0, per_seq_mod):
    n_tok = x2.shape[0]
    d = D_MODEL
    assert not per_seq_mod or seq_len % OUT_ROW_TILE == 0
    tiles_per_seq = max(1, seq_len // OUT_ROW_TILE)

    def mod_map(i):
        row = mod_row0 + (i // tiles_per_seq if per_seq_mod else 0)
        return (1, row, 0, 0)

    row_spec = pl.BlockSpec((OUT_ROW_TILE, d), lambda i: (i, 0))
    full = lambda i: (0, 0)
    return pl.pallas_call(
        _rec_out_kernel,
        grid=(n_tok // OUT_ROW_TILE,),
        in_specs=[
            row_spec, row_spec, row_spec,
            pl.BlockSpec((None, None, 1, 3 * d), mod_map),
            pl.BlockSpec(norm_gain.shape, full),
            pl.BlockSpec(w_out_bf.shape, full),
            pl.BlockSpec(ln_g.shape, full),
            pl.BlockSpec(ln_b.shape, full),
        ],
        out_specs=row_spec,
        out_shape=jax.ShapeDtypeStruct((n_tok, d), F32),
        compiler_params=pltpu.CompilerParams(vmem_limit_bytes=VMEM_LIMIT),
        name="rec_out",
    )(o, sg, x2, mods4, norm_gain, w_out_bf, ln_g, ln_b)


def _rope_tables(n_tokens):
    n_rows = n_tokens // GRID_W
    rows = jnp.repeat(jnp.arange(n_rows, dtype=F32), GRID_W)
    cols = jnp.tile(jnp.arange(GRID_W, dtype=F32), n_rows)
    inv_freq = 1.0 / (ROPE_THETA ** (jnp.arange(0, AXIS_DIM, 2, dtype=F32) / AXIS_DIM))
    ang_r = rows[:, None] * inv_freq[None, :]
    ang_c = cols[:, None] * inv_freq[None, :]
    ang = jnp.concatenate([ang_r, ang_r, ang_c, ang_c], axis=-1)
    cos, sin = jnp.cos(ang), jnp.sin(ang)
    first = (jnp.arange(HEAD_DIM) % AXIS_DIM) < AXIS_DIM // 2
    sin_a = jnp.where(first[None, :], -sin, 0.0)
    sin_b = jnp.where(first[None, :], 0.0, sin)
    reps = GROUP_LANES // HEAD_DIM
    return tuple(jnp.tile(t, (1, reps)) for t in (cos, sin_a, sin_b))


def _head_mean_matrix():
    idx = jnp.arange(GROUP_LANES) // HEAD_DIM
    return jnp.where(idx[:, None] == idx[None, :], 1.0 / HEAD_DIM, 0.0).astype(BF16)


def _gla_consts():
    t = GLA_TILE
    r = jnp.arange(t)[:, None]
    c = jnp.arange(t)[None, :]
    fw = [c <= r]
    bw = [c >= r]
    for half in GLA_LEVELS:
        start = 2 * half * (r // (2 * half))
        upper = r - start >= half
        last_low, first_up = start + half - 1, start + half
        fw.append(jnp.where(upper, (c > last_low) & (c <= r), (c > r) & (c <= last_low)))
        bw.append(jnp.where(upper, (c >= first_up) & (c < r), (c >= r) & (c < first_up)))
    twice = lambda a: jnp.concatenate([a, a], axis=-1)

    def summation(weights):
        cols = jnp.concatenate([w.T for w in weights], axis=1).astype(BF16)
        return jnp.concatenate([cols, cols], axis=0)

    sums_fw, sums_bw = summation(fw), summation(bw)
    masks = [(r == c)]
    for half in GLA_LEVELS:
        size = 2 * half
        masks.append((r // size == c // size) & (r % size >= half) & (c % size < half))
    m_fw = jnp.stack(masks).astype(BF16)
    m_bw = jnp.swapaxes(m_fw, 1, 2)
    return sums_fw, sums_bw, twice(m_fw), twice(m_bw)


def kernel(x_prompt, x_sample, cache_k, cache_v, state_rec, c, c_ctx, ada_w, ada_b, attn_w_in,
           attn_q_gain, attn_k_gain, attn_w_out, rec_w_in, rec_lower_bounds, rec_norm_gain,
           rec_w_out, ln_gain, ln_bias):
    d = D_MODEL
    n_p, len_p, _ = x_prompt.shape
    n_s, len_s, _ = x_sample.shape
    past = cache_k.shape[2]
    assert past % SLAB_LANES == 0 and len_p % SLAB_LANES == 0 and len_s % SLAB_LANES == 0

    cond = jnp.zeros((COND_ROWS, d), F32).at[0].set(c_ctx).at[1:1 + n_s].set(c)
    mods4 = _mods(cond, ada_w, ada_b).reshape(DEPTH, COND_ROWS, 1, 3 * d)
    lb_all = _lower_bounds(rec_lower_bounds)

    xp = x_prompt.reshape(n_p * len_p, d)
    xs = x_sample.reshape(n_s * len_s, d)

    w_in = attn_w_in[0].astype(BF16)
    w_out = attn_w_out[0].astype(BF16)
    reps = GROUP_LANES // HEAD_DIM
    qg = (jnp.tile(attn_q_gain[0], reps) * (LOG2_E / math.sqrt(HEAD_DIM))).reshape(1, GROUP_LANES)
    kg = jnp.tile(attn_k_gain[0], reps).reshape(1, GROUP_LANES)
    pn = _head_mean_matrix()
    ln_g = ln_gain[0].reshape(1, d)
    ln_b = ln_bias[0].reshape(1, d)

    xp1, k_p, v_p = _attn_seq(xp, mods4, w_in, pn, qg, kg, w_out, ln_g, ln_b, n_seq=n_p,
                              seq_len=len_p, mod_row=0)
    q_s, kp_s, vt_s, g_s = _attn_proj(xs, mods4, w_in, pn, qg, kg, _rope_tables(len_s), seq_len=len_s,
                                      mod_row0=1)
    cache = (cache_k[:, 0].reshape(n_s, past, KV_WIDTH), cache_v[:, 0].reshape(n_s, past, KV_WIDTH))
    xs1 = _attn_core(q_s, kp_s, vt_s, cache, g_s, xs, mods4, w_out, ln_g, ln_b, n_seq=n_s,
                     seq_len=len_s, q_tile=ATTN_Q_TILE, mod_row0=1)
    new_cache_k = k_p.reshape(n_p, 1, len_p, N_KV_HEADS, HEAD_DIM)
    new_cache_v = v_p.reshape(n_p, 1, len_p, N_KV_HEADS, HEAD_DIM)

    rw_in = rec_w_in[0].astype(BF16)
    rw_out = rec_w_out[0].astype(BF16)
    lb = lb_all[1]
    ng = rec_norm_gain[0].reshape(1, REC_DV)
    ln_g = ln_gain[1].reshape(1, d)
    ln_b = ln_bias[1].reshape(1, d)
    consts = _gla_consts()

    outs = []
    states = None
    for x1, n_seq, seq_len, row0, per_seq, s0 in (
            (xp1, n_p, len_p, 0, False, None),
            (xs1, n_s, len_s, 1, True, state_rec[:, 0])):
        q, v, g, lf_fw, lf_bw, k_fw, k_bw = _rec_proj(x1, mods4, rw_in, lb, seq_len=seq_len,
                                                       mod_row0=row0, per_seq_mod=per_seq)
        o, st = _gla(q, v, lf_fw, lf_bw, k_fw, k_bw, s0, consts, n_seq=n_seq, seq_len=seq_len,
                     want_state=s0 is None)
        if st is not None:
            states = st
        outs.append(_rec_out(o, g, x1, mods4, ng, rw_out, ln_g, ln_b, seq_len=seq_len,
                             mod_row0=row0, per_seq_mod=per_seq))

    y_prompt = outs[0].reshape(n_p, len_p, d)
    y_sample = outs[1].reshape(n_s, len_s, d)
    new_state_rec = states.reshape(n_p, 1, 2, N_REC_HEADS, REC_DK, REC_DV)
    return (y_prompt, y_sample, new_cache_k, new_cache_v, new_state_rec)
```

```python
import functools
import math

import jax
import jax.numpy as jnp
from jax import lax
from jax.experimental import pallas as pl
from jax.experimental.pallas import tpu as pltpu

F32 = jnp.float32
BF16 = jnp.bfloat16

D_MODEL = 1024
DEPTH = 2
GRID_W = 64
N_HEADS = 16
N_KV_HEADS = 4
HEAD_DIM = 64
AXIS_DIM = HEAD_DIM // 2
ATTN_WIDTH = N_HEADS * HEAD_DIM
KV_WIDTH = N_KV_HEADS * HEAD_DIM
ROPE_THETA = 10000.0
N_REC_HEADS = 8
REC_DK = 128
REC_DV = 128
REC_WIDTH = N_REC_HEADS * REC_DK
NORM_EPS = 1e-6
LN_EPS = 1e-5
DEEPNORM_ALPHA = (2.0 * DEPTH) ** 0.25

SUBLANES = 8
BF16_ROWS = 16
COND_ROWS = SUBLANES
ATTN_Q_TILE = 256
LATENT_TILES_PER_STEP = 2
ATTN_PROJ_ROW_TILE = 512
REC_ROW_TILE = 512
OUT_ROW_TILE = 512
OUT_ROW_CHUNK = 256
ATTN_SEQS_PER_STEP = 2
GROUP_LANES = 256
SLAB_LANES = 128
KPAD_WIDTH = 2 * N_KV_HEADS * SLAB_LANES
REC_PROJ_SLAB = 256
GLA_TILE = 128
GLA_LEVELS = (1, 2, 4, 8, 16, 32, 64)
GLA_HEADS_PER_STEP = 4
GLA_ITEMS_PER_STEP = 8
LOG2_E = 1.4426950408889634
VMEM_LIMIT = 56 * 1024 * 1024


def _sigmoid(x):
    return 1.0 / (1.0 + jnp.exp(-x))


def _dot(a, b):
    return jnp.dot(a, b, preferred_element_type=F32)


def _dot_nt(a, b):
    return lax.dot_general(a, b, (((1,), (1,)), ((), ())), preferred_element_type=F32)


def _dot_tn(a, b):
    return lax.dot_general(a, b, (((0,), (0,)), ((), ())), preferred_element_type=F32)


def _layer_norm(y, g, b):
    mu = jnp.mean(y, axis=-1, keepdims=True)
    yc = y - mu
    var = jnp.mean(yc * yc, axis=-1, keepdims=True)
    return yc * lax.rsqrt(var + LN_EPS) * g + b


def _mods_kernel(cond_ref, w_ref, b_ref, out_ref):
    c = cond_ref[...]
    s = (c * _sigmoid(c)).astype(BF16)
    out_ref[...] = _dot(s, w_ref[...].astype(BF16)) + b_ref[...]


def _mods(cond, ada_w, ada_b):
    d = D_MODEL
    return pl.pallas_call(
        _mods_kernel,
        grid=(DEPTH, 3),
        in_specs=[
            pl.BlockSpec((COND_ROWS, d), lambda l, j: (0, 0)),
            pl.BlockSpec((None, d, d), lambda l, j: (l, 0, j)),
            pl.BlockSpec((None, 1, d), lambda l, j: (l, 0, j)),
        ],
        out_specs=pl.BlockSpec((None, COND_ROWS, d), lambda l, j: (l, 0, j)),
        out_shape=jax.ShapeDtypeStruct((DEPTH, COND_ROWS, 3 * d), F32),
        compiler_params=pltpu.CompilerParams(vmem_limit_bytes=VMEM_LIMIT),
        name="adaln_mods",
    )(cond, ada_w, ada_b.reshape(DEPTH, 1, 3 * d))


def _lower_bounds_kernel(r_ref, out_ref):
    r = [r_ref[i] for i in range(DEPTH)]
    m = functools.reduce(jnp.maximum, r)
    e = [jnp.exp(x - m) for x in r]
    tot = functools.reduce(lambda a, b: a + b, e)
    soft = [x / tot for x in e]
    acc = soft[0]
    for i in range(DEPTH):
        if i > 0:
            acc = acc + soft[i]
        out_ref[i] = acc - soft[0]


def _lower_bounds(rec_lower_bounds):
    return pl.pallas_call(
        _lower_bounds_kernel,
        out_shape=jax.ShapeDtypeStruct(rec_lower_bounds.shape, F32),
        name="rec_lower_bounds",
    )(rec_lower_bounds)


def _kv_head_slabs(k):
    lane = lax.broadcasted_iota(jnp.int32, (k.shape[0], SLAB_LANES), 1)
    low = lane < HEAD_DIM
    heads_per_slab = SLAB_LANES // HEAD_DIM
    out = []
    for j in range(N_KV_HEADS):
        tile = k[:, (j // heads_per_slab) * SLAB_LANES:(j // heads_per_slab + 1) * SLAB_LANES]
        moved = pltpu.roll(tile, HEAD_DIM, 1)
        at_low, at_high = (tile, moved) if j % heads_per_slab == 0 else (moved, tile)
        out.append(jnp.where(low, at_low, 0.0))
        out.append(jnp.where(low, 0.0, at_high))
    return jnp.concatenate(out, axis=1)


def _head_norm(u, pn, gain):
    ms = _dot((u * u).astype(BF16), pn)
    return u * lax.rsqrt(ms + NORM_EPS) * gain


def _attn_proj_kernel(x_ref, mod_ref, w_ref, pn_ref, qg_ref, kg_ref, cos_ref, sa_ref, sb_ref,
                      q_out, kp_out, vt_out, g_out):
    d = D_MODEL
    mod = mod_ref[...]
    shift, scale = mod[:, :d], mod[:, d:2 * d]
    h = (x_ref[...] * (1.0 + scale) + shift).astype(BF16)
    pn = pn_ref[...]

    def norm_rope(u, gain):
        y = _head_norm(u, pn, gain)
        return (y * cos_ref[...]
                + pltpu.roll(y, GROUP_LANES - AXIS_DIM // 2, 1) * sa_ref[...]
                + pltpu.roll(y, AXIS_DIM // 2, 1) * sb_ref[...])

    def finish_q(j, u):
        q_out[:, j * GROUP_LANES:(j + 1) * GROUP_LANES] = norm_rope(u, qg_ref[...]).astype(q_out.dtype)

    def finish_k(_, u):
        kp_out[...] = _kv_head_slabs(norm_rope(u, kg_ref[...])).astype(kp_out.dtype)

    def finish_v(_, v):
        vt_out[...] = v.T.astype(vt_out.dtype)

    def finish_g(j, u):
        g_out[:, j * GROUP_LANES:(j + 1) * GROUP_LANES] = (u * _sigmoid(u)).astype(g_out.dtype)

    slabs = ([(finish_q, j) for j in range(N_KV_HEADS)] + [(finish_k, 0), (finish_v, 0)]
             + [(finish_g, j) for j in range(N_KV_HEADS)])
    project = lambda i: _dot(h, w_ref[:, i * GROUP_LANES:(i + 1) * GROUP_LANES])
    v_slab = N_KV_HEADS + 1
    order = [i for i in range(len(slabs)) if i != v_slab] + [v_slab]
    u_next = project(order[0])
    for n, i in enumerate(order):
        u = u_next
        if n + 1 < len(order):
            u_next = project(order[n + 1])
        finish, j = slabs[i]
        finish(j, u)


def _attn_proj(x2, mods4, w_bf, pn, qg, kg, rope_tabs, *, seq_len, mod_row0):
    n_tok = x2.shape[0]
    d = D_MODEL
    tile = ATTN_PROJ_ROW_TILE
    assert seq_len % tile == 0
    tiles_per_seq = seq_len // tile
    full = lambda i: (0, 0)
    rows = lambda width: pl.BlockSpec((tile, width), lambda i: (i, 0))
    table = pl.BlockSpec((tile, GROUP_LANES), lambda i: (i % tiles_per_seq, 0))
    consts = [w_bf, pn, qg, kg]
    return pl.pallas_call(
        _attn_proj_kernel,
        grid=(n_tok // tile,),
        in_specs=[rows(d),
                  pl.BlockSpec((None, None, 1, 3 * d), lambda i: (0, mod_row0 + i // tiles_per_seq, 0, 0))]
        + [pl.BlockSpec(c.shape, full) for c in consts] + [table] * len(rope_tabs),
        out_specs=(
            rows(ATTN_WIDTH),
            rows(KPAD_WIDTH),
            pl.BlockSpec((None, KV_WIDTH, tile), lambda i: (i // tiles_per_seq, 0, i % tiles_per_seq)),
            rows(ATTN_WIDTH),
        ),
        out_shape=(
            jax.ShapeDtypeStruct((n_tok, ATTN_WIDTH), BF16),
            jax.ShapeDtypeStruct((n_tok, KPAD_WIDTH), BF16),
            jax.ShapeDtypeStruct((n_tok // seq_len, KV_WIDTH, seq_len), BF16),
            jax.ShapeDtypeStruct((n_tok, ATTN_WIDTH), BF16),
        ),
        compiler_params=pltpu.CompilerParams(vmem_limit_bytes=VMEM_LIMIT),
        name="attn_proj_rope",
    )(x2, mods4, *consts, *rope_tabs)


def _attention_phases(q_ref, kp_ref, vt_ref, cache, s_scr, p_scr, ot_scr, between=()):
    n_new = kp_ref.shape[0]
    n_old = cache[0].shape[0] if cache is not None else 0
    n_keys = n_old + n_new
    group = N_HEADS // N_KV_HEADS
    heads_per_slab = SLAB_LANES // HEAD_DIM
    pending_work = list(between)

    def fold_rows(a, op):
        return functools.reduce(
            op, [a[r * SUBLANES:(r + 1) * SUBLANES, :] for r in range(a.shape[0] // SUBLANES)])

    def scores_phase(j):
        slot = j % 2
        maxima = []
        for hh in range(group):
            hd = j * group + hh
            slab, where = hd // heads_per_slab, hd % heads_per_slab
            q_slab = q_ref[:, slab * SLAB_LANES:(slab + 1) * SLAB_LANES]
            k_slab = heads_per_slab * j + where
            lanes = slice(k_slab * SLAB_LANES, (k_slab + 1) * SLAB_LANES)
            s = _dot_nt(kp_ref[:, lanes], q_slab)
            s_scr[slot, hh, n_old:, :] = s
            mx = fold_rows(s, jnp.maximum)
            if cache is not None:
                s = _dot_nt(cache[0][:, lanes], q_slab)
                s_scr[slot, hh, :n_old, :] = s
                mx = jnp.maximum(mx, fold_rows(s, jnp.maximum))
            maxima.append(jnp.max(mx, axis=0, keepdims=True))
        return maxima

    def values_phase(j, maxima):
        slot = j % 2
        rows = slice(j * HEAD_DIM, (j + 1) * HEAD_DIM)
        v_t = vt_ref[rows, :]
        if cache is not None:
            v_t = jnp.concatenate([cache[1][rows, :], v_t], axis=1)
        v_ext = jnp.concatenate([v_t, jnp.ones((BF16_ROWS, n_keys), BF16)], axis=0)
        for hh in range(group):
            hd = j * group + hh
            p_scr[hh % 2] = jnp.exp2(s_scr[slot, hh] - maxima[hh]).astype(BF16)
            acc = _dot(v_ext, p_scr[hh % 2])
            l = acc[HEAD_DIM:HEAD_DIM + 1, :]
            ot_scr[hd * HEAD_DIM:(hd + 1) * HEAD_DIM, :] = acc[:HEAD_DIM, :] * (1.0 / l)

    pending = scores_phase(0)
    for j in range(N_KV_HEADS):
        nxt = scores_phase(j + 1) if j + 1 < N_KV_HEADS else None
        if pending_work:
            pending_work.pop(0)()
        values_phase(j, pending)
        pending = nxt
    for thunk in pending_work:
        thunk()


def _branch_epilogue(ot_scr, sg, x_ref, mod_ref, w_ref, lng_ref, lnb_ref, out_ref):
    d = D_MODEL
    o = ot_scr[...].T
    gated = (o * sg.astype(F32)).astype(BF16)
    branch = _dot(gated, w_ref[...])
    gate = mod_ref[...][:, 2 * d:]
    y = DEEPNORM_ALPHA * x_ref[...] + gate * branch
    out_ref[...] = _layer_norm(y, lng_ref[...], lnb_ref[...])


def _attn_core_kernel(q_ref, kp_ref, vt_ref, kc_ref, vc_ref, sg_ref, x_ref, mod_ref, w_ref,
                      lng_ref, lnb_ref, out_ref, s_scr, p_scr, ot_scr):
    cache = (_kv_head_slabs(kc_ref[...]).astype(BF16), vc_ref[...].T.astype(BF16))
    tq = ot_scr.shape[2]

    def epilogue(t):
        rows = pl.ds(t * tq, tq)
        _branch_epilogue(ot_scr.at[t], sg_ref[rows, :], x_ref.at[rows], mod_ref, w_ref, lng_ref,
                         lnb_ref, out_ref.at[rows])

    n_sub = ot_scr.shape[0]
    for t in range(n_sub):
        between = [functools.partial(epilogue, t - 1)] if t > 0 else []
        _attention_phases(q_ref.at[pl.ds(t * tq, tq)], kp_ref, vt_ref, cache, s_scr, p_scr,
                          ot_scr.at[t], between=between)
    epilogue(n_sub - 1)


def _attn_seq_kernel(x_ref, mod_ref, w_in_ref, pn_ref, qg_ref, kg_ref, w_out_ref, lng_ref, lnb_ref,
                     out_ref, k_out, v_out,
                     h_scr, q_scr, kp_scr, vt_scr, sg_scr, s_scr, p_scr, ot_scr):
    d = D_MODEL
    n_sub = x_ref.shape[0]
    mod = mod_ref[...]
    shift, scale = mod[:, :d], mod[:, d:2 * d]
    pn = pn_ref[...]
    project = lambda i: _dot(h_scr[...], w_in_ref[:, i * GROUP_LANES:(i + 1) * GROUP_LANES])

    def epilogue(s):
        _branch_epilogue(ot_scr.at[s], sg_scr[s], x_ref.at[s], mod_ref, w_out_ref, lng_ref, lnb_ref,
                         out_ref.at[s])

    for s in range(n_sub):
        h_scr[...] = (x_ref[s] * (1.0 + scale) + shift).astype(BF16)

        def finish_q(j, u):
            q_scr[:, j * GROUP_LANES:(j + 1) * GROUP_LANES] = _head_norm(u, pn, qg_ref[...]).astype(BF16)

        def finish_k(_, u, s=s):
            k = _head_norm(u, pn, kg_ref[...])
            kp_scr[...] = _kv_head_slabs(k).astype(BF16)
            k_out[s] = k.reshape(k_out.shape[1:])

        def finish_v(_, v, s=s):
            vt_scr[...] = v.T.astype(BF16)
            v_out[s] = v.reshape(v_out.shape[1:])

        def finish_g(j, u, s=s):
            sg_scr[s, :, j * GROUP_LANES:(j + 1) * GROUP_LANES] = (u * _sigmoid(u)).astype(BF16)

        slabs = [(finish_q, j) for j in range(N_KV_HEADS)] + [(finish_k, 0), (finish_v, 0)]
        u_next = project(0)
        if s > 0:
            epilogue(s - 1)
        for i, (finish, j) in enumerate(slabs):
            u = u_next
            if i + 1 < len(slabs):
                u_next = project(i + 1)
            finish(j, u)
        gate_slabs = [functools.partial(lambda j, g: g(j, project(len(slabs) + j)), j, finish_g)
                      for j in range(N_KV_HEADS)]
        _attention_phases(q_scr, kp_scr, vt_scr, None, s_scr, p_scr, ot_scr.at[s], between=gate_slabs)
    epilogue(n_sub - 1)


def _attn_seq(x2, mods4, w_in_bf, pn, qg, kg, w_out_bf, ln_g, ln_b, *, n_seq, seq_len, mod_row):
    d = D_MODEL
    n_sub = ATTN_SEQS_PER_STEP
    assert n_seq % n_sub == 0
    full = lambda b: (0, 0)
    rows = pl.BlockSpec((n_sub, seq_len, d), lambda b: (b, 0, 0))
    heads = pl.BlockSpec((n_sub, seq_len, N_KV_HEADS, HEAD_DIM), lambda b: (b, 0, 0, 0))
    consts = [w_in_bf, pn, qg, kg, w_out_bf, ln_g, ln_b]
    group = N_HEADS // N_KV_HEADS
    out, k, v = pl.pallas_call(
        _attn_seq_kernel,
        grid=(n_seq // n_sub,),
        in_specs=[rows, pl.BlockSpec((None, None, 1, 3 * d), lambda b: (0, mod_row, 0, 0))]
        + [pl.BlockSpec(c.shape, full) for c in consts],
        out_specs=(rows, heads, heads),
        out_shape=(jax.ShapeDtypeStruct((n_seq, seq_len, d), F32),
                   jax.ShapeDtypeStruct((n_seq, seq_len, N_KV_HEADS, HEAD_DIM), F32),
                   jax.ShapeDtypeStruct((n_seq, seq_len, N_KV_HEADS, HEAD_DIM), F32)),
        scratch_shapes=[
            pltpu.VMEM((seq_len, d), BF16),
            pltpu.VMEM((seq_len, ATTN_WIDTH), BF16),
            pltpu.VMEM((seq_len, KPAD_WIDTH), BF16),
            pltpu.VMEM((KV_WIDTH, seq_len), BF16),
            pltpu.VMEM((n_sub, seq_len, ATTN_WIDTH), BF16),
            pltpu.VMEM((2, group, seq_len, seq_len), F32),
            pltpu.VMEM((2, seq_len, seq_len), BF16),
            pltpu.VMEM((n_sub, ATTN_WIDTH, seq_len), F32),
        ],
        compiler_params=pltpu.CompilerParams(vmem_limit_bytes=VMEM_LIMIT),
        name="attn_seq",
    )(x2.reshape(n_seq, seq_len, d), mods4, *consts)
    return out.reshape(n_seq * seq_len, d), k, v


def _attn_core(q, kp, vt, cache, sg, x2, mods4, w_out_bf, ln_g, ln_b, *, n_seq, seq_len, q_tile,
               mod_row0):
    d = D_MODEL
    n_sub = LATENT_TILES_PER_STEP
    rows = n_sub * q_tile
    tiles = seq_len // rows
    ck, cv = cache
    n_keys = seq_len + ck.shape[1]
    q3 = q.reshape(n_seq, seq_len, ATTN_WIDTH)
    kp3 = kp.reshape(n_seq, seq_len, KPAD_WIDTH)
    sg3 = sg.reshape(n_seq, seq_len, ATTN_WIDTH)
    x3 = x2.reshape(n_seq, seq_len, d)
    tile_spec = lambda width: pl.BlockSpec((None, rows, width), lambda b, i: (b, i, 0))
    seq_spec = lambda length, width: pl.BlockSpec((None, length, width), lambda b, i: (b, 0, 0))
    full2 = lambda b, i: (0, 0)
    out = pl.pallas_call(
        _attn_core_kernel,
        grid=(n_seq, tiles),
        in_specs=[
            tile_spec(ATTN_WIDTH), seq_spec(seq_len, KPAD_WIDTH), seq_spec(KV_WIDTH, seq_len),
            seq_spec(ck.shape[1], KV_WIDTH), seq_spec(cv.shape[1], KV_WIDTH),
            tile_spec(ATTN_WIDTH), tile_spec(d),
            pl.BlockSpec((None, None, 1, 3 * d), lambda b, i: (0, mod_row0 + b, 0, 0)),
            pl.BlockSpec(w_out_bf.shape, full2),
            pl.BlockSpec(ln_g.shape, full2),
            pl.BlockSpec(ln_b.shape, full2),
        ],
        out_specs=tile_spec(d),
        out_shape=jax.ShapeDtypeStruct((n_seq, seq_len, d), F32),
        scratch_shapes=[
            pltpu.VMEM((2, N_HEADS // N_KV_HEADS, n_keys, q_tile), F32),
            pltpu.VMEM((2, n_keys, q_tile), BF16),
            pltpu.VMEM((n_sub, ATTN_WIDTH, q_tile), F32),
        ],
        compiler_params=pltpu.CompilerParams(vmem_limit_bytes=VMEM_LIMIT),
        name="attn_core_cache",
    )(q3, kp3, vt, ck, cv, sg3, x3, mods4, w_out_bf, ln_g, ln_b)
    return out.reshape(n_seq * seq_len, d)


def _rec_proj_kernel(x_ref, mod_ref, w_ref, lb_ref, q_out, v_out, g_out, lf_fw, lf_bw, k_fw, k_bw):
    d = D_MODEL
    mod = mod_ref[...]
    shift, scale = mod[:, :d], mod[:, d:2 * d]
    h = (x_ref[...] * (1.0 + scale) + shift).astype(BF16)
    slab = REC_PROJ_SLAB

    def store_channel_major(out_ref, cols, val):
        val_t = val.T.astype(out_ref.dtype)
        tiles = out_ref.shape[1]
        for i in range(val.shape[0] // GLA_TILE):
            out_ref[i // tiles, i % tiles, cols, :] = val_t[:, i * GLA_TILE:(i + 1) * GLA_TILE]

    def finish_q(cols, u):
        store_channel_major(q_out, cols, u * _sigmoid(u))

    def finish_gate(direction, lf_out, k_out):
        def finish(cols, z):
            lb = lb_ref[direction:direction + 1, cols]
            sig = _sigmoid(z)
            lf_out[:, cols] = jnp.log(lb + (1.0 - lb) * sig)
            store_channel_major(k_out, cols, (1.0 - lb) * (1.0 - sig))
        return finish

    def finish_v(cols, u):
        v_out[:, cols] = u.astype(v_out.dtype)

    def finish_g(cols, u):
        g_out[:, cols] = (u * _sigmoid(u)).astype(g_out.dtype)

    sections = (finish_q, finish_gate(0, lf_fw, k_fw), finish_gate(1, lf_bw, k_bw),
                finish_v, finish_g)
    per_section = REC_WIDTH // slab
    n_slabs = len(sections) * per_section
    project = lambda i: _dot(h, w_ref[:, i * slab:(i + 1) * slab])
    v_section = sections.index(finish_v)
    order = ([i for i in range(n_slabs) if i // per_section != v_section]
             + [i for i in range(n_slabs) if i // per_section == v_section])
    u_next = project(order[0])
    for n, i in enumerate(order):
        u = u_next
        if n + 1 < n_slabs:
            u_next = project(order[n + 1])
        within = i % per_section
        sections[i // per_section](slice(within * slab, (within + 1) * slab), u)


def _rec_proj(x2, mods4, w_bf, lb, *, seq_len, mod_row0, per_seq_mod):
    n_tok = x2.shape[0]
    d = D_MODEL
    rows = REC_ROW_TILE
    steps_per_seq = max(1, seq_len // rows)
    seqs = max(1, rows // seq_len)
    assert rows % GLA_TILE == 0 and (seq_len % rows == 0 or rows % seq_len == 0)
    assert not per_seq_mod or seqs == 1

    def mod_map(i):
        row = mod_row0 + (i // steps_per_seq if per_seq_mod else 0)
        return (1, row, 0, 0)

    row_spec = pl.BlockSpec((rows, REC_WIDTH), lambda i: (i, 0))
    tile_spec = pl.BlockSpec((seqs, rows // seqs // GLA_TILE, REC_WIDTH, GLA_TILE),
                             lambda i: (i // steps_per_seq, i % steps_per_seq, 0, 0))
    full = lambda i: (0, 0)
    row_out = lambda t: (row_spec, jax.ShapeDtypeStruct((n_tok, REC_WIDTH), t))
    tile_out = (tile_spec, jax.ShapeDtypeStruct(
        (n_tok // seq_len, seq_len // GLA_TILE, REC_WIDTH, GLA_TILE), BF16))
    outs = (tile_out, row_out(BF16), row_out(BF16), row_out(F32), row_out(F32), tile_out, tile_out)
    return pl.pallas_call(
        _rec_proj_kernel,
        grid=(n_tok // rows,),
        in_specs=[
            pl.BlockSpec((rows, d), lambda i: (i, 0)),
            pl.BlockSpec((None, None, 1, 3 * d), mod_map),
            pl.BlockSpec(w_bf.shape, full),
            pl.BlockSpec(lb.shape, full),
        ],
        out_specs=tuple(spec for spec, _ in outs),
        out_shape=tuple(shape for _, shape in outs),
        compiler_params=pltpu.CompilerParams(vmem_limit_bytes=VMEM_LIMIT),
        name="rec_proj",
    )(x2, mods4, w_bf, lb)


def _block_diag(a, b):
    za = jnp.zeros(a.shape, a.dtype)
    return jnp.concatenate(
        [jnp.concatenate([a, za], axis=1), jnp.concatenate([za, b], axis=1)], axis=0)


def _pair_scores(a_t, c_t):
    return _dot_tn(a_t, _block_diag(c_t[:REC_DK, :], c_t[REC_DK:, :]))


def _gla_scores_stage(units):
    t = GLA_TILE
    for u in units:
        lf2 = u["lf"]() * LOG2_E
        hi = lf2.astype(BF16)
        lo = (lf2 - hi.astype(F32)).astype(BF16)
        u["sums"] = _dot_tn(jnp.concatenate([hi, lo], axis=0), u["sums_ref"][...])
        u["scores"] = u["masks_ref"][0] * _pair_scores(u["q_t"](), u["k_t"]()).astype(BF16)
    for li in range(len(GLA_LEVELS)):
        for u in units:
            x = jnp.exp2(u["sums"][:, (1 + li) * t:(2 + li) * t]).astype(BF16)
            z = _pair_scores(u["q_t"]() * x, u["k_t"]() * x).astype(BF16)
            u["scores"] = u["scores"] + u["masks_ref"][1 + li] * z
    for u in units:
        u["save"](u["scores"], u["sums"][:, :t])


def _gla_state_stage(u):
    t = GLA_TILE
    dk = REC_DK
    b_t, st_ref = u["load_b"](), u["st_ref"]
    q_t, k_t, v = u["q_t"](), u["k_t"](), u["v"]()
    edge = b_t[:, 0:1] if u["backward"] else b_t[:, t - 1:t]
    o = _dot(u["load_scores"](), _block_diag(v[:, :dk], v[:, dk:]))
    st_a, st_b = st_ref[0], st_ref[1]
    q_in = q_t * jnp.exp2(b_t).astype(BF16)
    o = o + _dot_tn(q_in, _block_diag(st_a.astype(BF16), st_b.astype(BF16)))
    k_edge = k_t * jnp.exp2(edge - b_t).astype(BF16)
    carry = jnp.exp2(edge)
    st_ref[0] = st_a * carry[:dk, :] + _dot(k_edge[:dk, :], v[:, :dk])
    st_ref[1] = st_b * carry[dk:, :] + _dot(k_edge[dk:, :], v[:, dk:])
    u["store"](o)


def _gla_kernel(*refs, n_tiles, has_state, want_state):
    refs = list(refs)
    q_ref, v_ref, lff_ref, lfb_ref, kf_ref, kb_ref = refs[:6]
    pos = 6
    if has_state:
        s0_ref = refs[pos]
        pos += 1
    sums_f_ref, sums_b_ref, mf_ref, mb_ref = refs[pos:pos + 4]
    pos += 4
    o_ref = refs[pos]
    pos += 1
    if want_state:
        s_out_ref = refs[pos]
        pos += 1
    st_ref, pipe_s, pipe_b = refs[pos:pos + 3]

    t = GLA_TILE
    n_seqs = v_ref.shape[0]
    n_items = n_seqs * n_tiles
    pair_lanes = 2 * REC_DK
    o_ref[...] = jnp.zeros(o_ref.shape, o_ref.dtype)
    if has_state:
        st_ref[...] = s0_ref[...]
    else:
        st_ref[...] = jnp.zeros(st_ref.shape, st_ref.dtype)

    def units_of(item):
        seq, step = item // n_tiles, item % n_tiles
        units = []
        for direction, (lf_ref, k_ref, sums_ref, m_ref) in enumerate(
                ((lff_ref, kf_ref, sums_f_ref, mf_ref), (lfb_ref, kb_ref, sums_b_ref, mb_ref))):
            tile = step if direction == 0 else n_tiles - 1 - step
            rows = pl.ds(pl.multiple_of(tile * t, t), t)
            for pair in range(GLA_HEADS_PER_STEP // 2):
                lanes = slice(pair * pair_lanes, (pair + 1) * pair_lanes)
                ui = len(units)

                def store(o, rows=rows, lanes=lanes):
                    o_ref[seq, rows, lanes] += o

                def save(scores, b_t, ui=ui):
                    pipe_s[ui] = scores
                    pipe_b[ui] = b_t

                rows_of = lambda ref, rows=rows, lanes=lanes: (lambda: ref[seq, rows, lanes])
                channels_of = lambda ref, tile=tile, lanes=lanes: (lambda: ref[seq, tile, lanes, :])
                units.append(dict(
                    q_t=channels_of(q_ref), k_t=channels_of(k_ref), v=rows_of(v_ref),
                    lf=rows_of(lf_ref), st_ref=st_ref.at[seq, direction, pl.ds(2 * pair, 2)],
                    sums_ref=sums_ref, masks_ref=m_ref, backward=direction == 1, store=store,
                    save=save, load_scores=lambda ui=ui: pipe_s[ui], load_b=lambda ui=ui: pipe_b[ui]))
        return units

    def body(item, carry):
        units = units_of(item)
        _gla_scores_stage(units)
        for u in units:
            _gla_state_stage(u)
        return carry

    lax.fori_loop(0, n_items, body, 0)
    if want_state:
        s_out_ref[...] = st_ref[...]


def _gla(q, v, lf_fw, lf_bw, k_fw, k_bw, s0, consts, *, n_seq, seq_len, want_state):
    has_state = s0 is not None
    width = REC_WIDTH
    hps = GLA_HEADS_PER_STEP
    n_tiles = seq_len // GLA_TILE
    seqs = min(n_seq, max(1, GLA_ITEMS_PER_STEP // n_tiles))
    assert n_seq % seqs == 0
    n_units = hps
    seq3 = lambda a: a.reshape(n_seq, seq_len, width)
    head_spec = pl.BlockSpec((seqs, seq_len, hps * REC_DK), lambda b, h: (b, 0, h))
    tile_spec = pl.BlockSpec((seqs, n_tiles, hps * REC_DK, GLA_TILE), lambda b, h: (b, 0, h, 0))
    state_spec = pl.BlockSpec((seqs, 2, hps, REC_DK, REC_DV), lambda b, h: (b, 0, h, 0, 0))
    in_specs = [tile_spec, head_spec, head_spec, head_spec, tile_spec, tile_spec]
    args = [q, seq3(v), seq3(lf_fw), seq3(lf_bw), k_fw, k_bw]
    if has_state:
        in_specs.append(state_spec)
        args.append(s0)
    for c in consts:
        in_specs.append(pl.BlockSpec(c.shape, lambda b, h, nd=c.ndim: (0,) * nd))
        args.append(c)
    out_shape = [jax.ShapeDtypeStruct((n_seq, seq_len, width), F32)]
    out_specs = [head_spec]
    if want_state:
        out_shape.append(jax.ShapeDtypeStruct((n_seq, 2, N_REC_HEADS, REC_DK, REC_DV), F32))
        out_specs.append(state_spec)
    res = pl.pallas_call(
        functools.partial(_gla_kernel, n_tiles=n_tiles, has_state=has_state, want_state=want_state),
        grid=(n_seq // seqs, N_REC_HEADS // hps),
        in_specs=in_specs,
        out_specs=tuple(out_specs),
        out_shape=tuple(out_shape),
        scratch_shapes=[
            pltpu.VMEM((seqs, 2, hps, REC_DK, REC_DV), F32),
            pltpu.VMEM((n_units, GLA_TILE, 2 * GLA_TILE), BF16),
            pltpu.VMEM((n_units, 2 * REC_DK, GLA_TILE), F32),
        ],
        compiler_params=pltpu.CompilerParams(vmem_limit_bytes=VMEM_LIMIT),
        name="gla_state_in" if has_state else "gla_state_out",
    )(*args)
    o = res[0].reshape(n_seq * seq_len, width)
    return (o, res[1]) if want_state else (o, None)


def _rec_out_kernel(o_ref, sg_ref, x_ref, mod_ref, ng_ref, w_ref, lng_ref, lnb_ref, out_ref):
    d = D_MODEL
    gate = mod_ref[...][:, 2 * d:]

    def branch_of(rows):
        parts = []
        for hd in range(N_REC_HEADS):
            oh = o_ref[rows, hd * REC_DV:(hd + 1) * REC_DV]
            ms = jnp.mean(oh * oh, axis=-1, keepdims=True)
            parts.append(oh * lax.rsqrt(ms + NORM_EPS) * ng_ref[...])
        o = jnp.concatenate(parts, axis=1)
        gated = (o * sg_ref[rows, :].astype(F32)).astype(BF16)
        return _dot(gated, w_ref[...])

    chunks = [pl.ds(c * OUT_ROW_CHUNK, OUT_ROW_CHUNK) for c in range(o_ref.shape[0] // OUT_ROW_CHUNK)]
    branches = [branch_of(rows) for rows in chunks]
    for rows, branch in zip(chunks, branches):
        y = DEEPNORM_ALPHA * x_ref[rows, :] + gate * branch
        out_ref[rows, :] = _layer_norm(y, lng_ref[...], lnb_ref[...])


def _rec_out(o, sg, x2, mods4, norm_gain, w_out_bf, ln_g, ln_b, *, seq_len, mod_row0, per_seq_mod):
    n_tok = x2.shape[0]
    d = D_MODEL
    assert not per_seq_mod or seq_len % OUT_ROW_TILE == 0
    tiles_per_seq = max(1, seq_len // OUT_ROW_TILE)

    def mod_map(i):
        row = mod_row0 + (i // tiles_per_seq if per_seq_mod else 0)
        return (1, row, 0, 0)

    row_spec = pl.BlockSpec((OUT_ROW_TILE, d), lambda i: (i, 0))
    full = lambda i: (0, 0)
    return pl.pallas_call(
        _rec_out_kernel,
        grid=(n_tok // OUT_ROW_TILE,),
        in_specs=[
            row_spec, row_spec, row_spec,
            pl.BlockSpec((None, None, 1, 3 * d), mod_map),
            pl.BlockSpec(norm_gain.shape, full),
            pl.BlockSpec(w_out_bf.shape, full),
            pl.BlockSpec(ln_g.shape, full),
            pl.BlockSpec(ln_b.shape, full),
        ],
        out_specs=row_spec,
        out_shape=jax.ShapeDtypeStruct((n_tok, d), F32),
        compiler_params=pltpu.CompilerParams(vmem_limit_bytes=VMEM_LIMIT),
        name="rec_out",
    )(o, sg, x2, mods4, norm_gain, w_out_bf, ln_g, ln_b)


def _rope_tables(n_tokens):
    n_rows = n_tokens // GRID_W
    rows = jnp.repeat(jnp.arange(n_rows, dtype=F32), GRID_W)
    cols = jnp.tile(jnp.arange(GRID_W, dtype=F32), n_rows)
    inv_freq = 1.0 / (ROPE_THETA ** (jnp.arange(0, AXIS_DIM, 2, dtype=F32) / AXIS_DIM))
    ang_r = rows[:, None] * inv_freq[None, :]
    ang_c = cols[:, None] * inv_freq[None, :]
    ang = jnp.concatenate([ang_r, ang_r, ang_c, ang_c], axis=-1)
    cos, sin = jnp.cos(ang), jnp.sin(ang)
    first = (jnp.arange(HEAD_DIM) % AXIS_DIM) < AXIS_DIM // 2
    sin_a = jnp.where(first[None, :], -sin, 0.0)
    sin_b = jnp.where(first[None, :], 0.0, sin)
    reps = GROUP_LANES // HEAD_DIM
    return tuple(jnp.tile(t, (1, reps)) for t in (cos, sin_a, sin_b))


def _head_mean_matrix():
    idx = jnp.arange(GROUP_LANES) // HEAD_DIM
    return jnp.where(idx[:, None] == idx[None, :], 1.0 / HEAD_DIM, 0.0).astype(BF16)


def _gla_consts():
    t = GLA_TILE
    r = jnp.arange(t)[:, None]
    c = jnp.arange(t)[None, :]
    fw = [c <= r]
    bw = [c >= r]
    for half in GLA_LEVELS:
        start = 2 * half * (r // (2 * half))
        upper = r - start >= half
        last_low, first_up = start + half - 1, start + half
        fw.append(jnp.where(upper, (c > last_low) & (c <= r), (c > r) & (c <= last_low)))
        bw.append(jnp.where(upper, (c >= first_up) & (c < r), (c >= r) & (c < first_up)))
    twice = lambda a: jnp.concatenate([a, a], axis=-1)

    def summation(weights):
        cols = jnp.concatenate([w.T for w in weights], axis=1).astype(BF16)
        return jnp.concatenate([cols, cols], axis=0)

    sums_fw, sums_bw = summation(fw), summation(bw)
    masks = [(r == c)]
    for half in GLA_LEVELS:
        size = 2 * half
        masks.append((r // size == c // size) & (r % size >= half) & (c % size < half))
    m_fw = jnp.stack(masks).astype(BF16)
    m_bw = jnp.swapaxes(m_fw, 1, 2)
    return sums_fw, sums_bw, twice(m_fw), twice(m_bw)


def kernel(x_prompt, x_sample, cache_k, cache_v, state_rec, c, c_ctx, ada_w, ada_b, attn_w_in,
           attn_q_gain, attn_k_gain, attn_w_out, rec_w_in, rec_lower_bounds, rec_norm_gain,
           rec_w_out, ln_gain, ln_bias):
    d = D_MODEL
    n_p, len_p, _ = x_prompt.shape
    n_s, len_s, _ = x_sample.shape
    past = cache_k.shape[2]
    assert past % SLAB_LANES == 0 and len_p % SLAB_LANES == 0 and len_s % SLAB_LANES == 0

    cond = jnp.zeros((COND_ROWS, d), F32).at[0].set(c_ctx).at[1:1 + n_s].set(c)
    mods4 = _mods(cond, ada_w, ada_b).reshape(DEPTH, COND_ROWS, 1, 3 * d)
    lb_all = _lower_bounds(rec_lower_bounds)

    xp = x_prompt.reshape(n_p * len_p, d)
    xs = x_sample.reshape(n_s * len_s, d)

    w_in = attn_w_in[0].astype(BF16)
    w_out = attn_w_out[0].astype(BF16)
    reps = GROUP_LANES // HEAD_DIM
    qg = (jnp.tile(attn_q_gain[0], reps) * (LOG2_E / math.sqrt(HEAD_DIM))).reshape(1, GROUP_LANES)
    kg = jnp.tile(attn_k_gain[0], reps).reshape(1, GROUP_LANES)
    pn = _head_mean_matrix()
    ln_g = ln_gain[0].reshape(1, d)
    ln_b = ln_bias[0].reshape(1, d)

    xp1, k_p, v_p = _attn_seq(xp, mods4, w_in, pn, qg, kg, w_out, ln_g, ln_b, n_seq=n_p,
                              seq_len=len_p, mod_row=0)
    q_s, kp_s, vt_s, g_s = _attn_proj(xs, mods4, w_in, pn, qg, kg, _rope_tables(len_s), seq_len=len_s,
                                      mod_row0=1)
    cache = (cache_k[:, 0].reshape(n_s, past, KV_WIDTH), cache_v[:, 0].reshape(n_s, past, KV_WIDTH))
    xs1 = _attn_core(q_s, kp_s, vt_s, cache, g_s, xs, mods4, w_out, ln_g, ln_b, n_seq=n_s,
                     seq_len=len_s, q_tile=ATTN_Q_TILE, mod_row0=1)
    new_cache_k = k_p.reshape(n_p, 1, len_p, N_KV_HEADS, HEAD_DIM)
    new_cache_v = v_p.reshape(n_p, 1, len_p, N_KV_HEADS, HEAD_DIM)

    rw_in = rec_w_in[0].astype(BF16)
    rw_out = rec_w_out[0].astype(BF16)
    lb = lb_all[1]
    ng = rec_norm_gain[0].reshape(1, REC_DV)
    ln_g = ln_gain[1].reshape(1, d)
    ln_b = ln_bias[1].reshape(1, d)
    consts = _gla_consts()

    outs = []
    states = None
    for x1, n_seq, seq_len, row0, per_seq, s0 in (
            (xp1, n_p, len_p, 0, False, None),
            (xs1, n_s, len_s, 1, True, state_rec[:, 0])):
        q, v, g, lf_fw, lf_bw, k_fw, k_bw = _rec_proj(x1, mods4, rw_in, lb, seq_len=seq_len,
                                                       mod_row0=row0, per_seq_mod=per_seq)
        o, st = _gla(q, v, lf_fw, lf_bw, k_fw, k_bw, s0, consts, n_seq=n_seq, seq_len=seq_len,
                     want_state=s0 is None)
        if st is not None:
            states = st
        outs.append(_rec_out(o, g, x1, mods4, ng, rw_out, ln_g, ln_b, seq_len=seq_len,
                             mod_row0=row0, per_seq_mod=per_seq))

    y_prompt = outs[0].reshape(n_p, len_p, d)
    y_sample = outs[1].reshape(n_s, len_s, d)
    new_state_rec = states.reshape(n_p, 1, 2, N_REC_HEADS, REC_DK, REC_DV)
    return (y_prompt, y_sample, new_cache_k, new_cache_v, new_state_rec)
```

```python
import functools
import math

import jax
import jax.numpy as jnp
import numpy as np
from jax import lax
from jax.experimental import pallas as pl
from jax.experimental.pallas import tpu as pltpu

F32 = jnp.float32
BF16 = jnp.bfloat16

D_MODEL = 1024
DEPTH = 2
GRID_W = 64
N_HEADS = 16
N_KV_HEADS = 4
HEAD_DIM = 64
AXIS_DIM = HEAD_DIM // 2
ATTN_WIDTH = N_HEADS * HEAD_DIM
KV_WIDTH = N_KV_HEADS * HEAD_DIM
ROPE_THETA = 10000.0
N_REC_HEADS = 8
REC_DK = 128
REC_DV = 128
REC_WIDTH = N_REC_HEADS * REC_DK
NORM_EPS = 1e-6
LN_EPS = 1e-5
DEEPNORM_ALPHA = (2.0 * DEPTH) ** 0.25

SUBLANES = 8
BF16_ROWS = 16
COND_ROWS = SUBLANES
ATTN_Q_TILE = 256
LATENT_TILES_PER_STEP = 2
ATTN_PROJ_ROW_TILE = 512
REC_ROW_TILE = 512
OUT_ROW_TILE = 1024
OUT_ROW_CHUNK = 256
ATTN_SEQS_PER_STEP = 2
GROUP_LANES = 256
SLAB_LANES = 128
KPAD_WIDTH = 2 * N_KV_HEADS * SLAB_LANES
REC_PROJ_SLAB = 256
GLA_TILE = 128
GLA_LEVELS = (1, 2, 4, 8, 16, 32, 64)
GLA_HEADS_PER_STEP = 4
GLA_ITEMS_PER_STEP = 8
LOG2_E = 1.4426950408889634
MOD_PARTS = 3
VMEM_LIMIT = 56 * 1024 * 1024


def _sigmoid(x):
    return 1.0 / (1.0 + jnp.exp(-x))


def _dot(a, b):
    return jnp.dot(a, b, preferred_element_type=F32)


def _dot_nt(a, b):
    return lax.dot_general(a, b, (((1,), (1,)), ((), ())), preferred_element_type=F32)


def _dot_tn(a, b):
    return lax.dot_general(a, b, (((0,), (0,)), ((), ())), preferred_element_type=F32)


def _layer_norm(y, g, b):
    mu = jnp.mean(y, axis=-1, keepdims=True)
    yc = y - mu
    var = jnp.mean(yc * yc, axis=-1, keepdims=True)
    return yc * lax.rsqrt(var + LN_EPS) * g + b


def _mods_kernel(cond_ref, w_ref, b_ref, out_ref):
    c = cond_ref[...]
    s = (c * _sigmoid(c)).astype(BF16)
    out_ref[...] = _dot(s, w_ref[...].astype(BF16)) + b_ref[...]


def _mods(cond, ada_w, ada_b):
    d = D_MODEL
    return pl.pallas_call(
        _mods_kernel,
        grid=(DEPTH, MOD_PARTS),
        in_specs=[
            pl.BlockSpec((COND_ROWS, d), lambda l, j: (0, 0)),
            pl.BlockSpec((None, d, d), lambda l, j: (l, 0, j)),
            pl.BlockSpec((None, 1, d), lambda l, j: (l, 0, j)),
        ],
        out_specs=pl.BlockSpec((None, COND_ROWS, d), lambda l, j: (l, 0, j)),
        out_shape=jax.ShapeDtypeStruct((DEPTH, COND_ROWS, MOD_PARTS * d), F32),
        compiler_params=pltpu.CompilerParams(vmem_limit_bytes=VMEM_LIMIT),
        name="adaln_mods",
    )(cond, ada_w, ada_b.reshape(DEPTH, 1, MOD_PARTS * d))


def _lower_bounds_kernel(r_ref, out_ref):
    r = [r_ref[i] for i in range(DEPTH)]
    m = functools.reduce(jnp.maximum, r)
    e = [jnp.exp(x - m) for x in r]
    tot = functools.reduce(lambda a, b: a + b, e)
    soft = [x / tot for x in e]
    acc = soft[0]
    for i in range(DEPTH):
        if i > 0:
            acc = acc + soft[i]
        out_ref[i] = acc - soft[0]


def _lower_bounds(rec_lower_bounds):
    return pl.pallas_call(
        _lower_bounds_kernel,
        out_shape=jax.ShapeDtypeStruct(rec_lower_bounds.shape, F32),
        name="rec_lower_bounds",
    )(rec_lower_bounds)


def _kv_head_slabs(k):
    lane = lax.broadcasted_iota(jnp.int32, (k.shape[0], SLAB_LANES), 1)
    low = lane < HEAD_DIM
    heads_per_slab = SLAB_LANES // HEAD_DIM
    out = []
    for j in range(N_KV_HEADS):
        tile = k[:, (j // heads_per_slab) * SLAB_LANES:(j // heads_per_slab + 1) * SLAB_LANES]
        moved = pltpu.roll(tile, HEAD_DIM, 1)
        at_low, at_high = (tile, moved) if j % heads_per_slab == 0 else (moved, tile)
        out.append(jnp.where(low, at_low, 0.0))
        out.append(jnp.where(low, 0.0, at_high))
    return jnp.concatenate(out, axis=1)


def _head_norm(u, pn, gain):
    ms = _dot((u * u).astype(BF16), pn)
    return u * lax.rsqrt(ms + NORM_EPS) * gain


def _attn_proj_kernel(x_ref, mod_ref, w_ref, pn_ref, qg_ref, kg_ref, cos_ref, sa_ref, sb_ref,
                      q_out, kp_out, vt_out, g_out):
    d = D_MODEL
    mod = mod_ref[...]
    shift, scale = mod[:, :d], mod[:, d:2 * d]
    h = (x_ref[...] * (1.0 + scale) + shift).astype(BF16)
    pn = pn_ref[...]

    def norm_rope(u, gain):
        y = _head_norm(u, pn, gain)
        return (y * cos_ref[...]
                + pltpu.roll(y, GROUP_LANES - AXIS_DIM // 2, 1) * sa_ref[...]
                + pltpu.roll(y, AXIS_DIM // 2, 1) * sb_ref[...])

    def finish_q(j, u):
        q_out[:, j * GROUP_LANES:(j + 1) * GROUP_LANES] = norm_rope(u, qg_ref[...]).astype(q_out.dtype)

    def finish_k(_, u):
        kp_out[...] = _kv_head_slabs(norm_rope(u, kg_ref[...])).astype(kp_out.dtype)

    def finish_v(_, v):
        vt_out[...] = v.T.astype(vt_out.dtype)

    def finish_g(j, u):
        g_out[:, j * GROUP_LANES:(j + 1) * GROUP_LANES] = (u * _sigmoid(u)).astype(g_out.dtype)

    slabs = ([(finish_q, j) for j in range(N_KV_HEADS)] + [(finish_k, 0), (finish_v, 0)]
             + [(finish_g, j) for j in range(N_KV_HEADS)])
    project = lambda i: _dot(h, w_ref[:, i * GROUP_LANES:(i + 1) * GROUP_LANES])
    v_slab = N_KV_HEADS + 1
    order = [i for i in range(len(slabs)) if i != v_slab] + [v_slab]
    u_next = project(order[0])
    for n, i in enumerate(order):
        u = u_next
        if n + 1 < len(order):
            u_next = project(order[n + 1])
        finish, j = slabs[i]
        finish(j, u)


def _attn_proj(x2, mods4, w_bf, pn, qg, kg, rope_tabs, *, seq_len, mod_row0):
    n_tok = x2.shape[0]
    d = D_MODEL
    tile = ATTN_PROJ_ROW_TILE
    assert seq_len % tile == 0
    tiles_per_seq = seq_len // tile
    full = lambda i: (0, 0)
    rows = lambda width: pl.BlockSpec((tile, width), lambda i: (i, 0))
    table = pl.BlockSpec((tile, GROUP_LANES), lambda i: (i % tiles_per_seq, 0))
    consts = [w_bf, pn, qg, kg]
    return pl.pallas_call(
        _attn_proj_kernel,
        grid=(n_tok // tile,),
        in_specs=[rows(d),
                  pl.BlockSpec((None, None, 1, 3 * d), lambda i: (0, mod_row0 + i // tiles_per_seq, 0, 0))]
        + [pl.BlockSpec(c.shape, full) for c in consts] + [table] * len(rope_tabs),
        out_specs=(
            rows(ATTN_WIDTH),
            rows(KPAD_WIDTH),
            pl.BlockSpec((None, KV_WIDTH, tile), lambda i: (i // tiles_per_seq, 0, i % tiles_per_seq)),
            rows(ATTN_WIDTH),
        ),
        out_shape=(
            jax.ShapeDtypeStruct((n_tok, ATTN_WIDTH), BF16),
            jax.ShapeDtypeStruct((n_tok, KPAD_WIDTH), BF16),
            jax.ShapeDtypeStruct((n_tok // seq_len, KV_WIDTH, seq_len), BF16),
            jax.ShapeDtypeStruct((n_tok, ATTN_WIDTH), BF16),
        ),
        compiler_params=pltpu.CompilerParams(vmem_limit_bytes=VMEM_LIMIT),
        name="attn_proj_rope",
    )(x2, mods4, *consts, *rope_tabs)


def _attention_phases(q_ref, kp_ref, vt_ref, cache, s_scr, p_scr, ot_scr, between=()):
    n_new = kp_ref.shape[0]
    n_old = cache[0].shape[0] if cache is not None else 0
    n_keys = n_old + n_new
    group = N_HEADS // N_KV_HEADS
    heads_per_slab = SLAB_LANES // HEAD_DIM
    pending_work = list(between)

    def fold_rows(a, op):
        return functools.reduce(
            op, [a[r * SUBLANES:(r + 1) * SUBLANES, :] for r in range(a.shape[0] // SUBLANES)])

    def scores_phase(j):
        slot = j % 2
        maxima = []
        for hh in range(group):
            hd = j * group + hh
            slab, where = hd // heads_per_slab, hd % heads_per_slab
            q_slab = q_ref[:, slab * SLAB_LANES:(slab + 1) * SLAB_LANES]
            k_slab = heads_per_slab * j + where
            lanes = slice(k_slab * SLAB_LANES, (k_slab + 1) * SLAB_LANES)
            s = _dot_nt(kp_ref[:, lanes], q_slab)
            s_scr[slot, hh, n_old:, :] = s
            mx = fold_rows(s, jnp.maximum)
            if cache is not None:
                s = _dot_nt(cache[0][:, lanes], q_slab)
                s_scr[slot, hh, :n_old, :] = s
                mx = jnp.maximum(mx, fold_rows(s, jnp.maximum))
            maxima.append(jnp.max(mx, axis=0, keepdims=True))
        return maxima

    def values_phase(j, maxima):
        slot = j % 2
        rows = slice(j * HEAD_DIM, (j + 1) * HEAD_DIM)
        v_t = vt_ref[rows, :]
        if cache is not None:
            v_t = jnp.concatenate([cache[1][rows, :], v_t], axis=1)
        v_ext = jnp.concatenate([v_t, jnp.ones((BF16_ROWS, n_keys), BF16)], axis=0)
        for hh in range(group):
            hd = j * group + hh
            p_scr[hh % 2] = jnp.exp2(s_scr[slot, hh] - maxima[hh]).astype(BF16)
            acc = _dot(v_ext, p_scr[hh % 2])
            l = acc[HEAD_DIM:HEAD_DIM + 1, :]
            ot_scr[hd * HEAD_DIM:(hd + 1) * HEAD_DIM, :] = acc[:HEAD_DIM, :] * (1.0 / l)

    pending = scores_phase(0)
    for j in range(N_KV_HEADS):
        nxt = scores_phase(j + 1) if j + 1 < N_KV_HEADS else None
        if pending_work:
            pending_work.pop(0)()
        values_phase(j, pending)
        pending = nxt
    for thunk in pending_work:
        thunk()


def _branch_epilogue(ot_scr, sg, x_ref, mod_ref, w_ref, lng_ref, lnb_ref, out_ref):
    d = D_MODEL
    o = ot_scr[...].T
    gated = (o * sg.astype(F32)).astype(BF16)
    branch = _dot(gated, w_ref[...])
    gate = mod_ref[...][:, 2 * d:]
    y = DEEPNORM_ALPHA * x_ref[...] + gate * branch
    out_ref[...] = _layer_norm(y, lng_ref[...], lnb_ref[...])


def _attn_core_kernel(q_ref, kp_ref, vt_ref, kc_ref, vc_ref, sg_ref, x_ref, mod_ref, w_ref,
                      lng_ref, lnb_ref, out_ref, s_scr, p_scr, ot_scr):
    cache = (_kv_head_slabs(kc_ref[...]).astype(BF16), vc_ref[...].T.astype(BF16))
    tq = ot_scr.shape[2]

    def epilogue(t):
        rows = pl.ds(t * tq, tq)
        _branch_epilogue(ot_scr.at[t], sg_ref[rows, :], x_ref.at[rows], mod_ref, w_ref, lng_ref,
                         lnb_ref, out_ref.at[rows])

    n_sub = ot_scr.shape[0]
    for t in range(n_sub):
        between = [functools.partial(epilogue, t - 1)] if t > 0 else []
        _attention_phases(q_ref.at[pl.ds(t * tq, tq)], kp_ref, vt_ref, cache, s_scr, p_scr,
                          ot_scr.at[t], between=between)
    epilogue(n_sub - 1)


def _attn_seq_kernel(x_ref, mod_ref, w_in_ref, pn_ref, qg_ref, kg_ref, w_out_ref, lng_ref, lnb_ref,
                     out_ref, k_out, v_out,
                     h_scr, q_scr, kp_scr, vt_scr, sg_scr, s_scr, p_scr, ot_scr):
    d = D_MODEL
    n_sub = x_ref.shape[0]
    mod = mod_ref[...]
    shift, scale = mod[:, :d], mod[:, d:2 * d]
    pn = pn_ref[...]
    project = lambda i: _dot(h_scr[...], w_in_ref[:, i * GROUP_LANES:(i + 1) * GROUP_LANES])

    def epilogue(s):
        _branch_epilogue(ot_scr.at[s], sg_scr[s], x_ref.at[s], mod_ref, w_out_ref, lng_ref, lnb_ref,
                         out_ref.at[s])

    for s in range(n_sub):
        h_scr[...] = (x_ref[s] * (1.0 + scale) + shift).astype(BF16)

        def finish_q(j, u):
            q_scr[:, j * GROUP_LANES:(j + 1) * GROUP_LANES] = _head_norm(u, pn, qg_ref[...]).astype(BF16)

        def finish_k(_, u, s=s):
            k = _head_norm(u, pn, kg_ref[...])
            kp_scr[...] = _kv_head_slabs(k).astype(BF16)
            k_out[s] = k.reshape(k_out.shape[1:])

        def finish_v(_, v, s=s):
            vt_scr[...] = v.T.astype(BF16)
            v_out[s] = v.reshape(v_out.shape[1:])

        def finish_g(j, u, s=s):
            sg_scr[s, :, j * GROUP_LANES:(j + 1) * GROUP_LANES] = (u * _sigmoid(u)).astype(BF16)

        slabs = [(finish_q, j) for j in range(N_KV_HEADS)] + [(finish_k, 0), (finish_v, 0)]
        u_next = project(0)
        if s > 0:
            epilogue(s - 1)
        for i, (finish, j) in enumerate(slabs):
            u = u_next
            if i + 1 < len(slabs):
                u_next = project(i + 1)
            finish(j, u)
        gate_slabs = [functools.partial(lambda j, g: g(j, project(len(slabs) + j)), j, finish_g)
                      for j in range(N_KV_HEADS)]
        _attention_phases(q_scr, kp_scr, vt_scr, None, s_scr, p_scr, ot_scr.at[s], between=gate_slabs)
    epilogue(n_sub - 1)


def _attn_seq(x2, mods4, w_in_bf, pn, qg, kg, w_out_bf, ln_g, ln_b, *, n_seq, seq_len, mod_row):
    d = D_MODEL
    n_sub = ATTN_SEQS_PER_STEP
    assert n_seq % n_sub == 0
    full = lambda b: (0, 0)
    rows = pl.BlockSpec((n_sub, seq_len, d), lambda b: (b, 0, 0))
    heads = pl.BlockSpec((n_sub, seq_len, N_KV_HEADS, HEAD_DIM), lambda b: (b, 0, 0, 0))
    consts = [w_in_bf, pn, qg, kg, w_out_bf, ln_g, ln_b]
    group = N_HEADS // N_KV_HEADS
    out, k, v = pl.pallas_call(
        _attn_seq_kernel,
        grid=(n_seq // n_sub,),
        in_specs=[rows, pl.BlockSpec((None, None, 1, 3 * d), lambda b: (0, mod_row, 0, 0))]
        + [pl.BlockSpec(c.shape, full) for c in consts],
        out_specs=(rows, heads, heads),
        out_shape=(jax.ShapeDtypeStruct((n_seq, seq_len, d), F32),
                   jax.ShapeDtypeStruct((n_seq, seq_len, N_KV_HEADS, HEAD_DIM), F32),
                   jax.ShapeDtypeStruct((n_seq, seq_len, N_KV_HEADS, HEAD_DIM), F32)),
        scratch_shapes=[
            pltpu.VMEM((seq_len, d), BF16),
            pltpu.VMEM((seq_len, ATTN_WIDTH), BF16),
            pltpu.VMEM((seq_len, KPAD_WIDTH), BF16),
            pltpu.VMEM((KV_WIDTH, seq_len), BF16),
            pltpu.VMEM((n_sub, seq_len, ATTN_WIDTH), BF16),
            pltpu.VMEM((2, group, seq_len, seq_len), F32),
            pltpu.VMEM((2, seq_len, seq_len), BF16),
            pltpu.VMEM((n_sub, ATTN_WIDTH, seq_len), F32),
        ],
        compiler_params=pltpu.CompilerParams(vmem_limit_bytes=VMEM_LIMIT),
        name="attn_seq",
    )(x2.reshape(n_seq, seq_len, d), mods4, *consts)
    return out.reshape(n_seq * seq_len, d), k, v


def _attn_core(q, kp, vt, cache, sg, x2, mods4, w_out_bf, ln_g, ln_b, *, n_seq, seq_len, q_tile,
               mod_row0):
    d = D_MODEL
    n_sub = LATENT_TILES_PER_STEP
    rows = n_sub * q_tile
    tiles = seq_len // rows
    ck, cv = cache
    n_keys = seq_len + ck.shape[1]
    q3 = q.reshape(n_seq, seq_len, ATTN_WIDTH)
    kp3 = kp.reshape(n_seq, seq_len, KPAD_WIDTH)
    sg3 = sg.reshape(n_seq, seq_len, ATTN_WIDTH)
    x3 = x2.reshape(n_seq, seq_len, d)
    tile_spec = lambda width: pl.BlockSpec((None, rows, width), lambda b, i: (b, i, 0))
    seq_spec = lambda length, width: pl.BlockSpec((None, length, width), lambda b, i: (b, 0, 0))
    full2 = lambda b, i: (0, 0)
    out = pl.pallas_call(
        _attn_core_kernel,
        grid=(n_seq, tiles),
        in_specs=[
            tile_spec(ATTN_WIDTH), seq_spec(seq_len, KPAD_WIDTH), seq_spec(KV_WIDTH, seq_len),
            seq_spec(ck.shape[1], KV_WIDTH), seq_spec(cv.shape[1], KV_WIDTH),
            tile_spec(ATTN_WIDTH), tile_spec(d),
            pl.BlockSpec((None, None, 1, 3 * d), lambda b, i: (0, mod_row0 + b, 0, 0)),
            pl.BlockSpec(w_out_bf.shape, full2),
            pl.BlockSpec(ln_g.shape, full2),
            pl.BlockSpec(ln_b.shape, full2),
        ],
        out_specs=tile_spec(d),
        out_shape=jax.ShapeDtypeStruct((n_seq, seq_len, d), F32),
        scratch_shapes=[
            pltpu.VMEM((2, N_HEADS // N_KV_HEADS, n_keys, q_tile), F32),
            pltpu.VMEM((2, n_keys, q_tile), BF16),
            pltpu.VMEM((n_sub, ATTN_WIDTH, q_tile), F32),
        ],
        compiler_params=pltpu.CompilerParams(vmem_limit_bytes=VMEM_LIMIT),
        name="attn_core_cache",
    )(q3, kp3, vt, ck, cv, sg3, x3, mods4, w_out_bf, ln_g, ln_b)
    return out.reshape(n_seq * seq_len, d)


def _rec_proj_kernel(x_ref, mod_ref, w_ref, lb_ref, q_out, v_out, g_out, lf_fw, lf_bw, k_fw, k_bw):
    d = D_MODEL
    mod = mod_ref[...]
    shift, scale = mod[:, :d], mod[:, d:2 * d]
    h = (x_ref[...] * (1.0 + scale) + shift).astype(BF16)
    slab = REC_PROJ_SLAB

    def store_channel_major(out_ref, cols, val):
        val_t = val.T.astype(out_ref.dtype)
        tiles = out_ref.shape[1]
        for i in range(val.shape[0] // GLA_TILE):
            out_ref[i // tiles, i % tiles, cols, :] = val_t[:, i * GLA_TILE:(i + 1) * GLA_TILE]

    def finish_q(cols, u):
        store_channel_major(q_out, cols, u * _sigmoid(u))

    def finish_gate(direction, lf_out, k_out):
        def finish(cols, z):
            lb = lb_ref[direction:direction + 1, cols]
            sig = _sigmoid(z)
            lf_out[:, cols] = jnp.log(lb + (1.0 - lb) * sig)
            store_channel_major(k_out, cols, (1.0 - lb) * (1.0 - sig))
        return finish

    def finish_v(cols, u):
        v_out[:, cols] = u.astype(v_out.dtype)

    def finish_g(cols, u):
        g_out[:, cols] = (u * _sigmoid(u)).astype(g_out.dtype)

    sections = (finish_q, finish_gate(0, lf_fw, k_fw), finish_gate(1, lf_bw, k_bw),
                finish_v, finish_g)
    per_section = REC_WIDTH // slab
    n_slabs = len(sections) * per_section
    project = lambda i: _dot(h, w_ref[:, i * slab:(i + 1) * slab])
    v_section = sections.index(finish_v)
    order = ([i for i in range(n_slabs) if i // per_section != v_section]
             + [i for i in range(n_slabs) if i // per_section == v_section])
    u_next = project(order[0])
    for n, i in enumerate(order):
        u = u_next
        if n + 1 < n_slabs:
            u_next = project(order[n + 1])
        within = i % per_section
        sections[i // per_section](slice(within * slab, (within + 1) * slab), u)


def _rec_proj(x2, mods4, w_bf, lb, *, seq_len, mod_row0, per_seq_mod):
    n_tok = x2.shape[0]
    d = D_MODEL
    rows = REC_ROW_TILE
    steps_per_seq = max(1, seq_len // rows)
    seqs = max(1, rows // seq_len)
    assert rows % GLA_TILE == 0 and (seq_len % rows == 0 or rows % seq_len == 0)
    assert not per_seq_mod or seqs == 1

    def mod_map(i):
        row = mod_row0 + (i // steps_per_seq if per_seq_mod else 0)
        return (1, row, 0, 0)

    row_spec = pl.BlockSpec((rows, REC_WIDTH), lambda i: (i, 0))
    tile_spec = pl.BlockSpec((seqs, rows // seqs // GLA_TILE, REC_WIDTH, GLA_TILE),
                             lambda i: (i // steps_per_seq, i % steps_per_seq, 0, 0))
    full = lambda i: (0, 0)
    row_out = lambda t: (row_spec, jax.ShapeDtypeStruct((n_tok, REC_WIDTH), t))
    tile_out = (tile_spec, jax.ShapeDtypeStruct(
        (n_tok // seq_len, seq_len // GLA_TILE, REC_WIDTH, GLA_TILE), BF16))
    outs = (tile_out, row_out(BF16), row_out(BF16), row_out(F32), row_out(F32), tile_out, tile_out)
    return pl.pallas_call(
        _rec_proj_kernel,
        grid=(n_tok // rows,),
        in_specs=[
            pl.BlockSpec((rows, d), lambda i: (i, 0)),
            pl.BlockSpec((None, None, 1, 3 * d), mod_map),
            pl.BlockSpec(w_bf.shape, full),
            pl.BlockSpec(lb.shape, full),
        ],
        out_specs=tuple(spec for spec, _ in outs),
        out_shape=tuple(shape for _, shape in outs),
        compiler_params=pltpu.CompilerParams(vmem_limit_bytes=VMEM_LIMIT),
        name="rec_proj",
    )(x2, mods4, w_bf, lb)


def _block_diag(a, b):
    za = jnp.zeros(a.shape, a.dtype)
    return jnp.concatenate(
        [jnp.concatenate([a, za], axis=1), jnp.concatenate([za, b], axis=1)], axis=0)


def _pair_scores(a_t, c_t):
    return _dot_tn(a_t, _block_diag(c_t[:REC_DK, :], c_t[REC_DK:, :]))


def _gla_scores_stage(units):
    t = GLA_TILE
    for u in units:
        lf2 = u["lf"]() * LOG2_E
        hi = lf2.astype(BF16)
        lo = (lf2 - hi.astype(F32)).astype(BF16)
        u["sums"] = _dot_tn(jnp.concatenate([hi, lo], axis=0), u["sums_ref"][...])
        u["scores"] = u["masks_ref"][0] * _pair_scores(u["q_t"](), u["k_t"]()).astype(BF16)
    for li in range(len(GLA_LEVELS)):
        for u in units:
            x = jnp.exp2(u["sums"][:, (1 + li) * t:(2 + li) * t]).astype(BF16)
            z = _pair_scores(u["q_t"]() * x, u["k_t"]() * x).astype(BF16)
            u["scores"] = u["scores"] + u["masks_ref"][1 + li] * z
    for u in units:
        u["save"](u["scores"], u["sums"][:, :t])


def _gla_state_stage(u):
    t = GLA_TILE
    dk = REC_DK
    b_t, st_ref = u["load_b"](), u["st_ref"]
    q_t, k_t, v = u["q_t"](), u["k_t"](), u["v"]()
    edge = b_t[:, 0:1] if u["backward"] else b_t[:, t - 1:t]
    o = _dot(u["load_scores"](), _block_diag(v[:, :dk], v[:, dk:]))
    st_a, st_b = st_ref[0], st_ref[1]
    q_in = q_t * jnp.exp2(b_t).astype(BF16)
    o = o + _dot_tn(q_in, _block_diag(st_a.astype(BF16), st_b.astype(BF16)))
    k_edge = k_t * jnp.exp2(edge - b_t).astype(BF16)
    carry = jnp.exp2(edge)
    st_ref[0] = st_a * carry[:dk, :] + _dot(k_edge[:dk, :], v[:, :dk])
    st_ref[1] = st_b * carry[dk:, :] + _dot(k_edge[dk:, :], v[:, dk:])
    u["store"](o)


def _gla_kernel(*refs, n_tiles, has_state, want_state):
    refs = list(refs)
    q_ref, v_ref, lff_ref, lfb_ref, kf_ref, kb_ref = refs[:6]
    pos = 6
    if has_state:
        s0_ref = refs[pos]
        pos += 1
    sums_f_ref, sums_b_ref, mf_ref, mb_ref = refs[pos:pos + 4]
    pos += 4
    o_ref = refs[pos]
    pos += 1
    if want_state:
        s_out_ref = refs[pos]
        pos += 1
    st_ref, pipe_s, pipe_b = refs[pos:pos + 3]

    t = GLA_TILE
    n_seqs = v_ref.shape[0]
    n_items = n_seqs * n_tiles
    pair_lanes = 2 * REC_DK
    o_ref[...] = jnp.zeros(o_ref.shape, o_ref.dtype)
    if has_state:
        st_ref[...] = s0_ref[...]
    else:
        st_ref[...] = jnp.zeros(st_ref.shape, st_ref.dtype)

    def units_of(item):
        seq, step = item // n_tiles, item % n_tiles
        units = []
        for direction, (lf_ref, k_ref, sums_ref, m_ref) in enumerate(
                ((lff_ref, kf_ref, sums_f_ref, mf_ref), (lfb_ref, kb_ref, sums_b_ref, mb_ref))):
            tile = step if direction == 0 else n_tiles - 1 - step
            rows = pl.ds(pl.multiple_of(tile * t, t), t)
            for pair in range(GLA_HEADS_PER_STEP // 2):
                lanes = slice(pair * pair_lanes, (pair + 1) * pair_lanes)
                ui = len(units)

                def store(o, rows=rows, lanes=lanes):
                    o_ref[seq, rows, lanes] += o

                def save(scores, b_t, ui=ui):
                    pipe_s[ui] = scores
                    pipe_b[ui] = b_t

                rows_of = lambda ref, rows=rows, lanes=lanes: (lambda: ref[seq, rows, lanes])
                channels_of = lambda ref, tile=tile, lanes=lanes: (lambda: ref[seq, tile, lanes, :])
                units.append(dict(
                    q_t=channels_of(q_ref), k_t=channels_of(k_ref), v=rows_of(v_ref),
                    lf=rows_of(lf_ref), st_ref=st_ref.at[seq, direction, pl.ds(2 * pair, 2)],
                    sums_ref=sums_ref, masks_ref=m_ref, backward=direction == 1, store=store,
                    save=save, load_scores=lambda ui=ui: pipe_s[ui], load_b=lambda ui=ui: pipe_b[ui]))
        return units

    def body(item, carry):
        units = units_of(item)
        _gla_scores_stage(units)
        for u in units:
            _gla_state_stage(u)
        return carry

    lax.fori_loop(0, n_items, body, 0)
    if want_state:
        s_out_ref[...] = st_ref[...]


def _gla(q, v, lf_fw, lf_bw, k_fw, k_bw, s0, consts, *, n_seq, seq_len, want_state):
    has_state = s0 is not None
    width = REC_WIDTH
    hps = GLA_HEADS_PER_STEP
    n_tiles = seq_len // GLA_TILE
    seqs = min(n_seq, max(1, GLA_ITEMS_PER_STEP // n_tiles))
    assert n_seq % seqs == 0
    n_units = hps
    seq3 = lambda a: a.reshape(n_seq, seq_len, width)
    head_spec = pl.BlockSpec((seqs, seq_len, hps * REC_DK), lambda b, h: (b, 0, h))
    tile_spec = pl.BlockSpec((seqs, n_tiles, hps * REC_DK, GLA_TILE), lambda b, h: (b, 0, h, 0))
    state_spec = pl.BlockSpec((seqs, 2, hps, REC_DK, REC_DV), lambda b, h: (b, 0, h, 0, 0))
    in_specs = [tile_spec, head_spec, head_spec, head_spec, tile_spec, tile_spec]
    args = [q, seq3(v), seq3(lf_fw), seq3(lf_bw), k_fw, k_bw]
    if has_state:
        in_specs.append(state_spec)
        args.append(s0)
    for c in consts:
        in_specs.append(pl.BlockSpec(c.shape, lambda b, h, nd=c.ndim: (0,) * nd))
        args.append(c)
    out_shape = [jax.ShapeDtypeStruct((n_seq, seq_len, width), F32)]
    out_specs = [head_spec]
    if want_state:
        out_shape.append(jax.ShapeDtypeStruct((n_seq, 2, N_REC_HEADS, REC_DK, REC_DV), F32))
        out_specs.append(state_spec)
    res = pl.pallas_call(
        functools.partial(_gla_kernel, n_tiles=n_tiles, has_state=has_state, want_state=want_state),
        grid=(n_seq // seqs, N_REC_HEADS // hps),
        in_specs=in_specs,
        out_specs=tuple(out_specs),
        out_shape=tuple(out_shape),
        scratch_shapes=[
            pltpu.VMEM((seqs, 2, hps, REC_DK, REC_DV), F32),
            pltpu.VMEM((n_units, GLA_TILE, 2 * GLA_TILE), BF16),
            pltpu.VMEM((n_units, 2 * REC_DK, GLA_TILE), F32),
        ],
        compiler_params=pltpu.CompilerParams(vmem_limit_bytes=VMEM_LIMIT),
        name="gla_state_in" if has_state else "gla_state_out",
    )(*args)
    o = res[0].reshape(n_seq * seq_len, width)
    return (o, res[1]) if want_state else (o, None)


def _rec_out_kernel(o_ref, sg_ref, x_ref, mod_ref, ng_ref, w_ref, lng_ref, lnb_ref, out_ref):
    d = D_MODEL
    gate = mod_ref[...][:, 2 * d:]

    def branch_of(rows):
        parts = []
        for hd in range(N_REC_HEADS):
            oh = o_ref[rows, hd * REC_DV:(hd + 1) * REC_DV]
            ms = jnp.mean(oh * oh, axis=-1, keepdims=True)
            parts.append(oh * lax.rsqrt(ms + NORM_EPS) * ng_ref[...])
        o = jnp.concatenate(parts, axis=1)
        gated = (o * sg_ref[rows, :].astype(F32)).astype(BF16)
        return _dot(gated, w_ref[...])

    chunks = [pl.ds(c * OUT_ROW_CHUNK, OUT_ROW_CHUNK) for c in range(o_ref.shape[0] // OUT_ROW_CHUNK)]
    branches = [branch_of(rows) for rows in chunks]
    for rows, branch in zip(chunks, branches):
        y = DEEPNORM_ALPHA * x_ref[rows, :] + gate * branch
        out_ref[rows, :] = _layer_norm(y, lng_ref[...], lnb_ref[...])


def _rec_out(o, sg, x2, mods4, norm_gain, w_out_bf, ln_g, ln_b, *, seq_len, mod_row0, per_seq_mod):
    n_tok = x2.shape[0]
    d = D_MODEL
    assert not per_seq_mod or seq_len % OUT_ROW_TILE == 0
    tiles_per_seq = max(1, seq_len // OUT_ROW_TILE)

    def mod_map(i):
        row = mod_row0 + (i // tiles_per_seq if per_seq_mod else 0)
        return (1, row, 0, 0)

    row_spec = pl.BlockSpec((OUT_ROW_TILE, d), lambda i: (i, 0))
    full = lambda i: (0, 0)
    return pl.pallas_call(
        _rec_out_kernel,
        grid=(n_tok // OUT_ROW_TILE,),
        in_specs=[
            row_spec, row_spec, row_spec,
            pl.BlockSpec((None, None, 1, 3 * d), mod_map),
            pl.BlockSpec(norm_gain.shape, full),
            pl.BlockSpec(w_out_bf.shape, full),
            pl.BlockSpec(ln_g.shape, full),
            pl.BlockSpec(ln_b.shape, full),
        ],
        out_specs=row_spec,
        out_shape=jax.ShapeDtypeStruct((n_tok, d), F32),
        compiler_params=pltpu.CompilerParams(vmem_limit_bytes=VMEM_LIMIT),
        name="rec_out",
    )(o, sg, x2, mods4, norm_gain, w_out_bf, ln_g, ln_b)


def _rope_tables(n_tokens):
    n_rows = n_tokens // GRID_W
    rows = np.repeat(np.arange(n_rows, dtype=np.float64), GRID_W)
    cols = np.tile(np.arange(GRID_W, dtype=np.float64), n_rows)
    inv_freq = 1.0 / (ROPE_THETA ** (np.arange(0, AXIS_DIM, 2, dtype=np.float64) / AXIS_DIM))
    ang_r = rows[:, None] * inv_freq[None, :]
    ang_c = cols[:, None] * inv_freq[None, :]
    ang = np.concatenate([ang_r, ang_r, ang_c, ang_c], axis=-1)
    cos, sin = np.cos(ang), np.sin(ang)
    first = (np.arange(HEAD_DIM) % AXIS_DIM) < AXIS_DIM // 2
    sin_a = np.where(first[None, :], -sin, 0.0)
    sin_b = np.where(first[None, :], 0.0, sin)
    reps = GROUP_LANES // HEAD_DIM
    return tuple(jnp.asarray(np.tile(t, (1, reps)), dtype=F32) for t in (cos, sin_a, sin_b))


def _head_mean_matrix():
    idx = np.arange(GROUP_LANES) // HEAD_DIM
    return jnp.asarray(np.where(idx[:, None] == idx[None, :], 1.0 / HEAD_DIM, 0.0), dtype=BF16)


def _gla_consts():
    t = GLA_TILE
    r = np.arange(t)[:, None]
    c = np.arange(t)[None, :]
    fw = [c <= r]
    bw = [c >= r]
    for half in GLA_LEVELS:
        start = 2 * half * (r // (2 * half))
        upper = r - start >= half
        last_low, first_up = start + half - 1, start + half
        fw.append(np.where(upper, (c > last_low) & (c <= r), (c > r) & (c <= last_low)))
        bw.append(np.where(upper, (c >= first_up) & (c < r), (c >= r) & (c < first_up)))
    as_bf16 = lambda a: jnp.asarray(a.astype(np.float32), dtype=BF16)
    twice = lambda a: np.concatenate([a, a], axis=-1)

    def summation(weights):
        cols = np.concatenate([w.T for w in weights], axis=1)
        return as_bf16(np.concatenate([cols, cols], axis=0))

    masks = [(r == c)]
    for half in GLA_LEVELS:
        size = 2 * half
        masks.append((r // size == c // size) & (r % size >= half) & (c % size < half))
    m_fw = np.stack(masks)
    m_bw = np.swapaxes(m_fw, 1, 2)
    return summation(fw), summation(bw), as_bf16(twice(m_fw)), as_bf16(twice(m_bw))


def kernel(x_prompt, x_sample, cache_k, cache_v, state_rec, c, c_ctx, ada_w, ada_b, attn_w_in,
           attn_q_gain, attn_k_gain, attn_w_out, rec_w_in, rec_lower_bounds, rec_norm_gain,
           rec_w_out, ln_gain, ln_bias):
    d = D_MODEL
    n_p, len_p, _ = x_prompt.shape
    n_s, len_s, _ = x_sample.shape
    past = cache_k.shape[2]
    assert past % SLAB_LANES == 0 and len_p % SLAB_LANES == 0 and len_s % SLAB_LANES == 0

    cond = jnp.zeros((COND_ROWS, d), F32).at[0].set(c_ctx).at[1:1 + n_s].set(c)
    mods4 = _mods(cond, ada_w, ada_b).reshape(DEPTH, COND_ROWS, 1, 3 * d)
    lb_all = _lower_bounds(rec_lower_bounds)

    xp = x_prompt.reshape(n_p * len_p, d)
    xs = x_sample.reshape(n_s * len_s, d)

    w_in = attn_w_in[0].astype(BF16)
    w_out = attn_w_out[0].astype(BF16)
    reps = GROUP_LANES // HEAD_DIM
    qg = (jnp.tile(attn_q_gain[0], reps) * (LOG2_E / math.sqrt(HEAD_DIM))).reshape(1, GROUP_LANES)
    kg = jnp.tile(attn_k_gain[0], reps).reshape(1, GROUP_LANES)
    pn = _head_mean_matrix()
    ln_g = ln_gain[0].reshape(1, d)
    ln_b = ln_bias[0].reshape(1, d)

    xp1, k_p, v_p = _attn_seq(xp, mods4, w_in, pn, qg, kg, w_out, ln_g, ln_b, n_seq=n_p,
                              seq_len=len_p, mod_row=0)
    q_s, kp_s, vt_s, g_s = _attn_proj(xs, mods4, w_in, pn, qg, kg, _rope_tables(len_s), seq_len=len_s,
                                      mod_row0=1)
    cache = (cache_k[:, 0].reshape(n_s, past, KV_WIDTH), cache_v[:, 0].reshape(n_s, past, KV_WIDTH))
    xs1 = _attn_core(q_s, kp_s, vt_s, cache, g_s, xs, mods4, w_out, ln_g, ln_b, n_seq=n_s,
                     seq_len=len_s, q_tile=ATTN_Q_TILE, mod_row0=1)
    new_cache_k = k_p.reshape(n_p, 1, len_p, N_KV_HEADS, HEAD_DIM)
    new_cache_v = v_p.reshape(n_p, 1, len_p, N_KV_HEADS, HEAD_DIM)

    rw_in = rec_w_in[0].astype(BF16)
    rw_out = rec_w_out[0].astype(BF16)
    lb = lb_all[1]
    ng = rec_norm_gain[0].reshape(1, REC_DV)
    ln_g = ln_gain[1].reshape(1, d)
    ln_b = ln_bias[1].reshape(1, d)
    consts = _gla_consts()

    outs = []
    states = None
    for x1, n_seq, seq_len, row0, per_seq, s0 in (
            (xp1, n_p, len_p, 0, False, None),
            (xs1, n_s, len_s, 1, True, state_rec[:, 0])):
        q, v, g, lf_fw, lf_bw, k_fw, k_bw = _rec_proj(x1, mods4, rw_in, lb, seq_len=seq_len,
                                                       mod_row0=row0, per_seq_mod=per_seq)
        o, st = _gla(q, v, lf_fw, lf_bw, k_fw, k_bw, s0, consts, n_seq=n_seq, seq_len=seq_len,
                     want_state=s0 is None)
        if st is not None:
            states = st
        outs.append(_rec_out(o, g, x1, mods4, ng, rw_out, ln_g, ln_b, seq_len=seq_len,
                             mod_row0=row0, per_seq_mod=per_seq))

    y_prompt = outs[0].reshape(n_p, len_p, d)
    y_sample = outs[1].reshape(n_s, len_s, d)
    new_state_rec = states.reshape(n_p, 1, 2, N_REC_HEADS, REC_DK, REC_DV)
    return (y_prompt, y_sample, new_cache_k, new_cache_v, new_state_rec)
```

```python
import functools
import math

import jax
import jax.numpy as jnp
import numpy as np
from jax import lax
from jax.experimental import pallas as pl
from jax.experimental.pallas import tpu as pltpu

F32 = jnp.float32
BF16 = jnp.bfloat16

D_MODEL = 1024
DEPTH = 2
GRID_W = 64
N_HEADS = 16
N_KV_HEADS = 4
HEAD_DIM = 64
AXIS_DIM = HEAD_DIM // 2
ATTN_WIDTH = N_HEADS * HEAD_DIM
KV_WIDTH = N_KV_HEADS * HEAD_DIM
ROPE_THETA = 10000.0
N_REC_HEADS = 8
REC_DK = 128
REC_DV = 128
REC_WIDTH = N_REC_HEADS * REC_DK
NORM_EPS = 1e-6
LN_EPS = 1e-5
DEEPNORM_ALPHA = (2.0 * DEPTH) ** 0.25

SUBLANES = 8
BF16_ROWS = 16
COND_ROWS = SUBLANES
ATTN_Q_TILE = 256
LATENT_TILES_PER_STEP = 2
ATTN_PROJ_ROW_TILE = 1024
REC_ROW_TILE = 512
OUT_ROW_TILE = 1024
OUT_ROW_CHUNK = 256
ATTN_SEQS_PER_STEP = 2
GROUP_LANES = 256
SLAB_LANES = 128
KPAD_WIDTH = 2 * N_KV_HEADS * SLAB_LANES
REC_PROJ_SLAB = 256
GLA_TILE = 128
GLA_LEVELS = (1, 2, 4, 8, 16, 32, 64)
GLA_HEADS_PER_STEP = 4
GLA_ITEMS_PER_STEP = 8
LOG2_E = 1.4426950408889634
MOD_PARTS = 3
VMEM_LIMIT = 56 * 1024 * 1024
RESIDENT = pl.Buffered(1)


def _sigmoid(x):
    return 1.0 / (1.0 + jnp.exp(-x))


def _dot(a, b):
    return jnp.dot(a, b, preferred_element_type=F32)


def _dot_nt(a, b):
    return lax.dot_general(a, b, (((1,), (1,)), ((), ())), preferred_element_type=F32)


def _dot_tn(a, b):
    return lax.dot_general(a, b, (((0,), (0,)), ((), ())), preferred_element_type=F32)


def _layer_norm(y, g, b):
    mu = jnp.mean(y, axis=-1, keepdims=True)
    yc = y - mu
    var = jnp.mean(yc * yc, axis=-1, keepdims=True)
    return yc * lax.rsqrt(var + LN_EPS) * g + b


def _mods_kernel(cond_ref, w_ref, b_ref, out_ref):
    c = cond_ref[...]
    s = (c * _sigmoid(c)).astype(BF16)
    out_ref[...] = _dot(s, w_ref[...].astype(BF16)) + b_ref[...]


def _mods(cond, ada_w, ada_b):
    d = D_MODEL
    return pl.pallas_call(
        _mods_kernel,
        grid=(DEPTH, MOD_PARTS),
        in_specs=[
            pl.BlockSpec((COND_ROWS, d), lambda l, j: (0, 0)),
            pl.BlockSpec((None, d, d), lambda l, j: (l, 0, j)),
            pl.BlockSpec((None, 1, d), lambda l, j: (l, 0, j)),
        ],
        out_specs=pl.BlockSpec((None, COND_ROWS, d), lambda l, j: (l, 0, j)),
        out_shape=jax.ShapeDtypeStruct((DEPTH, COND_ROWS, MOD_PARTS * d), F32),
        compiler_params=pltpu.CompilerParams(vmem_limit_bytes=VMEM_LIMIT),
        name="adaln_mods",
    )(cond, ada_w, ada_b.reshape(DEPTH, 1, MOD_PARTS * d))


def _lower_bounds_kernel(r_ref, out_ref):
    r = [r_ref[i] for i in range(DEPTH)]
    m = functools.reduce(jnp.maximum, r)
    e = [jnp.exp(x - m) for x in r]
    tot = functools.reduce(lambda a, b: a + b, e)
    soft = [x / tot for x in e]
    acc = soft[0]
    for i in range(DEPTH):
        if i > 0:
            acc = acc + soft[i]
        out_ref[i] = acc - soft[0]


def _lower_bounds(rec_lower_bounds):
    return pl.pallas_call(
        _lower_bounds_kernel,
        out_shape=jax.ShapeDtypeStruct(rec_lower_bounds.shape, F32),
        name="rec_lower_bounds",
    )(rec_lower_bounds)


def _kv_head_slabs(k):
    lane = lax.broadcasted_iota(jnp.int32, (k.shape[0], SLAB_LANES), 1)
    low = lane < HEAD_DIM
    heads_per_slab = SLAB_LANES // HEAD_DIM
    out = []
    for j in range(N_KV_HEADS):
        tile = k[:, (j // heads_per_slab) * SLAB_LANES:(j // heads_per_slab + 1) * SLAB_LANES]
        moved = pltpu.roll(tile, HEAD_DIM, 1)
        at_low, at_high = (tile, moved) if j % heads_per_slab == 0 else (moved, tile)
        out.append(jnp.where(low, at_low, 0.0))
        out.append(jnp.where(low, 0.0, at_high))
    return jnp.concatenate(out, axis=1)


def _head_norm(u, pn, gain):
    ms = _dot((u * u).astype(BF16), pn)
    return u * lax.rsqrt(ms + NORM_EPS) * gain


def _attn_proj_kernel(x_ref, mod_ref, w_ref, pn_ref, qg_ref, kg_ref, cos_ref, sa_ref, sb_ref,
                      q_out, kp_out, vt_out, g_out):
    d = D_MODEL
    mod = mod_ref[...]
    shift, scale = mod[:, :d], mod[:, d:2 * d]
    h = (x_ref[...] * (1.0 + scale) + shift).astype(BF16)
    pn = pn_ref[...]

    def norm_rope(u, gain):
        y = _head_norm(u, pn, gain)
        return (y * cos_ref[...]
                + pltpu.roll(y, GROUP_LANES - AXIS_DIM // 2, 1) * sa_ref[...]
                + pltpu.roll(y, AXIS_DIM // 2, 1) * sb_ref[...])

    def finish_q(j, u):
        q_out[:, j * GROUP_LANES:(j + 1) * GROUP_LANES] = norm_rope(u, qg_ref[...]).astype(q_out.dtype)

    def finish_k(_, u):
        kp_out[...] = _kv_head_slabs(norm_rope(u, kg_ref[...])).astype(kp_out.dtype)

    def finish_v(_, v):
        vt_out[...] = v.T.astype(vt_out.dtype)

    def finish_g(j, u):
        g_out[:, j * GROUP_LANES:(j + 1) * GROUP_LANES] = (u * _sigmoid(u)).astype(g_out.dtype)

    slabs = ([(finish_q, j) for j in range(N_KV_HEADS)] + [(finish_k, 0), (finish_v, 0)]
             + [(finish_g, j) for j in range(N_KV_HEADS)])
    project = lambda i: _dot(h, w_ref[:, i * GROUP_LANES:(i + 1) * GROUP_LANES])
    v_slab = N_KV_HEADS + 1
    order = [i for i in range(len(slabs)) if i != v_slab] + [v_slab]
    u_next = project(order[0])
    for n, i in enumerate(order):
        u = u_next
        if n + 1 < len(order):
            u_next = project(order[n + 1])
        finish, j = slabs[i]
        finish(j, u)


def _attn_proj(x2, mods4, w_bf, pn, qg, kg, rope_tabs, *, seq_len, mod_row0):
    n_tok = x2.shape[0]
    d = D_MODEL
    tile = ATTN_PROJ_ROW_TILE
    assert seq_len % tile == 0
    tiles_per_seq = seq_len // tile
    full = lambda i: (0, 0)
    rows = lambda width: pl.BlockSpec((tile, width), lambda i: (i, 0))
    table = pl.BlockSpec((tile, GROUP_LANES), lambda i: (i % tiles_per_seq, 0))
    consts = [w_bf, pn, qg, kg]
    return pl.pallas_call(
        _attn_proj_kernel,
        grid=(n_tok // tile,),
        in_specs=[rows(d),
                  pl.BlockSpec((None, None, 1, 3 * d), lambda i: (0, mod_row0 + i // tiles_per_seq, 0, 0))]
        + [pl.BlockSpec(c.shape, full, pipeline_mode=RESIDENT) for c in consts]
        + [table] * len(rope_tabs),
        out_specs=(
            rows(ATTN_WIDTH),
            rows(KPAD_WIDTH),
            pl.BlockSpec((None, KV_WIDTH, tile), lambda i: (i // tiles_per_seq, 0, i % tiles_per_seq)),
            rows(ATTN_WIDTH),
        ),
        out_shape=(
            jax.ShapeDtypeStruct((n_tok, ATTN_WIDTH), BF16),
            jax.ShapeDtypeStruct((n_tok, KPAD_WIDTH), BF16),
            jax.ShapeDtypeStruct((n_tok // seq_len, KV_WIDTH, seq_len), BF16),
            jax.ShapeDtypeStruct((n_tok, ATTN_WIDTH), BF16),
        ),
        compiler_params=pltpu.CompilerParams(vmem_limit_bytes=VMEM_LIMIT),
        name="attn_proj_rope",
    )(x2, mods4, *consts, *rope_tabs)


def _attention_phases(q_ref, kp_ref, vt_ref, cache, s_scr, p_scr, ot_scr, between=()):
    n_new = kp_ref.shape[0]
    n_old = cache[0].shape[0] if cache is not None else 0
    n_keys = n_old + n_new
    group = N_HEADS // N_KV_HEADS
    heads_per_slab = SLAB_LANES // HEAD_DIM
    pending_work = list(between)

    def fold_rows(a, op):
        return functools.reduce(
            op, [a[r * SUBLANES:(r + 1) * SUBLANES, :] for r in range(a.shape[0] // SUBLANES)])

    def scores_phase(j):
        slot = j % 2
        maxima = []
        for hh in range(group):
            hd = j * group + hh
            slab, where = hd // heads_per_slab, hd % heads_per_slab
            q_slab = q_ref[:, slab * SLAB_LANES:(slab + 1) * SLAB_LANES]
            k_slab = heads_per_slab * j + where
            lanes = slice(k_slab * SLAB_LANES, (k_slab + 1) * SLAB_LANES)
            s = _dot_nt(kp_ref[:, lanes], q_slab)
            s_scr[slot, hh, n_old:, :] = s
            mx = fold_rows(s, jnp.maximum)
            if cache is not None:
                s = _dot_nt(cache[0][:, lanes], q_slab)
                s_scr[slot, hh, :n_old, :] = s
                mx = jnp.maximum(mx, fold_rows(s, jnp.maximum))
            maxima.append(jnp.max(mx, axis=0, keepdims=True))
        return maxima

    def values_phase(j, maxima):
        slot = j % 2
        rows = slice(j * HEAD_DIM, (j + 1) * HEAD_DIM)
        v_t = vt_ref[rows, :]
        if cache is not None:
            v_t = jnp.concatenate([cache[1][rows, :], v_t], axis=1)
        v_ext = jnp.concatenate([v_t, jnp.ones((BF16_ROWS, n_keys), BF16)], axis=0)
        for hh in range(group):
            hd = j * group + hh
            p_scr[hh % 2] = jnp.exp2(s_scr[slot, hh] - maxima[hh]).astype(BF16)
            acc = _dot(v_ext, p_scr[hh % 2])
            l = acc[HEAD_DIM:HEAD_DIM + 1, :]
            ot_scr[hd * HEAD_DIM:(hd + 1) * HEAD_DIM, :] = acc[:HEAD_DIM, :] * (1.0 / l)

    pending = scores_phase(0)
    for j in range(N_KV_HEADS):
        nxt = scores_phase(j + 1) if j + 1 < N_KV_HEADS else None
        if pending_work:
            pending_work.pop(0)()
        values_phase(j, pending)
        pending = nxt
    for thunk in pending_work:
        thunk()


def _branch_epilogue(ot_scr, sg, x_ref, mod_ref, w_ref, lng_ref, lnb_ref, out_ref):
    d = D_MODEL
    o = ot_scr[...].T
    gated = (o * sg.astype(F32)).astype(BF16)
    branch = _dot(gated, w_ref[...])
    gate = mod_ref[...][:, 2 * d:]
    y = DEEPNORM_ALPHA * x_ref[...] + gate * branch
    out_ref[...] = _layer_norm(y, lng_ref[...], lnb_ref[...])


def _attn_core_kernel(q_ref, kp_ref, vt_ref, kc_ref, vc_ref, sg_ref, x_ref, mod_ref, w_ref,
                      lng_ref, lnb_ref, out_ref, s_scr, p_scr, ot_scr):
    cache = (_kv_head_slabs(kc_ref[...]).astype(BF16), vc_ref[...].T.astype(BF16))
    tq = ot_scr.shape[2]

    def epilogue(t):
        rows = pl.ds(t * tq, tq)
        _branch_epilogue(ot_scr.at[t], sg_ref[rows, :], x_ref.at[rows], mod_ref, w_ref, lng_ref,
                         lnb_ref, out_ref.at[rows])

    n_sub = ot_scr.shape[0]
    for t in range(n_sub):
        between = [functools.partial(epilogue, t - 1)] if t > 0 else []
        _attention_phases(q_ref.at[pl.ds(t * tq, tq)], kp_ref, vt_ref, cache, s_scr, p_scr,
                          ot_scr.at[t], between=between)
    epilogue(n_sub - 1)


def _attn_seq_kernel(x_ref, mod_ref, w_in_ref, pn_ref, qg_ref, kg_ref, w_out_ref, lng_ref, lnb_ref,
                     out_ref, k_out, v_out,
                     h_scr, q_scr, kp_scr, vt_scr, sg_scr, s_scr, p_scr, ot_scr):
    d = D_MODEL
    n_sub = x_ref.shape[0]
    mod = mod_ref[...]
    shift, scale = mod[:, :d], mod[:, d:2 * d]
    pn = pn_ref[...]
    project = lambda i: _dot(h_scr[...], w_in_ref[:, i * GROUP_LANES:(i + 1) * GROUP_LANES])

    def epilogue(s):
        _branch_epilogue(ot_scr.at[s], sg_scr[s], x_ref.at[s], mod_ref, w_out_ref, lng_ref, lnb_ref,
                         out_ref.at[s])

    for s in range(n_sub):
        h_scr[...] = (x_ref[s] * (1.0 + scale) + shift).astype(BF16)

        def finish_q(j, u):
            q_scr[:, j * GROUP_LANES:(j + 1) * GROUP_LANES] = _head_norm(u, pn, qg_ref[...]).astype(BF16)

        def finish_k(_, u, s=s):
            k = _head_norm(u, pn, kg_ref[...])
            kp_scr[...] = _kv_head_slabs(k).astype(BF16)
            k_out[s] = k.reshape(k_out.shape[1:])

        def finish_v(_, v, s=s):
            vt_scr[...] = v.T.astype(BF16)
            v_out[s] = v.reshape(v_out.shape[1:])

        def finish_g(j, u, s=s):
            sg_scr[s, :, j * GROUP_LANES:(j + 1) * GROUP_LANES] = (u * _sigmoid(u)).astype(BF16)

        slabs = [(finish_q, j) for j in range(N_KV_HEADS)] + [(finish_k, 0), (finish_v, 0)]
        u_next = project(0)
        if s > 0:
            epilogue(s - 1)
        for i, (finish, j) in enumerate(slabs):
            u = u_next
            if i + 1 < len(slabs):
                u_next = project(i + 1)
            finish(j, u)
        gate_slabs = [functools.partial(lambda j, g: g(j, project(len(slabs) + j)), j, finish_g)
                      for j in range(N_KV_HEADS)]
        _attention_phases(q_scr, kp_scr, vt_scr, None, s_scr, p_scr, ot_scr.at[s], between=gate_slabs)
    epilogue(n_sub - 1)


def _attn_seq(x2, mods4, w_in_bf, pn, qg, kg, w_out_bf, ln_g, ln_b, *, n_seq, seq_len, mod_row):
    d = D_MODEL
    n_sub = ATTN_SEQS_PER_STEP
    assert n_seq % n_sub == 0
    full = lambda b: (0, 0)
    rows = pl.BlockSpec((n_sub, seq_len, d), lambda b: (b, 0, 0))
    heads = pl.BlockSpec((n_sub, seq_len, N_KV_HEADS, HEAD_DIM), lambda b: (b, 0, 0, 0))
    consts = [w_in_bf, pn, qg, kg, w_out_bf, ln_g, ln_b]
    group = N_HEADS // N_KV_HEADS
    out, k, v = pl.pallas_call(
        _attn_seq_kernel,
        grid=(n_seq // n_sub,),
        in_specs=[rows, pl.BlockSpec((None, None, 1, 3 * d), lambda b: (0, mod_row, 0, 0))]
        + [pl.BlockSpec(c.shape, full, pipeline_mode=RESIDENT) for c in consts],
        out_specs=(rows, heads, heads),
        out_shape=(jax.ShapeDtypeStruct((n_seq, seq_len, d), F32),
                   jax.ShapeDtypeStruct((n_seq, seq_len, N_KV_HEADS, HEAD_DIM), F32),
                   jax.ShapeDtypeStruct((n_seq, seq_len, N_KV_HEADS, HEAD_DIM), F32)),
        scratch_shapes=[
            pltpu.VMEM((seq_len, d), BF16),
            pltpu.VMEM((seq_len, ATTN_WIDTH), BF16),
            pltpu.VMEM((seq_len, KPAD_WIDTH), BF16),
            pltpu.VMEM((KV_WIDTH, seq_len), BF16),
            pltpu.VMEM((n_sub, seq_len, ATTN_WIDTH), BF16),
            pltpu.VMEM((2, group, seq_len, seq_len), F32),
            pltpu.VMEM((2, seq_len, seq_len), BF16),
            pltpu.VMEM((n_sub, ATTN_WIDTH, seq_len), F32),
        ],
        compiler_params=pltpu.CompilerParams(vmem_limit_bytes=VMEM_LIMIT),
        name="attn_seq",
    )(x2.reshape(n_seq, seq_len, d), mods4, *consts)
    return out.reshape(n_seq * seq_len, d), k, v


def _attn_core(q, kp, vt, cache, sg, x2, mods4, w_out_bf, ln_g, ln_b, *, n_seq, seq_len, q_tile,
               mod_row0):
    d = D_MODEL
    n_sub = LATENT_TILES_PER_STEP
    rows = n_sub * q_tile
    tiles = seq_len // rows
    ck, cv = cache
    n_keys = seq_len + ck.shape[1]
    q3 = q.reshape(n_seq, seq_len, ATTN_WIDTH)
    kp3 = kp.reshape(n_seq, seq_len, KPAD_WIDTH)
    sg3 = sg.reshape(n_seq, seq_len, ATTN_WIDTH)
    x3 = x2.reshape(n_seq, seq_len, d)
    tile_spec = lambda width: pl.BlockSpec((None, rows, width), lambda b, i: (b, i, 0))
    seq_spec = lambda length, width: pl.BlockSpec((None, length, width), lambda b, i: (b, 0, 0))
    full2 = lambda b, i: (0, 0)
    out = pl.pallas_call(
        _attn_core_kernel,
        grid=(n_seq, tiles),
        in_specs=[
            tile_spec(ATTN_WIDTH), seq_spec(seq_len, KPAD_WIDTH), seq_spec(KV_WIDTH, seq_len),
            seq_spec(ck.shape[1], KV_WIDTH), seq_spec(cv.shape[1], KV_WIDTH),
            tile_spec(ATTN_WIDTH), tile_spec(d),
            pl.BlockSpec((None, None, 1, 3 * d), lambda b, i: (0, mod_row0 + b, 0, 0)),
            pl.BlockSpec(w_out_bf.shape, full2, pipeline_mode=RESIDENT),
            pl.BlockSpec(ln_g.shape, full2),
            pl.BlockSpec(ln_b.shape, full2),
        ],
        out_specs=tile_spec(d),
        out_shape=jax.ShapeDtypeStruct((n_seq, seq_len, d), F32),
        scratch_shapes=[
            pltpu.VMEM((2, N_HEADS // N_KV_HEADS, n_keys, q_tile), F32),
            pltpu.VMEM((2, n_keys, q_tile), BF16),
            pltpu.VMEM((n_sub, ATTN_WIDTH, q_tile), F32),
        ],
        compiler_params=pltpu.CompilerParams(vmem_limit_bytes=VMEM_LIMIT),
        name="attn_core_cache",
    )(q3, kp3, vt, ck, cv, sg3, x3, mods4, w_out_bf, ln_g, ln_b)
    return out.reshape(n_seq * seq_len, d)


def _rec_proj_kernel(x_ref, mod_ref, w_ref, lb_ref, q_out, v_out, g_out, lf_fw, lf_bw, k_fw, k_bw):
    d = D_MODEL
    mod = mod_ref[...]
    shift, scale = mod[:, :d], mod[:, d:2 * d]
    h = (x_ref[...] * (1.0 + scale) + shift).astype(BF16)
    slab = REC_PROJ_SLAB

    def store_channel_major(out_ref, cols, val):
        val_t = val.T.astype(out_ref.dtype)
        tiles = out_ref.shape[1]
        for i in range(val.shape[0] // GLA_TILE):
            out_ref[i // tiles, i % tiles, cols, :] = val_t[:, i * GLA_TILE:(i + 1) * GLA_TILE]

    def finish_q(cols, u):
        store_channel_major(q_out, cols, u * _sigmoid(u))

    def finish_gate(direction, lf_out, k_out):
        def finish(cols, z):
            lb = lb_ref[direction:direction + 1, cols]
            sig = _sigmoid(z)
            lf_out[:, cols] = jnp.log(lb + (1.0 - lb) * sig)
            store_channel_major(k_out, cols, (1.0 - lb) * (1.0 - sig))
        return finish

    def finish_v(cols, u):
        v_out[:, cols] = u.astype(v_out.dtype)

    def finish_g(cols, u):
        g_out[:, cols] = (u * _sigmoid(u)).astype(g_out.dtype)

    sections = (finish_q, finish_gate(0, lf_fw, k_fw), finish_gate(1, lf_bw, k_bw),
                finish_v, finish_g)
    per_section = REC_WIDTH // slab
    n_slabs = len(sections) * per_section
    project = lambda i: _dot(h, w_ref[:, i * slab:(i + 1) * slab])
    v_section = sections.index(finish_v)
    order = ([i for i in range(n_slabs) if i // per_section != v_section]
             + [i for i in range(n_slabs) if i // per_section == v_section])
    u_next = project(order[0])
    for n, i in enumerate(order):
        u = u_next
        if n + 1 < n_slabs:
            u_next = project(order[n + 1])
        within = i % per_section
        sections[i // per_section](slice(within * slab, (within + 1) * slab), u)


def _rec_proj(x2, mods4, w_bf, lb, *, seq_len, mod_row0, per_seq_mod):
    n_tok = x2.shape[0]
    d = D_MODEL
    rows = REC_ROW_TILE
    steps_per_seq = max(1, seq_len // rows)
    seqs = max(1, rows // seq_len)
    assert rows % GLA_TILE == 0 and (seq_len % rows == 0 or rows % seq_len == 0)
    assert not per_seq_mod or seqs == 1

    def mod_map(i):
        row = mod_row0 + (i // steps_per_seq if per_seq_mod else 0)
        return (1, row, 0, 0)

    row_spec = pl.BlockSpec((rows, REC_WIDTH), lambda i: (i, 0))
    tile_spec = pl.BlockSpec((seqs, rows // seqs // GLA_TILE, REC_WIDTH, GLA_TILE),
                             lambda i: (i // steps_per_seq, i % steps_per_seq, 0, 0))
    full = lambda i: (0, 0)
    row_out = lambda t: (row_spec, jax.ShapeDtypeStruct((n_tok, REC_WIDTH), t))
    tile_out = (tile_spec, jax.ShapeDtypeStruct(
        (n_tok // seq_len, seq_len // GLA_TILE, REC_WIDTH, GLA_TILE), BF16))
    outs = (tile_out, row_out(BF16), row_out(BF16), row_out(F32), row_out(F32), tile_out, tile_out)
    return pl.pallas_call(
        _rec_proj_kernel,
        grid=(n_tok // rows,),
        in_specs=[
            pl.BlockSpec((rows, d), lambda i: (i, 0)),
            pl.BlockSpec((None, None, 1, 3 * d), mod_map),
            pl.BlockSpec(w_bf.shape, full, pipeline_mode=RESIDENT),
            pl.BlockSpec(lb.shape, full),
        ],
        out_specs=tuple(spec for spec, _ in outs),
        out_shape=tuple(shape for _, shape in outs),
        compiler_params=pltpu.CompilerParams(vmem_limit_bytes=VMEM_LIMIT),
        name="rec_proj",
    )(x2, mods4, w_bf, lb)


def _block_diag(a, b):
    za = jnp.zeros(a.shape, a.dtype)
    return jnp.concatenate(
        [jnp.concatenate([a, za], axis=1), jnp.concatenate([za, b], axis=1)], axis=0)


def _pair_scores(a_t, c_t):
    return _dot_tn(a_t, _block_diag(c_t[:REC_DK, :], c_t[REC_DK:, :]))


def _gla_scores_stage(units):
    t = GLA_TILE
    for u in units:
        lf2 = u["lf"]() * LOG2_E
        hi = lf2.astype(BF16)
        lo = (lf2 - hi.astype(F32)).astype(BF16)
        u["sums"] = _dot_tn(jnp.concatenate([hi, lo], axis=0), u["sums_ref"][...])
        u["scores"] = u["masks_ref"][0] * _pair_scores(u["q_t"](), u["k_t"]()).astype(BF16)
    for li in range(len(GLA_LEVELS)):
        for u in units:
            x = jnp.exp2(u["sums"][:, (1 + li) * t:(2 + li) * t]).astype(BF16)
            z = _pair_scores(u["q_t"]() * x, u["k_t"]() * x).astype(BF16)
            u["scores"] = u["scores"] + u["masks_ref"][1 + li] * z
    for u in units:
        u["save"](u["scores"], u["sums"][:, :t])


def _gla_state_stage(u):
    t = GLA_TILE
    dk = REC_DK
    b_t, st_ref = u["load_b"](), u["st_ref"]
    q_t, k_t, v = u["q_t"](), u["k_t"](), u["v"]()
    edge = b_t[:, 0:1] if u["backward"] else b_t[:, t - 1:t]
    o = _dot(u["load_scores"](), _block_diag(v[:, :dk], v[:, dk:]))
    st_a, st_b = st_ref[0], st_ref[1]
    q_in = q_t * jnp.exp2(b_t).astype(BF16)
    o = o + _dot_tn(q_in, _block_diag(st_a.astype(BF16), st_b.astype(BF16)))
    k_edge = k_t * jnp.exp2(edge - b_t).astype(BF16)
    carry = jnp.exp2(edge)
    st_ref[0] = st_a * carry[:dk, :] + _dot(k_edge[:dk, :], v[:, :dk])
    st_ref[1] = st_b * carry[dk:, :] + _dot(k_edge[dk:, :], v[:, dk:])
    u["store"](o)


def _gla_kernel(*refs, n_tiles, has_state, want_state):
    refs = list(refs)
    q_ref, v_ref, lff_ref, lfb_ref, kf_ref, kb_ref = refs[:6]
    pos = 6
    if has_state:
        s0_ref = refs[pos]
        pos += 1
    sums_f_ref, sums_b_ref, mf_ref, mb_ref = refs[pos:pos + 4]
    pos += 4
    o_ref = refs[pos]
    pos += 1
    if want_state:
        s_out_ref = refs[pos]
        pos += 1
    st_ref, pipe_s, pipe_b = refs[pos:pos + 3]

    t = GLA_TILE
    n_seqs = v_ref.shape[0]
    n_items = n_seqs * n_tiles
    pair_lanes = 2 * REC_DK
    o_ref[...] = jnp.zeros(o_ref.shape, o_ref.dtype)
    if has_state:
        st_ref[...] = s0_ref[...]
    else:
        st_ref[...] = jnp.zeros(st_ref.shape, st_ref.dtype)

    def units_of(item):
        seq, step = item // n_tiles, item % n_tiles
        units = []
        for direction, (lf_ref, k_ref, sums_ref, m_ref) in enumerate(
                ((lff_ref, kf_ref, sums_f_ref, mf_ref), (lfb_ref, kb_ref, sums_b_ref, mb_ref))):
            tile = step if direction == 0 else n_tiles - 1 - step
            rows = pl.ds(pl.multiple_of(tile * t, t), t)
            for pair in range(GLA_HEADS_PER_STEP // 2):
                lanes = slice(pair * pair_lanes, (pair + 1) * pair_lanes)
                ui = len(units)

                def store(o, rows=rows, lanes=lanes):
                    o_ref[seq, rows, lanes] += o

                def save(scores, b_t, ui=ui):
                    pipe_s[ui] = scores
                    pipe_b[ui] = b_t

                rows_of = lambda ref, rows=rows, lanes=lanes: (lambda: ref[seq, rows, lanes])
                channels_of = lambda ref, tile=tile, lanes=lanes: (lambda: ref[seq, tile, lanes, :])
                units.append(dict(
                    q_t=channels_of(q_ref), k_t=channels_of(k_ref), v=rows_of(v_ref),
                    lf=rows_of(lf_ref), st_ref=st_ref.at[seq, direction, pl.ds(2 * pair, 2)],
                    sums_ref=sums_ref, masks_ref=m_ref, backward=direction == 1, store=store,
                    save=save, load_scores=lambda ui=ui: pipe_s[ui], load_b=lambda ui=ui: pipe_b[ui]))
        return units

    def body(item, carry):
        units = units_of(item)
        _gla_scores_stage(units)
        for u in units:
            _gla_state_stage(u)
        return carry

    lax.fori_loop(0, n_items, body, 0)
    if want_state:
        s_out_ref[...] = st_ref[...]


def _gla(q, v, lf_fw, lf_bw, k_fw, k_bw, s0, consts, *, n_seq, seq_len, want_state):
    has_state = s0 is not None
    width = REC_WIDTH
    hps = GLA_HEADS_PER_STEP
    n_tiles = seq_len // GLA_TILE
    seqs = min(n_seq, max(1, GLA_ITEMS_PER_STEP // n_tiles))
    assert n_seq % seqs == 0
    n_units = hps
    seq3 = lambda a: a.reshape(n_seq, seq_len, width)
    head_spec = pl.BlockSpec((seqs, seq_len, hps * REC_DK), lambda b, h: (b, 0, h))
    tile_spec = pl.BlockSpec((seqs, n_tiles, hps * REC_DK, GLA_TILE), lambda b, h: (b, 0, h, 0))
    state_spec = pl.BlockSpec((seqs, 2, hps, REC_DK, REC_DV), lambda b, h: (b, 0, h, 0, 0))
    in_specs = [tile_spec, head_spec, head_spec, head_spec, tile_spec, tile_spec]
    args = [q, seq3(v), seq3(lf_fw), seq3(lf_bw), k_fw, k_bw]
    if has_state:
        in_specs.append(state_spec)
        args.append(s0)
    for c in consts:
        in_specs.append(pl.BlockSpec(c.shape, lambda b, h, nd=c.ndim: (0,) * nd, pipeline_mode=RESIDENT))
        args.append(c)
    out_shape = [jax.ShapeDtypeStruct((n_seq, seq_len, width), F32)]
    out_specs = [head_spec]
    if want_state:
        out_shape.append(jax.ShapeDtypeStruct((n_seq, 2, N_REC_HEADS, REC_DK, REC_DV), F32))
        out_specs.append(state_spec)
    res = pl.pallas_call(
        functools.partial(_gla_kernel, n_tiles=n_tiles, has_state=has_state, want_state=want_state),
        grid=(n_seq // seqs, N_REC_HEADS // hps),
        in_specs=in_specs,
        out_specs=tuple(out_specs),
        out_shape=tuple(out_shape),
        scratch_shapes=[
            pltpu.VMEM((seqs, 2, hps, REC_DK, REC_DV), F32),
            pltpu.VMEM((n_units, GLA_TILE, 2 * GLA_TILE), BF16),
            pltpu.VMEM((n_units, 2 * REC_DK, GLA_TILE), F32),
        ],
        compiler_params=pltpu.CompilerParams(vmem_limit_bytes=VMEM_LIMIT),
        name="gla_state_in" if has_state else "gla_state_out",
    )(*args)
    o = res[0].reshape(n_seq * seq_len, width)
    return (o, res[1]) if want_state else (o, None)


def _rec_out_kernel(o_ref, sg_ref, x_ref, mod_ref, ng_ref, w_ref, lng_ref, lnb_ref, out_ref):
    d = D_MODEL
    gate = mod_ref[...][:, 2 * d:]

    def branch_of(rows):
        parts = []
        for hd in range(N_REC_HEADS):
            oh = o_ref[rows, hd * REC_DV:(hd + 1) * REC_DV]
            ms = jnp.mean(oh * oh, axis=-1, keepdims=True)
            parts.append(oh * lax.rsqrt(ms + NORM_EPS) * ng_ref[...])
        o = jnp.concatenate(parts, axis=1)
        gated = (o * sg_ref[rows, :].astype(F32)).astype(BF16)
        return _dot(gated, w_ref[...])

    chunks = [pl.ds(c * OUT_ROW_CHUNK, OUT_ROW_CHUNK) for c in range(o_ref.shape[0] // OUT_ROW_CHUNK)]
    branches = [branch_of(rows) for rows in chunks]
    for rows, branch in zip(chunks, branches):
        y = DEEPNORM_ALPHA * x_ref[rows, :] + gate * branch
        out_ref[rows, :] = _layer_norm(y, lng_ref[...], lnb_ref[...])


def _rec_out(o, sg, x2, mods4, norm_gain, w_out_bf, ln_g, ln_b, *, seq_len, mod_row0, per_seq_mod):
    n_tok = x2.shape[0]
    d = D_MODEL
    assert not per_seq_mod or seq_len % OUT_ROW_TILE == 0
    tiles_per_seq = max(1, seq_len // OUT_ROW_TILE)

    def mod_map(i):
        row = mod_row0 + (i // tiles_per_seq if per_seq_mod else 0)
        return (1, row, 0, 0)

    row_spec = pl.BlockSpec((OUT_ROW_TILE, d), lambda i: (i, 0))
    full = lambda i: (0, 0)
    return pl.pallas_call(
        _rec_out_kernel,
        grid=(n_tok // OUT_ROW_TILE,),
        in_specs=[
            row_spec, row_spec, row_spec,
            pl.BlockSpec((None, None, 1, 3 * d), mod_map),
            pl.BlockSpec(norm_gain.shape, full),
            pl.BlockSpec(w_out_bf.shape, full, pipeline_mode=RESIDENT),
            pl.BlockSpec(ln_g.shape, full),
            pl.BlockSpec(ln_b.shape, full),
        ],
        out_specs=row_spec,
        out_shape=jax.ShapeDtypeStruct((n_tok, d), F32),
        compiler_params=pltpu.CompilerParams(vmem_limit_bytes=VMEM_LIMIT),
        name="rec_out",
    )(o, sg, x2, mods4, norm_gain, w_out_bf, ln_g, ln_b)


def _rope_tables(n_tokens):
    n_rows = n_tokens // GRID_W
    rows = np.repeat(np.arange(n_rows, dtype=np.float64), GRID_W)
    cols = np.tile(np.arange(GRID_W, dtype=np.float64), n_rows)
    inv_freq = 1.0 / (ROPE_THETA ** (np.arange(0, AXIS_DIM, 2, dtype=np.float64) / AXIS_DIM))
    ang_r = rows[:, None] * inv_freq[None, :]
    ang_c = cols[:, None] * inv_freq[None, :]
    ang = np.concatenate([ang_r, ang_r, ang_c, ang_c], axis=-1)
    cos, sin = np.cos(ang), np.sin(ang)
    first = (np.arange(HEAD_DIM) % AXIS_DIM) < AXIS_DIM // 2
    sin_a = np.where(first[None, :], -sin, 0.0)
    sin_b = np.where(first[None, :], 0.0, sin)
    reps = GROUP_LANES // HEAD_DIM
    return tuple(jnp.asarray(np.tile(t, (1, reps)), dtype=F32) for t in (cos, sin_a, sin_b))


def _head_mean_matrix():
    idx = np.arange(GROUP_LANES) // HEAD_DIM
    return jnp.asarray(np.where(idx[:, None] == idx[None, :], 1.0 / HEAD_DIM, 0.0), dtype=BF16)


def _gla_consts():
    t = GLA_TILE
    r = np.arange(t)[:, None]
    c = np.arange(t)[None, :]
    fw = [c <= r]
    bw = [c >= r]
    for half in GLA_LEVELS:
        start = 2 * half * (r // (2 * half))
        upper = r - start >= half
        last_low, first_up = start + half - 1, start + half
        fw.append(np.where(upper, (c > last_low) & (c <= r), (c > r) & (c <= last_low)))
        bw.append(np.where(upper, (c >= first_up) & (c < r), (c >= r) & (c < first_up)))
    as_bf16 = lambda a: jnp.asarray(a.astype(np.float32), dtype=BF16)
    twice = lambda a: np.concatenate([a, a], axis=-1)

    def summation(weights):
        cols = np.concatenate([w.T for w in weights], axis=1)
        return as_bf16(np.concatenate([cols, cols], axis=0))

    masks = [(r == c)]
    for half in GLA_LEVELS:
        size = 2 * half
        masks.append((r // size == c // size) & (r % size >= half) & (c % size < half))
    m_fw = np.stack(masks)
    m_bw = np.swapaxes(m_fw, 1, 2)
    return summation(fw), summation(bw), as_bf16(twice(m_fw)), as_bf16(twice(m_bw))


def kernel(x_prompt, x_sample, cache_k, cache_v, state_rec, c, c_ctx, ada_w, ada_b, attn_w_in,
           attn_q_gain, attn_k_gain, attn_w_out, rec_w_in, rec_lower_bounds, rec_norm_gain,
           rec_w_out, ln_gain, ln_bias):
    d = D_MODEL
    n_p, len_p, _ = x_prompt.shape
    n_s, len_s, _ = x_sample.shape
    past = cache_k.shape[2]
    assert past % SLAB_LANES == 0 and len_p % SLAB_LANES == 0 and len_s % SLAB_LANES == 0

    cond = jnp.zeros((COND_ROWS, d), F32).at[0].set(c_ctx).at[1:1 + n_s].set(c)
    mods4 = _mods(cond, ada_w, ada_b).reshape(DEPTH, COND_ROWS, 1, 3 * d)
    lb_all = _lower_bounds(rec_lower_bounds)

    xp = x_prompt.reshape(n_p * len_p, d)
    xs = x_sample.reshape(n_s * len_s, d)

    w_in = attn_w_in[0].astype(BF16)
    w_out = attn_w_out[0].astype(BF16)
    reps = GROUP_LANES // HEAD_DIM
    qg = (jnp.tile(attn_q_gain[0], reps) * (LOG2_E / math.sqrt(HEAD_DIM))).reshape(1, GROUP_LANES)
    kg = jnp.tile(attn_k_gain[0], reps).reshape(1, GROUP_LANES)
    pn = _head_mean_matrix()
    ln_g = ln_gain[0].reshape(1, d)
    ln_b = ln_bias[0].reshape(1, d)

    xp1, k_p, v_p = _attn_seq(xp, mods4, w_in, pn, qg, kg, w_out, ln_g, ln_b, n_seq=n_p,
                              seq_len=len_p, mod_row=0)
    q_s, kp_s, vt_s, g_s = _attn_proj(xs, mods4, w_in, pn, qg, kg, _rope_tables(len_s), seq_len=len_s,
                                      mod_row0=1)
    cache = (cache_k[:, 0].reshape(n_s, past, KV_WIDTH), cache_v[:, 0].reshape(n_s, past, KV_WIDTH))
    xs1 = _attn_core(q_s, kp_s, vt_s, cache, g_s, xs, mods4, w_out, ln_g, ln_b, n_seq=n_s,
                     seq_len=len_s, q_tile=ATTN_Q_TILE, mod_row0=1)
    new_cache_k = k_p.reshape(n_p, 1, len_p, N_KV_HEADS, HEAD_DIM)
    new_cache_v = v_p.reshape(n_p, 1, len_p, N_KV_HEADS, HEAD_DIM)

    rw_in = rec_w_in[0].astype(BF16)
    rw_out = rec_w_out[0].astype(BF16)
    lb = lb_all[1]
    ng = rec_norm_gain[0].reshape(1, REC_DV)
    ln_g = ln_gain[1].reshape(1, d)
    ln_b = ln_bias[1].reshape(1, d)
    consts = _gla_consts()

    outs = []
    states = None
    for x1, n_seq, seq_len, row0, per_seq, s0 in (
            (xp1, n_p, len_p, 0, False, None),
            (xs1, n_s, len_s, 1, True, state_rec[:, 0])):
        q, v, g, lf_fw, lf_bw, k_fw, k_bw = _rec_proj(x1, mods4, rw_in, lb, seq_len=seq_len,
                                                       mod_row0=row0, per_seq_mod=per_seq)
        o, st = _gla(q, v, lf_fw, lf_bw, k_fw, k_bw, s0, consts, n_seq=n_seq, seq_len=seq_len,
                     want_state=s0 is None)
        if st is not None:
            states = st
        outs.append(_rec_out(o, g, x1, mods4, ng, rw_out, ln_g, ln_b, seq_len=seq_len,
                             mod_row0=row0, per_seq_mod=per_seq))

    y_prompt = outs[0].reshape(n_p, len_p, d)
    y_sample = outs[1].reshape(n_s, len_s, d)
    new_state_rec = states.reshape(n_p, 1, 2, N_REC_HEADS, REC_DK, REC_DV)
    return (y_prompt, y_sample, new_cache_k, new_cache_v, new_state_rec)
```
